```python
import math
import jax
import jax.numpy as jnp
from jax import lax
import numpy as np

D_MODEL = 1024
BATCH = 2
SEQ = 8192
DEPTH = 4
DEC_BATCH = 128
DEC_SEQ = 1
PAST_LEN = 8192
PAGE_SIZE = 128

N_MIXERS = 4
N_A = len(range(0, DEPTH, N_MIXERS))
N_B = len(range(1, DEPTH, N_MIXERS))
N_C = len(range(2, DEPTH, N_MIXERS))
N_D = len(range(3, DEPTH, N_MIXERS))
DN_ALPHA = (2 * DEPTH) ** 0.25
DN_BETA = (8 * DEPTH) ** -0.25
LN_EPS = 1e-5
CONV = 4
HA = 8
DKA = D_MODEL // HA
DVA = D_MODEL // HA
QKV_A = HA * (2 * DKA + DVA)
CHUNK_A = 64
LRU_W = D_MODEL
LRU_BLOCKS = 8
LRU_BW = LRU_W // LRU_BLOCKS
LRU_C = 8.0
HD = 64
HQ_C = D_MODEL // HD
HKV_C = 4
WINDOW = 128
N_BUCKETS = 32
MAX_DIST = 128
HQ_D = D_MODEL // HD
HKV_D = 4
QBLOCK = 128
N_EXPERTS = 16
N_GROUPS = 4
EPG = N_EXPERTS // N_GROUPS
TOP_K = 2
D_EXPERT = D_MODEL // 2

kernel_name = 'hybrid_gdn_rglru_swa_fox_moe_step'


def layer_norm(x, g, b):
    xf = x.astype(jnp.float32)
    mu = jnp.mean(xf, -1, keepdims=True)
    var = jnp.mean(jnp.square(xf - mu), -1, keepdims=True)
    return ((xf - mu) * lax.rsqrt(var + LN_EPS)).astype(x.dtype) * g + b


def causal_conv(x, buf, w):
    L = x.shape[1]
    xp = jnp.concatenate([buf.astype(x.dtype), x], axis=1)
    y = xp[:, 0:L] * w[0]
    for j in range(1, CONV):
        y = y + xp[:, j:j + L] * w[j]
    return y, xp[:, -(CONV - 1):]


def l2norm(t):
    tf = t.astype(jnp.float32)
    return tf * lax.rsqrt(jnp.sum(tf * tf, -1, keepdims=True) + 1e-6)


def gdn_project(x, w_in, conv_w, conv_buf, a_log, dt_bias):
    B, L, _ = x.shape
    h = x @ w_in
    qkv, gate, a_raw, b_raw = jnp.split(h, [QKV_A, QKV_A + HA * DVA, QKV_A + HA * DVA + HA], axis=-1)
    qkv, new_buf = causal_conv(qkv, conv_buf, conv_w)
    qkv = jax.nn.silu(qkv)
    q, k, v = jnp.split(qkv, [HA * DKA, 2 * HA * DKA], axis=-1)
    q = l2norm(q.reshape(B, L, HA, DKA)) * (DKA ** -0.5)
    k = l2norm(k.reshape(B, L, HA, DKA))
    v = v.reshape(B, L, HA, DVA).astype(jnp.float32)
    g = -jnp.exp(a_log.astype(jnp.float32)) * jax.nn.softplus(a_raw.astype(jnp.float32) + dt_bias.astype(jnp.float32))
    beta = jax.nn.sigmoid(b_raw.astype(jnp.float32))
    return q, k, v, g, beta, gate, new_buf


def gdn_chunked(q, k, v, g, beta, s0):
    B, L, H, DK = q.shape
    DV = v.shape[-1]
    C = CHUNK_A
    N = L // C

    def chunks(t):
        return jnp.moveaxis(t.reshape((B, N, C) + t.shape[2:]), 3, 2)

    qc, kc, vc, bc = chunks(q), chunks(k), chunks(v), chunks(beta)
    gc = jnp.cumsum(chunks(g), axis=-1)
    strict = jnp.tril(jnp.ones((C, C), bool), -1)
    incl = jnp.tril(jnp.ones((C, C), bool))
    dec = jnp.exp(jnp.where(incl, gc[..., :, None] - gc[..., None, :], -jnp.inf))
    a_mat = jnp.where(strict, jnp.einsum('bnhcd,bnhsd->bnhcs', kc, kc) * dec, 0.0) * bc[..., None]
    rhs = jnp.concatenate([vc * bc[..., None], kc * (bc * jnp.exp(gc))[..., None]], axis=-1)
    eye = jnp.eye(C, dtype=jnp.float32)
    sol = lax.linalg.triangular_solve(eye + a_mat, rhs, left_side=True, lower=True)
    u, w = sol[..., :DV], sol[..., DV:]
    qk = jnp.einsum('bnhcd,bnhsd->bnhcs', qc, kc) * dec
    q_dec = qc * jnp.exp(gc)[..., None]
    k_dec = kc * jnp.exp(gc[..., -1:] - gc)[..., None]
    g_last = jnp.exp(gc[..., -1])

    def step(S, inp):
        u_n, w_n, qk_n, qd_n, kd_n, gl_n = inp
        v_new = u_n - jnp.einsum('bhcd,bhde->bhce', w_n, S)
        o = jnp.einsum('bhcd,bhde->bhce', qd_n, S) + jnp.einsum('bhcs,bhse->bhce', qk_n, v_new)
        S = S * gl_n[..., None, None] + jnp.einsum('bhcd,bhce->bhde', kd_n, v_new)
        return S, o

    xs = tuple(jnp.moveaxis(t, 1, 0) for t in (u, w, qk, q_dec, k_dec, g_last))
    S, o = lax.scan(step, s0.astype(jnp.float32), xs)
    o = jnp.moveaxis(jnp.moveaxis(o, 0, 1), 3, 2).reshape(B, L, H, DV)
    return o, S


def gdn_recurrent(q, k, v, g, beta, s0):
    def step(S, inp):
        q_t, k_t, v_t, g_t, b_t = inp
        S = S * jnp.exp(g_t)[..., None, None]
        pred = jnp.einsum('bhd,bhde->bhe', k_t, S)
        S = S + jnp.einsum('bhd,bhe->bhde', k_t, b_t[..., None] * (v_t - pred))
        return S, jnp.einsum('bhd,bhde->bhe', q_t, S)

    xs = tuple(jnp.moveaxis(t.astype(jnp.float32), 1, 0) for t in (q, k, v, g, beta))
    S, o = lax.scan(step, s0.astype(jnp.float32), xs)
    return jnp.moveaxis(o, 0, 1), S


def gdn_mixer(x, conv_buf, s0, w_in, conv_w, a_log, dt_bias, norm_w, w_out, chunked):
    B, L, _ = x.shape
    q, k, v, g, beta, gate, new_buf = gdn_project(x, w_in, conv_w, conv_buf, a_log, dt_bias)
    if chunked:
        o, s = gdn_chunked(q, k, v, g, beta, s0)
    else:
        o, s = gdn_recurrent(q, k, v, g, beta, s0)
    of = o * lax.rsqrt(jnp.mean(o * o, -1, keepdims=True) + 1e-6)
    y = of.astype(x.dtype) * norm_w * jax.nn.silu(gate.reshape(B, L, HA, DVA))
    return y.reshape(B, L, HA * DVA) @ w_out, s, new_buf


def rglru_mixer(x, conv_buf, h0, w_in, conv_w, conv_b, w_a, b_a, w_x, b_x, lam, w_out):
    B, L, _ = x.shape
    gate_in, rec_in = jnp.split(x @ w_in, 2, axis=-1)
    u, new_buf = causal_conv(rec_in, conv_buf, conv_w)
    u = u + conv_b
    ub = u.reshape(B, L, LRU_BLOCKS, LRU_BW)
    r = jax.nn.sigmoid((jnp.einsum('blni,nij->blnj', ub, w_a).reshape(B, L, LRU_W) + b_a).astype(jnp.float32))
    i_g = jax.nn.sigmoid((jnp.einsum('blni,nij->blnj', ub, w_x).reshape(B, L, LRU_W) + b_x).astype(jnp.float32))
    log_a = -LRU_C * r * jax.nn.softplus(-lam.astype(jnp.float32))
    a = jnp.exp(log_a)
    b = jnp.sqrt(-jnp.expm1(2.0 * log_a)) * (i_g * u.astype(jnp.float32))
    b = b.at[:, 0].add(a[:, 0] * h0.astype(jnp.float32))

    def combine(left, right):
        a_l, b_l = left
        a_r, b_r = right
        return a_l * a_r, a_r * b_l + b_r

    _, h = lax.associative_scan(combine, (a, b), axis=1)
    y = jax.nn.gelu(gate_in) * h.astype(x.dtype)
    return y @ w_out, h[:, -1], new_buf


def t5_bucket(rel):
    n = jnp.maximum(rel, 0)
    max_exact = N_BUCKETS // 2
    large = max_exact + (jnp.log(jnp.maximum(n, 1).astype(jnp.float32) / max_exact)
                         / math.log(MAX_DIST / max_exact) * (N_BUCKETS - max_exact)).astype(jnp.int32)
    return jnp.where(n < max_exact, n, jnp.minimum(large, N_BUCKETS - 1))


def swa_project(x, w_in):
    B, L, _ = x.shape
    q, k, v = jnp.split(x @ w_in, [HQ_C * HD, (HQ_C + HKV_C) * HD], axis=-1)
    return q.reshape(B, L, HQ_C, HD), k.reshape(B, L, HKV_C, HD), v.reshape(B, L, HKV_C, HD)


def sink_attend(q, k, v, rel, valid, sinks, rel_bias):
    G = HQ_C // HKV_C
    lead, Lq = q.shape[:-3], q.shape[-3]
    qg = q.reshape(lead + (Lq, HKV_C, G, HD))
    s = jnp.einsum('...qhgd,...khd->...hgqk', qg, k).astype(jnp.float32) * (HD ** -0.5)
    bias = jnp.moveaxis(rel_bias[t5_bucket(rel)], -1, 0).reshape((HKV_C, G) + rel.shape).astype(jnp.float32)
    s = jnp.where(valid[..., None, None, :, :], s + bias, -jnp.inf)
    sink = jnp.broadcast_to(sinks.astype(jnp.float32).reshape(HKV_C, G, 1, 1), s.shape[:-1] + (1,))
    p = jax.nn.softmax(jnp.concatenate([s, sink], axis=-1), axis=-1)[..., :-1]
    o = jnp.einsum('...hgqk,...khd->...qhgd', p.astype(v.dtype), v)
    return o.reshape(lead + (Lq, HQ_C, HD))


def swa_prompt(q, k, v, sinks, rel_bias):
    B, L = q.shape[:2]
    nb = L // WINDOW
    qb = q.reshape(B, nb, WINDOW, HQ_C, HD)
    kb = k.reshape(B, nb, WINDOW, HKV_C, HD)
    vb = v.reshape(B, nb, WINDOW, HKV_C, HD)
    kk = jnp.concatenate([jnp.concatenate([jnp.zeros_like(kb[:, :1]), kb[:, :-1]], 1), kb], axis=2)
    vv = jnp.concatenate([jnp.concatenate([jnp.zeros_like(vb[:, :1]), vb[:, :-1]], 1), vb], axis=2)
    ii = jnp.arange(WINDOW)[:, None]
    jj = jnp.arange(2 * WINDOW)[None, :]
    rel = ii + WINDOW - jj
    blk = jnp.arange(nb)[:, None, None]
    valid = ((rel >= 0) & (rel <= WINDOW))[None] & ((blk > 0) | (jj >= WINDOW))[:, :, :]
    o = sink_attend(qb, kk, vv, rel, valid, sinks, rel_bias)
    return o.reshape(B, L, HQ_C, HD)


def swa_sample(q, k_new, v_new, k_buf, v_buf, sinks, rel_bias):
    lb, n = k_buf.shape[1], q.shape[1]
    kk = jnp.concatenate([k_buf.astype(k_new.dtype), k_new], axis=1)
    vv = jnp.concatenate([v_buf.astype(v_new.dtype), v_new], axis=1)
    rel = jnp.arange(n)[:, None] + lb - jnp.arange(lb + n)[None, :]
    valid = (rel >= 0) & (rel <= WINDOW)
    o = sink_attend(q, kk, vv, rel, valid, sinks, rel_bias)
    return o, kk[:, -WINDOW:], vv[:, -WINDOW:]


def fox_project(x, w_in, b_f):
    B, L, _ = x.shape
    q, k, v, f = jnp.split(x @ w_in, [HQ_D * HD, (HQ_D + HKV_D) * HD, (HQ_D + 2 * HKV_D) * HD], axis=-1)
    logf = jax.nn.log_sigmoid(f.astype(jnp.float32) + b_f.astype(jnp.float32))
    return (q.reshape(B, L, HQ_D, HD), k.reshape(B, L, HKV_D, HD), v.reshape(B, L, HKV_D, HD), logf)


def fox_attend(q, k, v, fq, fk, q_pos, k_pos):
    B, Lq = q.shape[:2]
    Lk = k.shape[1]
    G = HQ_D // HKV_D
    qb = QBLOCK if Lq % QBLOCK == 0 else Lq
    nqb = Lq // qb
    fk_h = jnp.moveaxis(fk, 1, 2).reshape(B, HKV_D, G, 1, Lk)

    def block(args):
        q_blk, fq_blk, pos_blk = args
        qg = q_blk.reshape(B, qb, HKV_D, G, HD)
        s = jnp.einsum('bqhgd,bkhd->bhgqk', qg, k).astype(jnp.float32) * (HD ** -0.5)
        s = s + jnp.moveaxis(fq_blk, 1, 2).reshape(B, HKV_D, G, qb, 1) - fk_h
        s = jnp.where(k_pos[None, :] <= pos_blk[:, None], s, -jnp.inf)
        p = jax.nn.softmax(s, axis=-1)
        o = jnp.einsum('bhgqk,bkhd->bqhgd', p.astype(v.dtype), v)
        return o.reshape(B, qb, HQ_D, HD)

    qs = jnp.moveaxis(q.reshape(B, nqb, qb, HQ_D, HD), 1, 0)
    fqs = jnp.moveaxis(fq.reshape(B, nqb, qb, HQ_D), 1, 0)
    ps = q_pos.reshape(nqb, qb)
    o = lax.map(block, (qs, fqs, ps))
    return jnp.moveaxis(o, 0, 1).reshape(B, Lq, HQ_D, HD)


def gather_pages(pool, layer, page_table):
    g = pool[layer, page_table]
    return g.reshape((page_table.shape[0], -1) + pool.shape[3:])


def moe(x, router_w, router_b, w_gate, w_up, w_down):
    shp = x.shape
    t = x.reshape(-1, shp[-1])
    scores = jax.nn.sigmoid((t @ router_w).astype(jnp.float32))
    sel = (scores + router_b.astype(jnp.float32)).reshape(-1, N_GROUPS, EPG)
    group_score = lax.top_k(sel, TOP_K)[0].sum(-1)
    g_idx = jnp.argmax(group_score, axis=-1)
    in_group = jnp.take_along_axis(sel, g_idx[:, None, None], axis=1)[:, 0]
    _, local = lax.top_k(in_group, TOP_K)
    expert_idx = g_idx[:, None] * EPG + local
    gate = jnp.take_along_axis(scores, expert_idx, axis=-1)
    gate = gate / jnp.sum(gate, -1, keepdims=True)
    combine = jnp.einsum('tk,tke->te', gate, jax.nn.one_hot(expert_idx, N_EXPERTS, dtype=jnp.float32)).astype(t.dtype)
    y = jnp.zeros_like(t)
    for e in range(N_EXPERTS):
        h = jax.nn.silu(t @ w_gate[e]) * (t @ w_up[e])
        y = y + combine[:, e:e + 1] * (h @ w_down[e])
    return y.reshape(shp)


def setup_inputs(seed: int = 0) -> dict:
    key = jax.random.key(seed)
    ks = iter(jax.random.split(key, 64))
    f32 = jnp.float32

    def nrm(shape, scale=1.0):
        return jax.random.normal(next(ks), shape, f32) * scale

    def unif(shape, lo, hi):
        return jax.random.uniform(next(ks), shape, f32, lo, hi)

    n_pages = PAST_LEN // PAGE_SIZE
    n_used = DEC_BATCH * n_pages
    n_pool = n_used + max(1, n_used // 4)
    win_buf = min(WINDOW, PAST_LEN)
    page_table = jax.random.permutation(next(ks), n_pool)[:n_used].reshape(DEC_BATCH, n_pages).astype(jnp.int32)
    dt = jnp.exp(unif((N_A, HA), math.log(1e-3), math.log(1e-1)))
    lru_a = unif((N_B, LRU_W), 0.9, 0.999) ** (1.0 / LRU_C)
    return {
        'x_prompt': nrm((BATCH, SEQ, D_MODEL)),
        'x_sample': nrm((DEC_BATCH, DEC_SEQ, D_MODEL)),
        'state_a_ssm': nrm((N_A, DEC_BATCH, HA, DKA, DVA), 0.3),
        'state_a_conv': nrm((N_A, DEC_BATCH, CONV - 1, QKV_A)),
        'state_b_h': nrm((N_B, DEC_BATCH, LRU_W), 0.5),
        'state_b_conv': nrm((N_B, DEC_BATCH, CONV - 1, LRU_W)),
        'cache_c_k': nrm((N_C, DEC_BATCH, win_buf, HKV_C, HD)),
        'cache_c_v': nrm((N_C, DEC_BATCH, win_buf, HKV_C, HD)),
        'cache_d_k': nrm((N_D, n_pool, PAGE_SIZE, HKV_D, HD)),
        'cache_d_v': nrm((N_D, n_pool, PAGE_SIZE, HKV_D, HD)),
        'cache_d_logf': jax.nn.log_sigmoid(nrm((N_D, n_pool, PAGE_SIZE, HQ_D)) + 3.0),
        'page_table': page_table,
        'ln_g': 1.0 + nrm((DEPTH, 2, D_MODEL), 0.02),
        'ln_b': nrm((DEPTH, 2, D_MODEL), 0.02),
        'a_w_in': nrm((N_A, D_MODEL, QKV_A + HA * DVA + 2 * HA), D_MODEL ** -0.5),
        'a_conv_w': nrm((N_A, CONV, QKV_A), CONV ** -0.5),
        'a_a_log': jnp.log(unif((N_A, HA), 1.0, 16.0)),
        'a_dt_bias': dt + jnp.log(-jnp.expm1(-dt)),
        'a_norm_w': 1.0 + nrm((N_A, DVA), 0.02),
        'a_w_out': nrm((N_A, HA * DVA, D_MODEL), (HA * DVA) ** -0.5 * DN_BETA),
        'b_w_in': nrm((N_B, D_MODEL, 2 * LRU_W), D_MODEL ** -0.5),
        'b_conv_w': nrm((N_B, CONV, LRU_W), CONV ** -0.5),
        'b_conv_b': nrm((N_B, LRU_W), 0.02),
        'b_w_a': nrm((N_B, LRU_BLOCKS, LRU_BW, LRU_BW), LRU_BW ** -0.5),
        'b_b_a': nrm((N_B, LRU_W), 0.02),
        'b_w_x': nrm((N_B, LRU_BLOCKS, LRU_BW, LRU_BW), LRU_BW ** -0.5),
        'b_b_x': nrm((N_B, LRU_W), 0.02),
        'b_lambda': jnp.log(lru_a) - jnp.log1p(-lru_a),
        'b_w_out': nrm((N_B, LRU_W, D_MODEL), LRU_W ** -0.5 * DN_BETA),
        'c_w_in': nrm((N_C, D_MODEL, (HQ_C + 2 * HKV_C) * HD), D_MODEL ** -0.5),
        'c_sinks': nrm((N_C, HQ_C), 0.5),
        'c_w_out': nrm((N_C, HQ_C * HD, D_MODEL), (HQ_C * HD) ** -0.5 * DN_BETA),
        'rel_bias': nrm((N_BUCKETS, HQ_C), 0.5),
        'd_w_in': nrm((N_D, D_MODEL, (HQ_D + 2 * HKV_D) * HD + HQ_D), D_MODEL ** -0.5),
        'd_b_f': unif((N_D, HQ_D), 2.0, 5.0),
        'd_w_out': nrm((N_D, HQ_D * HD, D_MODEL), (HQ_D * HD) ** -0.5 * DN_BETA),
        'router_w': nrm((D_MODEL, N_EXPERTS), D_MODEL ** -0.5),
        'router_b': nrm((N_EXPERTS,), 0.01),
        'moe_w_gate': nrm((DEPTH, N_EXPERTS, D_MODEL, D_EXPERT), D_MODEL ** -0.5),
        'moe_w_up': nrm((DEPTH, N_EXPERTS, D_MODEL, D_EXPERT), D_MODEL ** -0.5),
        'moe_w_down': nrm((DEPTH, N_EXPERTS, D_EXPERT, D_MODEL), D_EXPERT ** -0.5 * DN_BETA),
    }


def reference(x_prompt, x_sample, state_a_ssm, state_a_conv, state_b_h, state_b_conv,
              cache_c_k, cache_c_v, cache_d_k, cache_d_v, cache_d_logf, page_table,
              ln_g, ln_b, a_w_in, a_conv_w, a_a_log, a_dt_bias, a_norm_w, a_w_out,
              b_w_in, b_conv_w, b_conv_b, b_w_a, b_b_a, b_w_x, b_b_x, b_lambda, b_w_out,
              c_w_in, c_sinks, c_w_out, rel_bias, d_w_in, d_b_f, d_w_out,
              router_w, router_b, moe_w_gate, moe_w_up, moe_w_down):
    xp, xs = x_prompt, x_sample
    bp, lp = xp.shape[:2]
    bs, ls = xs.shape[:2]
    past_len = page_table.shape[1] * PAGE_SIZE
    names = ('a_ssm', 'a_conv', 'b_h', 'b_conv', 'c_k', 'c_v', 'd_k', 'd_v', 'd_logf')
    newp = {n: [] for n in names}
    news = {n: [] for n in names}
    for i in range(DEPTH):
        m, j = i % N_MIXERS, i // N_MIXERS
        if m == 0:
            mp, sp, cp = gdn_mixer(xp, jnp.zeros((bp, CONV - 1, QKV_A), xp.dtype), jnp.zeros((bp, HA, DKA, DVA), jnp.float32),
                                   a_w_in[j], a_conv_w[j], a_a_log[j], a_dt_bias[j], a_norm_w[j], a_w_out[j], True)
            ms, ss, cs = gdn_mixer(xs, state_a_conv[j], state_a_ssm[j],
                                   a_w_in[j], a_conv_w[j], a_a_log[j], a_dt_bias[j], a_norm_w[j], a_w_out[j], False)
            newp['a_ssm'].append(sp)
            newp['a_conv'].append(cp)
            news['a_ssm'].append(ss)
            news['a_conv'].append(cs)
        elif m == 1:
            mp, hp, cp = rglru_mixer(xp, jnp.zeros((bp, CONV - 1, LRU_W), xp.dtype), jnp.zeros((bp, LRU_W), jnp.float32),
                                     b_w_in[j], b_conv_w[j], b_conv_b[j], b_w_a[j], b_b_a[j], b_w_x[j], b_b_x[j], b_lambda[j], b_w_out[j])
            ms, hs, cs = rglru_mixer(xs, state_b_conv[j], state_b_h[j],
                                     b_w_in[j], b_conv_w[j], b_conv_b[j], b_w_a[j], b_b_a[j], b_w_x[j], b_b_x[j], b_lambda[j], b_w_out[j])
            newp['b_h'].append(hp)
            newp['b_conv'].append(cp)
            news['b_h'].append(hs)
            news['b_conv'].append(cs)
        elif m == 2:
            qp, kp, vp = swa_project(xp, c_w_in[j])
            op = swa_prompt(qp, kp, vp, c_sinks[j], rel_bias)
            mp = op.reshape(bp, lp, HQ_C * HD) @ c_w_out[j]
            qs_, ks_, vs_ = swa_project(xs, c_w_in[j])
            os_, kbuf, vbuf = swa_sample(qs_, ks_, vs_, cache_c_k[j], cache_c_v[j], c_sinks[j], rel_bias)
            ms = os_.reshape(bs, ls, HQ_C * HD) @ c_w_out[j]
            newp['c_k'].append(kp[:, -WINDOW:])
            newp['c_v'].append(vp[:, -WINDOW:])
            news['c_k'].append(kbuf)
            news['c_v'].append(vbuf)
        else:
            qp, kp, vp, lfp = fox_project(xp, d_w_in[j], d_b_f[j])
            fp = jnp.cumsum(lfp, axis=1)
            pos_p = jnp.arange(lp)
            op = fox_attend(qp, kp, vp, fp, fp, pos_p, pos_p)
            mp = op.reshape(bp, lp, HQ_D * HD) @ d_w_out[j]
            qs_, ks_, vs_, lfs = fox_project(xs, d_w_in[j], d_b_f[j])
            k_all = jnp.concatenate([gather_pages(cache_d_k, j, page_table).astype(ks_.dtype), ks_], axis=1)
            v_all = jnp.concatenate([gather_pages(cache_d_v, j, page_table).astype(vs_.dtype), vs_], axis=1)
            lf_all = jnp.concatenate([gather_pages(cache_d_logf, j, page_table).astype(jnp.float32), lfs], axis=1)
            f_all = jnp.cumsum(lf_all, axis=1)
            os_ = fox_attend(qs_, k_all, v_all, f_all[:, past_len:], f_all,
                             past_len + jnp.arange(ls), jnp.arange(past_len + ls))
            ms = os_.reshape(bs, ls, HQ_D * HD) @ d_w_out[j]
            newp['d_k'].append(kp)
            newp['d_v'].append(vp)
            newp['d_logf'].append(lfp)
            news['d_k'].append(ks_)
            news['d_v'].append(vs_)
            news['d_logf'].append(lfs)
        xp = layer_norm(DN_ALPHA * xp + mp, ln_g[i, 0], ln_b[i, 0])
        xs = layer_norm(DN_ALPHA * xs + ms, ln_g[i, 0], ln_b[i, 0])
        xp = layer_norm(DN_ALPHA * xp + moe(xp, router_w, router_b, moe_w_gate[i], moe_w_up[i], moe_w_down[i]), ln_g[i, 1], ln_b[i, 1])
        xs = layer_norm(DN_ALPHA * xs + moe(xs, router_w, router_b, moe_w_gate[i], moe_w_up[i], moe_w_down[i]), ln_g[i, 1], ln_b[i, 1])

    def stk(lst, like):
        return jnp.stack(lst).astype(like.dtype)

    return (xp, xs,
            stk(newp['a_ssm'], state_a_ssm), stk(newp['a_conv'], state_a_conv),
            stk(newp['b_h'], state_b_h), stk(newp['b_conv'], state_b_conv),
            stk(newp['c_k'], cache_c_k), stk(newp['c_v'], cache_c_v),
            stk(newp['d_k'], cache_d_k), stk(newp['d_v'], cache_d_v), stk(newp['d_logf'], cache_d_logf),
            stk(news['a_ssm'], state_a_ssm), stk(news['a_conv'], state_a_conv),
            stk(news['b_h'], state_b_h), stk(news['b_conv'], state_b_conv),
            stk(news['c_k'], cache_c_k), stk(news['c_v'], cache_c_v),
            stk(news['d_k'], cache_d_k), stk(news['d_v'], cache_d_v), stk(news['d_logf'], cache_d_logf))
```

```python
import functools
import math

import jax
import jax.numpy as jnp
import numpy as np
from jax import lax
from jax.experimental import pallas as pl
from jax.experimental.pallas import tpu as pltpu

f32 = jnp.float32
bf16 = jnp.bfloat16
HI = lax.Precision.HIGHEST

D_MODEL = 1024
DEPTH = 4
DN_ALPHA = (2 * DEPTH) ** 0.25
LN_EPS = 1e-5
CONV = 4
HA = 8
DKA = 128
DVA = 128
QKV_A = HA * (2 * DKA + DVA)
CHUNK_A = 64
LRU_W = 1024
LRU_BLOCKS = 8
LRU_BW = 128
LRU_C = 8.0
HD = 64
HQ = 16
HKV = 4
GQ = HQ // HKV
WINDOW = 128
N_BUCKETS = 32
MAX_DIST = 128
PAGE_SIZE = 128
N_EXPERTS = 16
N_GROUPS = 4
EPG = 4
D_EXPERT = 512

VMEM_LIMIT = 56 * 1024 * 1024


def _cp(*sem):
    return pltpu.CompilerParams(dimension_semantics=sem, vmem_limit_bytes=VMEM_LIMIT)


def _dot(a, b):
    return jnp.dot(a.astype(bf16), b.astype(bf16), preferred_element_type=f32)


def _dot_nt(a, b):
    return lax.dot_general(a.astype(bf16), b.astype(bf16), (((1,), (1,)), ((), ())), preferred_element_type=f32)


def _dot_tn(a, b):
    return lax.dot_general(a.astype(bf16), b.astype(bf16), (((0,), (0,)), ((), ())), preferred_element_type=f32)


def _dot_hi(a, b):
    return jnp.dot(a, b, precision=HI, preferred_element_type=f32)


def _dot_nt_hi(a, b):
    return lax.dot_general(a, b, (((1,), (1,)), ((), ())), precision=HI, preferred_element_type=f32)


def _sigmoid(x):
    return 1.0 / (1.0 + jnp.exp(-x))


def _silu(x):
    return x * _sigmoid(x)


def _softplus(x):
    return jnp.maximum(x, 0.0) + jnp.log1p(jnp.exp(-jnp.abs(x)))


def _log_sigmoid(x):
    return -_softplus(-x)


def _eye(n):
    r = lax.broadcasted_iota(jnp.int32, (n, n), 0)
    c = lax.broadcasted_iota(jnp.int32, (n, n), 1)
    return (r == c).astype(f32)


def _proj_body(n_out, x_ref, *refs):
    x = x_ref[...].astype(bf16)
    for w_ref, o_ref in zip(refs[:n_out], refs[n_out:]):
        o_ref[...] = jnp.dot(x, w_ref[...], preferred_element_type=f32)


def _proj(x, ws, tm):
    m, k = x.shape
    n_out = len(ws)
    return pl.pallas_call(
        functools.partial(_proj_body, n_out),
        grid=(m // tm,),
        in_specs=[pl.BlockSpec((tm, k), lambda i: (i, 0))]
        + [pl.BlockSpec(w.shape, lambda i: (0, 0)) for w in ws],
        out_specs=[pl.BlockSpec((tm, w.shape[1]), lambda i: (i, 0)) for w in ws],
        out_shape=[jax.ShapeDtypeStruct((m, w.shape[1]), f32) for w in ws],
        compiler_params=_cp("parallel"),
    )(x, *ws)


def _layer_norm(z, g, b):
    mu = jnp.mean(z, -1, keepdims=True)
    zc = z - mu
    var = jnp.mean(zc * zc, -1, keepdims=True)
    return zc * lax.rsqrt(var + LN_EPS) * g + b


def _proj_ln_body(h_ref, w_ref, x_ref, g_ref, b_ref, o_ref):
    m = jnp.dot(h_ref[...].astype(bf16), w_ref[...], preferred_element_type=f32)
    o_ref[...] = _layer_norm(DN_ALPHA * x_ref[...] + m, g_ref[...], b_ref[...])


def _proj_ln(h, w, x, g, b, tm):
    m, k = h.shape
    d = x.shape[1]
    return pl.pallas_call(
        _proj_ln_body,
        grid=(m // tm,),
        in_specs=[pl.BlockSpec((tm, k), lambda i: (i, 0)),
                  pl.BlockSpec((k, d), lambda i: (0, 0)),
                  pl.BlockSpec((tm, d), lambda i: (i, 0)),
                  pl.BlockSpec((1, d), lambda i: (0, 0)),
                  pl.BlockSpec((1, d), lambda i: (0, 0))],
        out_specs=pl.BlockSpec((tm, d), lambda i: (i, 0)),
        out_shape=jax.ShapeDtypeStruct((m, d), f32),
        compiler_params=_cp("parallel"),
    )(h, w, x, g.reshape(1, d), b.reshape(1, d))


GDN_TILE = 256


def _gdn_gates(ab, alog, dtb):
    g = -jnp.exp(alog) * _softplus(ab[:, 0:HA] + dtb)
    beta = _sigmoid(ab[:, HA:2 * HA])
    return g, beta


def _l2norm(t):
    return t * lax.rsqrt(jnp.sum(t * t, -1, keepdims=True) + 1e-6)


def _gdn_prep_body(qkv_ref, halo_ref, ab_ref, cw_ref, alog_ref, dtb_ref,
                   u_ref, w_ref, qd_ref, kd_ref, qk_ref, eg_ref, xp_scr, y_scr):
    i = pl.program_id(1)
    tc = qkv_ref.shape[1]
    c = CHUNK_A
    xp_scr[0:8, :] = jnp.where(i > 0, halo_ref[0], 0.0)
    xp_scr[8:8 + tc, :] = qkv_ref[0]
    cw = cw_ref[...]
    y = xp_scr[5:5 + tc, :] * cw[0:1, :]
    for j in range(1, CONV):
        y = y + xp_scr[5 + j:5 + j + tc, :] * cw[j:j + 1, :]
    y_scr[...] = _silu(y)

    row = lax.broadcasted_iota(jnp.int32, (c, c), 0)
    col = lax.broadcasted_iota(jnp.int32, (c, c), 1)
    incl = row >= col
    strict = row > col
    tri = incl.astype(f32)
    eye_c = (row == col).astype(f32)
    eye_h = _eye(HA)
    alog = alog_ref[...]
    dtb = dtb_ref[...]

    def chunk(n, carry):
        r0 = pl.multiple_of(n * c, c)
        rows = pl.ds(r0, c)
        g, beta = _gdn_gates(ab_ref[0, rows, :], alog, dtb)
        gc = _dot_hi(tri, g)
        gct = _dot_nt_hi(eye_h, gc)
        egc = jnp.exp(gc)
        eg_ref[0, rows, :] = egc
        for h in range(HA):
            q = _l2norm(y_scr[rows, h * DKA:(h + 1) * DKA]) * (DKA ** -0.5)
            k = _l2norm(y_scr[rows, HA * DKA + h * DKA:HA * DKA + (h + 1) * DKA])
            v = y_scr[rows, 2 * HA * DKA + h * DVA:2 * HA * DKA + (h + 1) * DVA]
            gcol = gc[:, h:h + 1]
            bcol = beta[:, h:h + 1]
            diff = gcol - gct[h:h + 1, :]
            dec = jnp.where(incl, jnp.exp(jnp.where(incl, diff, 0.0)), 0.0)
            a = jnp.where(strict, _dot_nt(k, k) * dec, 0.0) * bcol
            x = eye_c - a
            p = _dot_hi(a, a)
            for _ in range(4):
                x = x + _dot_hi(x, p)
                p = _dot_hi(p, p)
            x = x + _dot_hi(x, p)
            rhs = jnp.concatenate([v * bcol, k * (bcol * egc[:, h:h + 1])], axis=-1)
            sol = _dot_hi(x, rhs)
            hs = slice(h * DVA, (h + 1) * DVA)
            u_ref[0, rows, hs] = sol[:, :DVA]
            w_ref[0, rows, hs] = sol[:, DVA:]
            qd_ref[0, rows, hs] = q * egc[:, h:h + 1]
            kd_ref[0, rows, hs] = k * jnp.exp(gc[c - 1:c, h:h + 1] - gcol)
            qk_ref[0, rows, h * c:(h + 1) * c] = _dot_nt(q, k) * dec
        return carry

    lax.fori_loop(0, tc // c, chunk, 0)


def _gdn_prep(qkv, ab, conv_w, a_log, dt_bias):
    b, l, _ = qkv.shape
    tc = min(GDN_TILE, l)
    hb = tc // 8
    big = lambda n: pl.BlockSpec((1, tc, n), lambda bi, i: (bi, i, 0))
    full = lambda a: pl.BlockSpec(a.shape, lambda bi, i: (0, 0))
    wide = jax.ShapeDtypeStruct((b, l, HA * DVA), f32)
    return pl.pallas_call(
        _gdn_prep_body,
        grid=(b, l // tc),
        in_specs=[big(QKV_A),
                  pl.BlockSpec((1, 8, QKV_A), lambda bi, i: (bi, jnp.maximum(i * hb - 1, 0), 0)),
                  big(2 * HA), full(conv_w), full(a_log), full(dt_bias)],
        out_specs=[big(HA * DVA)] * 4 + [big(HA * CHUNK_A), big(HA)],
        out_shape=[wide] * 4 + [jax.ShapeDtypeStruct((b, l, HA * CHUNK_A), f32),
                                jax.ShapeDtypeStruct((b, l, HA), f32)],
        scratch_shapes=[pltpu.VMEM((tc + 8, QKV_A), f32), pltpu.VMEM((tc, QKV_A), f32)],
        compiler_params=_cp("parallel", "parallel"),
    )(qkv, qkv, ab, conv_w, a_log, dt_bias)


def _gdn_scan_body(u_ref, w_ref, qd_ref, kd_ref, qk_ref, eg_ref, gate_ref, nw_ref, y_ref, s_ref):
    i = pl.program_id(1)
    tc = u_ref.shape[1]
    c = CHUNK_A

    @pl.when(i == 0)
    def _():
        s_ref[...] = jnp.zeros_like(s_ref)

    nw = nw_ref[...]

    def chunk(n, carry):
        r0 = pl.multiple_of(n * c, c)
        rows = pl.ds(r0, c)
        eg_last = eg_ref[0, pl.ds(r0 + c - 1, 1), :]
        for h in range(HA):
            hs = slice(h * DVA, (h + 1) * DVA)
            s = s_ref[0, h]
            v_new = u_ref[0, rows, hs] - _dot(w_ref[0, rows, hs], s)
            o = _dot(qd_ref[0, rows, hs], s) + _dot(qk_ref[0, rows, h * c:(h + 1) * c], v_new)
            s_ref[0, h] = s * eg_last[:, h:h + 1] + _dot_tn(kd_ref[0, rows, hs], v_new)
            of = o * lax.rsqrt(jnp.mean(o * o, -1, keepdims=True) + 1e-6)
            y_ref[0, rows, hs] = of * nw * _silu(gate_ref[0, rows, hs])
        return carry

    lax.fori_loop(0, tc // c, chunk, 0)


def _gdn_scan(u, w, qd, kd, qk, eg, gate, norm_w):
    b, l, _ = u.shape
    tc = min(GDN_TILE, l)
    big = lambda n: pl.BlockSpec((1, tc, n), lambda bi, i: (bi, i, 0))
    return pl.pallas_call(
        _gdn_scan_body,
        grid=(b, l // tc),
        in_specs=[big(HA * DVA)] * 4 + [big(HA * CHUNK_A), big(HA), big(HA * DVA),
                                       pl.BlockSpec((1, DVA), lambda bi, i: (0, 0))],
        out_specs=[big(HA * DVA), pl.BlockSpec((1, HA, DKA, DVA), lambda bi, i: (bi, 0, 0, 0))],
        out_shape=[jax.ShapeDtypeStruct((b, l, HA * DVA), f32),
                   jax.ShapeDtypeStruct((b, HA, DKA, DVA), f32)],
        compiler_params=_cp("parallel", "arbitrary"),
    )(u, w, qd, kd, qk, eg, gate, norm_w.reshape(1, DVA))


def _gdn_prompt(x, w_in, conv_w, a_log, dt_bias, norm_w):
    b, l, d = x.shape
    wq = w_in[:, :QKV_A].astype(bf16)
    wg = w_in[:, QKV_A:QKV_A + HA * DVA].astype(bf16)
    wab = w_in[:, QKV_A + HA * DVA:].astype(bf16)
    qkv, gate, ab = _proj(x.reshape(b * l, d), [wq, wg, wab], 256)
    qkv = qkv.reshape(b, l, QKV_A)
    u, w, qd, kd, qk, eg = _gdn_prep(qkv, ab.reshape(b, l, 2 * HA), conv_w,
                                     a_log.reshape(1, HA), dt_bias.reshape(1, HA))
    y, s = _gdn_scan(u, w, qd, kd, qk, eg, gate.reshape(b, l, HA * DVA), norm_w)
    return y.reshape(b * l, HA * DVA), s, qkv[:, l - (CONV - 1):]


GDN_BT = 8


def _gdn_step_body(qkv_ref, gate_ref, ab_ref, buf_ref, s_ref, cw_ref, alog_ref, dtb_ref, nw_ref,
                   y_ref, s_out_ref, buf_out_ref, q_scr, k_scr, v_scr, eg_scr, beta_scr):
    n = QKV_A
    new = qkv_ref[...]
    cw = cw_ref[...]
    y = new * cw[CONV - 1:CONV, :]
    for j in range(CONV - 1):
        y = y + buf_ref[:, j * n:(j + 1) * n] * cw[j:j + 1, :]
    y = _silu(y)
    buf_out_ref[:, 0:2 * n] = buf_ref[:, n:3 * n]
    buf_out_ref[:, 2 * n:3 * n] = new
    for h in range(HA):
        q_scr[:, h * DKA:(h + 1) * DKA] = _l2norm(y[:, h * DKA:(h + 1) * DKA]) * (DKA ** -0.5)
        k_scr[:, h * DKA:(h + 1) * DKA] = _l2norm(y[:, HA * DKA + h * DKA:HA * DKA + (h + 1) * DKA])
    v_scr[...] = y[:, 2 * HA * DKA:]
    g, beta = _gdn_gates(ab_ref[...], alog_ref[...], dtb_ref[...])
    eg_scr[...] = jnp.exp(g)
    beta_scr[...] = beta
    eye = _eye(DKA)
    row8 = lax.broadcasted_iota(jnp.int32, (8, DKA), 0)
    nw = nw_ref[...]

    for bi in range(qkv_ref.shape[0]):
        r = slice(bi, bi + 1)
        for h in range(HA):
            hs = slice(h * DKA, (h + 1) * DKA)
            k_row = k_scr[r, hs]
            q_row = q_scr[r, hs]
            kq = jnp.where(row8 == 0, k_row, jnp.where(row8 == 1, q_row, 0.0))
            cols = _dot_nt_hi(eye, kq)
            k_col = cols[:, 0:1]
            q_col = cols[:, 1:2]
            sd = s_ref[bi, h] * eg_scr[r, h:h + 1]
            pred = jnp.sum(k_col * sd, axis=0, keepdims=True)
            delta = beta_scr[r, h:h + 1] * (v_scr[r, hs] - pred)
            s_new = sd + k_col * delta
            s_out_ref[bi, h] = s_new
            o = jnp.sum(q_col * s_new, axis=0, keepdims=True)
            of = o * lax.rsqrt(jnp.mean(o * o, -1, keepdims=True) + 1e-6)
            y_ref[r, hs] = of * nw * _silu(gate_ref[r, hs])


def _gdn_step(qkv, gate, ab, conv_buf, s0, conv_w, a_log, dt_bias, norm_w):
    nb = qkv.shape[0]
    bt = GDN_BT
    rowblk = lambda n: pl.BlockSpec((bt, n), lambda i: (i, 0))
    full = lambda a: pl.BlockSpec(a.shape, lambda i: (0, 0))
    sblk = pl.BlockSpec((bt, HA, DKA, DVA), lambda i: (i, 0, 0, 0))
    nw = norm_w.reshape(1, DVA)
    al = a_log.reshape(1, HA)
    db = dt_bias.reshape(1, HA)
    return pl.pallas_call(
        _gdn_step_body,
        grid=(nb // bt,),
        in_specs=[rowblk(QKV_A), rowblk(HA * DVA), rowblk(2 * HA), rowblk(3 * QKV_A), sblk,
                  full(conv_w), full(al), full(db), full(nw)],
        out_specs=[rowblk(HA * DVA), sblk, rowblk(3 * QKV_A)],
        out_shape=[jax.ShapeDtypeStruct((nb, HA * DVA), f32),
                   jax.ShapeDtypeStruct(s0.shape, f32),
                   jax.ShapeDtypeStruct((nb, 3 * QKV_A), f32)],
        scratch_shapes=[pltpu.VMEM((bt, HA * DKA), f32)] * 3 + [pltpu.VMEM((bt, HA), f32)] * 2,
        compiler_params=_cp("parallel"),
    )(qkv, gate, ab, conv_buf.reshape(nb, 3 * QKV_A), s0, conv_w, al, db, nw)


def _gdn_sample(x, conv_buf, s0, w_in, conv_w, a_log, dt_bias, norm_w):
    nb = x.shape[0]
    wq = w_in[:, :QKV_A].astype(bf16)
    wg = w_in[:, QKV_A:QKV_A + HA * DVA].astype(bf16)
    wab = w_in[:, QKV_A + HA * DVA:].astype(bf16)
    qkv, gate, ab = _proj(x, [wq, wg, wab], nb)
    y, s, buf = _gdn_step(qkv, gate, ab, conv_buf, s0, conv_w, a_log, dt_bias, norm_w)
    return y, s, buf.reshape(nb, CONV - 1, QKV_A)


LRU_TILE = 256


def _gelu(x):
    return 0.5 * x * (1.0 + jnp.tanh(math.sqrt(2.0 / math.pi) * (x + 0.044715 * x * x * x)))


def _lru_gates(u, wa_ref, ba, wx_ref, bx, lam):
    ra, xa = [], []
    for n in range(LRU_BLOCKS):
        ub = u[:, n * LRU_BW:(n + 1) * LRU_BW].astype(bf16)
        ra.append(jnp.dot(ub, wa_ref[n], preferred_element_type=f32))
        xa.append(jnp.dot(ub, wx_ref[n], preferred_element_type=f32))
    r = _sigmoid(jnp.concatenate(ra, axis=-1) + ba)
    i_g = _sigmoid(jnp.concatenate(xa, axis=-1) + bx)
    log_a = -LRU_C * r * _softplus(-lam)
    a = jnp.exp(log_a)
    t = jnp.tanh(log_a)
    b = jnp.sqrt(-2.0 * t / (1.0 - t)) * (i_g * u)
    return a, b


def _lru_prompt_body(gate_ref, rec_ref, halo_ref, cw_ref, cb_ref, wa_ref, ba_ref, wx_ref, bx_ref, lam_ref,
                     y_ref, h_ref, xp_scr):
    i = pl.program_id(1)
    tt = rec_ref.shape[1]

    @pl.when(i == 0)
    def _():
        h_ref[...] = jnp.zeros_like(h_ref)

    xp_scr[0:8, :] = jnp.where(i > 0, halo_ref[0], 0.0)
    xp_scr[8:8 + tt, :] = rec_ref[0]
    cw = cw_ref[...]
    u = xp_scr[5:5 + tt, :] * cw[0:1, :]
    for j in range(1, CONV):
        u = u + xp_scr[5 + j:5 + j + tt, :] * cw[j:j + 1, :]
    u = u + cb_ref[...]
    a, b = _lru_gates(u, wa_ref, ba_ref[...], wx_ref, bx_ref[...], lam_ref[...])
    row = lax.broadcasted_iota(jnp.int32, a.shape, 0)
    s = 1
    while s < tt:
        keep = row >= s
        a_sh = jnp.where(keep, pltpu.roll(a, s, 0), 1.0)
        b_sh = jnp.where(keep, pltpu.roll(b, s, 0), 0.0)
        b = a * b_sh + b
        a = a * a_sh
        s *= 2
    h = b + a * h_ref[0]
    h_ref[0] = h[tt - 1:tt, :]
    y_ref[0] = _gelu(gate_ref[0]) * h


def _lru_prompt(x, w_in, conv_w, conv_b, w_a, b_a, w_x, b_x, lam):
    bsz, l, d = x.shape
    w = LRU_W
    gate_in, rec_in = _proj(x.reshape(bsz * l, d), [w_in[:, :w].astype(bf16), w_in[:, w:].astype(bf16)], 256)
    rec3 = rec_in.reshape(bsz, l, w)
    tt = min(LRU_TILE, l)
    hb = tt // 8
    big = pl.BlockSpec((1, tt, w), lambda bi, i: (bi, i, 0))
    vec = pl.BlockSpec((1, w), lambda bi, i: (0, 0))
    blkw = pl.BlockSpec((LRU_BLOCKS, LRU_BW, LRU_BW), lambda bi, i: (0, 0, 0))
    y, h = pl.pallas_call(
        _lru_prompt_body,
        grid=(bsz, l // tt),
        in_specs=[big, big, pl.BlockSpec((1, 8, w), lambda bi, i: (bi, jnp.maximum(i * hb - 1, 0), 0)),
                  pl.BlockSpec((CONV, w), lambda bi, i: (0, 0)), vec, blkw, vec, blkw, vec, vec],
        out_specs=[big, pl.BlockSpec((1, 1, w), lambda bi, i: (bi, 0, 0))],
        out_shape=[jax.ShapeDtypeStruct((bsz, l, w), f32), jax.ShapeDtypeStruct((bsz, 1, w), f32)],
        scratch_shapes=[pltpu.VMEM((tt + 8, w), f32)],
        compiler_params=_cp("parallel", "arbitrary"),
    )(gate_in.reshape(bsz, l, w), rec3, rec3, conv_w, conv_b.reshape(1, w), w_a.astype(bf16), b_a.reshape(1, w),
      w_x.astype(bf16), b_x.reshape(1, w), lam.reshape(1, w))
    return y.reshape(bsz * l, w), h.reshape(bsz, w), rec3[:, l - (CONV - 1):]


def _lru_step_body(gate_ref, rec_ref, buf_ref, h0_ref, cw_ref, cb_ref, wa_ref, ba_ref, wx_ref, bx_ref, lam_ref,
                   y_ref, h_ref, buf_out_ref):
    w = LRU_W
    new = rec_ref[...]
    cw = cw_ref[...]
    u = new * cw[CONV - 1:CONV, :]
    for j in range(CONV - 1):
        u = u + buf_ref[:, j * w:(j + 1) * w] * cw[j:j + 1, :]
    u = u + cb_ref[...]
    buf_out_ref[:, 0:2 * w] = buf_ref[:, w:3 * w]
    buf_out_ref[:, 2 * w:3 * w] = new
    a, b = _lru_gates(u, wa_ref, ba_ref[...], wx_ref, bx_ref[...], lam_ref[...])
    h = b + a * h0_ref[...]
    h_ref[...] = h
    y_ref[...] = _gelu(gate_ref[...]) * h


def _lru_sample(x, conv_buf, h0, w_in, conv_w, conv_b, w_a, b_a, w_x, b_x, lam):
    nb = x.shape[0]
    w = LRU_W
    gate_in, rec_in = _proj(x, [w_in[:, :w].astype(bf16), w_in[:, w:].astype(bf16)], nb)
    y, h, buf = pl.pallas_call(
        _lru_step_body,
        out_shape=[jax.ShapeDtypeStruct((nb, w), f32), jax.ShapeDtypeStruct((nb, w), f32),
                   jax.ShapeDtypeStruct((nb, 3 * w), f32)],
        compiler_params=pltpu.CompilerParams(vmem_limit_bytes=VMEM_LIMIT),
    )(gate_in, rec_in, conv_buf.reshape(nb, 3 * w), h0, conv_w, conv_b.reshape(1, w), w_a.astype(bf16),
      b_a.reshape(1, w), w_x.astype(bf16), b_x.reshape(1, w), lam.reshape(1, w))
    return y, h, buf.reshape(nb, CONV - 1, w)


NEG = -1e30


def _t5_bucket(rel):
    n = jnp.maximum(rel, 0)
    max_exact = N_BUCKETS // 2
    large = max_exact + (jnp.log(jnp.maximum(n, 1).astype(f32) / max_exact)
                         / math.log(MAX_DIST / max_exact) * (N_BUCKETS - max_exact)).astype(jnp.int32)
    return jnp.where(n < max_exact, n, jnp.minimum(large, N_BUCKETS - 1))


def _swa_prompt_body(q_ref, kp_ref, kc_ref, vp_ref, vc_ref, bucket_ref, sinks_ref, rb_ref, o_ref, bias_scr):
    i = pl.program_id(1)
    w = WINDOW

    @pl.when(i == 0)
    def _():
        bucket = bucket_ref[...]
        for h in range(HQ):
            b = jnp.zeros((w, 2 * w), f32)
            for n in range(N_BUCKETS):
                b = jnp.where(bucket == n, rb_ref[n, h], b)
            bias_scr[h] = b

    ii = lax.broadcasted_iota(jnp.int32, (w, 2 * w), 0)
    jj = lax.broadcasted_iota(jnp.int32, (w, 2 * w), 1)
    rel = ii + w - jj
    valid = (rel >= 0) & (rel <= w) & ((i > 0) | (jj >= w))
    q = q_ref[0].astype(bf16)
    kp = kp_ref[0].astype(bf16)
    kc = kc_ref[0].astype(bf16)
    vp = vp_ref[0].astype(bf16)
    vc = vc_ref[0].astype(bf16)
    for h in range(HQ):
        kv = slice((h // GQ) * HD, (h // GQ + 1) * HD)
        qh = q[:, h * HD:(h + 1) * HD]
        s = jnp.concatenate([_dot_nt(qh, kp[:, kv]), _dot_nt(qh, kc[:, kv])], axis=-1) * (HD ** -0.5)
        s = jnp.where(valid, s + bias_scr[h], NEG)
        sink = sinks_ref[h]
        m = jnp.maximum(jnp.max(s, -1, keepdims=True), sink)
        e = jnp.exp(s - m)
        denom = jnp.sum(e, -1, keepdims=True) + jnp.exp(sink - m)
        p = (e / denom).astype(bf16)
        o_ref[0, :, h * HD:(h + 1) * HD] = (jnp.dot(p[:, :w], vp[:, kv], preferred_element_type=f32)
                                            + jnp.dot(p[:, w:], vc[:, kv], preferred_element_type=f32))


def _swa_prompt(x, w_in, sinks, rel_bias):
    b, l, d = x.shape
    nq = HQ * HD
    nk = HKV * HD
    q, k, v = _proj(x.reshape(b * l, d), [w_in[:, :nq].astype(bf16), w_in[:, nq:nq + nk].astype(bf16),
                                          w_in[:, nq + nk:].astype(bf16)], 256)
    q = q.reshape(b, l, nq)
    k = k.reshape(b, l, nk)
    v = v.reshape(b, l, nk)
    w = WINDOW
    rel = jnp.arange(w)[:, None] + w - jnp.arange(2 * w)[None, :]
    bucket = _t5_bucket(rel).astype(jnp.int32)
    cur = lambda n: pl.BlockSpec((1, w, n), lambda bi, i: (bi, i, 0))
    prev = lambda n: pl.BlockSpec((1, w, n), lambda bi, i: (bi, jnp.maximum(i - 1, 0), 0))
    smem = pl.BlockSpec(memory_space=pltpu.SMEM)
    o = pl.pallas_call(
        _swa_prompt_body,
        grid=(b, l // w),
        in_specs=[cur(nq), prev(nk), cur(nk), prev(nk), cur(nk),
                  pl.BlockSpec((w, 2 * w), lambda bi, i: (0, 0)), smem, smem],
        out_specs=cur(nq),
        out_shape=jax.ShapeDtypeStruct((b, l, nq), f32),
        scratch_shapes=[pltpu.VMEM((HQ, w, 2 * w), f32)],
        compiler_params=_cp("parallel", "arbitrary"),
    )(q, k, k, v, v, bucket, sinks, rel_bias)
    return o.reshape(b * l, nq), k, v


SWA_BT = 8


def _head_mask():
    r = lax.broadcasted_iota(jnp.int32, (HQ, HKV * HD), 0)
    c = lax.broadcasted_iota(jnp.int32, (HQ, HKV * HD), 1)
    return (r // GQ) == (c // HD)


def _fold_heads(o, mask):
    o = jnp.where(mask, o, 0.0)
    acc = o[:, 0:HD]
    for c in range(1, HKV):
        acc = acc + o[:, c * HD:(c + 1) * HD]
    return acc


def _swa_step_body(q_ref, kn_ref, vn_ref, kb_ref, vb_ref, onehot_ref, rbt_ref, sinks_ref, o_ref, kb_out, vb_out):
    w = WINDOW
    mask = _head_mask()
    bias_all = _dot_hi(rbt_ref[...], onehot_ref[...])
    bias = bias_all[:, :w]
    bias_new = bias_all[:, w:w + 1]
    sink = sinks_ref[...]
    for bi in range(q_ref.shape[0]):
        qm = jnp.where(mask, jnp.concatenate([q_ref[bi]] * HKV, axis=-1), 0.0)
        kn = kn_ref[bi]
        vn = vn_ref[bi]
        s = _dot_nt(qm, kb_ref[bi]) * (HD ** -0.5) + bias
        s_new = jnp.sum(qm * kn, -1, keepdims=True) * (HD ** -0.5) + bias_new
        m = jnp.maximum(jnp.maximum(jnp.max(s, -1, keepdims=True), s_new), sink)
        e = jnp.exp(s - m)
        e_new = jnp.exp(s_new - m)
        denom = jnp.sum(e, -1, keepdims=True) + e_new + jnp.exp(sink - m)
        o = (_dot(e / denom, vb_ref[bi]) + (e_new / denom) * vn)
        o_ref[bi] = _fold_heads(o, mask)
        kb_out[bi, 0:w - 1, :] = kb_ref[bi, 1:w, :]
        kb_out[bi, w - 1:w, :] = kn
        vb_out[bi, 0:w - 1, :] = vb_ref[bi, 1:w, :]
        vb_out[bi, w - 1:w, :] = vn


def _swa_sample(x, k_buf, v_buf, w_in, sinks, rel_bias):
    nb = x.shape[0]
    nq = HQ * HD
    nk = HKV * HD
    w = WINDOW
    q, k, v = _proj(x, [w_in[:, :nq].astype(bf16), w_in[:, nq:nq + nk].astype(bf16), w_in[:, nq + nk:].astype(bf16)], nb)
    slots = jnp.arange(w + 128)
    bucket = _t5_bucket(jnp.where(slots <= w, w - slots, 0))
    onehot = (bucket[None, :] == jnp.arange(N_BUCKETS)[:, None]).astype(f32)
    bt = SWA_BT
    full = lambda a: pl.BlockSpec(a.shape, lambda i: (0,) * a.ndim)
    bufblk = pl.BlockSpec((bt, w, nk), lambda i: (i, 0, 0))
    rbt = rel_bias.T
    sk = sinks.reshape(HQ, 1)
    o, kb, vb = pl.pallas_call(
        _swa_step_body,
        grid=(nb // bt,),
        in_specs=[pl.BlockSpec((bt, HQ, HD), lambda i: (i, 0, 0)),
                  pl.BlockSpec((bt, 1, nk), lambda i: (i, 0, 0)), pl.BlockSpec((bt, 1, nk), lambda i: (i, 0, 0)),
                  bufblk, bufblk, full(onehot), full(rbt), full(sk)],
        out_specs=[pl.BlockSpec((bt, HQ, HD), lambda i: (i, 0, 0)), bufblk, bufblk],
        out_shape=[jax.ShapeDtypeStruct((nb, HQ, HD), f32), jax.ShapeDtypeStruct((nb, w, nk), f32),
                   jax.ShapeDtypeStruct((nb, w, nk), f32)],
        compiler_params=_cp("parallel"),
    )(q.reshape(nb, HQ, HD), k.reshape(nb, 1, nk), v.reshape(nb, 1, nk), k_buf.reshape(nb, w, nk),
      v_buf.reshape(nb, w, nk), onehot, rbt, sk)
    return o.reshape(nb, nq), k, v, kb.reshape(nb, w, HKV, HD), vb.reshape(nb, w, HKV, HD)


FOX_PREP_TILE = 512
FOX_TQ = 256
FOX_TK = 256


def _fox_prep_body(x_ref, wft_ref, bf_ref, lf_ref, cum_ref, carry):
    i = pl.program_id(1)
    tt = x_ref.shape[1]

    @pl.when(i == 0)
    def _():
        carry[...] = jnp.zeros_like(carry)

    f = lax.dot_general(wft_ref[...], x_ref[0].astype(bf16), (((1,), (1,)), ((), ())), preferred_element_type=f32)
    lf = _log_sigmoid(f + bf_ref[...])
    r = lax.broadcasted_iota(jnp.int32, (tt, tt), 0)
    c = lax.broadcasted_iota(jnp.int32, (tt, tt), 1)
    cum = _dot_hi(lf, (r <= c).astype(f32)) + carry[...]
    lf_ref[0] = lf
    cum_ref[0] = cum
    carry[...] = cum[:, tt - 1:tt]


def _fox_flash_body(q_ref, k_ref, v_ref, fq_ref, fk_ref, o_ref, m_scr, l_scr, acc_scr):
    i = pl.program_id(2)
    tq = q_ref.shape[1]
    tk = FOX_TK
    q = (q_ref[0] * (HD ** -0.5)).astype(bf16)
    fq = fq_ref[0, 0]
    m_scr[...] = jnp.full_like(m_scr, NEG)
    l_scr[...] = jnp.zeros_like(l_scr)
    acc_scr[...] = jnp.zeros_like(acc_scr)
    t_pos = i * tq + lax.broadcasted_iota(jnp.int32, (tq, tk), 0)

    def kv_step(j, carry):
        c0 = pl.multiple_of(j * tk, tk)
        k = k_ref[0, 0, pl.ds(c0, tk), :]
        v = v_ref[0, 0, pl.ds(c0, tk), :]
        fk = fk_ref[0, 0, :, pl.ds(c0, tk)]
        visible = (c0 + lax.broadcasted_iota(jnp.int32, (tq, tk), 1)) <= t_pos
        for g in range(GQ):
            s = lax.dot_general(q[:, g * HD:(g + 1) * HD], k, (((1,), (1,)), ((), ())), preferred_element_type=f32)
            s = jnp.where(visible, s + fq[:, g:g + 1] - fk[g:g + 1, :], NEG)
            m_old = m_scr[g]
            m_new = jnp.maximum(m_old, jnp.max(s, -1, keepdims=True))
            alpha = jnp.exp(m_old - m_new)
            p = jnp.exp(s - m_new)
            l_scr[g] = alpha * l_scr[g] + jnp.sum(p, -1, keepdims=True)
            acc_scr[g] = alpha * acc_scr[g] + jnp.dot(p.astype(bf16), v, preferred_element_type=f32)
            m_scr[g] = m_new
        return carry

    lax.fori_loop(0, (i * tq) // tk + tq // tk, kv_step, 0)
    for g in range(GQ):
        o_ref[0, :, g * HD:(g + 1) * HD] = acc_scr[g] / l_scr[g]


def _fox_prompt(x, w_in, b_f):
    b, l, d = x.shape
    nq = HQ * HD
    nk = HKV * HD
    q, k, v = _proj(x.reshape(b * l, d), [w_in[:, :nq].astype(bf16), w_in[:, nq:nq + nk].astype(bf16),
                                          w_in[:, nq + nk:nq + 2 * nk].astype(bf16)], 256)
    tt = min(FOX_PREP_TILE, l)
    wft = w_in[:, nq + 2 * nk:].T.astype(bf16)
    row = pl.BlockSpec((1, HQ, tt), lambda bi, i: (bi, 0, i))
    lft, cumt = pl.pallas_call(
        _fox_prep_body,
        grid=(b, l // tt),
        in_specs=[pl.BlockSpec((1, tt, d), lambda bi, i: (bi, i, 0)),
                  pl.BlockSpec((HQ, d), lambda bi, i: (0, 0)), pl.BlockSpec((HQ, 1), lambda bi, i: (0, 0))],
        out_specs=[row, row],
        out_shape=[jax.ShapeDtypeStruct((b, HQ, l), f32)] * 2,
        scratch_shapes=[pltpu.VMEM((HQ, 1), f32)],
        compiler_params=_cp("parallel", "arbitrary"),
    )(x, wft, b_f.reshape(HQ, 1))
    fk = cumt.reshape(b, HKV, GQ, l)
    fq = jnp.swapaxes(fk, 2, 3)
    kh = jnp.swapaxes(k.reshape(b, l, HKV, HD), 1, 2).astype(bf16)
    vh = jnp.swapaxes(v.reshape(b, l, HKV, HD), 1, 2).astype(bf16)
    tq = min(FOX_TQ, l)
    gw = GQ * HD
    seq = pl.BlockSpec((1, 1, l, HD), lambda bi, h, i: (bi, h, 0, 0))
    o = pl.pallas_call(
        _fox_flash_body,
        grid=(b, HKV, l // tq),
        in_specs=[pl.BlockSpec((1, tq, gw), lambda bi, h, i: (bi, i, h)), seq, seq,
                  pl.BlockSpec((1, 1, tq, GQ), lambda bi, h, i: (bi, h, i, 0)),
                  pl.BlockSpec((1, 1, GQ, l), lambda bi, h, i: (bi, h, 0, 0))],
        out_specs=pl.BlockSpec((1, tq, gw), lambda bi, h, i: (bi, i, h)),
        out_shape=jax.ShapeDtypeStruct((b, l, nq), f32),
        scratch_shapes=[pltpu.VMEM((GQ, tq, 1), f32), pltpu.VMEM((GQ, tq, 1), f32), pltpu.VMEM((GQ, tq, HD), f32)],
        compiler_params=_cp("parallel", "parallel", "arbitrary"),
    )(q.reshape(b, l, nq), kh, vh, fq, fk)
    return o.reshape(b * l, nq), k.reshape(b, l, nk), v.reshape(b, l, nk), jnp.swapaxes(lft, 1, 2)


FOX_PAGES = 8


def _fox_step_body(pt_ref, q_ref, kn_ref, vn_ref, lfn_ref, *refs):
    npg = FOX_PAGES
    k_refs = refs[0:npg]
    v_refs = refs[npg:2 * npg]
    lf_refs = refs[2 * npg:3 * npg]
    o_ref, m_scr, l_scr, acc_scr, f_scr = refs[3 * npg:]
    j = pl.program_id(1)
    ps = PAGE_SIZE
    mask = _head_mask()
    qm = jnp.where(mask, jnp.concatenate([q_ref[0]] * HKV, axis=-1), 0.0) * (HD ** -0.5)

    @pl.when(j == 0)
    def _():
        m_scr[...] = jnp.full_like(m_scr, NEG)
        l_scr[...] = jnp.zeros_like(l_scr)
        acc_scr[...] = jnp.zeros_like(acc_scr)
        f_scr[...] = jnp.zeros_like(f_scr)

    eye = _eye(HQ)
    r = lax.broadcasted_iota(jnp.int32, (ps, ps), 0)
    c = lax.broadcasted_iota(jnp.int32, (ps, ps), 1)
    upper = (r <= c).astype(f32)

    def update(s, v_rows, f_end):
        m_old = m_scr[...]
        m_new = jnp.maximum(m_old, jnp.max(s, -1, keepdims=True))
        alpha = jnp.exp(m_old - m_new)
        p = jnp.exp(s - m_new)
        l_scr[...] = alpha * l_scr[...] + jnp.sum(p, -1, keepdims=True)
        acc_scr[...] = alpha * acc_scr[...] + v_rows(p)
        m_scr[...] = m_new
        f_scr[...] = f_end

    for pg in range(npg):
        lft = _dot_nt_hi(eye, lf_refs[pg][0])
        cum = _dot_hi(lft, upper) + f_scr[...]
        s = _dot_nt(qm, k_refs[pg][0]) - cum
        update(s, lambda p, pg=pg: _dot(p, v_refs[pg][0]), cum[:, ps - 1:ps])

    @pl.when(j == pl.num_programs(1) - 1)
    def _():
        kn = kn_ref[0]
        vn = vn_ref[0]
        cum = f_scr[...] + lfn_ref[0]
        s = jnp.sum(qm * kn, -1, keepdims=True) - cum
        update(s, lambda p: p * vn, cum)
        o_ref[0] = _fold_heads(acc_scr[...] / l_scr[...], mask)


def _logf_body(f_ref, b_ref, o_ref):
    o_ref[...] = _log_sigmoid(f_ref[...] + b_ref[...])


def _fox_sample(x, cache_k, cache_v, cache_logf, page_table, w_in, b_f):
    nb = x.shape[0]
    nq = HQ * HD
    nk = HKV * HD
    n_pool = cache_k.shape[0]
    n_pages = page_table.shape[1]
    npg = FOX_PAGES
    q, k, v, f = _proj(x, [w_in[:, :nq].astype(bf16), w_in[:, nq:nq + nk].astype(bf16),
                           w_in[:, nq + nk:nq + 2 * nk].astype(bf16), w_in[:, nq + 2 * nk:].astype(bf16)], nb)
    lf_new = pl.pallas_call(_logf_body, out_shape=jax.ShapeDtypeStruct((nb, HQ), f32))(f, b_f.reshape(1, HQ))

    def page(width, pg):
        return pl.BlockSpec((1, PAGE_SIZE, width), lambda bi, j, pt: (pt[bi * n_pages + j * npg + pg], 0, 0))

    tok = lambda *shape: pl.BlockSpec((1,) + shape, lambda bi, j, pt: (bi,) + (0,) * len(shape))
    ck = cache_k.reshape(n_pool, PAGE_SIZE, nk)
    cv = cache_v.reshape(n_pool, PAGE_SIZE, nk)
    o = pl.pallas_call(
        _fox_step_body,
        grid_spec=pltpu.PrefetchScalarGridSpec(
            num_scalar_prefetch=1,
            grid=(nb, n_pages // npg),
            in_specs=[tok(HQ, HD), tok(1, nk), tok(1, nk), tok(HQ, 1)]
            + [page(nk, pg) for pg in range(npg)] + [page(nk, pg) for pg in range(npg)]
            + [page(HQ, pg) for pg in range(npg)],
            out_specs=tok(HQ, HD),
            scratch_shapes=[pltpu.VMEM((HQ, 1), f32), pltpu.VMEM((HQ, 1), f32), pltpu.VMEM((HQ, nk), f32),
                            pltpu.VMEM((HQ, 1), f32)]),
        out_shape=jax.ShapeDtypeStruct((nb, HQ, HD), f32),
        compiler_params=_cp("parallel", "arbitrary"),
    )(page_table.reshape(-1), q.reshape(nb, HQ, HD), k.reshape(nb, 1, nk), v.reshape(nb, 1, nk),
      lf_new.reshape(nb, HQ, 1), *([ck] * npg), *([cv] * npg), *([cache_logf] * npg))
    return o.reshape(nb, nq), k, v, lf_new


ROUTER_TILE = 512
MOE_TM = 1024


def _router_body(x_ref, rwt_ref, rb_ref, comb_ref):
    scores = _sigmoid(_dot_nt(rwt_ref[...], x_ref[...]))
    sel = scores + rb_ref[...]
    rows = [sel[e:e + 1, :] for e in range(N_EXPERTS)]
    srow = [scores[e:e + 1, :] for e in range(N_EXPERTS)]
    gs = []
    for g in range(N_GROUPS):
        r = rows[g * EPG:(g + 1) * EPG]
        best = None
        for a in range(EPG):
            for b in range(a + 1, EPG):
                pair = r[a] + r[b]
                best = pair if best is None else jnp.maximum(best, pair)
        gs.append(best)
    g_best = gs[0]
    g_idx = jnp.zeros_like(gs[0], dtype=jnp.int32)
    for g in range(1, N_GROUPS):
        better = gs[g] > g_best
        g_best = jnp.where(better, gs[g], g_best)
        g_idx = jnp.where(better, g, g_idx)

    def in_group(vals, j):
        out = vals[j]
        for g in range(1, N_GROUPS):
            out = jnp.where(g_idx == g, vals[g * EPG + j], out)
        return out

    ig = [in_group(rows, j) for j in range(EPG)]
    sg = [in_group(srow, j) for j in range(EPG)]
    v1, i1, s1 = ig[0], jnp.zeros_like(g_idx), sg[0]
    for j in range(1, EPG):
        better = ig[j] > v1
        v1 = jnp.where(better, ig[j], v1)
        i1 = jnp.where(better, j, i1)
        s1 = jnp.where(better, sg[j], s1)
    v2 = jnp.full_like(v1, -jnp.inf)
    i2 = jnp.zeros_like(g_idx)
    s2 = jnp.zeros_like(s1)
    for j in range(EPG):
        better = (i1 != j) & (ig[j] > v2)
        v2 = jnp.where(better, ig[j], v2)
        i2 = jnp.where(better, j, i2)
        s2 = jnp.where(better, sg[j], s2)
    tot = s1 + s2
    e_row = lax.broadcasted_iota(jnp.int32, scores.shape, 0)
    comb_ref[...] = (jnp.where(e_row == g_idx * EPG + i1, s1 / tot, 0.0)
                     + jnp.where(e_row == g_idx * EPG + i2, s2 / tot, 0.0))


def _router(x, router_w, router_b):
    t, d = x.shape
    tt = min(ROUTER_TILE, t)
    comb_t = pl.pallas_call(
        _router_body,
        grid=(t // tt,),
        in_specs=[pl.BlockSpec((tt, d), lambda i: (i, 0)), pl.BlockSpec((N_EXPERTS, d), lambda i: (0, 0)),
                  pl.BlockSpec((N_EXPERTS, 1), lambda i: (0, 0))],
        out_specs=pl.BlockSpec((N_EXPERTS, tt), lambda i: (0, i)),
        out_shape=jax.ShapeDtypeStruct((N_EXPERTS, t), f32),
        compiler_params=_cp("parallel"),
    )(x, router_w.T, router_b.reshape(N_EXPERTS, 1))
    return comb_t.T


def _moe_dense_body(x_ref, comb_ref, wg_ref, wu_ref, wd_ref, g_ref, b_ref, o_ref, acc_scr):
    e = pl.program_id(1)

    @pl.when(e == 0)
    def _():
        acc_scr[...] = jnp.zeros_like(acc_scr)

    x = x_ref[...].astype(bf16)
    h = _silu(jnp.dot(x, wg_ref[0].astype(bf16), preferred_element_type=f32)) \
        * jnp.dot(x, wu_ref[0].astype(bf16), preferred_element_type=f32)
    y = jnp.dot(h.astype(bf16), wd_ref[0].astype(bf16), preferred_element_type=f32)
    lane = lax.broadcasted_iota(jnp.int32, comb_ref.shape, 1)
    c = jnp.sum(jnp.where(lane == e, comb_ref[...], 0.0), -1, keepdims=True)
    acc_scr[...] += c * y

    @pl.when(e == pl.num_programs(1) - 1)
    def _():
        o_ref[...] = _layer_norm(DN_ALPHA * x_ref[...] + acc_scr[...], g_ref[...], b_ref[...])


def _moe_ln(x, router_w, router_b, w_gate, w_up, w_down, g, b):
    t, d = x.shape
    comb = _router(x, router_w, router_b)
    tm = min(MOE_TM, t)
    return pl.pallas_call(
        _moe_dense_body,
        grid=(t // tm, N_EXPERTS),
        in_specs=[pl.BlockSpec((tm, d), lambda i, e: (i, 0)), pl.BlockSpec((tm, N_EXPERTS), lambda i, e: (i, 0)),
                  pl.BlockSpec((1, d, D_EXPERT), lambda i, e: (e, 0, 0)),
                  pl.BlockSpec((1, d, D_EXPERT), lambda i, e: (e, 0, 0)),
                  pl.BlockSpec((1, D_EXPERT, d), lambda i, e: (e, 0, 0)),
                  pl.BlockSpec((1, d), lambda i, e: (0, 0)), pl.BlockSpec((1, d), lambda i, e: (0, 0))],
        out_specs=pl.BlockSpec((tm, d), lambda i, e: (i, 0)),
        out_shape=jax.ShapeDtypeStruct((t, d), f32),
        scratch_shapes=[pltpu.VMEM((tm, d), f32)],
        compiler_params=_cp("parallel", "arbitrary"),
    )(x, comb, w_gate, w_up, w_down, g.reshape(1, d), b.reshape(1, d))


def kernel(x_prompt, x_sample, state_a_ssm, state_a_conv, state_b_h, state_b_conv, cache_c_k, cache_c_v, cache_d_k, cache_d_v, cache_d_logf, page_table, ln_g, ln_b, a_w_in, a_conv_w, a_a_log, a_dt_bias, a_norm_w, a_w_out, b_w_in, b_conv_w, b_conv_b, b_w_a, b_b_a, b_w_x, b_b_x, b_lambda, b_w_out, c_w_in, c_sinks, c_w_out, rel_bias, d_w_in, d_b_f, d_w_out, router_w, router_b, moe_w_gate, moe_w_up, moe_w_down):
    bp, lp, d = x_prompt.shape
    nb = x_sample.shape[0]
    xp = x_prompt.reshape(bp * lp, d)
    xs = x_sample.reshape(nb, d)
    tm_p = 256

    def finish(xp, xs, mp, ms, w_out, i):
        w = w_out.astype(bf16)
        xp = _proj_ln(mp, w, xp, ln_g[i, 0], ln_b[i, 0], tm_p)
        xs = _proj_ln(ms, w, xs, ln_g[i, 0], ln_b[i, 0], nb)
        moe_args = (router_w, router_b, moe_w_gate[i], moe_w_up[i], moe_w_down[i], ln_g[i, 1], ln_b[i, 1])
        return _moe_ln(xp, *moe_args), _moe_ln(xs, *moe_args)

    a_args = (a_w_in[0], a_conv_w[0], a_a_log[0], a_dt_bias[0], a_norm_w[0])
    mp, p_a_ssm, p_a_conv = _gdn_prompt(xp.reshape(bp, lp, d), *a_args)
    ms, s_a_ssm, s_a_conv = _gdn_sample(xs, state_a_conv[0], state_a_ssm[0], *a_args)
    xp, xs = finish(xp, xs, mp, ms, a_w_out[0], 0)

    b_args = (b_w_in[0], b_conv_w[0], b_conv_b[0], b_w_a[0], b_b_a[0], b_w_x[0], b_b_x[0], b_lambda[0])
    mp, p_b_h, p_b_conv = _lru_prompt(xp.reshape(bp, lp, d), *b_args)
    ms, s_b_h, s_b_conv = _lru_sample(xs, state_b_conv[0], state_b_h[0], *b_args)
    xp, xs = finish(xp, xs, mp, ms, b_w_out[0], 1)

    mp, kp, vp = _swa_prompt(xp.reshape(bp, lp, d), c_w_in[0], c_sinks[0], rel_bias)
    ms, _, _, s_c_k, s_c_v = _swa_sample(xs, cache_c_k[0], cache_c_v[0], c_w_in[0], c_sinks[0], rel_bias)
    p_c_k = kp[:, lp - WINDOW:].reshape(bp, WINDOW, HKV, HD)
    p_c_v = vp[:, lp - WINDOW:].reshape(bp, WINDOW, HKV, HD)
    xp, xs = finish(xp, xs, mp, ms, c_w_out[0], 2)

    mp, kp, vp, p_d_logf = _fox_prompt(xp.reshape(bp, lp, d), d_w_in[0], d_b_f[0])
    ms, ks, vs, lfs = _fox_sample(xs, cache_d_k[0], cache_d_v[0], cache_d_logf[0], page_table, d_w_in[0], d_b_f[0])
    xp, xs = finish(xp, xs, mp, ms, d_w_out[0], 3)

    return (xp.reshape(bp, lp, d), xs.reshape(nb, 1, d),
            p_a_ssm[None], p_a_conv[None], p_b_h[None], p_b_conv[None], p_c_k[None], p_c_v[None],
            kp.reshape(1, bp, lp, HKV, HD), vp.reshape(1, bp, lp, HKV, HD), p_d_logf[None],
            s_a_ssm[None], s_a_conv[None], s_b_h[None], s_b_conv[None], s_c_k[None], s_c_v[None],
            ks.reshape(1, nb, 1, HKV, HD), vs.reshape(1, nb, 1, HKV, HD), lfs.reshape(1, nb, 1, HQ))
```

```python
import functools
import math

import jax
import jax.numpy as jnp
import numpy as np
from jax import lax
from jax.experimental import pallas as pl
from jax.experimental.pallas import tpu as pltpu

f32 = jnp.float32
bf16 = jnp.bfloat16
HI = lax.Precision.HIGHEST

D_MODEL = 1024
DEPTH = 4
DN_ALPHA = (2 * DEPTH) ** 0.25
LN_EPS = 1e-5
CONV = 4
HA = 8
DKA = 128
DVA = 128
QKV_A = HA * (2 * DKA + DVA)
CHUNK_A = 64
LRU_W = 1024
LRU_BLOCKS = 8
LRU_BW = 128
LRU_C = 8.0
HD = 64
HQ = 16
HKV = 4
GQ = HQ // HKV
WINDOW = 128
N_BUCKETS = 32
MAX_DIST = 128
PAGE_SIZE = 128
N_EXPERTS = 16
N_GROUPS = 4
EPG = 4
D_EXPERT = 512

VMEM_LIMIT = 56 * 1024 * 1024


def _cp(*sem):
    return pltpu.CompilerParams(dimension_semantics=sem, vmem_limit_bytes=VMEM_LIMIT)


def _dot(a, b):
    return jnp.dot(a.astype(bf16), b.astype(bf16), preferred_element_type=f32)


def _dot_nt(a, b):
    return lax.dot_general(a.astype(bf16), b.astype(bf16), (((1,), (1,)), ((), ())), preferred_element_type=f32)


def _dot_tn(a, b):
    return lax.dot_general(a.astype(bf16), b.astype(bf16), (((0,), (0,)), ((), ())), preferred_element_type=f32)


def _dot_hi(a, b):
    return jnp.dot(a, b, precision=HI, preferred_element_type=f32)


def _split(a):
    hi = a.astype(bf16)
    return hi, (a - hi.astype(f32)).astype(bf16)


def _dot_split(a, b):
    a_hi, a_lo = _split(a)
    b_hi, b_lo = _split(b)
    d = lambda u, v: jnp.dot(u, v, preferred_element_type=f32)
    return d(a_hi, b_hi) + (d(a_hi, b_lo) + d(a_lo, b_hi))


def _dot_nt_hi(a, b):
    return lax.dot_general(a, b, (((1,), (1,)), ((), ())), precision=HI, preferred_element_type=f32)


def _sigmoid(x):
    return 1.0 / (1.0 + jnp.exp(-x))


def _silu(x):
    return x * _sigmoid(x)


def _softplus(x):
    return jnp.maximum(x, 0.0) + jnp.log1p(jnp.exp(-jnp.abs(x)))


def _log_sigmoid(x):
    return -_softplus(-x)


def _eye(n):
    r = lax.broadcasted_iota(jnp.int32, (n, n), 0)
    c = lax.broadcasted_iota(jnp.int32, (n, n), 1)
    return (r == c).astype(f32)


def _proj_body(n_out, x_ref, *refs):
    x = x_ref[...].astype(bf16)
    for w_ref, o_ref in zip(refs[:n_out], refs[n_out:]):
        o_ref[...] = jnp.dot(x, w_ref[...], preferred_element_type=f32)


def _proj(x, ws, tm):
    m, k = x.shape
    n_out = len(ws)
    return pl.pallas_call(
        functools.partial(_proj_body, n_out),
        grid=(m // tm,),
        in_specs=[pl.BlockSpec((tm, k), lambda i: (i, 0))]
        + [pl.BlockSpec(w.shape, lambda i: (0, 0)) for w in ws],
        out_specs=[pl.BlockSpec((tm, w.shape[1]), lambda i: (i, 0)) for w in ws],
        out_shape=[jax.ShapeDtypeStruct((m, w.shape[1]), f32) for w in ws],
        compiler_params=_cp("parallel"),
        name="proj",
    )(x, *ws)


def _layer_norm(z, g, b):
    mu = jnp.mean(z, -1, keepdims=True)
    zc = z - mu
    var = jnp.mean(zc * zc, -1, keepdims=True)
    return zc * lax.rsqrt(var + LN_EPS) * g + b


def _proj_ln_body(h_ref, w_ref, x_ref, g_ref, b_ref, o_ref):
    m = jnp.dot(h_ref[...].astype(bf16), w_ref[...], preferred_element_type=f32)
    o_ref[...] = _layer_norm(DN_ALPHA * x_ref[...] + m, g_ref[...], b_ref[...])


def _proj_ln(h, w, x, g, b, tm):
    m, k = h.shape
    d = x.shape[1]
    return pl.pallas_call(
        _proj_ln_body,
        grid=(m // tm,),
        in_specs=[pl.BlockSpec((tm, k), lambda i: (i, 0)),
                  pl.BlockSpec((k, d), lambda i: (0, 0)),
                  pl.BlockSpec((tm, d), lambda i: (i, 0)),
                  pl.BlockSpec((1, d), lambda i: (0, 0)),
                  pl.BlockSpec((1, d), lambda i: (0, 0))],
        out_specs=pl.BlockSpec((tm, d), lambda i: (i, 0)),
        out_shape=jax.ShapeDtypeStruct((m, d), f32),
        compiler_params=_cp("parallel"),
        name="proj_ln",
    )(h, w, x, g.reshape(1, d), b.reshape(1, d))


GDN_TILE = 256


def _gdn_gates(ab, alog, dtb):
    g = -jnp.exp(alog) * _softplus(ab[:, 0:HA] + dtb)
    beta = _sigmoid(ab[:, HA:2 * HA])
    return g, beta


def _l2norm(t):
    return t * lax.rsqrt(jnp.sum(t * t, -1, keepdims=True) + 1e-6)


def _gdn_prep_body(qkv_ref, halo_ref, ab_ref, cw_ref, alog_ref, dtb_ref,
                   u_ref, w_ref, qd_ref, kd_ref, qk_ref, eg_ref, xp_scr, y_scr):
    i = pl.program_id(1)
    tc = qkv_ref.shape[1]
    c = CHUNK_A
    xp_scr[0:8, :] = jnp.where(i > 0, halo_ref[0], 0.0)
    xp_scr[8:8 + tc, :] = qkv_ref[0]
    cw = cw_ref[...]
    y = xp_scr[5:5 + tc, :] * cw[0:1, :]
    for j in range(1, CONV):
        y = y + xp_scr[5 + j:5 + j + tc, :] * cw[j:j + 1, :]
    y_scr[...] = _silu(y)

    row = lax.broadcasted_iota(jnp.int32, (c, c), 0)
    col = lax.broadcasted_iota(jnp.int32, (c, c), 1)
    incl = row >= col
    strict = row > col
    tri = incl.astype(f32)
    eye_c = (row == col).astype(f32)
    eye_h = _eye(HA)
    alog = alog_ref[...]
    dtb = dtb_ref[...]

    def chunk(n, carry):
        r0 = pl.multiple_of(n * c, c)
        rows = pl.ds(r0, c)
        g, beta = _gdn_gates(ab_ref[0, rows, :], alog, dtb)
        gc = _dot_hi(tri, g)
        gct = _dot_nt_hi(eye_h, gc)
        egc = jnp.exp(gc)
        eg_ref[0, rows, :] = egc
        for h in range(HA):
            q = _l2norm(y_scr[rows, h * DKA:(h + 1) * DKA]) * (DKA ** -0.5)
            k = _l2norm(y_scr[rows, HA * DKA + h * DKA:HA * DKA + (h + 1) * DKA])
            v = y_scr[rows, 2 * HA * DKA + h * DVA:2 * HA * DKA + (h + 1) * DVA]
            gcol = gc[:, h:h + 1]
            bcol = beta[:, h:h + 1]
            diff = gcol - gct[h:h + 1, :]
            dec = jnp.where(incl, jnp.exp(jnp.where(incl, diff, 0.0)), 0.0)
            a = jnp.where(strict, _dot_nt(k, k) * dec, 0.0) * bcol
            x = eye_c - a
            p = _dot_split(a, a)
            for _ in range(4):
                xp = _dot_split(jnp.concatenate([x, p], axis=0), p)
                x = x + xp[:c]
                p = xp[c:]
            x = x + _dot_split(x, p)
            rhs = jnp.concatenate([v * bcol, k * (bcol * egc[:, h:h + 1])], axis=-1)
            sol = _dot_split(x, rhs)
            hs = slice(h * DVA, (h + 1) * DVA)
            u_ref[0, rows, hs] = sol[:, :DVA]
            w_ref[0, rows, hs] = sol[:, DVA:]
            qd_ref[0, rows, hs] = q * egc[:, h:h + 1]
            kd_ref[0, rows, hs] = k * jnp.exp(gc[c - 1:c, h:h + 1] - gcol)
            qk_ref[0, rows, h * c:(h + 1) * c] = _dot_nt(q, k) * dec
        return carry

    lax.fori_loop(0, tc // c, chunk, 0)


def _gdn_prep(qkv, ab, conv_w, a_log, dt_bias):
    b, l, _ = qkv.shape
    tc = min(GDN_TILE, l)
    hb = tc // 8
    big = lambda n: pl.BlockSpec((1, tc, n), lambda bi, i: (bi, i, 0))
    full = lambda a: pl.BlockSpec(a.shape, lambda bi, i: (0, 0))
    wide = jax.ShapeDtypeStruct((b, l, HA * DVA), f32)
    return pl.pallas_call(
        _gdn_prep_body,
        grid=(b, l // tc),
        in_specs=[big(QKV_A),
                  pl.BlockSpec((1, 8, QKV_A), lambda bi, i: (bi, jnp.maximum(i * hb - 1, 0), 0)),
                  big(2 * HA), full(conv_w), full(a_log), full(dt_bias)],
        out_specs=[big(HA * DVA)] * 4 + [big(HA * CHUNK_A), big(HA)],
        out_shape=[wide] * 4 + [jax.ShapeDtypeStruct((b, l, HA * CHUNK_A), f32),
                                jax.ShapeDtypeStruct((b, l, HA), f32)],
        scratch_shapes=[pltpu.VMEM((tc + 8, QKV_A), f32), pltpu.VMEM((tc, QKV_A), f32)],
        compiler_params=_cp("parallel", "parallel"),
        name="gdn_prep",
    )(qkv, qkv, ab, conv_w, a_log, dt_bias)


def _gdn_scan_body(u_ref, w_ref, qd_ref, kd_ref, qk_ref, eg_ref, gate_ref, nw_ref, y_ref, s_ref):
    i = pl.program_id(1)
    tc = u_ref.shape[1]
    c = CHUNK_A

    @pl.when(i == 0)
    def _():
        s_ref[...] = jnp.zeros_like(s_ref)

    nw = nw_ref[...]

    def chunk(n, carry):
        r0 = pl.multiple_of(n * c, c)
        rows = pl.ds(r0, c)
        eg_last = eg_ref[0, pl.ds(r0 + c - 1, 1), :]
        for h in range(HA):
            hs = slice(h * DVA, (h + 1) * DVA)
            s = s_ref[0, h]
            v_new = u_ref[0, rows, hs] - _dot(w_ref[0, rows, hs], s)
            o = _dot(qd_ref[0, rows, hs], s) + _dot(qk_ref[0, rows, h * c:(h + 1) * c], v_new)
            s_ref[0, h] = s * eg_last[:, h:h + 1] + _dot_tn(kd_ref[0, rows, hs], v_new)
            of = o * lax.rsqrt(jnp.mean(o * o, -1, keepdims=True) + 1e-6)
            y_ref[0, rows, hs] = of * nw * _silu(gate_ref[0, rows, hs])
        return carry

    lax.fori_loop(0, tc // c, chunk, 0)


def _gdn_scan(u, w, qd, kd, qk, eg, gate, norm_w):
    b, l, _ = u.shape
    tc = min(GDN_TILE, l)
    big = lambda n: pl.BlockSpec((1, tc, n), lambda bi, i: (bi, i, 0))
    return pl.pallas_call(
        _gdn_scan_body,
        grid=(b, l // tc),
        in_specs=[big(HA * DVA)] * 4 + [big(HA * CHUNK_A), big(HA), big(HA * DVA),
                                       pl.BlockSpec((1, DVA), lambda bi, i: (0, 0))],
        out_specs=[big(HA * DVA), pl.BlockSpec((1, HA, DKA, DVA), lambda bi, i: (bi, 0, 0, 0))],
        out_shape=[jax.ShapeDtypeStruct((b, l, HA * DVA), f32),
                   jax.ShapeDtypeStruct((b, HA, DKA, DVA), f32)],
        compiler_params=_cp("parallel", "arbitrary"),
        name="gdn_scan",
    )(u, w, qd, kd, qk, eg, gate, norm_w.reshape(1, DVA))


def _gdn_prompt(x, w_in, conv_w, a_log, dt_bias, norm_w):
    b, l, d = x.shape
    wq = w_in[:, :QKV_A].astype(bf16)
    wg = w_in[:, QKV_A:QKV_A + HA * DVA].astype(bf16)
    wab = w_in[:, QKV_A + HA * DVA:].astype(bf16)
    qkv, gate, ab = _proj(x.reshape(b * l, d), [wq, wg, wab], 256)
    qkv = qkv.reshape(b, l, QKV_A)
    u, w, qd, kd, qk, eg = _gdn_prep(qkv, ab.reshape(b, l, 2 * HA), conv_w,
                                     a_log.reshape(1, HA), dt_bias.reshape(1, HA))
    y, s = _gdn_scan(u, w, qd, kd, qk, eg, gate.reshape(b, l, HA * DVA), norm_w)
    return y.reshape(b * l, HA * DVA), s, qkv[:, l - (CONV - 1):]


GDN_BT = 8


def _gdn_step_body(qkv_ref, gate_ref, ab_ref, buf_ref, s_ref, cw_ref, alog_ref, dtb_ref, nw_ref,
                   y_ref, s_out_ref, buf_out_ref, q_scr, k_scr, v_scr, eg_scr, beta_scr):
    n = QKV_A
    new = qkv_ref[...]
    cw = cw_ref[...]
    y = new * cw[CONV - 1:CONV, :]
    for j in range(CONV - 1):
        y = y + buf_ref[:, j * n:(j + 1) * n] * cw[j:j + 1, :]
    y = _silu(y)
    buf_out_ref[:, 0:2 * n] = buf_ref[:, n:3 * n]
    buf_out_ref[:, 2 * n:3 * n] = new
    for h in range(HA):
        q_scr[:, h * DKA:(h + 1) * DKA] = _l2norm(y[:, h * DKA:(h + 1) * DKA]) * (DKA ** -0.5)
        k_scr[:, h * DKA:(h + 1) * DKA] = _l2norm(y[:, HA * DKA + h * DKA:HA * DKA + (h + 1) * DKA])
    v_scr[...] = y[:, 2 * HA * DKA:]
    g, beta = _gdn_gates(ab_ref[...], alog_ref[...], dtb_ref[...])
    eg_scr[...] = jnp.exp(g)
    beta_scr[...] = beta
    eye = _eye(DKA)
    row8 = lax.broadcasted_iota(jnp.int32, (8, DKA), 0)
    nw = nw_ref[...]

    for bi in range(qkv_ref.shape[0]):
        r = slice(bi, bi + 1)
        for h in range(HA):
            hs = slice(h * DKA, (h + 1) * DKA)
            k_row = k_scr[r, hs]
            q_row = q_scr[r, hs]
            kq = jnp.where(row8 == 0, k_row, jnp.where(row8 == 1, q_row, 0.0))
            cols = _dot_nt_hi(eye, kq)
            k_col = cols[:, 0:1]
            q_col = cols[:, 1:2]
            sd = s_ref[bi, h] * eg_scr[r, h:h + 1]
            pred = jnp.sum(k_col * sd, axis=0, keepdims=True)
            delta = beta_scr[r, h:h + 1] * (v_scr[r, hs] - pred)
            s_new = sd + k_col * delta
            s_out_ref[bi, h] = s_new
            o = jnp.sum(q_col * s_new, axis=0, keepdims=True)
            of = o * lax.rsqrt(jnp.mean(o * o, -1, keepdims=True) + 1e-6)
            y_ref[r, hs] = of * nw * _silu(gate_ref[r, hs])


def _gdn_step(qkv, gate, ab, conv_buf, s0, conv_w, a_log, dt_bias, norm_w):
    nb = qkv.shape[0]
    bt = GDN_BT
    rowblk = lambda n: pl.BlockSpec((bt, n), lambda i: (i, 0))
    full = lambda a: pl.BlockSpec(a.shape, lambda i: (0, 0))
    sblk = pl.BlockSpec((bt, HA, DKA, DVA), lambda i: (i, 0, 0, 0))
    nw = norm_w.reshape(1, DVA)
    al = a_log.reshape(1, HA)
    db = dt_bias.reshape(1, HA)
    return pl.pallas_call(
        _gdn_step_body,
        grid=(nb // bt,),
        in_specs=[rowblk(QKV_A), rowblk(HA * DVA), rowblk(2 * HA), rowblk(3 * QKV_A), sblk,
                  full(conv_w), full(al), full(db), full(nw)],
        out_specs=[rowblk(HA * DVA), sblk, rowblk(3 * QKV_A)],
        out_shape=[jax.ShapeDtypeStruct((nb, HA * DVA), f32),
                   jax.ShapeDtypeStruct(s0.shape, f32),
                   jax.ShapeDtypeStruct((nb, 3 * QKV_A), f32)],
        scratch_shapes=[pltpu.VMEM((bt, HA * DKA), f32)] * 3 + [pltpu.VMEM((bt, HA), f32)] * 2,
        compiler_params=_cp("parallel"),
        name="gdn_step",
    )(qkv, gate, ab, conv_buf.reshape(nb, 3 * QKV_A), s0, conv_w, al, db, nw)


def _gdn_sample(x, conv_buf, s0, w_in, conv_w, a_log, dt_bias, norm_w):
    nb = x.shape[0]
    wq = w_in[:, :QKV_A].astype(bf16)
    wg = w_in[:, QKV_A:QKV_A + HA * DVA].astype(bf16)
    wab = w_in[:, QKV_A + HA * DVA:].astype(bf16)
    qkv, gate, ab = _proj(x, [wq, wg, wab], nb)
    y, s, buf = _gdn_step(qkv, gate, ab, conv_buf, s0, conv_w, a_log, dt_bias, norm_w)
    return y, s, buf.reshape(nb, CONV - 1, QKV_A)


LRU_TILE = 256


def _gelu(x):
    return 0.5 * x * (1.0 + jnp.tanh(math.sqrt(2.0 / math.pi) * (x + 0.044715 * x * x * x)))


def _lru_gates(u, wa_ref, ba, wx_ref, bx, lam):
    ra, xa = [], []
    for n in range(LRU_BLOCKS):
        ub = u[:, n * LRU_BW:(n + 1) * LRU_BW].astype(bf16)
        ra.append(jnp.dot(ub, wa_ref[n], preferred_element_type=f32))
        xa.append(jnp.dot(ub, wx_ref[n], preferred_element_type=f32))
    r = _sigmoid(jnp.concatenate(ra, axis=-1) + ba)
    i_g = _sigmoid(jnp.concatenate(xa, axis=-1) + bx)
    log_a = -LRU_C * r * _softplus(-lam)
    a = jnp.exp(log_a)
    t = jnp.tanh(log_a)
    b = jnp.sqrt(-2.0 * t / (1.0 - t)) * (i_g * u)
    return a, b


def _lru_prompt_body(gate_ref, rec_ref, halo_ref, cw_ref, cb_ref, wa_ref, ba_ref, wx_ref, bx_ref, lam_ref,
                     y_ref, h_ref, xp_scr):
    i = pl.program_id(1)
    tt = rec_ref.shape[1]

    @pl.when(i == 0)
    def _():
        h_ref[...] = jnp.zeros_like(h_ref)

    xp_scr[0:8, :] = jnp.where(i > 0, halo_ref[0], 0.0)
    xp_scr[8:8 + tt, :] = rec_ref[0]
    cw = cw_ref[...]
    u = xp_scr[5:5 + tt, :] * cw[0:1, :]
    for j in range(1, CONV):
        u = u + xp_scr[5 + j:5 + j + tt, :] * cw[j:j + 1, :]
    u = u + cb_ref[...]
    a, b = _lru_gates(u, wa_ref, ba_ref[...], wx_ref, bx_ref[...], lam_ref[...])
    row = lax.broadcasted_iota(jnp.int32, a.shape, 0)
    s = 1
    while s < tt:
        keep = row >= s
        a_sh = jnp.where(keep, pltpu.roll(a, s, 0), 1.0)
        b_sh = jnp.where(keep, pltpu.roll(b, s, 0), 0.0)
        b = a * b_sh + b
        a = a * a_sh
        s *= 2
    h = b + a * h_ref[0]
    h_ref[0] = h[tt - 1:tt, :]
    y_ref[0] = _gelu(gate_ref[0]) * h


def _lru_prompt(x, w_in, conv_w, conv_b, w_a, b_a, w_x, b_x, lam):
    bsz, l, d = x.shape
    w = LRU_W
    gate_in, rec_in = _proj(x.reshape(bsz * l, d), [w_in[:, :w].astype(bf16), w_in[:, w:].astype(bf16)], 256)
    rec3 = rec_in.reshape(bsz, l, w)
    tt = min(LRU_TILE, l)
    hb = tt // 8
    big = pl.BlockSpec((1, tt, w), lambda bi, i: (bi, i, 0))
    vec = pl.BlockSpec((1, w), lambda bi, i: (0, 0))
    blkw = pl.BlockSpec((LRU_BLOCKS, LRU_BW, LRU_BW), lambda bi, i: (0, 0, 0))
    y, h = pl.pallas_call(
        _lru_prompt_body,
        grid=(bsz, l // tt),
        in_specs=[big, big, pl.BlockSpec((1, 8, w), lambda bi, i: (bi, jnp.maximum(i * hb - 1, 0), 0)),
                  pl.BlockSpec((CONV, w), lambda bi, i: (0, 0)), vec, blkw, vec, blkw, vec, vec],
        out_specs=[big, pl.BlockSpec((1, 1, w), lambda bi, i: (bi, 0, 0))],
        out_shape=[jax.ShapeDtypeStruct((bsz, l, w), f32), jax.ShapeDtypeStruct((bsz, 1, w), f32)],
        scratch_shapes=[pltpu.VMEM((tt + 8, w), f32)],
        compiler_params=_cp("parallel", "arbitrary"),
        name="lru_prompt",
    )(gate_in.reshape(bsz, l, w), rec3, rec3, conv_w, conv_b.reshape(1, w), w_a.astype(bf16), b_a.reshape(1, w),
      w_x.astype(bf16), b_x.reshape(1, w), lam.reshape(1, w))
    return y.reshape(bsz * l, w), h.reshape(bsz, w), rec3[:, l - (CONV - 1):]


def _lru_step_body(gate_ref, rec_ref, buf_ref, h0_ref, cw_ref, cb_ref, wa_ref, ba_ref, wx_ref, bx_ref, lam_ref,
                   y_ref, h_ref, buf_out_ref):
    w = LRU_W
    new = rec_ref[...]
    cw = cw_ref[...]
    u = new * cw[CONV - 1:CONV, :]
    for j in range(CONV - 1):
        u = u + buf_ref[:, j * w:(j + 1) * w] * cw[j:j + 1, :]
    u = u + cb_ref[...]
    buf_out_ref[:, 0:2 * w] = buf_ref[:, w:3 * w]
    buf_out_ref[:, 2 * w:3 * w] = new
    a, b = _lru_gates(u, wa_ref, ba_ref[...], wx_ref, bx_ref[...], lam_ref[...])
    h = b + a * h0_ref[...]
    h_ref[...] = h
    y_ref[...] = _gelu(gate_ref[...]) * h


def _lru_sample(x, conv_buf, h0, w_in, conv_w, conv_b, w_a, b_a, w_x, b_x, lam):
    nb = x.shape[0]
    w = LRU_W
    gate_in, rec_in = _proj(x, [w_in[:, :w].astype(bf16), w_in[:, w:].astype(bf16)], nb)
    y, h, buf = pl.pallas_call(
        _lru_step_body,
        out_shape=[jax.ShapeDtypeStruct((nb, w), f32), jax.ShapeDtypeStruct((nb, w), f32),
                   jax.ShapeDtypeStruct((nb, 3 * w), f32)],
        compiler_params=pltpu.CompilerParams(vmem_limit_bytes=VMEM_LIMIT),
        name="lru_step",
    )(gate_in, rec_in, conv_buf.reshape(nb, 3 * w), h0, conv_w, conv_b.reshape(1, w), w_a.astype(bf16),
      b_a.reshape(1, w), w_x.astype(bf16), b_x.reshape(1, w), lam.reshape(1, w))
    return y, h, buf.reshape(nb, CONV - 1, w)


NEG = -1e30


def _t5_bucket(rel):
    n = jnp.maximum(rel, 0)
    max_exact = N_BUCKETS // 2
    large = max_exact + (jnp.log(jnp.maximum(n, 1).astype(f32) / max_exact)
                         / math.log(MAX_DIST / max_exact) * (N_BUCKETS - max_exact)).astype(jnp.int32)
    return jnp.where(n < max_exact, n, jnp.minimum(large, N_BUCKETS - 1))


def _swa_prompt_body(q_ref, kp_ref, kc_ref, vp_ref, vc_ref, bucket_ref, sinks_ref, rb_ref, o_ref, bias_scr):
    i = pl.program_id(1)
    w = WINDOW

    @pl.when(i == 0)
    def _():
        bucket = bucket_ref[...]
        for h in range(HQ):
            b = jnp.zeros((w, 2 * w), f32)
            for n in range(N_BUCKETS):
                b = jnp.where(bucket == n, rb_ref[n, h], b)
            bias_scr[h] = b

    ii = lax.broadcasted_iota(jnp.int32, (w, 2 * w), 0)
    jj = lax.broadcasted_iota(jnp.int32, (w, 2 * w), 1)
    rel = ii + w - jj
    valid = (rel >= 0) & (rel <= w) & ((i > 0) | (jj >= w))
    q = q_ref[0].astype(bf16)
    kp = kp_ref[0].astype(bf16)
    kc = kc_ref[0].astype(bf16)
    vp = vp_ref[0].astype(bf16)
    vc = vc_ref[0].astype(bf16)
    for h in range(HQ):
        kv = slice((h // GQ) * HD, (h // GQ + 1) * HD)
        qh = q[:, h * HD:(h + 1) * HD]
        s = jnp.concatenate([_dot_nt(qh, kp[:, kv]), _dot_nt(qh, kc[:, kv])], axis=-1) * (HD ** -0.5)
        s = jnp.where(valid, s + bias_scr[h], NEG)
        sink = sinks_ref[h]
        m = jnp.maximum(jnp.max(s, -1, keepdims=True), sink)
        e = jnp.exp(s - m)
        denom = jnp.sum(e, -1, keepdims=True) + jnp.exp(sink - m)
        p = (e / denom).astype(bf16)
        o_ref[0, :, h * HD:(h + 1) * HD] = (jnp.dot(p[:, :w], vp[:, kv], preferred_element_type=f32)
                                            + jnp.dot(p[:, w:], vc[:, kv], preferred_element_type=f32))


def _swa_prompt(x, w_in, sinks, rel_bias):
    b, l, d = x.shape
    nq = HQ * HD
    nk = HKV * HD
    q, k, v = _proj(x.reshape(b * l, d), [w_in[:, :nq].astype(bf16), w_in[:, nq:nq + nk].astype(bf16),
                                          w_in[:, nq + nk:].astype(bf16)], 256)
    q = q.reshape(b, l, nq)
    k = k.reshape(b, l, nk)
    v = v.reshape(b, l, nk)
    w = WINDOW
    rel = jnp.arange(w)[:, None] + w - jnp.arange(2 * w)[None, :]
    bucket = _t5_bucket(rel).astype(jnp.int32)
    cur = lambda n: pl.BlockSpec((1, w, n), lambda bi, i: (bi, i, 0))
    prev = lambda n: pl.BlockSpec((1, w, n), lambda bi, i: (bi, jnp.maximum(i - 1, 0), 0))
    smem = pl.BlockSpec(memory_space=pltpu.SMEM)
    o = pl.pallas_call(
        _swa_prompt_body,
        grid=(b, l // w),
        in_specs=[cur(nq), prev(nk), cur(nk), prev(nk), cur(nk),
                  pl.BlockSpec((w, 2 * w), lambda bi, i: (0, 0)), smem, smem],
        out_specs=cur(nq),
        out_shape=jax.ShapeDtypeStruct((b, l, nq), f32),
        scratch_shapes=[pltpu.VMEM((HQ, w, 2 * w), f32)],
        compiler_params=_cp("parallel", "arbitrary"),
        name="swa_prompt",
    )(q, k, k, v, v, bucket, sinks, rel_bias)
    return o.reshape(b * l, nq), k, v


SWA_BT = 8


def _head_mask():
    r = lax.broadcasted_iota(jnp.int32, (HQ, HKV * HD), 0)
    c = lax.broadcasted_iota(jnp.int32, (HQ, HKV * HD), 1)
    return (r // GQ) == (c // HD)


def _fold_heads(o, mask):
    o = jnp.where(mask, o, 0.0)
    acc = o[:, 0:HD]
    for c in range(1, HKV):
        acc = acc + o[:, c * HD:(c + 1) * HD]
    return acc


def _swa_step_body(q_ref, kn_ref, vn_ref, kb_ref, vb_ref, onehot_ref, rbt_ref, sinks_ref, o_ref, kb_out, vb_out):
    w = WINDOW
    mask = _head_mask()
    bias_all = _dot_hi(rbt_ref[...], onehot_ref[...])
    bias = bias_all[:, :w]
    bias_new = bias_all[:, w:w + 1]
    sink = sinks_ref[...]
    for bi in range(q_ref.shape[0]):
        qm = jnp.where(mask, jnp.concatenate([q_ref[bi]] * HKV, axis=-1), 0.0)
        kn = kn_ref[bi]
        vn = vn_ref[bi]
        s = _dot_nt(qm, kb_ref[bi]) * (HD ** -0.5) + bias
        s_new = jnp.sum(qm * kn, -1, keepdims=True) * (HD ** -0.5) + bias_new
        m = jnp.maximum(jnp.maximum(jnp.max(s, -1, keepdims=True), s_new), sink)
        e = jnp.exp(s - m)
        e_new = jnp.exp(s_new - m)
        denom = jnp.sum(e, -1, keepdims=True) + e_new + jnp.exp(sink - m)
        o = (_dot(e / denom, vb_ref[bi]) + (e_new / denom) * vn)
        o_ref[bi] = _fold_heads(o, mask)
        kb_out[bi, 0:w - 1, :] = kb_ref[bi, 1:w, :]
        kb_out[bi, w - 1:w, :] = kn
        vb_out[bi, 0:w - 1, :] = vb_ref[bi, 1:w, :]
        vb_out[bi, w - 1:w, :] = vn


def _swa_sample(x, k_buf, v_buf, w_in, sinks, rel_bias):
    nb = x.shape[0]
    nq = HQ * HD
    nk = HKV * HD
    w = WINDOW
    q, k, v = _proj(x, [w_in[:, :nq].astype(bf16), w_in[:, nq:nq + nk].astype(bf16), w_in[:, nq + nk:].astype(bf16)], nb)
    slots = jnp.arange(w + 128)
    bucket = _t5_bucket(jnp.where(slots <= w, w - slots, 0))
    onehot = (bucket[None, :] == jnp.arange(N_BUCKETS)[:, None]).astype(f32)
    bt = SWA_BT
    full = lambda a: pl.BlockSpec(a.shape, lambda i: (0,) * a.ndim)
    bufblk = pl.BlockSpec((bt, w, nk), lambda i: (i, 0, 0))
    rbt = rel_bias.T
    sk = sinks.reshape(HQ, 1)
    o, kb, vb = pl.pallas_call(
        _swa_step_body,
        grid=(nb // bt,),
        in_specs=[pl.BlockSpec((bt, HQ, HD), lambda i: (i, 0, 0)),
                  pl.BlockSpec((bt, 1, nk), lambda i: (i, 0, 0)), pl.BlockSpec((bt, 1, nk), lambda i: (i, 0, 0)),
                  bufblk, bufblk, full(onehot), full(rbt), full(sk)],
        out_specs=[pl.BlockSpec((bt, HQ, HD), lambda i: (i, 0, 0)), bufblk, bufblk],
        out_shape=[jax.ShapeDtypeStruct((nb, HQ, HD), f32), jax.ShapeDtypeStruct((nb, w, nk), f32),
                   jax.ShapeDtypeStruct((nb, w, nk), f32)],
        compiler_params=_cp("parallel"),
        name="swa_step",
    )(q.reshape(nb, HQ, HD), k.reshape(nb, 1, nk), v.reshape(nb, 1, nk), k_buf.reshape(nb, w, nk),
      v_buf.reshape(nb, w, nk), onehot, rbt, sk)
    return o.reshape(nb, nq), k, v, kb.reshape(nb, w, HKV, HD), vb.reshape(nb, w, HKV, HD)


FOX_PREP_TILE = 512
FOX_TQ = 256
FOX_TK = 512


def _fox_prep_body(x_ref, wft_ref, bf_ref, lf_ref, cum_ref, carry):
    i = pl.program_id(1)
    tt = x_ref.shape[1]

    @pl.when(i == 0)
    def _():
        carry[...] = jnp.zeros_like(carry)

    f = lax.dot_general(wft_ref[...], x_ref[0].astype(bf16), (((1,), (1,)), ((), ())), preferred_element_type=f32)
    lf = _log_sigmoid(f + bf_ref[...])
    r = lax.broadcasted_iota(jnp.int32, (tt, tt), 0)
    c = lax.broadcasted_iota(jnp.int32, (tt, tt), 1)
    cum = _dot_hi(lf, (r <= c).astype(f32)) + carry[...]
    lf_ref[0] = lf
    cum_ref[0] = cum
    carry[...] = cum[:, tt - 1:tt]


def _fox_flash_body(qt_ref, k_ref, vt_ref, fq_ref, fk_ref, o_ref, m_scr, l_scr, acc_scr):
    i = pl.program_id(2)
    tq = qt_ref.shape[3]
    tk = FOX_TK
    fq = fq_ref[0, 0]
    m_scr[...] = jnp.full_like(m_scr, NEG)
    l_scr[...] = jnp.zeros_like(l_scr)
    acc_scr[...] = jnp.zeros_like(acc_scr)

    def block(j, masked):
        c0 = pl.multiple_of(j * tk, tk)
        k = k_ref[0, 0, pl.ds(c0, tk), :]
        vt = vt_ref[0, 0, :, pl.ds(c0, tk)]
        fk = fk_ref[0, 0, pl.ds(c0, tk), :]
        if masked:
            s_pos = c0 + lax.broadcasted_iota(jnp.int32, (tk, tq), 0)
            t_pos = i * tq + lax.broadcasted_iota(jnp.int32, (tk, tq), 1)
            visible = s_pos <= t_pos
        for g in range(GQ):
            s = jnp.dot(k, qt_ref[0, g], preferred_element_type=f32) + (fq[g:g + 1, :] - fk[:, g:g + 1])
            if masked:
                s = jnp.where(visible, s, NEG)
            m_old = m_scr[g]
            m_new = jnp.maximum(m_old, jnp.max(s, 0, keepdims=True))
            alpha = jnp.exp(m_old - m_new)
            p = jnp.exp(s - m_new)
            l_scr[g] = alpha * l_scr[g] + jnp.sum(p, 0, keepdims=True)
            acc_scr[g] = alpha * acc_scr[g] + jnp.dot(vt, p.astype(bf16), preferred_element_type=f32)
            m_scr[g] = m_new

    n_full = (i * tq) // tk

    def full_block(j, carry):
        block(j, False)
        return carry

    lax.fori_loop(0, n_full, full_block, 0)
    for jj in range(pl.cdiv(tq, tk)):
        block(n_full + jj, True)
    for g in range(GQ):
        o_ref[0, g] = acc_scr[g] / l_scr[g]


def _fox_prompt(x, w_in, b_f):
    b, l, d = x.shape
    nq = HQ * HD
    nk = HKV * HD
    q, k, v = _proj(x.reshape(b * l, d), [w_in[:, :nq].astype(bf16), w_in[:, nq:nq + nk].astype(bf16),
                                          w_in[:, nq + nk:nq + 2 * nk].astype(bf16)], 256)
    tt = min(FOX_PREP_TILE, l)
    wft = w_in[:, nq + 2 * nk:].T.astype(bf16)
    row = pl.BlockSpec((1, HQ, tt), lambda bi, i: (bi, 0, i))
    lft, cumt = pl.pallas_call(
        _fox_prep_body,
        grid=(b, l // tt),
        in_specs=[pl.BlockSpec((1, tt, d), lambda bi, i: (bi, i, 0)),
                  pl.BlockSpec((HQ, d), lambda bi, i: (0, 0)), pl.BlockSpec((HQ, 1), lambda bi, i: (0, 0))],
        out_specs=[row, row],
        out_shape=[jax.ShapeDtypeStruct((b, HQ, l), f32)] * 2,
        scratch_shapes=[pltpu.VMEM((HQ, 1), f32)],
        compiler_params=_cp("parallel", "arbitrary"),
        name="fox_prep",
    )(x, wft, b_f.reshape(HQ, 1))
    fq = cumt.reshape(b, HKV, GQ, l)
    fk = jnp.swapaxes(fq, 2, 3)
    qt = jnp.transpose((q * (HD ** -0.5)).astype(bf16).reshape(b, l, HQ, HD), (0, 2, 3, 1))
    kh = jnp.swapaxes(k.reshape(b, l, HKV, HD), 1, 2).astype(bf16)
    vt = jnp.transpose(v.astype(bf16).reshape(b, l, HKV, HD), (0, 2, 3, 1))
    tq = min(FOX_TQ, l)
    qblk = pl.BlockSpec((1, GQ, HD, tq), lambda bi, h, i: (bi, h, 0, i))
    ot = pl.pallas_call(
        _fox_flash_body,
        grid=(b, HKV, l // tq),
        in_specs=[qblk,
                  pl.BlockSpec((1, 1, l, HD), lambda bi, h, i: (bi, h, 0, 0)),
                  pl.BlockSpec((1, 1, HD, l), lambda bi, h, i: (bi, h, 0, 0)),
                  pl.BlockSpec((1, 1, GQ, tq), lambda bi, h, i: (bi, h, 0, i)),
                  pl.BlockSpec((1, 1, l, GQ), lambda bi, h, i: (bi, h, 0, 0))],
        out_specs=qblk,
        out_shape=jax.ShapeDtypeStruct((b, HQ, HD, l), f32),
        scratch_shapes=[pltpu.VMEM((GQ, 1, tq), f32), pltpu.VMEM((GQ, 1, tq), f32), pltpu.VMEM((GQ, HD, tq), f32)],
        compiler_params=_cp("parallel", "parallel", "arbitrary"),
        name="fox_flash",
    )(qt, kh, vt, fq, fk)
    o = jnp.transpose(ot, (0, 3, 1, 2)).reshape(b * l, nq)
    return o, k.reshape(b, l, nk), v.reshape(b, l, nk), jnp.swapaxes(lft, 1, 2)


FOX_PAGES = 16


def _fox_step_body(pt_ref, q_ref, kn_ref, vn_ref, lfn_ref, *refs):
    npg = FOX_PAGES
    k_refs = refs[0:npg]
    v_refs = refs[npg:2 * npg]
    lf_refs = refs[2 * npg:3 * npg]
    o_ref, m_scr, l_scr, acc_scr, f_scr = refs[3 * npg:]
    j = pl.program_id(1)
    ps = PAGE_SIZE
    mask = _head_mask()
    qm = jnp.where(mask, jnp.concatenate([q_ref[0]] * HKV, axis=-1), 0.0) * (HD ** -0.5)

    @pl.when(j == 0)
    def _():
        m_scr[...] = jnp.full_like(m_scr, NEG)
        l_scr[...] = jnp.zeros_like(l_scr)
        acc_scr[...] = jnp.zeros_like(acc_scr)
        f_scr[...] = jnp.zeros_like(f_scr)

    r = lax.broadcasted_iota(jnp.int32, (ps, ps), 0)
    c = lax.broadcasted_iota(jnp.int32, (ps, ps), 1)
    upper = (r <= c).astype(f32)

    def update(scores, weighted_values, f_end):
        m_old = m_scr[...]
        m_new = m_old
        for s in scores:
            m_new = jnp.maximum(m_new, jnp.max(s, -1, keepdims=True))
        alpha = jnp.exp(m_old - m_new)
        l_new = alpha * l_scr[...]
        acc = alpha * acc_scr[...]
        for n, s in enumerate(scores):
            p = jnp.exp(s - m_new)
            l_new = l_new + jnp.sum(p, -1, keepdims=True)
            acc = acc + weighted_values(n, p)
        l_scr[...] = l_new
        acc_scr[...] = acc
        m_scr[...] = m_new
        f_scr[...] = f_end

    cum_all = _dot_hi(jnp.concatenate([lf_refs[pg][0] for pg in range(npg)], axis=0), upper)
    f_run = f_scr[...]
    scores = []
    for pg in range(npg):
        cum = cum_all[pg * HQ:(pg + 1) * HQ] + f_run
        kt = k_refs[pg][0].reshape(HKV * HD, ps)
        scores.append(_dot(qm, kt) - cum)
        f_run = cum[:, ps - 1:ps]
    update(scores, lambda n, p: _dot_nt(p, v_refs[n][0].reshape(HKV * HD, ps)), f_run)

    @pl.when(j == pl.num_programs(1) - 1)
    def _():
        kn = kn_ref[0]
        vn = vn_ref[0]
        cum = f_scr[...] + lfn_ref[0]
        s = jnp.sum(qm * kn, -1, keepdims=True) - cum
        update([s], lambda n, p: p * vn, cum)
        o_ref[0] = _fold_heads(acc_scr[...] / l_scr[...], mask)


def _logf_body(f_ref, b_ref, o_ref):
    o_ref[...] = _log_sigmoid(f_ref[...] + b_ref[...])


def _fox_sample(x, cache_k, cache_v, cache_logf, page_table, w_in, b_f):
    nb = x.shape[0]
    nq = HQ * HD
    nk = HKV * HD
    n_pool = cache_k.shape[0]
    n_pages = page_table.shape[1]
    npg = FOX_PAGES
    q, k, v, f = _proj(x, [w_in[:, :nq].astype(bf16), w_in[:, nq:nq + nk].astype(bf16),
                           w_in[:, nq + nk:nq + 2 * nk].astype(bf16), w_in[:, nq + 2 * nk:].astype(bf16)], nb)
    lf_new = pl.pallas_call(_logf_body, out_shape=jax.ShapeDtypeStruct((nb, HQ), f32))(f, b_f.reshape(1, HQ))

    def page(pg, *shape):
        return pl.BlockSpec((1,) + shape,
                            lambda bi, j, pt: (pt[bi * n_pages + j * npg + pg],) + (0,) * len(shape))

    tok = lambda *shape: pl.BlockSpec((1,) + shape, lambda bi, j, pt: (bi,) + (0,) * len(shape))
    ck = jnp.transpose(cache_k, (0, 2, 3, 1))
    cv = jnp.transpose(cache_v, (0, 2, 3, 1))
    clf = jnp.transpose(cache_logf, (0, 2, 1))
    o = pl.pallas_call(
        _fox_step_body,
        grid_spec=pltpu.PrefetchScalarGridSpec(
            num_scalar_prefetch=1,
            grid=(nb, n_pages // npg),
            in_specs=[tok(HQ, HD), tok(1, nk), tok(1, nk), tok(HQ, 1)]
            + [page(pg, HKV, HD, PAGE_SIZE) for pg in range(npg)]
            + [page(pg, HKV, HD, PAGE_SIZE) for pg in range(npg)]
            + [page(pg, HQ, PAGE_SIZE) for pg in range(npg)],
            out_specs=tok(HQ, HD),
            scratch_shapes=[pltpu.VMEM((HQ, 1), f32), pltpu.VMEM((HQ, 1), f32), pltpu.VMEM((HQ, nk), f32),
                            pltpu.VMEM((HQ, 1), f32)]),
        out_shape=jax.ShapeDtypeStruct((nb, HQ, HD), f32),
        compiler_params=_cp("parallel", "arbitrary"),
        name="fox_decode",
    )(page_table.reshape(-1), q.reshape(nb, HQ, HD), k.reshape(nb, 1, nk), v.reshape(nb, 1, nk),
      lf_new.reshape(nb, HQ, 1), *([ck] * npg), *([cv] * npg), *([clf] * npg))
    return o.reshape(nb, nq), k, v, lf_new


ROUTER_TILE = 512
MOE_TM = 1024


def _router_body(x_ref, rwt_ref, rb_ref, comb_ref):
    scores = _sigmoid(_dot_nt(rwt_ref[...], x_ref[...]))
    sel = scores + rb_ref[...]
    rows = [sel[e:e + 1, :] for e in range(N_EXPERTS)]
    srow = [scores[e:e + 1, :] for e in range(N_EXPERTS)]
    gs = []
    for g in range(N_GROUPS):
        r = rows[g * EPG:(g + 1) * EPG]
        best = None
        for a in range(EPG):
            for b in range(a + 1, EPG):
                pair = r[a] + r[b]
                best = pair if best is None else jnp.maximum(best, pair)
        gs.append(best)
    g_best = gs[0]
    g_idx = jnp.zeros_like(gs[0], dtype=jnp.int32)
    for g in range(1, N_GROUPS):
        better = gs[g] > g_best
        g_best = jnp.where(better, gs[g], g_best)
        g_idx = jnp.where(better, g, g_idx)

    def in_group(vals, j):
        out = vals[j]
        for g in range(1, N_GROUPS):
            out = jnp.where(g_idx == g, vals[g * EPG + j], out)
        return out

    ig = [in_group(rows, j) for j in range(EPG)]
    sg = [in_group(srow, j) for j in range(EPG)]
    v1, i1, s1 = ig[0], jnp.zeros_like(g_idx), sg[0]
    for j in range(1, EPG):
        better = ig[j] > v1
        v1 = jnp.where(better, ig[j], v1)
        i1 = jnp.where(better, j, i1)
        s1 = jnp.where(better, sg[j], s1)
    v2 = jnp.full_like(v1, -jnp.inf)
    i2 = jnp.zeros_like(g_idx)
    s2 = jnp.zeros_like(s1)
    for j in range(EPG):
        better = (i1 != j) & (ig[j] > v2)
        v2 = jnp.where(better, ig[j], v2)
        i2 = jnp.where(better, j, i2)
        s2 = jnp.where(better, sg[j], s2)
    tot = s1 + s2
    e_row = lax.broadcasted_iota(jnp.int32, scores.shape, 0)
    comb_ref[...] = (jnp.where(e_row == g_idx * EPG + i1, s1 / tot, 0.0)
                     + jnp.where(e_row == g_idx * EPG + i2, s2 / tot, 0.0))


def _router(x, router_w, router_b):
    t, d = x.shape
    tt = min(ROUTER_TILE, t)
    comb_t = pl.pallas_call(
        _router_body,
        grid=(t // tt,),
        in_specs=[pl.BlockSpec((tt, d), lambda i: (i, 0)), pl.BlockSpec((N_EXPERTS, d), lambda i: (0, 0)),
                  pl.BlockSpec((N_EXPERTS, 1), lambda i: (0, 0))],
        out_specs=pl.BlockSpec((N_EXPERTS, tt), lambda i: (0, i)),
        out_shape=jax.ShapeDtypeStruct((N_EXPERTS, t), f32),
        compiler_params=_cp("parallel"),
        name="router",
    )(x, router_w.T, router_b.reshape(N_EXPERTS, 1))
    return comb_t.T


def _moe_dense_body(x_ref, comb_ref, wg_ref, wu_ref, wd_ref, g_ref, b_ref, o_ref, acc_scr):
    e = pl.program_id(1)

    @pl.when(e == 0)
    def _():
        acc_scr[...] = jnp.zeros_like(acc_scr)

    x = x_ref[...].astype(bf16)
    h = _silu(jnp.dot(x, wg_ref[0].astype(bf16), preferred_element_type=f32)) \
        * jnp.dot(x, wu_ref[0].astype(bf16), preferred_element_type=f32)
    y = jnp.dot(h.astype(bf16), wd_ref[0].astype(bf16), preferred_element_type=f32)
    lane = lax.broadcasted_iota(jnp.int32, comb_ref.shape, 1)
    c = jnp.sum(jnp.where(lane == e, comb_ref[...], 0.0), -1, keepdims=True)
    acc_scr[...] += c * y

    @pl.when(e == pl.num_programs(1) - 1)
    def _():
        o_ref[...] = _layer_norm(DN_ALPHA * x_ref[...] + acc_scr[...], g_ref[...], b_ref[...])


def _moe_ln(x, router_w, router_b, w_gate, w_up, w_down, g, b):
    t, d = x.shape
    comb = _router(x, router_w, router_b)
    tm = min(MOE_TM, t)
    return pl.pallas_call(
        _moe_dense_body,
        grid=(t // tm, N_EXPERTS),
        in_specs=[pl.BlockSpec((tm, d), lambda i, e: (i, 0)), pl.BlockSpec((tm, N_EXPERTS), lambda i, e: (i, 0)),
                  pl.BlockSpec((1, d, D_EXPERT), lambda i, e: (e, 0, 0)),
                  pl.BlockSpec((1, d, D_EXPERT), lambda i, e: (e, 0, 0)),
                  pl.BlockSpec((1, D_EXPERT, d), lambda i, e: (e, 0, 0)),
                  pl.BlockSpec((1, d), lambda i, e: (0, 0)), pl.BlockSpec((1, d), lambda i, e: (0, 0))],
        out_specs=pl.BlockSpec((tm, d), lambda i, e: (i, 0)),
        out_shape=jax.ShapeDtypeStruct((t, d), f32),
        scratch_shapes=[pltpu.VMEM((tm, d), f32)],
        compiler_params=_cp("parallel", "arbitrary"),
        name="moe_dense",
    )(x, comb, w_gate, w_up, w_down, g.reshape(1, d), b.reshape(1, d))


def kernel(x_prompt, x_sample, state_a_ssm, state_a_conv, state_b_h, state_b_conv, cache_c_k, cache_c_v, cache_d_k, cache_d_v, cache_d_logf, page_table, ln_g, ln_b, a_w_in, a_conv_w, a_a_log, a_dt_bias, a_norm_w, a_w_out, b_w_in, b_conv_w, b_conv_b, b_w_a, b_b_a, b_w_x, b_b_x, b_lambda, b_w_out, c_w_in, c_sinks, c_w_out, rel_bias, d_w_in, d_b_f, d_w_out, router_w, router_b, moe_w_gate, moe_w_up, moe_w_down):
    bp, lp, d = x_prompt.shape
    nb = x_sample.shape[0]
    xp = x_prompt.reshape(bp * lp, d)
    xs = x_sample.reshape(nb, d)
    tm_p = 256

    def finish(xp, xs, mp, ms, w_out, i):
        w = w_out.astype(bf16)
        xp = _proj_ln(mp, w, xp, ln_g[i, 0], ln_b[i, 0], tm_p)
        xs = _proj_ln(ms, w, xs, ln_g[i, 0], ln_b[i, 0], nb)
        moe_args = (router_w, router_b, moe_w_gate[i], moe_w_up[i], moe_w_down[i], ln_g[i, 1], ln_b[i, 1])
        return _moe_ln(xp, *moe_args), _moe_ln(xs, *moe_args)

    a_args = (a_w_in[0], a_conv_w[0], a_a_log[0], a_dt_bias[0], a_norm_w[0])
    mp, p_a_ssm, p_a_conv = _gdn_prompt(xp.reshape(bp, lp, d), *a_args)
    ms, s_a_ssm, s_a_conv = _gdn_sample(xs, state_a_conv[0], state_a_ssm[0], *a_args)
    xp, xs = finish(xp, xs, mp, ms, a_w_out[0], 0)

    b_args = (b_w_in[0], b_conv_w[0], b_conv_b[0], b_w_a[0], b_b_a[0], b_w_x[0], b_b_x[0], b_lambda[0])
    mp, p_b_h, p_b_conv = _lru_prompt(xp.reshape(bp, lp, d), *b_args)
    ms, s_b_h, s_b_conv = _lru_sample(xs, state_b_conv[0], state_b_h[0], *b_args)
    xp, xs = finish(xp, xs, mp, ms, b_w_out[0], 1)

    mp, kp, vp = _swa_prompt(xp.reshape(bp, lp, d), c_w_in[0], c_sinks[0], rel_bias)
    ms, _, _, s_c_k, s_c_v = _swa_sample(xs, cache_c_k[0], cache_c_v[0], c_w_in[0], c_sinks[0], rel_bias)
    p_c_k = kp[:, lp - WINDOW:].reshape(bp, WINDOW, HKV, HD)
    p_c_v = vp[:, lp - WINDOW:].reshape(bp, WINDOW, HKV, HD)
    xp, xs = finish(xp, xs, mp, ms, c_w_out[0], 2)

    mp, kp, vp, p_d_logf = _fox_prompt(xp.reshape(bp, lp, d), d_w_in[0], d_b_f[0])
    ms, ks, vs, lfs = _fox_sample(xs, cache_d_k[0], cache_d_v[0], cache_d_logf[0], page_table, d_w_in[0], d_b_f[0])
    xp, xs = finish(xp, xs, mp, ms, d_w_out[0], 3)

    return (xp.reshape(bp, lp, d), xs.reshape(nb, 1, d),
            p_a_ssm[None], p_a_conv[None], p_b_h[None], p_b_conv[None], p_c_k[None], p_c_v[None],
            kp.reshape(1, bp, lp, HKV, HD), vp.reshape(1, bp, lp, HKV, HD), p_d_logf[None],
            s_a_ssm[None], s_a_conv[None], s_b_h[None], s_b_conv[None], s_c_k[None], s_c_v[None],
            ks.reshape(1, nb, 1, HKV, HD), vs.reshape(1, nb, 1, HKV, HD), lfs.reshape(1, nb, 1, HQ))
```

```python
import functools
import math

import jax
import jax.numpy as jnp
import numpy as np
from jax import lax
from jax.experimental import pallas as pl
from jax.experimental.pallas import tpu as pltpu

f32 = jnp.float32
bf16 = jnp.bfloat16
HI = lax.Precision.HIGHEST

D_MODEL = 1024
DEPTH = 4
DN_ALPHA = (2 * DEPTH) ** 0.25
LN_EPS = 1e-5
CONV = 4
HA = 8
DKA = 128
DVA = 128
QKV_A = HA * (2 * DKA + DVA)
CHUNK_A = 64
LRU_W = 1024
LRU_BLOCKS = 8
LRU_BW = 128
LRU_C = 8.0
HD = 64
HQ = 16
HKV = 4
GQ = HQ // HKV
WINDOW = 128
N_BUCKETS = 32
MAX_DIST = 128
PAGE_SIZE = 128
N_EXPERTS = 16
N_GROUPS = 4
EPG = 4
D_EXPERT = 512

VMEM_LIMIT = 56 * 1024 * 1024


def _cp(*sem):
    return pltpu.CompilerParams(dimension_semantics=sem, vmem_limit_bytes=VMEM_LIMIT)


def _dot(a, b):
    return jnp.dot(a.astype(bf16), b.astype(bf16), preferred_element_type=f32)


def _dot_nt(a, b):
    return lax.dot_general(a.astype(bf16), b.astype(bf16), (((1,), (1,)), ((), ())), preferred_element_type=f32)


def _dot_tn(a, b):
    return lax.dot_general(a.astype(bf16), b.astype(bf16), (((0,), (0,)), ((), ())), preferred_element_type=f32)


def _dot_hi(a, b):
    return jnp.dot(a, b, precision=HI, preferred_element_type=f32)


def _split(a):
    hi = a.astype(bf16)
    return hi, (a - hi.astype(f32)).astype(bf16)


def _dot_split(a, b):
    a_hi, a_lo = _split(a)
    b_hi, b_lo = _split(b)
    d = lambda u, v: jnp.dot(u, v, preferred_element_type=f32)
    return d(a_hi, b_hi) + (d(a_hi, b_lo) + d(a_lo, b_hi))


def _bdot_split(a, b):
    a_hi, a_lo = _split(a)
    b_hi, b_lo = _split(b)
    d = lambda u, v: jnp.einsum('hij,hjk->hik', u, v, preferred_element_type=f32)
    return d(a_hi, b_hi) + (d(a_hi, b_lo) + d(a_lo, b_hi))


def _bdot_nt(a, b):
    return jnp.einsum('hid,hjd->hij', a.astype(bf16), b.astype(bf16), preferred_element_type=f32)


def _dot_nt_hi(a, b):
    return lax.dot_general(a, b, (((1,), (1,)), ((), ())), precision=HI, preferred_element_type=f32)


def _sigmoid(x):
    return 1.0 / (1.0 + jnp.exp(-x))


def _silu(x):
    return x * _sigmoid(x)


def _softplus(x):
    return jnp.maximum(x, 0.0) + jnp.log1p(jnp.exp(-jnp.abs(x)))


def _log_sigmoid(x):
    return -_softplus(-x)


def _eye(n):
    r = lax.broadcasted_iota(jnp.int32, (n, n), 0)
    c = lax.broadcasted_iota(jnp.int32, (n, n), 1)
    return (r == c).astype(f32)


def _proj_body(n_out, x_ref, *refs):
    x = x_ref[...].astype(bf16)
    for w_ref, o_ref in zip(refs[:n_out], refs[n_out:]):
        o_ref[...] = jnp.dot(x, w_ref[...], preferred_element_type=f32)


def _proj(x, ws, tm):
    m, k = x.shape
    n_out = len(ws)
    return pl.pallas_call(
        functools.partial(_proj_body, n_out),
        grid=(m // tm,),
        in_specs=[pl.BlockSpec((tm, k), lambda i: (i, 0))]
        + [pl.BlockSpec(w.shape, lambda i: (0, 0)) for w in ws],
        out_specs=[pl.BlockSpec((tm, w.shape[1]), lambda i: (i, 0)) for w in ws],
        out_shape=[jax.ShapeDtypeStruct((m, w.shape[1]), f32) for w in ws],
        compiler_params=_cp("parallel"),
        name="proj",
    )(x, *ws)


def _layer_norm(z, g, b):
    mu = jnp.mean(z, -1, keepdims=True)
    zc = z - mu
    var = jnp.mean(zc * zc, -1, keepdims=True)
    return zc * lax.rsqrt(var + LN_EPS) * g + b


def _proj_ln_body(h_ref, w_ref, x_ref, g_ref, b_ref, o_ref):
    m = jnp.dot(h_ref[...].astype(bf16), w_ref[...], preferred_element_type=f32)
    o_ref[...] = _layer_norm(DN_ALPHA * x_ref[...] + m, g_ref[...], b_ref[...])


def _proj_ln(h, w, x, g, b, tm):
    m, k = h.shape
    d = x.shape[1]
    return pl.pallas_call(
        _proj_ln_body,
        grid=(m // tm,),
        in_specs=[pl.BlockSpec((tm, k), lambda i: (i, 0)),
                  pl.BlockSpec((k, d), lambda i: (0, 0)),
                  pl.BlockSpec((tm, d), lambda i: (i, 0)),
                  pl.BlockSpec((1, d), lambda i: (0, 0)),
                  pl.BlockSpec((1, d), lambda i: (0, 0))],
        out_specs=pl.BlockSpec((tm, d), lambda i: (i, 0)),
        out_shape=jax.ShapeDtypeStruct((m, d), f32),
        compiler_params=_cp("parallel"),
        name="proj_ln",
    )(h, w, x, g.reshape(1, d), b.reshape(1, d))


GDN_TILE = 256


def _gdn_gates(ab, alog, dtb):
    g = -jnp.exp(alog) * _softplus(ab[:, 0:HA] + dtb)
    beta = _sigmoid(ab[:, HA:2 * HA])
    return g, beta


def _l2norm(t):
    return t * lax.rsqrt(jnp.sum(t * t, -1, keepdims=True) + 1e-6)


def _gdn_prep_body(qkv_ref, halo_ref, ab_ref, cw_ref, alog_ref, dtb_ref,
                   u_ref, w_ref, qd_ref, kd_ref, qk_ref, eg_ref, xp_scr, y_scr):
    i = pl.program_id(1)
    tc = qkv_ref.shape[1]
    c = CHUNK_A
    xp_scr[0:8, :] = jnp.where(i > 0, halo_ref[0], 0.0)
    xp_scr[8:8 + tc, :] = qkv_ref[0]
    cw = cw_ref[...]
    y = xp_scr[5:5 + tc, :] * cw[0:1, :]
    for j in range(1, CONV):
        y = y + xp_scr[5 + j:5 + j + tc, :] * cw[j:j + 1, :]
    y_scr[...] = _silu(y)

    row = lax.broadcasted_iota(jnp.int32, (c, c), 0)
    col = lax.broadcasted_iota(jnp.int32, (c, c), 1)
    incl = row >= col
    strict = row > col
    tri = incl.astype(f32)
    eye_c = (row == col).astype(f32)
    eye_h = _eye(HA)
    alog = alog_ref[...]
    dtb = dtb_ref[...]

    def chunk(n, carry):
        r0 = pl.multiple_of(n * c, c)
        rows = pl.ds(r0, c)
        g, beta = _gdn_gates(ab_ref[0, rows, :], alog, dtb)
        gc = _dot_hi(tri, g)
        gct = _dot_nt_hi(eye_h, gc)
        egc = jnp.exp(gc)
        eg_ref[0, rows, :] = egc
        heads = lambda t, off: jnp.stack([t(y_scr[rows, off + h * DKA:off + (h + 1) * DKA]) for h in range(HA)])
        q = heads(_l2norm, 0) * (DKA ** -0.5)
        k = heads(_l2norm, HA * DKA)
        v = heads(lambda t: t, 2 * HA * DKA)
        col = lambda t: jnp.stack([t[:, h:h + 1] for h in range(HA)])
        gcol, bcol, egcol = col(gc), col(beta), col(egc)
        diff = gcol - gct[:, None, :]
        dec = jnp.where(incl, jnp.exp(jnp.where(incl, diff, 0.0)), 0.0)
        a = jnp.where(strict, _bdot_nt(k, k) * dec, 0.0) * bcol
        x = eye_c - a
        p = _bdot_split(a, a)
        for _ in range(4):
            xp = _bdot_split(jnp.concatenate([x, p], axis=1), p)
            x = x + xp[:, :c]
            p = xp[:, c:]
        x = x + _bdot_split(x, p)
        sol = _bdot_split(x, jnp.concatenate([v * bcol, k * (bcol * egcol)], axis=-1))
        qd = q * egcol
        kd = k * jnp.exp(jnp.stack([gc[c - 1:c, h:h + 1] for h in range(HA)]) - gcol)
        qk = _bdot_nt(q, k) * dec
        for h in range(HA):
            hs = slice(h * DVA, (h + 1) * DVA)
            u_ref[0, rows, hs] = sol[h, :, :DVA]
            w_ref[0, rows, hs] = sol[h, :, DVA:]
            qd_ref[0, rows, hs] = qd[h]
            kd_ref[0, rows, hs] = kd[h]
            qk_ref[0, rows, h * c:(h + 1) * c] = qk[h]
        return carry

    lax.fori_loop(0, tc // c, chunk, 0)


def _gdn_prep(qkv, ab, conv_w, a_log, dt_bias):
    b, l, _ = qkv.shape
    tc = min(GDN_TILE, l)
    hb = tc // 8
    big = lambda n: pl.BlockSpec((1, tc, n), lambda bi, i: (bi, i, 0))
    full = lambda a: pl.BlockSpec(a.shape, lambda bi, i: (0, 0))
    wide = jax.ShapeDtypeStruct((b, l, HA * DVA), f32)
    return pl.pallas_call(
        _gdn_prep_body,
        grid=(b, l // tc),
        in_specs=[big(QKV_A),
                  pl.BlockSpec((1, 8, QKV_A), lambda bi, i: (bi, jnp.maximum(i * hb - 1, 0), 0)),
                  big(2 * HA), full(conv_w), full(a_log), full(dt_bias)],
        out_specs=[big(HA * DVA)] * 4 + [big(HA * CHUNK_A), big(HA)],
        out_shape=[wide] * 4 + [jax.ShapeDtypeStruct((b, l, HA * CHUNK_A), f32),
                                jax.ShapeDtypeStruct((b, l, HA), f32)],
        scratch_shapes=[pltpu.VMEM((tc + 8, QKV_A), f32), pltpu.VMEM((tc, QKV_A), f32)],
        compiler_params=_cp("parallel", "parallel"),
        name="gdn_prep",
    )(qkv, qkv, ab, conv_w, a_log, dt_bias)


def _gdn_scan_body(u_ref, w_ref, qd_ref, kd_ref, qk_ref, eg_ref, gate_ref, nw_ref, y_ref, s_ref):
    i = pl.program_id(1)
    tc = u_ref.shape[1]
    c = CHUNK_A

    @pl.when(i == 0)
    def _():
        s_ref[...] = jnp.zeros_like(s_ref)

    nw = nw_ref[...]

    def chunk(n, carry):
        r0 = pl.multiple_of(n * c, c)
        rows = pl.ds(r0, c)
        eg_last = eg_ref[0, pl.ds(r0 + c - 1, 1), :]
        for h in range(HA):
            hs = slice(h * DVA, (h + 1) * DVA)
            s = s_ref[0, h]
            v_new = u_ref[0, rows, hs] - _dot(w_ref[0, rows, hs], s)
            o = _dot(qd_ref[0, rows, hs], s) + _dot(qk_ref[0, rows, h * c:(h + 1) * c], v_new)
            s_ref[0, h] = s * eg_last[:, h:h + 1] + _dot_tn(kd_ref[0, rows, hs], v_new)
            of = o * lax.rsqrt(jnp.mean(o * o, -1, keepdims=True) + 1e-6)
            y_ref[0, rows, hs] = of * nw * _silu(gate_ref[0, rows, hs])
        return carry

    lax.fori_loop(0, tc // c, chunk, 0)


def _gdn_scan(u, w, qd, kd, qk, eg, gate, norm_w):
    b, l, _ = u.shape
    tc = min(GDN_TILE, l)
    big = lambda n: pl.BlockSpec((1, tc, n), lambda bi, i: (bi, i, 0))
    return pl.pallas_call(
        _gdn_scan_body,
        grid=(b, l // tc),
        in_specs=[big(HA * DVA)] * 4 + [big(HA * CHUNK_A), big(HA), big(HA * DVA),
                                       pl.BlockSpec((1, DVA), lambda bi, i: (0, 0))],
        out_specs=[big(HA * DVA), pl.BlockSpec((1, HA, DKA, DVA), lambda bi, i: (bi, 0, 0, 0))],
        out_shape=[jax.ShapeDtypeStruct((b, l, HA * DVA), f32),
                   jax.ShapeDtypeStruct((b, HA, DKA, DVA), f32)],
        compiler_params=_cp("parallel", "arbitrary"),
        name="gdn_scan",
    )(u, w, qd, kd, qk, eg, gate, norm_w.reshape(1, DVA))


def _gdn_prompt(x, w_in, conv_w, a_log, dt_bias, norm_w):
    b, l, d = x.shape
    wq = w_in[:, :QKV_A].astype(bf16)
    wg = w_in[:, QKV_A:QKV_A + HA * DVA].astype(bf16)
    wab = w_in[:, QKV_A + HA * DVA:].astype(bf16)
    qkv, gate, ab = _proj(x.reshape(b * l, d), [wq, wg, wab], 256)
    qkv = qkv.reshape(b, l, QKV_A)
    u, w, qd, kd, qk, eg = _gdn_prep(qkv, ab.reshape(b, l, 2 * HA), conv_w,
                                     a_log.reshape(1, HA), dt_bias.reshape(1, HA))
    y, s = _gdn_scan(u, w, qd, kd, qk, eg, gate.reshape(b, l, HA * DVA), norm_w)
    return y.reshape(b * l, HA * DVA), s, qkv[:, l - (CONV - 1):]


GDN_BT = 8


def _gdn_step_body(qkv_ref, gate_ref, ab_ref, buf_ref, s_ref, cw_ref, alog_ref, dtb_ref, nw_ref,
                   y_ref, s_out_ref, buf_out_ref, q_scr, k_scr, v_scr, eg_scr, beta_scr):
    n = QKV_A
    new = qkv_ref[...]
    cw = cw_ref[...]
    y = new * cw[CONV - 1:CONV, :]
    for j in range(CONV - 1):
        y = y + buf_ref[:, j * n:(j + 1) * n] * cw[j:j + 1, :]
    y = _silu(y)
    buf_out_ref[:, 0:2 * n] = buf_ref[:, n:3 * n]
    buf_out_ref[:, 2 * n:3 * n] = new
    for h in range(HA):
        q_scr[:, h * DKA:(h + 1) * DKA] = _l2norm(y[:, h * DKA:(h + 1) * DKA]) * (DKA ** -0.5)
        k_scr[:, h * DKA:(h + 1) * DKA] = _l2norm(y[:, HA * DKA + h * DKA:HA * DKA + (h + 1) * DKA])
    v_scr[...] = y[:, 2 * HA * DKA:]
    g, beta = _gdn_gates(ab_ref[...], alog_ref[...], dtb_ref[...])
    eg_scr[...] = jnp.exp(g)
    beta_scr[...] = beta
    eye = _eye(DKA)
    row8 = lax.broadcasted_iota(jnp.int32, (8, DKA), 0)
    nw = nw_ref[...]

    for bi in range(qkv_ref.shape[0]):
        r = slice(bi, bi + 1)
        for h in range(HA):
            hs = slice(h * DKA, (h + 1) * DKA)
            k_row = k_scr[r, hs]
            q_row = q_scr[r, hs]
            kq = jnp.where(row8 == 0, k_row, jnp.where(row8 == 1, q_row, 0.0))
            cols = _dot_nt_hi(eye, kq)
            k_col = cols[:, 0:1]
            q_col = cols[:, 1:2]
            sd = s_ref[bi, h] * eg_scr[r, h:h + 1]
            pred = jnp.sum(k_col * sd, axis=0, keepdims=True)
            delta = beta_scr[r, h:h + 1] * (v_scr[r, hs] - pred)
            s_new = sd + k_col * delta
            s_out_ref[bi, h] = s_new
            o = jnp.sum(q_col * s_new, axis=0, keepdims=True)
            of = o * lax.rsqrt(jnp.mean(o * o, -1, keepdims=True) + 1e-6)
            y_ref[r, hs] = of * nw * _silu(gate_ref[r, hs])


def _gdn_step(qkv, gate, ab, conv_buf, s0, conv_w, a_log, dt_bias, norm_w):
    nb = qkv.shape[0]
    bt = GDN_BT
    rowblk = lambda n: pl.BlockSpec((bt, n), lambda i: (i, 0))
    full = lambda a: pl.BlockSpec(a.shape, lambda i: (0, 0))
    sblk = pl.BlockSpec((bt, HA, DKA, DVA), lambda i: (i, 0, 0, 0))
    nw = norm_w.reshape(1, DVA)
    al = a_log.reshape(1, HA)
    db = dt_bias.reshape(1, HA)
    return pl.pallas_call(
        _gdn_step_body,
        grid=(nb // bt,),
        in_specs=[rowblk(QKV_A), rowblk(HA * DVA), rowblk(2 * HA), rowblk(3 * QKV_A), sblk,
                  full(conv_w), full(al), full(db), full(nw)],
        out_specs=[rowblk(HA * DVA), sblk, rowblk(3 * QKV_A)],
        out_shape=[jax.ShapeDtypeStruct((nb, HA * DVA), f32),
                   jax.ShapeDtypeStruct(s0.shape, f32),
                   jax.ShapeDtypeStruct((nb, 3 * QKV_A), f32)],
        scratch_shapes=[pltpu.VMEM((bt, HA * DKA), f32)] * 3 + [pltpu.VMEM((bt, HA), f32)] * 2,
        compiler_params=_cp("parallel"),
        name="gdn_step",
    )(qkv, gate, ab, conv_buf.reshape(nb, 3 * QKV_A), s0, conv_w, al, db, nw)


def _gdn_sample(x, conv_buf, s0, w_in, conv_w, a_log, dt_bias, norm_w):
    nb = x.shape[0]
    wq = w_in[:, :QKV_A].astype(bf16)
    wg = w_in[:, QKV_A:QKV_A + HA * DVA].astype(bf16)
    wab = w_in[:, QKV_A + HA * DVA:].astype(bf16)
    qkv, gate, ab = _proj(x, [wq, wg, wab], nb)
    y, s, buf = _gdn_step(qkv, gate, ab, conv_buf, s0, conv_w, a_log, dt_bias, norm_w)
    return y, s, buf.reshape(nb, CONV - 1, QKV_A)


LRU_TILE = 256


def _gelu(x):
    return 0.5 * x * (1.0 + jnp.tanh(math.sqrt(2.0 / math.pi) * (x + 0.044715 * x * x * x)))


def _lru_gates(u, wa_ref, ba, wx_ref, bx, lam):
    ra, xa = [], []
    for n in range(LRU_BLOCKS):
        ub = u[:, n * LRU_BW:(n + 1) * LRU_BW].astype(bf16)
        ra.append(jnp.dot(ub, wa_ref[n], preferred_element_type=f32))
        xa.append(jnp.dot(ub, wx_ref[n], preferred_element_type=f32))
    r = _sigmoid(jnp.concatenate(ra, axis=-1) + ba)
    i_g = _sigmoid(jnp.concatenate(xa, axis=-1) + bx)
    log_a = -LRU_C * r * _softplus(-lam)
    a = jnp.exp(log_a)
    t = jnp.tanh(log_a)
    b = jnp.sqrt(-2.0 * t / (1.0 - t)) * (i_g * u)
    return a, b


def _lru_prompt_body(gate_ref, rec_ref, halo_ref, cw_ref, cb_ref, wa_ref, ba_ref, wx_ref, bx_ref, lam_ref,
                     y_ref, h_ref, xp_scr):
    i = pl.program_id(1)
    tt = rec_ref.shape[1]

    @pl.when(i == 0)
    def _():
        h_ref[...] = jnp.zeros_like(h_ref)

    xp_scr[0:8, :] = jnp.where(i > 0, halo_ref[0], 0.0)
    xp_scr[8:8 + tt, :] = rec_ref[0]
    cw = cw_ref[...]
    u = xp_scr[5:5 + tt, :] * cw[0:1, :]
    for j in range(1, CONV):
        u = u + xp_scr[5 + j:5 + j + tt, :] * cw[j:j + 1, :]
    u = u + cb_ref[...]
    a, b = _lru_gates(u, wa_ref, ba_ref[...], wx_ref, bx_ref[...], lam_ref[...])
    row = lax.broadcasted_iota(jnp.int32, a.shape, 0)
    s = 1
    while s < tt:
        keep = row >= s
        a_sh = jnp.where(keep, pltpu.roll(a, s, 0), 1.0)
        b_sh = jnp.where(keep, pltpu.roll(b, s, 0), 0.0)
        b = a * b_sh + b
        a = a * a_sh
        s *= 2
    h = b + a * h_ref[0]
    h_ref[0] = h[tt - 1:tt, :]
    y_ref[0] = _gelu(gate_ref[0]) * h


def _lru_prompt(x, w_in, conv_w, conv_b, w_a, b_a, w_x, b_x, lam):
    bsz, l, d = x.shape
    w = LRU_W
    gate_in, rec_in = _proj(x.reshape(bsz * l, d), [w_in[:, :w].astype(bf16), w_in[:, w:].astype(bf16)], 256)
    rec3 = rec_in.reshape(bsz, l, w)
    tt = min(LRU_TILE, l)
    hb = tt // 8
    big = pl.BlockSpec((1, tt, w), lambda bi, i: (bi, i, 0))
    vec = pl.BlockSpec((1, w), lambda bi, i: (0, 0))
    blkw = pl.BlockSpec((LRU_BLOCKS, LRU_BW, LRU_BW), lambda bi, i: (0, 0, 0))
    y, h = pl.pallas_call(
        _lru_prompt_body,
        grid=(bsz, l // tt),
        in_specs=[big, big, pl.BlockSpec((1, 8, w), lambda bi, i: (bi, jnp.maximum(i * hb - 1, 0), 0)),
                  pl.BlockSpec((CONV, w), lambda bi, i: (0, 0)), vec, blkw, vec, blkw, vec, vec],
        out_specs=[big, pl.BlockSpec((1, 1, w), lambda bi, i: (bi, 0, 0))],
        out_shape=[jax.ShapeDtypeStruct((bsz, l, w), f32), jax.ShapeDtypeStruct((bsz, 1, w), f32)],
        scratch_shapes=[pltpu.VMEM((tt + 8, w), f32)],
        compiler_params=_cp("parallel", "arbitrary"),
        name="lru_prompt",
    )(gate_in.reshape(bsz, l, w), rec3, rec3, conv_w, conv_b.reshape(1, w), w_a.astype(bf16), b_a.reshape(1, w),
      w_x.astype(bf16), b_x.reshape(1, w), lam.reshape(1, w))
    return y.reshape(bsz * l, w), h.reshape(bsz, w), rec3[:, l - (CONV - 1):]


def _lru_step_body(gate_ref, rec_ref, buf_ref, h0_ref, cw_ref, cb_ref, wa_ref, ba_ref, wx_ref, bx_ref, lam_ref,
                   y_ref, h_ref, buf_out_ref):
    w = LRU_W
    new = rec_ref[...]
    cw = cw_ref[...]
    u = new * cw[CONV - 1:CONV, :]
    for j in range(CONV - 1):
        u = u + buf_ref[:, j * w:(j + 1) * w] * cw[j:j + 1, :]
    u = u + cb_ref[...]
    buf_out_ref[:, 0:2 * w] = buf_ref[:, w:3 * w]
    buf_out_ref[:, 2 * w:3 * w] = new
    a, b = _lru_gates(u, wa_ref, ba_ref[...], wx_ref, bx_ref[...], lam_ref[...])
    h = b + a * h0_ref[...]
    h_ref[...] = h
    y_ref[...] = _gelu(gate_ref[...]) * h


def _lru_sample(x, conv_buf, h0, w_in, conv_w, conv_b, w_a, b_a, w_x, b_x, lam):
    nb = x.shape[0]
    w = LRU_W
    gate_in, rec_in = _proj(x, [w_in[:, :w].astype(bf16), w_in[:, w:].astype(bf16)], nb)
    y, h, buf = pl.pallas_call(
        _lru_step_body,
        out_shape=[jax.ShapeDtypeStruct((nb, w), f32), jax.ShapeDtypeStruct((nb, w), f32),
                   jax.ShapeDtypeStruct((nb, 3 * w), f32)],
        compiler_params=pltpu.CompilerParams(vmem_limit_bytes=VMEM_LIMIT),
        name="lru_step",
    )(gate_in, rec_in, conv_buf.reshape(nb, 3 * w), h0, conv_w, conv_b.reshape(1, w), w_a.astype(bf16),
      b_a.reshape(1, w), w_x.astype(bf16), b_x.reshape(1, w), lam.reshape(1, w))
    return y, h, buf.reshape(nb, CONV - 1, w)


NEG = -1e30


def _t5_bucket(rel):
    n = jnp.maximum(rel, 0)
    max_exact = N_BUCKETS // 2
    large = max_exact + (jnp.log(jnp.maximum(n, 1).astype(f32) / max_exact)
                         / math.log(MAX_DIST / max_exact) * (N_BUCKETS - max_exact)).astype(jnp.int32)
    return jnp.where(n < max_exact, n, jnp.minimum(large, N_BUCKETS - 1))


def _swa_prompt_body(q_ref, kp_ref, kc_ref, vp_ref, vc_ref, bucket_ref, sinks_ref, rb_ref, o_ref, bias_scr):
    i = pl.program_id(1)
    w = WINDOW

    @pl.when(i == 0)
    def _():
        bucket = bucket_ref[...]
        for h in range(HQ):
            b = jnp.zeros((w, 2 * w), f32)
            for n in range(N_BUCKETS):
                b = jnp.where(bucket == n, rb_ref[n, h], b)
            bias_scr[h] = b

    ii = lax.broadcasted_iota(jnp.int32, (w, 2 * w), 0)
    jj = lax.broadcasted_iota(jnp.int32, (w, 2 * w), 1)
    rel = ii + w - jj
    valid = (rel >= 0) & (rel <= w) & ((i > 0) | (jj >= w))
    q = q_ref[0].astype(bf16)
    kp = kp_ref[0].astype(bf16)
    kc = kc_ref[0].astype(bf16)
    vp = vp_ref[0].astype(bf16)
    vc = vc_ref[0].astype(bf16)
    for h in range(HQ):
        kv = slice((h // GQ) * HD, (h // GQ + 1) * HD)
        qh = q[:, h * HD:(h + 1) * HD]
        s = jnp.concatenate([_dot_nt(qh, kp[:, kv]), _dot_nt(qh, kc[:, kv])], axis=-1) * (HD ** -0.5)
        s = jnp.where(valid, s + bias_scr[h], NEG)
        sink = sinks_ref[h]
        m = jnp.maximum(jnp.max(s, -1, keepdims=True), sink)
        e = jnp.exp(s - m)
        denom = jnp.sum(e, -1, keepdims=True) + jnp.exp(sink - m)
        p = (e / denom).astype(bf16)
        o_ref[0, :, h * HD:(h + 1) * HD] = (jnp.dot(p[:, :w], vp[:, kv], preferred_element_type=f32)
                                            + jnp.dot(p[:, w:], vc[:, kv], preferred_element_type=f32))


def _swa_prompt(x, w_in, sinks, rel_bias):
    b, l, d = x.shape
    nq = HQ * HD
    nk = HKV * HD
    q, k, v = _proj(x.reshape(b * l, d), [w_in[:, :nq].astype(bf16), w_in[:, nq:nq + nk].astype(bf16),
                                          w_in[:, nq + nk:].astype(bf16)], 256)
    q = q.reshape(b, l, nq)
    k = k.reshape(b, l, nk)
    v = v.reshape(b, l, nk)
    w = WINDOW
    rel = jnp.arange(w)[:, None] + w - jnp.arange(2 * w)[None, :]
    bucket = _t5_bucket(rel).astype(jnp.int32)
    cur = lambda n: pl.BlockSpec((1, w, n), lambda bi, i: (bi, i, 0))
    prev = lambda n: pl.BlockSpec((1, w, n), lambda bi, i: (bi, jnp.maximum(i - 1, 0), 0))
    smem = pl.BlockSpec(memory_space=pltpu.SMEM)
    o = pl.pallas_call(
        _swa_prompt_body,
        grid=(b, l // w),
        in_specs=[cur(nq), prev(nk), cur(nk), prev(nk), cur(nk),
                  pl.BlockSpec((w, 2 * w), lambda bi, i: (0, 0)), smem, smem],
        out_specs=cur(nq),
        out_shape=jax.ShapeDtypeStruct((b, l, nq), f32),
        scratch_shapes=[pltpu.VMEM((HQ, w, 2 * w), f32)],
        compiler_params=_cp("parallel", "arbitrary"),
        name="swa_prompt",
    )(q, k, k, v, v, bucket, sinks, rel_bias)
    return o.reshape(b * l, nq), k, v


SWA_BT = 8


def _head_mask():
    r = lax.broadcasted_iota(jnp.int32, (HQ, HKV * HD), 0)
    c = lax.broadcasted_iota(jnp.int32, (HQ, HKV * HD), 1)
    return (r // GQ) == (c // HD)


def _fold_heads(o, mask):
    o = jnp.where(mask, o, 0.0)
    acc = o[:, 0:HD]
    for c in range(1, HKV):
        acc = acc + o[:, c * HD:(c + 1) * HD]
    return acc


def _swa_step_body(q_ref, kn_ref, vn_ref, kb_ref, vb_ref, onehot_ref, rbt_ref, sinks_ref, o_ref, kb_out, vb_out):
    w = WINDOW
    mask = _head_mask()
    bias_all = _dot_hi(rbt_ref[...], onehot_ref[...])
    bias = bias_all[:, :w]
    bias_new = bias_all[:, w:w + 1]
    sink = sinks_ref[...]
    for bi in range(q_ref.shape[0]):
        qm = jnp.where(mask, jnp.concatenate([q_ref[bi]] * HKV, axis=-1), 0.0)
        kn = kn_ref[bi]
        vn = vn_ref[bi]
        s = _dot_nt(qm, kb_ref[bi]) * (HD ** -0.5) + bias
        s_new = jnp.sum(qm * kn, -1, keepdims=True) * (HD ** -0.5) + bias_new
        m = jnp.maximum(jnp.maximum(jnp.max(s, -1, keepdims=True), s_new), sink)
        e = jnp.exp(s - m)
        e_new = jnp.exp(s_new - m)
        denom = jnp.sum(e, -1, keepdims=True) + e_new + jnp.exp(sink - m)
        o = (_dot(e / denom, vb_ref[bi]) + (e_new / denom) * vn)
        o_ref[bi] = _fold_heads(o, mask)
        kb_out[bi, 0:w - 1, :] = kb_ref[bi, 1:w, :]
        kb_out[bi, w - 1:w, :] = kn
        vb_out[bi, 0:w - 1, :] = vb_ref[bi, 1:w, :]
        vb_out[bi, w - 1:w, :] = vn


def _swa_sample(x, k_buf, v_buf, w_in, sinks, rel_bias):
    nb = x.shape[0]
    nq = HQ * HD
    nk = HKV * HD
    w = WINDOW
    q, k, v = _proj(x, [w_in[:, :nq].astype(bf16), w_in[:, nq:nq + nk].astype(bf16), w_in[:, nq + nk:].astype(bf16)], nb)
    slots = jnp.arange(w + 128)
    bucket = _t5_bucket(jnp.where(slots <= w, w - slots, 0))
    onehot = (bucket[None, :] == jnp.arange(N_BUCKETS)[:, None]).astype(f32)
    bt = SWA_BT
    full = lambda a: pl.BlockSpec(a.shape, lambda i: (0,) * a.ndim)
    bufblk = pl.BlockSpec((bt, w, nk), lambda i: (i, 0, 0))
    rbt = rel_bias.T
    sk = sinks.reshape(HQ, 1)
    o, kb, vb = pl.pallas_call(
        _swa_step_body,
        grid=(nb // bt,),
        in_specs=[pl.BlockSpec((bt, HQ, HD), lambda i: (i, 0, 0)),
                  pl.BlockSpec((bt, 1, nk), lambda i: (i, 0, 0)), pl.BlockSpec((bt, 1, nk), lambda i: (i, 0, 0)),
                  bufblk, bufblk, full(onehot), full(rbt), full(sk)],
        out_specs=[pl.BlockSpec((bt, HQ, HD), lambda i: (i, 0, 0)), bufblk, bufblk],
        out_shape=[jax.ShapeDtypeStruct((nb, HQ, HD), f32), jax.ShapeDtypeStruct((nb, w, nk), f32),
                   jax.ShapeDtypeStruct((nb, w, nk), f32)],
        compiler_params=_cp("parallel"),
        name="swa_step",
    )(q.reshape(nb, HQ, HD), k.reshape(nb, 1, nk), v.reshape(nb, 1, nk), k_buf.reshape(nb, w, nk),
      v_buf.reshape(nb, w, nk), onehot, rbt, sk)
    return o.reshape(nb, nq), k, v, kb.reshape(nb, w, HKV, HD), vb.reshape(nb, w, HKV, HD)


FOX_PREP_TILE = 512
FOX_TQ = 512
FOX_TK = 1024


def _fox_prep_body(x_ref, wft_ref, bf_ref, lf_ref, cum_ref, carry):
    i = pl.program_id(1)
    tt = x_ref.shape[1]

    @pl.when(i == 0)
    def _():
        carry[...] = jnp.zeros_like(carry)

    f = lax.dot_general(wft_ref[...], x_ref[0].astype(bf16), (((1,), (1,)), ((), ())), preferred_element_type=f32)
    lf = _log_sigmoid(f + bf_ref[...])
    r = lax.broadcasted_iota(jnp.int32, (tt, tt), 0)
    c = lax.broadcasted_iota(jnp.int32, (tt, tt), 1)
    cum = _dot_hi(lf, (r <= c).astype(f32)) + carry[...]
    lf_ref[0] = lf
    cum_ref[0] = cum
    carry[...] = cum[:, tt - 1:tt]


def _fox_flash_body(qt_ref, k_ref, vt_ref, fk_ref, o_ref, m_scr, l_scr, acc_scr):
    i = pl.program_id(2)
    tq = qt_ref.shape[3]
    tk = FOX_TK
    m_scr[...] = jnp.full_like(m_scr, NEG)
    l_scr[...] = jnp.zeros_like(l_scr)
    acc_scr[...] = jnp.zeros_like(acc_scr)

    def block(j, masked):
        c0 = pl.multiple_of(j * tk, tk)
        k = k_ref[0, 0, pl.ds(c0, tk), :]
        vt = vt_ref[0, 0, :, pl.ds(c0, tk)]
        fk = fk_ref[0, 0, pl.ds(c0, tk), :]
        if masked:
            s_pos = c0 + lax.broadcasted_iota(jnp.int32, (tk, tq), 0)
            t_pos = i * tq + lax.broadcasted_iota(jnp.int32, (tk, tq), 1)
            visible = s_pos <= t_pos
        for g in range(GQ):
            s = jnp.dot(k, qt_ref[0, g], preferred_element_type=f32) - fk[:, g:g + 1]
            if masked:
                s = jnp.where(visible, s, NEG)
            m_old = m_scr[g]
            m_new = jnp.maximum(m_old, jnp.max(s, 0, keepdims=True))
            alpha = jnp.exp(m_old - m_new)
            p = jnp.exp(s - m_new)
            l_scr[g] = alpha * l_scr[g] + jnp.sum(p, 0, keepdims=True)
            acc_scr[g] = alpha * acc_scr[g] + jnp.dot(vt, p.astype(bf16), preferred_element_type=f32)
            m_scr[g] = m_new

    n_full = (i * tq) // tk

    def full_block(j, carry):
        block(j, False)
        return carry

    lax.fori_loop(0, n_full, full_block, 0)
    for jj in range(pl.cdiv(tq, tk)):
        block(n_full + jj, True)
    for g in range(GQ):
        o_ref[0, g] = acc_scr[g] / l_scr[g]


def _fox_prompt(x, w_in, b_f):
    b, l, d = x.shape
    nq = HQ * HD
    nk = HKV * HD
    q, k, v = _proj(x.reshape(b * l, d), [w_in[:, :nq].astype(bf16), w_in[:, nq:nq + nk].astype(bf16),
                                          w_in[:, nq + nk:nq + 2 * nk].astype(bf16)], 256)
    tt = min(FOX_PREP_TILE, l)
    wft = w_in[:, nq + 2 * nk:].T.astype(bf16)
    row = pl.BlockSpec((1, HQ, tt), lambda bi, i: (bi, 0, i))
    lft, cumt = pl.pallas_call(
        _fox_prep_body,
        grid=(b, l // tt),
        in_specs=[pl.BlockSpec((1, tt, d), lambda bi, i: (bi, i, 0)),
                  pl.BlockSpec((HQ, d), lambda bi, i: (0, 0)), pl.BlockSpec((HQ, 1), lambda bi, i: (0, 0))],
        out_specs=[row, row],
        out_shape=[jax.ShapeDtypeStruct((b, HQ, l), f32)] * 2,
        scratch_shapes=[pltpu.VMEM((HQ, 1), f32)],
        compiler_params=_cp("parallel", "arbitrary"),
        name="fox_prep",
    )(x, wft, b_f.reshape(HQ, 1))
    fk = jnp.swapaxes(cumt.reshape(b, HKV, GQ, l), 2, 3)
    qt = jnp.transpose((q * (HD ** -0.5)).astype(bf16).reshape(b, l, HQ, HD), (0, 2, 3, 1))
    kh = jnp.swapaxes(k.reshape(b, l, HKV, HD), 1, 2).astype(bf16)
    vt = jnp.transpose(v.astype(bf16).reshape(b, l, HKV, HD), (0, 2, 3, 1))
    tq = min(FOX_TQ, l)
    qblk = pl.BlockSpec((1, GQ, HD, tq), lambda bi, h, i: (bi, h, 0, i))
    ot = pl.pallas_call(
        _fox_flash_body,
        grid=(b, HKV, l // tq),
        in_specs=[qblk,
                  pl.BlockSpec((1, 1, l, HD), lambda bi, h, i: (bi, h, 0, 0)),
                  pl.BlockSpec((1, 1, HD, l), lambda bi, h, i: (bi, h, 0, 0)),
                  pl.BlockSpec((1, 1, l, GQ), lambda bi, h, i: (bi, h, 0, 0))],
        out_specs=qblk,
        out_shape=jax.ShapeDtypeStruct((b, HQ, HD, l), f32),
        scratch_shapes=[pltpu.VMEM((GQ, 1, tq), f32), pltpu.VMEM((GQ, 1, tq), f32), pltpu.VMEM((GQ, HD, tq), f32)],
        compiler_params=_cp("parallel", "parallel", "arbitrary"),
        name="fox_flash",
    )(qt, kh, vt, fk)
    o = jnp.transpose(ot, (0, 3, 1, 2)).reshape(b * l, nq)
    return o, k.reshape(b, l, nk), v.reshape(b, l, nk), jnp.swapaxes(lft, 1, 2)


FOX_PAGES = 16


def _fox_step_body(pt_ref, q_ref, kn_ref, vn_ref, lfn_ref, *refs):
    npg = FOX_PAGES
    k_refs = refs[0:npg]
    v_refs = refs[npg:2 * npg]
    lf_refs = refs[2 * npg:3 * npg]
    o_ref, m_scr, l_scr, acc_scr, f_scr = refs[3 * npg:]
    j = pl.program_id(1)
    ps = PAGE_SIZE
    mask = _head_mask()
    qm = jnp.where(mask, jnp.concatenate([q_ref[0]] * HKV, axis=-1), 0.0) * (HD ** -0.5)

    @pl.when(j == 0)
    def _():
        m_scr[...] = jnp.full_like(m_scr, NEG)
        l_scr[...] = jnp.zeros_like(l_scr)
        acc_scr[...] = jnp.zeros_like(acc_scr)
        f_scr[...] = jnp.zeros_like(f_scr)

    r = lax.broadcasted_iota(jnp.int32, (ps, ps), 0)
    c = lax.broadcasted_iota(jnp.int32, (ps, ps), 1)
    upper = (r <= c).astype(f32)

    def update(scores, weighted_values, f_end):
        m_old = m_scr[...]
        m_new = m_old
        for s in scores:
            m_new = jnp.maximum(m_new, jnp.max(s, -1, keepdims=True))
        alpha = jnp.exp(m_old - m_new)
        l_new = alpha * l_scr[...]
        acc = alpha * acc_scr[...]
        for n, s in enumerate(scores):
            p = jnp.exp(s - m_new)
            l_new = l_new + jnp.sum(p, -1, keepdims=True)
            acc = acc + weighted_values(n, p)
        l_scr[...] = l_new
        acc_scr[...] = acc
        m_scr[...] = m_new
        f_scr[...] = f_end

    cum_all = _dot_hi(jnp.concatenate([lf_refs[pg][0] for pg in range(npg)], axis=0), upper)
    f_run = f_scr[...]
    scores = []
    for pg in range(npg):
        cum = cum_all[pg * HQ:(pg + 1) * HQ] + f_run
        kt = k_refs[pg][0].reshape(HKV * HD, ps)
        scores.append(_dot(qm, kt) - cum)
        f_run = cum[:, ps - 1:ps]
    update(scores, lambda n, p: _dot_nt(p, v_refs[n][0].reshape(HKV * HD, ps)), f_run)

    @pl.when(j == pl.num_programs(1) - 1)
    def _():
        kn = kn_ref[0]
        vn = vn_ref[0]
        cum = f_scr[...] + lfn_ref[0]
        s = jnp.sum(qm * kn, -1, keepdims=True) - cum
        update([s], lambda n, p: p * vn, cum)
        o_ref[0] = _fold_heads(acc_scr[...] / l_scr[...], mask)


def _logf_body(f_ref, b_ref, o_ref):
    o_ref[...] = _log_sigmoid(f_ref[...] + b_ref[...])


def _fox_sample(x, cache_k, cache_v, cache_logf, page_table, w_in, b_f):
    nb = x.shape[0]
    nq = HQ * HD
    nk = HKV * HD
    n_pool = cache_k.shape[0]
    n_pages = page_table.shape[1]
    npg = FOX_PAGES
    q, k, v, f = _proj(x, [w_in[:, :nq].astype(bf16), w_in[:, nq:nq + nk].astype(bf16),
                           w_in[:, nq + nk:nq + 2 * nk].astype(bf16), w_in[:, nq + 2 * nk:].astype(bf16)], nb)
    lf_new = pl.pallas_call(_logf_body, out_shape=jax.ShapeDtypeStruct((nb, HQ), f32))(f, b_f.reshape(1, HQ))

    def page(pg, *shape):
        return pl.BlockSpec((1,) + shape,
                            lambda bi, j, pt: (pt[bi * n_pages + j * npg + pg],) + (0,) * len(shape))

    tok = lambda *shape: pl.BlockSpec((1,) + shape, lambda bi, j, pt: (bi,) + (0,) * len(shape))
    ck = jnp.transpose(cache_k, (0, 2, 3, 1))
    cv = jnp.transpose(cache_v, (0, 2, 3, 1))
    clf = jnp.transpose(cache_logf, (0, 2, 1))
    o = pl.pallas_call(
        _fox_step_body,
        grid_spec=pltpu.PrefetchScalarGridSpec(
            num_scalar_prefetch=1,
            grid=(nb, n_pages // npg),
            in_specs=[tok(HQ, HD), tok(1, nk), tok(1, nk), tok(HQ, 1)]
            + [page(pg, HKV, HD, PAGE_SIZE) for pg in range(npg)]
            + [page(pg, HKV, HD, PAGE_SIZE) for pg in range(npg)]
            + [page(pg, HQ, PAGE_SIZE) for pg in range(npg)],
            out_specs=tok(HQ, HD),
            scratch_shapes=[pltpu.VMEM((HQ, 1), f32), pltpu.VMEM((HQ, 1), f32), pltpu.VMEM((HQ, nk), f32),
                            pltpu.VMEM((HQ, 1), f32)]),
        out_shape=jax.ShapeDtypeStruct((nb, HQ, HD), f32),
        compiler_params=_cp("parallel", "arbitrary"),
        name="fox_decode",
    )(page_table.reshape(-1), q.reshape(nb, HQ, HD), k.reshape(nb, 1, nk), v.reshape(nb, 1, nk),
      lf_new.reshape(nb, HQ, 1), *([ck] * npg), *([cv] * npg), *([clf] * npg))
    return o.reshape(nb, nq), k, v, lf_new


ROUTER_TILE = 512
MOE_TM = 1024


def _router_body(x_ref, rwt_ref, rb_ref, comb_ref):
    scores = _sigmoid(_dot_nt(rwt_ref[...], x_ref[...]))
    sel = scores + rb_ref[...]
    rows = [sel[e:e + 1, :] for e in range(N_EXPERTS)]
    srow = [scores[e:e + 1, :] for e in range(N_EXPERTS)]
    gs = []
    for g in range(N_GROUPS):
        r = rows[g * EPG:(g + 1) * EPG]
        best = None
        for a in range(EPG):
            for b in range(a + 1, EPG):
                pair = r[a] + r[b]
                best = pair if best is None else jnp.maximum(best, pair)
        gs.append(best)
    g_best = gs[0]
    g_idx = jnp.zeros_like(gs[0], dtype=jnp.int32)
    for g in range(1, N_GROUPS):
        better = gs[g] > g_best
        g_best = jnp.where(better, gs[g], g_best)
        g_idx = jnp.where(better, g, g_idx)

    def in_group(vals, j):
        out = vals[j]
        for g in range(1, N_GROUPS):
            out = jnp.where(g_idx == g, vals[g * EPG + j], out)
        return out

    ig = [in_group(rows, j) for j in range(EPG)]
    sg = [in_group(srow, j) for j in range(EPG)]
    v1, i1, s1 = ig[0], jnp.zeros_like(g_idx), sg[0]
    for j in range(1, EPG):
        better = ig[j] > v1
        v1 = jnp.where(better, ig[j], v1)
        i1 = jnp.where(better, j, i1)
        s1 = jnp.where(better, sg[j], s1)
    v2 = jnp.full_like(v1, -jnp.inf)
    i2 = jnp.zeros_like(g_idx)
    s2 = jnp.zeros_like(s1)
    for j in range(EPG):
        better = (i1 != j) & (ig[j] > v2)
        v2 = jnp.where(better, ig[j], v2)
        i2 = jnp.where(better, j, i2)
        s2 = jnp.where(better, sg[j], s2)
    tot = s1 + s2
    e_row = lax.broadcasted_iota(jnp.int32, scores.shape, 0)
    comb_ref[...] = (jnp.where(e_row == g_idx * EPG + i1, s1 / tot, 0.0)
                     + jnp.where(e_row == g_idx * EPG + i2, s2 / tot, 0.0))


def _router(x, router_w, router_b):
    t, d = x.shape
    tt = min(ROUTER_TILE, t)
    comb_t = pl.pallas_call(
        _router_body,
        grid=(t // tt,),
        in_specs=[pl.BlockSpec((tt, d), lambda i: (i, 0)), pl.BlockSpec((N_EXPERTS, d), lambda i: (0, 0)),
                  pl.BlockSpec((N_EXPERTS, 1), lambda i: (0, 0))],
        out_specs=pl.BlockSpec((N_EXPERTS, tt), lambda i: (0, i)),
        out_shape=jax.ShapeDtypeStruct((N_EXPERTS, t), f32),
        compiler_params=_cp("parallel"),
        name="router",
    )(x, router_w.T, router_b.reshape(N_EXPERTS, 1))
    return comb_t.T


def _moe_dense_body(x_ref, comb_ref, wg_ref, wu_ref, wd_ref, g_ref, b_ref, o_ref, acc_scr):
    e = pl.program_id(1)

    @pl.when(e == 0)
    def _():
        acc_scr[...] = jnp.zeros_like(acc_scr)

    x = x_ref[...].astype(bf16)
    h = _silu(jnp.dot(x, wg_ref[0].astype(bf16), preferred_element_type=f32)) \
        * jnp.dot(x, wu_ref[0].astype(bf16), preferred_element_type=f32)
    y = jnp.dot(h.astype(bf16), wd_ref[0].astype(bf16), preferred_element_type=f32)
    lane = lax.broadcasted_iota(jnp.int32, comb_ref.shape, 1)
    c = jnp.sum(jnp.where(lane == e, comb_ref[...], 0.0), -1, keepdims=True)
    acc_scr[...] += c * y

    @pl.when(e == pl.num_programs(1) - 1)
    def _():
        o_ref[...] = _layer_norm(DN_ALPHA * x_ref[...] + acc_scr[...], g_ref[...], b_ref[...])


def _moe_ln(x, router_w, router_b, w_gate, w_up, w_down, g, b):
    t, d = x.shape
    comb = _router(x, router_w, router_b)
    tm = min(MOE_TM, t)
    return pl.pallas_call(
        _moe_dense_body,
        grid=(t // tm, N_EXPERTS),
        in_specs=[pl.BlockSpec((tm, d), lambda i, e: (i, 0)), pl.BlockSpec((tm, N_EXPERTS), lambda i, e: (i, 0)),
                  pl.BlockSpec((1, d, D_EXPERT), lambda i, e: (e, 0, 0)),
                  pl.BlockSpec((1, d, D_EXPERT), lambda i, e: (e, 0, 0)),
                  pl.BlockSpec((1, D_EXPERT, d), lambda i, e: (e, 0, 0)),
                  pl.BlockSpec((1, d), lambda i, e: (0, 0)), pl.BlockSpec((1, d), lambda i, e: (0, 0))],
        out_specs=pl.BlockSpec((tm, d), lambda i, e: (i, 0)),
        out_shape=jax.ShapeDtypeStruct((t, d), f32),
        scratch_shapes=[pltpu.VMEM((tm, d), f32)],
        compiler_params=_cp("parallel", "arbitrary"),
        name="moe_dense",
    )(x, comb, w_gate, w_up, w_down, g.reshape(1, d), b.reshape(1, d))


def kernel(x_prompt, x_sample, state_a_ssm, state_a_conv, state_b_h, state_b_conv, cache_c_k, cache_c_v, cache_d_k, cache_d_v, cache_d_logf, page_table, ln_g, ln_b, a_w_in, a_conv_w, a_a_log, a_dt_bias, a_norm_w, a_w_out, b_w_in, b_conv_w, b_conv_b, b_w_a, b_b_a, b_w_x, b_b_x, b_lambda, b_w_out, c_w_in, c_sinks, c_w_out, rel_bias, d_w_in, d_b_f, d_w_out, router_w, router_b, moe_w_gate, moe_w_up, moe_w_down):
    bp, lp, d = x_prompt.shape
    nb = x_sample.shape[0]
    xp = x_prompt.reshape(bp * lp, d)
    xs = x_sample.reshape(nb, d)
    tm_p = 256

    def finish(xp, xs, mp, ms, w_out, i):
        w = w_out.astype(bf16)
        xp = _proj_ln(mp, w, xp, ln_g[i, 0], ln_b[i, 0], tm_p)
        xs = _proj_ln(ms, w, xs, ln_g[i, 0], ln_b[i, 0], nb)
        moe_args = (router_w, router_b, moe_w_gate[i], moe_w_up[i], moe_w_down[i], ln_g[i, 1], ln_b[i, 1])
        return _moe_ln(xp, *moe_args), _moe_ln(xs, *moe_args)

    a_args = (a_w_in[0], a_conv_w[0], a_a_log[0], a_dt_bias[0], a_norm_w[0])
    mp, p_a_ssm, p_a_conv = _gdn_prompt(xp.reshape(bp, lp, d), *a_args)
    ms, s_a_ssm, s_a_conv = _gdn_sample(xs, state_a_conv[0], state_a_ssm[0], *a_args)
    xp, xs = finish(xp, xs, mp, ms, a_w_out[0], 0)

    b_args = (b_w_in[0], b_conv_w[0], b_conv_b[0], b_w_a[0], b_b_a[0], b_w_x[0], b_b_x[0], b_lambda[0])
    mp, p_b_h, p_b_conv = _lru_prompt(xp.reshape(bp, lp, d), *b_args)
    ms, s_b_h, s_b_conv = _lru_sample(xs, state_b_conv[0], state_b_h[0], *b_args)
    xp, xs = finish(xp, xs, mp, ms, b_w_out[0], 1)

    mp, kp, vp = _swa_prompt(xp.reshape(bp, lp, d), c_w_in[0], c_sinks[0], rel_bias)
    ms, _, _, s_c_k, s_c_v = _swa_sample(xs, cache_c_k[0], cache_c_v[0], c_w_in[0], c_sinks[0], rel_bias)
    p_c_k = kp[:, lp - WINDOW:].reshape(bp, WINDOW, HKV, HD)
    p_c_v = vp[:, lp - WINDOW:].reshape(bp, WINDOW, HKV, HD)
    xp, xs = finish(xp, xs, mp, ms, c_w_out[0], 2)

    mp, kp, vp, p_d_logf = _fox_prompt(xp.reshape(bp, lp, d), d_w_in[0], d_b_f[0])
    ms, ks, vs, lfs = _fox_sample(xs, cache_d_k[0], cache_d_v[0], cache_d_logf[0], page_table, d_w_in[0], d_b_f[0])
    xp, xs = finish(xp, xs, mp, ms, d_w_out[0], 3)

    return (xp.reshape(bp, lp, d), xs.reshape(nb, 1, d),
            p_a_ssm[None], p_a_conv[None], p_b_h[None], p_b_conv[None], p_c_k[None], p_c_v[None],
            kp.reshape(1, bp, lp, HKV, HD), vp.reshape(1, bp, lp, HKV, HD), p_d_logf[None],
            s_a_ssm[None], s_a_conv[None], s_b_h[None], s_b_conv[None], s_c_k[None], s_c_v[None],
            ks.reshape(1, nb, 1, HKV, HD), vs.reshape(1, nb, 1, HKV, HD), lfs.reshape(1, nb, 1, HQ))
```

```python
import functools
import math

import jax
import jax.numpy as jnp
import numpy as np
from jax import lax
from jax.experimental import pallas as pl
from jax.experimental.pallas import tpu as pltpu

f32 = jnp.float32
bf16 = jnp.bfloat16
HI = lax.Precision.HIGHEST

D_MODEL = 1024
DEPTH = 4
DN_ALPHA = (2 * DEPTH) ** 0.25
LN_EPS = 1e-5
CONV = 4
HA = 8
DKA = 128
DVA = 128
QKV_A = HA * (2 * DKA + DVA)
CHUNK_A = 64
LRU_W = 1024
LRU_BLOCKS = 8
LRU_BW = 128
LRU_C = 8.0
HD = 64
HQ = 16
HKV = 4
GQ = HQ // HKV
WINDOW = 128
N_BUCKETS = 32
MAX_DIST = 128
PAGE_SIZE = 128
N_EXPERTS = 16
N_GROUPS = 4
EPG = 4
D_EXPERT = 512

VMEM_LIMIT = 56 * 1024 * 1024


def _cp(*sem):
    return pltpu.CompilerParams(dimension_semantics=sem, vmem_limit_bytes=VMEM_LIMIT)


def _dot(a, b):
    return jnp.dot(a.astype(bf16), b.astype(bf16), preferred_element_type=f32)


def _dot_nt(a, b):
    return lax.dot_general(a.astype(bf16), b.astype(bf16), (((1,), (1,)), ((), ())), preferred_element_type=f32)


def _dot_tn(a, b):
    return lax.dot_general(a.astype(bf16), b.astype(bf16), (((0,), (0,)), ((), ())), preferred_element_type=f32)


def _dot_hi(a, b):
    return jnp.dot(a, b, precision=HI, preferred_element_type=f32)


def _split(a):
    hi = a.astype(bf16)
    return hi, (a - hi.astype(f32)).astype(bf16)


def _dot_split(a, b):
    a_hi, a_lo = _split(a)
    b_hi, b_lo = _split(b)
    d = lambda u, v: jnp.dot(u, v, preferred_element_type=f32)
    return d(a_hi, b_hi) + (d(a_hi, b_lo) + d(a_lo, b_hi))


def _bdot_split(a, b):
    a_hi, a_lo = _split(a)
    b_hi, b_lo = _split(b)
    d = lambda u, v: jnp.einsum('hij,hjk->hik', u, v, preferred_element_type=f32)
    return d(a_hi, b_hi) + (d(a_hi, b_lo) + d(a_lo, b_hi))


def _bdot_nt(a, b):
    return jnp.einsum('hid,hjd->hij', a.astype(bf16), b.astype(bf16), preferred_element_type=f32)


def _dot_nt_hi(a, b):
    return lax.dot_general(a, b, (((1,), (1,)), ((), ())), precision=HI, preferred_element_type=f32)


def _sigmoid(x):
    return 1.0 / (1.0 + jnp.exp(-x))


def _silu(x):
    return x * _sigmoid(x)


def _softplus(x):
    return jnp.maximum(x, 0.0) + jnp.log1p(jnp.exp(-jnp.abs(x)))


def _log_sigmoid(x):
    return -_softplus(-x)


def _eye(n):
    r = lax.broadcasted_iota(jnp.int32, (n, n), 0)
    c = lax.broadcasted_iota(jnp.int32, (n, n), 1)
    return (r == c).astype(f32)


def _proj_body(n_out, x_ref, *refs):
    x = x_ref[...].astype(bf16)
    for w_ref, o_ref in zip(refs[:n_out], refs[n_out:]):
        o_ref[...] = jnp.dot(x, w_ref[...], preferred_element_type=f32)


def _proj(x, ws, tm):
    m, k = x.shape
    n_out = len(ws)
    return pl.pallas_call(
        functools.partial(_proj_body, n_out),
        grid=(m // tm,),
        in_specs=[pl.BlockSpec((tm, k), lambda i: (i, 0))]
        + [pl.BlockSpec(w.shape, lambda i: (0, 0)) for w in ws],
        out_specs=[pl.BlockSpec((tm, w.shape[1]), lambda i: (i, 0)) for w in ws],
        out_shape=[jax.ShapeDtypeStruct((m, w.shape[1]), f32) for w in ws],
        compiler_params=_cp("parallel"),
        name="proj",
    )(x, *ws)


def _layer_norm(z, g, b):
    mu = jnp.mean(z, -1, keepdims=True)
    zc = z - mu
    var = jnp.mean(zc * zc, -1, keepdims=True)
    return zc * lax.rsqrt(var + LN_EPS) * g + b


def _proj_ln_body(h_ref, w_ref, x_ref, g_ref, b_ref, o_ref):
    m = jnp.dot(h_ref[...].astype(bf16), w_ref[...], preferred_element_type=f32)
    o_ref[...] = _layer_norm(DN_ALPHA * x_ref[...] + m, g_ref[...], b_ref[...])


def _proj_ln(h, w, x, g, b, tm):
    m, k = h.shape
    d = x.shape[1]
    return pl.pallas_call(
        _proj_ln_body,
        grid=(m // tm,),
        in_specs=[pl.BlockSpec((tm, k), lambda i: (i, 0)),
                  pl.BlockSpec((k, d), lambda i: (0, 0)),
                  pl.BlockSpec((tm, d), lambda i: (i, 0)),
                  pl.BlockSpec((1, d), lambda i: (0, 0)),
                  pl.BlockSpec((1, d), lambda i: (0, 0))],
        out_specs=pl.BlockSpec((tm, d), lambda i: (i, 0)),
        out_shape=jax.ShapeDtypeStruct((m, d), f32),
        compiler_params=_cp("parallel"),
        name="proj_ln",
    )(h, w, x, g.reshape(1, d), b.reshape(1, d))


GDN_TILE = 256


def _gdn_gates(ab, alog, dtb):
    g = -jnp.exp(alog) * _softplus(ab[:, 0:HA] + dtb)
    beta = _sigmoid(ab[:, HA:2 * HA])
    return g, beta


def _l2norm(t):
    return t * lax.rsqrt(jnp.sum(t * t, -1, keepdims=True) + 1e-6)


def _gdn_prep_body(qkv_ref, halo_ref, ab_ref, cw_ref, alog_ref, dtb_ref,
                   u_ref, w_ref, qd_ref, kd_ref, qk_ref, eg_ref, xp_scr, y_scr):
    i = pl.program_id(1)
    tc = qkv_ref.shape[1]
    c = CHUNK_A
    xp_scr[0:8, :] = jnp.where(i > 0, halo_ref[0], 0.0)
    xp_scr[8:8 + tc, :] = qkv_ref[0]
    cw = cw_ref[...]
    y = xp_scr[5:5 + tc, :] * cw[0:1, :]
    for j in range(1, CONV):
        y = y + xp_scr[5 + j:5 + j + tc, :] * cw[j:j + 1, :]
    y_scr[...] = _silu(y)

    row = lax.broadcasted_iota(jnp.int32, (c, c), 0)
    col = lax.broadcasted_iota(jnp.int32, (c, c), 1)
    incl = row >= col
    strict = row > col
    tri = incl.astype(f32)
    eye_c = (row == col).astype(f32)
    eye_h = _eye(HA)
    alog = alog_ref[...]
    dtb = dtb_ref[...]

    def chunk(n, carry):
        r0 = pl.multiple_of(n * c, c)
        rows = pl.ds(r0, c)
        g, beta = _gdn_gates(ab_ref[0, rows, :], alog, dtb)
        gc = _dot_hi(tri, g)
        gct = _dot_nt_hi(eye_h, gc)
        egc = jnp.exp(gc)
        eg_ref[0, rows, :] = egc
        heads = lambda t, off: jnp.stack([t(y_scr[rows, off + h * DKA:off + (h + 1) * DKA]) for h in range(HA)])
        q = heads(_l2norm, 0) * (DKA ** -0.5)
        k = heads(_l2norm, HA * DKA)
        v = heads(lambda t: t, 2 * HA * DKA)
        col = lambda t: jnp.stack([t[:, h:h + 1] for h in range(HA)])
        gcol, bcol, egcol = col(gc), col(beta), col(egc)
        diff = gcol - gct[:, None, :]
        dec = jnp.where(incl, jnp.exp(jnp.where(incl, diff, 0.0)), 0.0)
        a = jnp.where(strict, _bdot_nt(k, k) * dec, 0.0) * bcol
        x = eye_c - a
        p = _bdot_split(a, a)
        for _ in range(4):
            xp = _bdot_split(jnp.concatenate([x, p], axis=1), p)
            x = x + xp[:, :c]
            p = xp[:, c:]
        x = x + _bdot_split(x, p)
        sol = _bdot_split(x, jnp.concatenate([v * bcol, k * (bcol * egcol)], axis=-1))
        qd = q * egcol
        kd = k * jnp.exp(jnp.stack([gc[c - 1:c, h:h + 1] for h in range(HA)]) - gcol)
        qk = _bdot_nt(q, k) * dec
        for h in range(HA):
            hs = slice(h * DVA, (h + 1) * DVA)
            u_ref[0, rows, hs] = sol[h, :, :DVA]
            w_ref[0, rows, hs] = sol[h, :, DVA:]
            qd_ref[0, rows, hs] = qd[h]
            kd_ref[0, rows, hs] = kd[h]
            qk_ref[0, rows, h * c:(h + 1) * c] = qk[h]
        return carry

    lax.fori_loop(0, tc // c, chunk, 0)


def _gdn_prep(qkv, ab, conv_w, a_log, dt_bias):
    b, l, _ = qkv.shape
    tc = min(GDN_TILE, l)
    hb = tc // 8
    big = lambda n: pl.BlockSpec((1, tc, n), lambda bi, i: (bi, i, 0))
    full = lambda a: pl.BlockSpec(a.shape, lambda bi, i: (0, 0))
    wide = jax.ShapeDtypeStruct((b, l, HA * DVA), f32)
    return pl.pallas_call(
        _gdn_prep_body,
        grid=(b, l // tc),
        in_specs=[big(QKV_A),
                  pl.BlockSpec((1, 8, QKV_A), lambda bi, i: (bi, jnp.maximum(i * hb - 1, 0), 0)),
                  big(2 * HA), full(conv_w), full(a_log), full(dt_bias)],
        out_specs=[big(HA * DVA)] * 4 + [big(HA * CHUNK_A), big(HA)],
        out_shape=[wide] * 4 + [jax.ShapeDtypeStruct((b, l, HA * CHUNK_A), f32),
                                jax.ShapeDtypeStruct((b, l, HA), f32)],
        scratch_shapes=[pltpu.VMEM((tc + 8, QKV_A), f32), pltpu.VMEM((tc, QKV_A), f32)],
        compiler_params=_cp("parallel", "parallel"),
        name="gdn_prep",
    )(qkv, qkv, ab, conv_w, a_log, dt_bias)


def _gdn_scan_body(u_ref, w_ref, qd_ref, kd_ref, qk_ref, eg_ref, gate_ref, nw_ref, y_ref, s_ref):
    i = pl.program_id(1)
    tc = u_ref.shape[1]
    c = CHUNK_A

    @pl.when(i == 0)
    def _():
        s_ref[...] = jnp.zeros_like(s_ref)

    nw = nw_ref[...]

    def chunk(n, carry):
        r0 = pl.multiple_of(n * c, c)
        rows = pl.ds(r0, c)
        eg_last = eg_ref[0, pl.ds(r0 + c - 1, 1), :]
        for h in range(HA):
            hs = slice(h * DVA, (h + 1) * DVA)
            s = s_ref[0, h]
            v_new = u_ref[0, rows, hs] - _dot(w_ref[0, rows, hs], s)
            o = _dot(qd_ref[0, rows, hs], s) + _dot(qk_ref[0, rows, h * c:(h + 1) * c], v_new)
            s_ref[0, h] = s * eg_last[:, h:h + 1] + _dot_tn(kd_ref[0, rows, hs], v_new)
            of = o * lax.rsqrt(jnp.mean(o * o, -1, keepdims=True) + 1e-6)
            y_ref[0, rows, hs] = of * nw * _silu(gate_ref[0, rows, hs])
        return carry

    lax.fori_loop(0, tc // c, chunk, 0)


def _gdn_scan(u, w, qd, kd, qk, eg, gate, norm_w):
    b, l, _ = u.shape
    tc = min(GDN_TILE, l)
    big = lambda n: pl.BlockSpec((1, tc, n), lambda bi, i: (bi, i, 0))
    return pl.pallas_call(
        _gdn_scan_body,
        grid=(b, l // tc),
        in_specs=[big(HA * DVA)] * 4 + [big(HA * CHUNK_A), big(HA), big(HA * DVA),
                                       pl.BlockSpec((1, DVA), lambda bi, i: (0, 0))],
        out_specs=[big(HA * DVA), pl.BlockSpec((1, HA, DKA, DVA), lambda bi, i: (bi, 0, 0, 0))],
        out_shape=[jax.ShapeDtypeStruct((b, l, HA * DVA), f32),
                   jax.ShapeDtypeStruct((b, HA, DKA, DVA), f32)],
        compiler_params=_cp("parallel", "arbitrary"),
        name="gdn_scan",
    )(u, w, qd, kd, qk, eg, gate, norm_w.reshape(1, DVA))


def _gdn_prompt(x, w_in, conv_w, a_log, dt_bias, norm_w):
    b, l, d = x.shape
    wq = w_in[:, :QKV_A].astype(bf16)
    wg = w_in[:, QKV_A:QKV_A + HA * DVA].astype(bf16)
    wab = w_in[:, QKV_A + HA * DVA:].astype(bf16)
    qkv, gate, ab = _proj(x.reshape(b * l, d), [wq, wg, wab], 256)
    qkv = qkv.reshape(b, l, QKV_A)
    u, w, qd, kd, qk, eg = _gdn_prep(qkv, ab.reshape(b, l, 2 * HA), conv_w,
                                     a_log.reshape(1, HA), dt_bias.reshape(1, HA))
    y, s = _gdn_scan(u, w, qd, kd, qk, eg, gate.reshape(b, l, HA * DVA), norm_w)
    return y.reshape(b * l, HA * DVA), s, qkv[:, l - (CONV - 1):]


GDN_BT = 8


def _gdn_step_body(qkv_ref, gate_ref, ab_ref, buf_ref, s_ref, cw_ref, alog_ref, dtb_ref, nw_ref,
                   y_ref, s_out_ref, buf_out_ref, q_scr, k_scr, v_scr, eg_scr, beta_scr):
    n = QKV_A
    new = qkv_ref[...]
    cw = cw_ref[...]
    y = new * cw[CONV - 1:CONV, :]
    for j in range(CONV - 1):
        y = y + buf_ref[:, j * n:(j + 1) * n] * cw[j:j + 1, :]
    y = _silu(y)
    buf_out_ref[:, 0:2 * n] = buf_ref[:, n:3 * n]
    buf_out_ref[:, 2 * n:3 * n] = new
    for h in range(HA):
        q_scr[:, h * DKA:(h + 1) * DKA] = _l2norm(y[:, h * DKA:(h + 1) * DKA]) * (DKA ** -0.5)
        k_scr[:, h * DKA:(h + 1) * DKA] = _l2norm(y[:, HA * DKA + h * DKA:HA * DKA + (h + 1) * DKA])
    v_scr[...] = y[:, 2 * HA * DKA:]
    g, beta = _gdn_gates(ab_ref[...], alog_ref[...], dtb_ref[...])
    eg_scr[...] = jnp.exp(g)
    beta_scr[...] = beta
    eye = _eye(DKA)
    row8 = lax.broadcasted_iota(jnp.int32, (8, DKA), 0)
    nw = nw_ref[...]

    for bi in range(qkv_ref.shape[0]):
        r = slice(bi, bi + 1)
        for h in range(HA):
            hs = slice(h * DKA, (h + 1) * DKA)
            k_row = k_scr[r, hs]
            q_row = q_scr[r, hs]
            kq = jnp.where(row8 == 0, k_row, jnp.where(row8 == 1, q_row, 0.0))
            cols = _dot_nt_hi(eye, kq)
            k_col = cols[:, 0:1]
            q_col = cols[:, 1:2]
            sd = s_ref[bi, h] * eg_scr[r, h:h + 1]
            pred = jnp.sum(k_col * sd, axis=0, keepdims=True)
            delta = beta_scr[r, h:h + 1] * (v_scr[r, hs] - pred)
            s_new = sd + k_col * delta
            s_out_ref[bi, h] = s_new
            o = jnp.sum(q_col * s_new, axis=0, keepdims=True)
            of = o * lax.rsqrt(jnp.mean(o * o, -1, keepdims=True) + 1e-6)
            y_ref[r, hs] = of * nw * _silu(gate_ref[r, hs])


def _gdn_step(qkv, gate, ab, conv_buf, s0, conv_w, a_log, dt_bias, norm_w):
    nb = qkv.shape[0]
    bt = GDN_BT
    rowblk = lambda n: pl.BlockSpec((bt, n), lambda i: (i, 0))
    full = lambda a: pl.BlockSpec(a.shape, lambda i: (0, 0))
    sblk = pl.BlockSpec((bt, HA, DKA, DVA), lambda i: (i, 0, 0, 0))
    nw = norm_w.reshape(1, DVA)
    al = a_log.reshape(1, HA)
    db = dt_bias.reshape(1, HA)
    return pl.pallas_call(
        _gdn_step_body,
        grid=(nb // bt,),
        in_specs=[rowblk(QKV_A), rowblk(HA * DVA), rowblk(2 * HA), rowblk(3 * QKV_A), sblk,
                  full(conv_w), full(al), full(db), full(nw)],
        out_specs=[rowblk(HA * DVA), sblk, rowblk(3 * QKV_A)],
        out_shape=[jax.ShapeDtypeStruct((nb, HA * DVA), f32),
                   jax.ShapeDtypeStruct(s0.shape, f32),
                   jax.ShapeDtypeStruct((nb, 3 * QKV_A), f32)],
        scratch_shapes=[pltpu.VMEM((bt, HA * DKA), f32)] * 3 + [pltpu.VMEM((bt, HA), f32)] * 2,
        compiler_params=_cp("parallel"),
        name="gdn_step",
    )(qkv, gate, ab, conv_buf.reshape(nb, 3 * QKV_A), s0, conv_w, al, db, nw)


def _gdn_sample(x, conv_buf, s0, w_in, conv_w, a_log, dt_bias, norm_w):
    nb = x.shape[0]
    wq = w_in[:, :QKV_A].astype(bf16)
    wg = w_in[:, QKV_A:QKV_A + HA * DVA].astype(bf16)
    wab = w_in[:, QKV_A + HA * DVA:].astype(bf16)
    qkv, gate, ab = _proj(x, [wq, wg, wab], nb)
    y, s, buf = _gdn_step(qkv, gate, ab, conv_buf, s0, conv_w, a_log, dt_bias, norm_w)
    return y, s, buf.reshape(nb, CONV - 1, QKV_A)


LRU_TILE = 256


def _gelu(x):
    return 0.5 * x * (1.0 + jnp.tanh(math.sqrt(2.0 / math.pi) * (x + 0.044715 * x * x * x)))


def _lru_gates(u, wa_ref, ba, wx_ref, bx, lam):
    ra, xa = [], []
    for n in range(LRU_BLOCKS):
        ub = u[:, n * LRU_BW:(n + 1) * LRU_BW].astype(bf16)
        ra.append(jnp.dot(ub, wa_ref[n], preferred_element_type=f32))
        xa.append(jnp.dot(ub, wx_ref[n], preferred_element_type=f32))
    r = _sigmoid(jnp.concatenate(ra, axis=-1) + ba)
    i_g = _sigmoid(jnp.concatenate(xa, axis=-1) + bx)
    log_a = -LRU_C * r * _softplus(-lam)
    a = jnp.exp(log_a)
    t = jnp.tanh(log_a)
    b = jnp.sqrt(-2.0 * t / (1.0 - t)) * (i_g * u)
    return a, b


def _lru_prompt_body(gate_ref, rec_ref, halo_ref, cw_ref, cb_ref, wa_ref, ba_ref, wx_ref, bx_ref, lam_ref,
                     y_ref, h_ref, xp_scr):
    i = pl.program_id(1)
    tt = rec_ref.shape[1]

    @pl.when(i == 0)
    def _():
        h_ref[...] = jnp.zeros_like(h_ref)

    xp_scr[0:8, :] = jnp.where(i > 0, halo_ref[0], 0.0)
    xp_scr[8:8 + tt, :] = rec_ref[0]
    cw = cw_ref[...]
    u = xp_scr[5:5 + tt, :] * cw[0:1, :]
    for j in range(1, CONV):
        u = u + xp_scr[5 + j:5 + j + tt, :] * cw[j:j + 1, :]
    u = u + cb_ref[...]
    a, b = _lru_gates(u, wa_ref, ba_ref[...], wx_ref, bx_ref[...], lam_ref[...])
    row = lax.broadcasted_iota(jnp.int32, a.shape, 0)
    s = 1
    while s < tt:
        keep = row >= s
        a_sh = jnp.where(keep, pltpu.roll(a, s, 0), 1.0)
        b_sh = jnp.where(keep, pltpu.roll(b, s, 0), 0.0)
        b = a * b_sh + b
        a = a * a_sh
        s *= 2
    h = b + a * h_ref[0]
    h_ref[0] = h[tt - 1:tt, :]
    y_ref[0] = _gelu(gate_ref[0]) * h


def _lru_prompt(x, w_in, conv_w, conv_b, w_a, b_a, w_x, b_x, lam):
    bsz, l, d = x.shape
    w = LRU_W
    gate_in, rec_in = _proj(x.reshape(bsz * l, d), [w_in[:, :w].astype(bf16), w_in[:, w:].astype(bf16)], 256)
    rec3 = rec_in.reshape(bsz, l, w)
    tt = min(LRU_TILE, l)
    hb = tt // 8
    big = pl.BlockSpec((1, tt, w), lambda bi, i: (bi, i, 0))
    vec = pl.BlockSpec((1, w), lambda bi, i: (0, 0))
    blkw = pl.BlockSpec((LRU_BLOCKS, LRU_BW, LRU_BW), lambda bi, i: (0, 0, 0))
    y, h = pl.pallas_call(
        _lru_prompt_body,
        grid=(bsz, l // tt),
        in_specs=[big, big, pl.BlockSpec((1, 8, w), lambda bi, i: (bi, jnp.maximum(i * hb - 1, 0), 0)),
                  pl.BlockSpec((CONV, w), lambda bi, i: (0, 0)), vec, blkw, vec, blkw, vec, vec],
        out_specs=[big, pl.BlockSpec((1, 1, w), lambda bi, i: (bi, 0, 0))],
        out_shape=[jax.ShapeDtypeStruct((bsz, l, w), f32), jax.ShapeDtypeStruct((bsz, 1, w), f32)],
        scratch_shapes=[pltpu.VMEM((tt + 8, w), f32)],
        compiler_params=_cp("parallel", "arbitrary"),
        name="lru_prompt",
    )(gate_in.reshape(bsz, l, w), rec3, rec3, conv_w, conv_b.reshape(1, w), w_a.astype(bf16), b_a.reshape(1, w),
      w_x.astype(bf16), b_x.reshape(1, w), lam.reshape(1, w))
    return y.reshape(bsz * l, w), h.reshape(bsz, w), rec3[:, l - (CONV - 1):]


def _lru_step_body(gate_ref, rec_ref, buf_ref, h0_ref, cw_ref, cb_ref, wa_ref, ba_ref, wx_ref, bx_ref, lam_ref,
                   y_ref, h_ref, buf_out_ref):
    w = LRU_W
    new = rec_ref[...]
    cw = cw_ref[...]
    u = new * cw[CONV - 1:CONV, :]
    for j in range(CONV - 1):
        u = u + buf_ref[:, j * w:(j + 1) * w] * cw[j:j + 1, :]
    u = u + cb_ref[...]
    buf_out_ref[:, 0:2 * w] = buf_ref[:, w:3 * w]
    buf_out_ref[:, 2 * w:3 * w] = new
    a, b = _lru_gates(u, wa_ref, ba_ref[...], wx_ref, bx_ref[...], lam_ref[...])
    h = b + a * h0_ref[...]
    h_ref[...] = h
    y_ref[...] = _gelu(gate_ref[...]) * h


def _lru_sample(x, conv_buf, h0, w_in, conv_w, conv_b, w_a, b_a, w_x, b_x, lam):
    nb = x.shape[0]
    w = LRU_W
    gate_in, rec_in = _proj(x, [w_in[:, :w].astype(bf16), w_in[:, w:].astype(bf16)], nb)
    y, h, buf = pl.pallas_call(
        _lru_step_body,
        out_shape=[jax.ShapeDtypeStruct((nb, w), f32), jax.ShapeDtypeStruct((nb, w), f32),
                   jax.ShapeDtypeStruct((nb, 3 * w), f32)],
        compiler_params=pltpu.CompilerParams(vmem_limit_bytes=VMEM_LIMIT),
        name="lru_step",
    )(gate_in, rec_in, conv_buf.reshape(nb, 3 * w), h0, conv_w, conv_b.reshape(1, w), w_a.astype(bf16),
      b_a.reshape(1, w), w_x.astype(bf16), b_x.reshape(1, w), lam.reshape(1, w))
    return y, h, buf.reshape(nb, CONV - 1, w)


NEG = -1e30


def _t5_bucket(rel):
    n = jnp.maximum(rel, 0)
    max_exact = N_BUCKETS // 2
    large = max_exact + (jnp.log(jnp.maximum(n, 1).astype(f32) / max_exact)
                         / math.log(MAX_DIST / max_exact) * (N_BUCKETS - max_exact)).astype(jnp.int32)
    return jnp.where(n < max_exact, n, jnp.minimum(large, N_BUCKETS - 1))


def _swa_prompt_body(qt_ref, kp_ref, kc_ref, vtp_ref, vtc_ref, bucket_ref, sinks_ref, rb_ref, o_ref, bias_scr):
    i = pl.program_id(1)
    w = WINDOW

    @pl.when(i == 0)
    def _():
        bucket = bucket_ref[...]
        rel = (lax.broadcasted_iota(jnp.int32, (2 * w, w), 1) + w
               - lax.broadcasted_iota(jnp.int32, (2 * w, w), 0))
        in_window = (rel >= 0) & (rel <= w)
        for h in range(HQ):
            b = jnp.zeros((2 * w, w), f32)
            for n in range(N_BUCKETS):
                b = jnp.where(bucket == n, rb_ref[n, h], b)
            bias_scr[h // GQ, :, (h % GQ) * w:(h % GQ + 1) * w] = jnp.where(in_window, b, NEG)

    hidden_rows = jnp.where(i == 0, w, 0)
    hide = lax.broadcasted_iota(jnp.int32, (2 * w, GQ * w), 0) < hidden_rows
    for kvh in range(HKV):
        heads = range(kvh * GQ, (kvh + 1) * GQ)
        qt = jnp.concatenate([qt_ref[0, h] for h in heads], axis=-1)
        kk = jnp.concatenate([kp_ref[0, kvh], kc_ref[0, kvh]], axis=0)
        s = jnp.where(hide, NEG, jnp.dot(kk, qt, preferred_element_type=f32) + bias_scr[kvh])
        sink = jnp.concatenate([jnp.full((1, w), sinks_ref[h], f32) for h in heads], axis=-1)
        m = jnp.maximum(jnp.max(s, 0, keepdims=True), sink)
        e = jnp.exp(s - m)
        inv = 1.0 / (jnp.sum(e, 0, keepdims=True) + jnp.exp(sink - m))
        vt = jnp.concatenate([vtp_ref[0, kvh], vtc_ref[0, kvh]], axis=-1)
        ot = jnp.dot(vt, (e * inv).astype(bf16), preferred_element_type=f32)
        for g, h in enumerate(heads):
            o_ref[0, h] = ot[:, g * w:(g + 1) * w]


def _swa_prompt(x, w_in, sinks, rel_bias):
    b, l, d = x.shape
    nq = HQ * HD
    nk = HKV * HD
    q, k, v = _proj(x.reshape(b * l, d), [w_in[:, :nq].astype(bf16), w_in[:, nq:nq + nk].astype(bf16),
                                          w_in[:, nq + nk:].astype(bf16)], 256)
    k = k.reshape(b, l, nk)
    v = v.reshape(b, l, nk)
    w = WINDOW
    rel = jnp.arange(w)[None, :] + w - jnp.arange(2 * w)[:, None]
    bucket = _t5_bucket(rel).astype(jnp.int32)
    qt = jnp.transpose((q * (HD ** -0.5)).astype(bf16).reshape(b, l, HQ, HD), (0, 2, 3, 1))
    kh = jnp.swapaxes(k.reshape(b, l, HKV, HD), 1, 2).astype(bf16)
    vt = jnp.transpose(v.astype(bf16).reshape(b, l, HKV, HD), (0, 2, 3, 1))
    qblk = pl.BlockSpec((1, HQ, HD, w), lambda bi, i: (bi, 0, 0, i))
    smem = pl.BlockSpec(memory_space=pltpu.SMEM)
    ot = pl.pallas_call(
        _swa_prompt_body,
        grid=(b, l // w),
        in_specs=[qblk,
                  pl.BlockSpec((1, HKV, w, HD), lambda bi, i: (bi, 0, jnp.maximum(i - 1, 0), 0)),
                  pl.BlockSpec((1, HKV, w, HD), lambda bi, i: (bi, 0, i, 0)),
                  pl.BlockSpec((1, HKV, HD, w), lambda bi, i: (bi, 0, 0, jnp.maximum(i - 1, 0))),
                  pl.BlockSpec((1, HKV, HD, w), lambda bi, i: (bi, 0, 0, i)),
                  pl.BlockSpec((2 * w, w), lambda bi, i: (0, 0)), smem, smem],
        out_specs=qblk,
        out_shape=jax.ShapeDtypeStruct((b, HQ, HD, l), f32),
        scratch_shapes=[pltpu.VMEM((HKV, 2 * w, GQ * w), f32)],
        compiler_params=_cp("parallel", "arbitrary"),
        name="swa_prompt",
    )(qt, kh, kh, vt, vt, bucket, sinks, rel_bias)
    o = jnp.transpose(ot, (0, 3, 1, 2)).reshape(b * l, nq)
    return o, k, v


SWA_BT = 8


def _head_mask():
    r = lax.broadcasted_iota(jnp.int32, (HQ, HKV * HD), 0)
    c = lax.broadcasted_iota(jnp.int32, (HQ, HKV * HD), 1)
    return (r // GQ) == (c // HD)


def _fold_heads(o, mask):
    o = jnp.where(mask, o, 0.0)
    acc = o[:, 0:HD]
    for c in range(1, HKV):
        acc = acc + o[:, c * HD:(c + 1) * HD]
    return acc


def _swa_step_body(q_ref, kn_ref, vn_ref, kb_ref, vb_ref, onehot_ref, rbt_ref, sinks_ref, o_ref, kb_out, vb_out):
    w = WINDOW
    mask = _head_mask()
    bias_all = _dot_hi(rbt_ref[...], onehot_ref[...])
    bias = bias_all[:, :w]
    bias_new = bias_all[:, w:w + 1]
    sink = sinks_ref[...]
    for bi in range(q_ref.shape[0]):
        qm = jnp.where(mask, jnp.concatenate([q_ref[bi]] * HKV, axis=-1), 0.0)
        kn = kn_ref[bi]
        vn = vn_ref[bi]
        s = _dot_nt(qm, kb_ref[bi]) * (HD ** -0.5) + bias
        s_new = jnp.sum(qm * kn, -1, keepdims=True) * (HD ** -0.5) + bias_new
        m = jnp.maximum(jnp.maximum(jnp.max(s, -1, keepdims=True), s_new), sink)
        e = jnp.exp(s - m)
        e_new = jnp.exp(s_new - m)
        denom = jnp.sum(e, -1, keepdims=True) + e_new + jnp.exp(sink - m)
        o = (_dot(e / denom, vb_ref[bi]) + (e_new / denom) * vn)
        o_ref[bi] = _fold_heads(o, mask)
        kb_out[bi, 0:w - 1, :] = kb_ref[bi, 1:w, :]
        kb_out[bi, w - 1:w, :] = kn
        vb_out[bi, 0:w - 1, :] = vb_ref[bi, 1:w, :]
        vb_out[bi, w - 1:w, :] = vn


def _swa_sample(x, k_buf, v_buf, w_in, sinks, rel_bias):
    nb = x.shape[0]
    nq = HQ * HD
    nk = HKV * HD
    w = WINDOW
    q, k, v = _proj(x, [w_in[:, :nq].astype(bf16), w_in[:, nq:nq + nk].astype(bf16), w_in[:, nq + nk:].astype(bf16)], nb)
    slots = jnp.arange(w + 128)
    bucket = _t5_bucket(jnp.where(slots <= w, w - slots, 0))
    onehot = (bucket[None, :] == jnp.arange(N_BUCKETS)[:, None]).astype(f32)
    bt = SWA_BT
    full = lambda a: pl.BlockSpec(a.shape, lambda i: (0,) * a.ndim)
    bufblk = pl.BlockSpec((bt, w, nk), lambda i: (i, 0, 0))
    rbt = rel_bias.T
    sk = sinks.reshape(HQ, 1)
    o, kb, vb = pl.pallas_call(
        _swa_step_body,
        grid=(nb // bt,),
        in_specs=[pl.BlockSpec((bt, HQ, HD), lambda i: (i, 0, 0)),
                  pl.BlockSpec((bt, 1, nk), lambda i: (i, 0, 0)), pl.BlockSpec((bt, 1, nk), lambda i: (i, 0, 0)),
                  bufblk, bufblk, full(onehot), full(rbt), full(sk)],
        out_specs=[pl.BlockSpec((bt, HQ, HD), lambda i: (i, 0, 0)), bufblk, bufblk],
        out_shape=[jax.ShapeDtypeStruct((nb, HQ, HD), f32), jax.ShapeDtypeStruct((nb, w, nk), f32),
                   jax.ShapeDtypeStruct((nb, w, nk), f32)],
        compiler_params=_cp("parallel"),
        name="swa_step",
    )(q.reshape(nb, HQ, HD), k.reshape(nb, 1, nk), v.reshape(nb, 1, nk), k_buf.reshape(nb, w, nk),
      v_buf.reshape(nb, w, nk), onehot, rbt, sk)
    return o.reshape(nb, nq), k, v, kb.reshape(nb, w, HKV, HD), vb.reshape(nb, w, HKV, HD)


FOX_PREP_TILE = 512
FOX_TQ = 512
FOX_TK = 1024


def _fox_prep_body(x_ref, wft_ref, bf_ref, lf_ref, cum_ref, carry):
    i = pl.program_id(1)
    tt = x_ref.shape[1]

    @pl.when(i == 0)
    def _():
        carry[...] = jnp.zeros_like(carry)

    f = lax.dot_general(wft_ref[...], x_ref[0].astype(bf16), (((1,), (1,)), ((), ())), preferred_element_type=f32)
    lf = _log_sigmoid(f + bf_ref[...])
    r = lax.broadcasted_iota(jnp.int32, (tt, tt), 0)
    c = lax.broadcasted_iota(jnp.int32, (tt, tt), 1)
    cum = _dot_hi(lf, (r <= c).astype(f32)) + carry[...]
    lf_ref[0] = lf
    cum_ref[0] = cum
    carry[...] = cum[:, tt - 1:tt]


def _fox_flash_body(qt_ref, k_ref, vt_ref, fk_ref, o_ref, m_scr, l_scr, acc_scr):
    i = pl.program_id(2)
    tq = qt_ref.shape[3]
    tk = FOX_TK
    m_scr[...] = jnp.full_like(m_scr, NEG)
    l_scr[...] = jnp.zeros_like(l_scr)
    acc_scr[...] = jnp.zeros_like(acc_scr)

    def block(j, masked):
        c0 = pl.multiple_of(j * tk, tk)
        k = k_ref[0, 0, pl.ds(c0, tk), :]
        vt = vt_ref[0, 0, :, pl.ds(c0, tk)]
        fk = fk_ref[0, 0, pl.ds(c0, tk), :]
        if masked:
            s_pos = c0 + lax.broadcasted_iota(jnp.int32, (tk, tq), 0)
            t_pos = i * tq + lax.broadcasted_iota(jnp.int32, (tk, tq), 1)
            visible = s_pos <= t_pos
        for g in range(GQ):
            s = jnp.dot(k, qt_ref[0, g], preferred_element_type=f32) - fk[:, g:g + 1]
            if masked:
                s = jnp.where(visible, s, NEG)
            m_old = m_scr[g]
            m_new = jnp.maximum(m_old, jnp.max(s, 0, keepdims=True))
            alpha = jnp.exp(m_old - m_new)
            p = jnp.exp(s - m_new)
            l_scr[g] = alpha * l_scr[g] + jnp.sum(p, 0, keepdims=True)
            acc_scr[g] = alpha * acc_scr[g] + jnp.dot(vt, p.astype(bf16), preferred_element_type=f32)
            m_scr[g] = m_new

    n_full = (i * tq) // tk

    def full_block(j, carry):
        block(j, False)
        return carry

    lax.fori_loop(0, n_full, full_block, 0)
    for jj in range(pl.cdiv(tq, tk)):
        block(n_full + jj, True)
    for g in range(GQ):
        o_ref[0, g] = acc_scr[g] / l_scr[g]


def _fox_prompt(x, w_in, b_f):
    b, l, d = x.shape
    nq = HQ * HD
    nk = HKV * HD
    q, k, v = _proj(x.reshape(b * l, d), [w_in[:, :nq].astype(bf16), w_in[:, nq:nq + nk].astype(bf16),
                                          w_in[:, nq + nk:nq + 2 * nk].astype(bf16)], 256)
    tt = min(FOX_PREP_TILE, l)
    wft = w_in[:, nq + 2 * nk:].T.astype(bf16)
    row = pl.BlockSpec((1, HQ, tt), lambda bi, i: (bi, 0, i))
    lft, cumt = pl.pallas_call(
        _fox_prep_body,
        grid=(b, l // tt),
        in_specs=[pl.BlockSpec((1, tt, d), lambda bi, i: (bi, i, 0)),
                  pl.BlockSpec((HQ, d), lambda bi, i: (0, 0)), pl.BlockSpec((HQ, 1), lambda bi, i: (0, 0))],
        out_specs=[row, row],
        out_shape=[jax.ShapeDtypeStruct((b, HQ, l), f32)] * 2,
        scratch_shapes=[pltpu.VMEM((HQ, 1), f32)],
        compiler_params=_cp("parallel", "arbitrary"),
        name="fox_prep",
    )(x, wft, b_f.reshape(HQ, 1))
    fk = jnp.swapaxes(cumt.reshape(b, HKV, GQ, l), 2, 3)
    qt = jnp.transpose((q * (HD ** -0.5)).astype(bf16).reshape(b, l, HQ, HD), (0, 2, 3, 1))
    kh = jnp.swapaxes(k.reshape(b, l, HKV, HD), 1, 2).astype(bf16)
    vt = jnp.transpose(v.astype(bf16).reshape(b, l, HKV, HD), (0, 2, 3, 1))
    tq = min(FOX_TQ, l)
    qblk = pl.BlockSpec((1, GQ, HD, tq), lambda bi, h, i: (bi, h, 0, i))
    ot = pl.pallas_call(
        _fox_flash_body,
        grid=(b, HKV, l // tq),
        in_specs=[qblk,
                  pl.BlockSpec((1, 1, l, HD), lambda bi, h, i: (bi, h, 0, 0)),
                  pl.BlockSpec((1, 1, HD, l), lambda bi, h, i: (bi, h, 0, 0)),
                  pl.BlockSpec((1, 1, l, GQ), lambda bi, h, i: (bi, h, 0, 0))],
        out_specs=qblk,
        out_shape=jax.ShapeDtypeStruct((b, HQ, HD, l), f32),
        scratch_shapes=[pltpu.VMEM((GQ, 1, tq), f32), pltpu.VMEM((GQ, 1, tq), f32), pltpu.VMEM((GQ, HD, tq), f32)],
        compiler_params=_cp("parallel", "parallel", "arbitrary"),
        name="fox_flash",
    )(qt, kh, vt, fk)
    o = jnp.transpose(ot, (0, 3, 1, 2)).reshape(b * l, nq)
    return o, k.reshape(b, l, nk), v.reshape(b, l, nk), jnp.swapaxes(lft, 1, 2)


FOX_PAGES = 16


def _fox_step_body(pt_ref, q_ref, kn_ref, vn_ref, lfn_ref, *refs):
    npg = FOX_PAGES
    k_refs = refs[0:npg]
    v_refs = refs[npg:2 * npg]
    lf_refs = refs[2 * npg:3 * npg]
    o_ref, m_scr, l_scr, acc_scr, f_scr = refs[3 * npg:]
    j = pl.program_id(1)
    ps = PAGE_SIZE
    mask = _head_mask()
    qm = jnp.where(mask, jnp.concatenate([q_ref[0]] * HKV, axis=-1), 0.0) * (HD ** -0.5)

    @pl.when(j == 0)
    def _():
        m_scr[...] = jnp.full_like(m_scr, NEG)
        l_scr[...] = jnp.zeros_like(l_scr)
        acc_scr[...] = jnp.zeros_like(acc_scr)
        f_scr[...] = jnp.zeros_like(f_scr)

    r = lax.broadcasted_iota(jnp.int32, (ps, ps), 0)
    c = lax.broadcasted_iota(jnp.int32, (ps, ps), 1)
    upper = (r <= c).astype(f32)

    def update(s, weighted_values, f_end):
        m_old = m_scr[...]
        m_new = jnp.maximum(m_old, jnp.max(s, -1, keepdims=True))
        alpha = jnp.exp(m_old - m_new)
        p = jnp.exp(s - m_new)
        l_scr[...] = alpha * l_scr[...] + jnp.sum(p, -1, keepdims=True)
        acc_scr[...] = alpha * acc_scr[...] + weighted_values(p)
        m_scr[...] = m_new
        f_scr[...] = f_end

    cum_all = _dot_hi(jnp.concatenate([lf_refs[pg][0] for pg in range(npg)], axis=0), upper)
    f_run = f_scr[...]
    cums = []
    for pg in range(npg):
        cums.append(cum_all[pg * HQ:(pg + 1) * HQ] + f_run)
        f_run = cums[-1][:, ps - 1:ps]
    kt = jnp.concatenate([k_refs[pg][0].reshape(HKV * HD, ps).astype(bf16) for pg in range(npg)], axis=-1)
    vt = jnp.concatenate([v_refs[pg][0].reshape(HKV * HD, ps).astype(bf16) for pg in range(npg)], axis=-1)
    s = jnp.dot(qm.astype(bf16), kt, preferred_element_type=f32) - jnp.concatenate(cums, axis=-1)
    update(s, lambda p: _dot_nt(p, vt), f_run)

    @pl.when(j == pl.num_programs(1) - 1)
    def _():
        kn = kn_ref[0]
        vn = vn_ref[0]
        cum = f_scr[...] + lfn_ref[0]
        s = jnp.sum(qm * kn, -1, keepdims=True) - cum
        update(s, lambda p: p * vn, cum)
        o_ref[0] = _fold_heads(acc_scr[...] / l_scr[...], mask)


def _logf_body(f_ref, b_ref, o_ref):
    o_ref[...] = _log_sigmoid(f_ref[...] + b_ref[...])


def _fox_sample(x, cache_k, cache_v, cache_logf, page_table, w_in, b_f):
    nb = x.shape[0]
    nq = HQ * HD
    nk = HKV * HD
    n_pool = cache_k.shape[0]
    n_pages = page_table.shape[1]
    npg = FOX_PAGES
    q, k, v, f = _proj(x, [w_in[:, :nq].astype(bf16), w_in[:, nq:nq + nk].astype(bf16),
                           w_in[:, nq + nk:nq + 2 * nk].astype(bf16), w_in[:, nq + 2 * nk:].astype(bf16)], nb)
    lf_new = pl.pallas_call(_logf_body, out_shape=jax.ShapeDtypeStruct((nb, HQ), f32))(f, b_f.reshape(1, HQ))

    def page(pg, *shape):
        return pl.BlockSpec((1,) + shape,
                            lambda bi, j, pt: (pt[bi * n_pages + j * npg + pg],) + (0,) * len(shape))

    tok = lambda *shape: pl.BlockSpec((1,) + shape, lambda bi, j, pt: (bi,) + (0,) * len(shape))
    ck = jnp.transpose(cache_k, (0, 2, 3, 1))
    cv = jnp.transpose(cache_v, (0, 2, 3, 1))
    clf = jnp.transpose(cache_logf, (0, 2, 1))
    o = pl.pallas_call(
        _fox_step_body,
        grid_spec=pltpu.PrefetchScalarGridSpec(
            num_scalar_prefetch=1,
            grid=(nb, n_pages // npg),
            in_specs=[tok(HQ, HD), tok(1, nk), tok(1, nk), tok(HQ, 1)]
            + [page(pg, HKV, HD, PAGE_SIZE) for pg in range(npg)]
            + [page(pg, HKV, HD, PAGE_SIZE) for pg in range(npg)]
            + [page(pg, HQ, PAGE_SIZE) for pg in range(npg)],
            out_specs=tok(HQ, HD),
            scratch_shapes=[pltpu.VMEM((HQ, 1), f32), pltpu.VMEM((HQ, 1), f32), pltpu.VMEM((HQ, nk), f32),
                            pltpu.VMEM((HQ, 1), f32)]),
        out_shape=jax.ShapeDtypeStruct((nb, HQ, HD), f32),
        compiler_params=_cp("parallel", "arbitrary"),
        name="fox_decode",
    )(page_table.reshape(-1), q.reshape(nb, HQ, HD), k.reshape(nb, 1, nk), v.reshape(nb, 1, nk),
      lf_new.reshape(nb, HQ, 1), *([ck] * npg), *([cv] * npg), *([clf] * npg))
    return o.reshape(nb, nq), k, v, lf_new


ROUTER_TILE = 512
MOE_TM = 1024


def _router_body(x_ref, rwt_ref, rb_ref, comb_ref):
    scores = _sigmoid(_dot_nt(rwt_ref[...], x_ref[...]))
    sel = scores + rb_ref[...]
    rows = [sel[e:e + 1, :] for e in range(N_EXPERTS)]
    srow = [scores[e:e + 1, :] for e in range(N_EXPERTS)]
    gs = []
    for g in range(N_GROUPS):
        r = rows[g * EPG:(g + 1) * EPG]
        best = None
        for a in range(EPG):
            for b in range(a + 1, EPG):
                pair = r[a] + r[b]
                best = pair if best is None else jnp.maximum(best, pair)
        gs.append(best)
    g_best = gs[0]
    g_idx = jnp.zeros_like(gs[0], dtype=jnp.int32)
    for g in range(1, N_GROUPS):
        better = gs[g] > g_best
        g_best = jnp.where(better, gs[g], g_best)
        g_idx = jnp.where(better, g, g_idx)

    def in_group(vals, j):
        out = vals[j]
        for g in range(1, N_GROUPS):
            out = jnp.where(g_idx == g, vals[g * EPG + j], out)
        return out

    ig = [in_group(rows, j) for j in range(EPG)]
    sg = [in_group(srow, j) for j in range(EPG)]
    v1, i1, s1 = ig[0], jnp.zeros_like(g_idx), sg[0]
    for j in range(1, EPG):
        better = ig[j] > v1
        v1 = jnp.where(better, ig[j], v1)
        i1 = jnp.where(better, j, i1)
        s1 = jnp.where(better, sg[j], s1)
    v2 = jnp.full_like(v1, -jnp.inf)
    i2 = jnp.zeros_like(g_idx)
    s2 = jnp.zeros_like(s1)
    for j in range(EPG):
        better = (i1 != j) & (ig[j] > v2)
        v2 = jnp.where(better, ig[j], v2)
        i2 = jnp.where(better, j, i2)
        s2 = jnp.where(better, sg[j], s2)
    tot = s1 + s2
    e_row = lax.broadcasted_iota(jnp.int32, scores.shape, 0)
    comb_ref[...] = (jnp.where(e_row == g_idx * EPG + i1, s1 / tot, 0.0)
                     + jnp.where(e_row == g_idx * EPG + i2, s2 / tot, 0.0))


def _router(x, router_w, router_b):
    t, d = x.shape
    tt = min(ROUTER_TILE, t)
    comb_t = pl.pallas_call(
        _router_body,
        grid=(t // tt,),
        in_specs=[pl.BlockSpec((tt, d), lambda i: (i, 0)), pl.BlockSpec((N_EXPERTS, d), lambda i: (0, 0)),
                  pl.BlockSpec((N_EXPERTS, 1), lambda i: (0, 0))],
        out_specs=pl.BlockSpec((N_EXPERTS, tt), lambda i: (0, i)),
        out_shape=jax.ShapeDtypeStruct((N_EXPERTS, t), f32),
        compiler_params=_cp("parallel"),
        name="router",
    )(x, router_w.T, router_b.reshape(N_EXPERTS, 1))
    return comb_t.T


def _moe_dense_body(x_ref, comb_ref, wg_ref, wu_ref, wd_ref, g_ref, b_ref, o_ref, acc_scr):
    e = pl.program_id(1)

    @pl.when(e == 0)
    def _():
        acc_scr[...] = jnp.zeros_like(acc_scr)

    x = x_ref[...].astype(bf16)
    h = _silu(jnp.dot(x, wg_ref[0].astype(bf16), preferred_element_type=f32)) \
        * jnp.dot(x, wu_ref[0].astype(bf16), preferred_element_type=f32)
    y = jnp.dot(h.astype(bf16), wd_ref[0].astype(bf16), preferred_element_type=f32)
    lane = lax.broadcasted_iota(jnp.int32, comb_ref.shape, 1)
    c = jnp.sum(jnp.where(lane == e, comb_ref[...], 0.0), -1, keepdims=True)
    acc_scr[...] += c * y

    @pl.when(e == pl.num_programs(1) - 1)
    def _():
        o_ref[...] = _layer_norm(DN_ALPHA * x_ref[...] + acc_scr[...], g_ref[...], b_ref[...])


def _moe_ln(x, router_w, router_b, w_gate, w_up, w_down, layer, g, b):
    t, d = x.shape
    comb = _router(x, router_w, router_b)
    tm = min(MOE_TM, t)
    return pl.pallas_call(
        _moe_dense_body,
        grid=(t // tm, N_EXPERTS),
        in_specs=[pl.BlockSpec((tm, d), lambda i, e: (i, 0)), pl.BlockSpec((tm, N_EXPERTS), lambda i, e: (i, 0)),
                  pl.BlockSpec((None, 1, d, D_EXPERT), lambda i, e: (layer, e, 0, 0)),
                  pl.BlockSpec((None, 1, d, D_EXPERT), lambda i, e: (layer, e, 0, 0)),
                  pl.BlockSpec((None, 1, D_EXPERT, d), lambda i, e: (layer, e, 0, 0)),
                  pl.BlockSpec((1, d), lambda i, e: (0, 0)), pl.BlockSpec((1, d), lambda i, e: (0, 0))],
        out_specs=pl.BlockSpec((tm, d), lambda i, e: (i, 0)),
        out_shape=jax.ShapeDtypeStruct((t, d), f32),
        scratch_shapes=[pltpu.VMEM((tm, d), f32)],
        compiler_params=_cp("parallel", "arbitrary"),
        name="moe_dense",
    )(x, comb, w_gate, w_up, w_down, g.reshape(1, d), b.reshape(1, d))


def kernel(x_prompt, x_sample, state_a_ssm, state_a_conv, state_b_h, state_b_conv, cache_c_k, cache_c_v, cache_d_k, cache_d_v, cache_d_logf, page_table, ln_g, ln_b, a_w_in, a_conv_w, a_a_log, a_dt_bias, a_norm_w, a_w_out, b_w_in, b_conv_w, b_conv_b, b_w_a, b_b_a, b_w_x, b_b_x, b_lambda, b_w_out, c_w_in, c_sinks, c_w_out, rel_bias, d_w_in, d_b_f, d_w_out, router_w, router_b, moe_w_gate, moe_w_up, moe_w_down):
    bp, lp, d = x_prompt.shape
    nb = x_sample.shape[0]
    xp = x_prompt.reshape(bp * lp, d)
    xs = x_sample.reshape(nb, d)
    tm_p = 256

    def finish(xp, xs, mp, ms, w_out, i):
        w = w_out.astype(bf16)
        xp = _proj_ln(mp, w, xp, ln_g[i, 0], ln_b[i, 0], tm_p)
        xs = _proj_ln(ms, w, xs, ln_g[i, 0], ln_b[i, 0], nb)
        moe_args = (router_w, router_b, moe_w_gate, moe_w_up, moe_w_down, i, ln_g[i, 1], ln_b[i, 1])
        return _moe_ln(xp, *moe_args), _moe_ln(xs, *moe_args)

    a_args = (a_w_in[0], a_conv_w[0], a_a_log[0], a_dt_bias[0], a_norm_w[0])
    mp, p_a_ssm, p_a_conv = _gdn_prompt(xp.reshape(bp, lp, d), *a_args)
    ms, s_a_ssm, s_a_conv = _gdn_sample(xs, state_a_conv[0], state_a_ssm[0], *a_args)
    xp, xs = finish(xp, xs, mp, ms, a_w_out[0], 0)

    b_args = (b_w_in[0], b_conv_w[0], b_conv_b[0], b_w_a[0], b_b_a[0], b_w_x[0], b_b_x[0], b_lambda[0])
    mp, p_b_h, p_b_conv = _lru_prompt(xp.reshape(bp, lp, d), *b_args)
    ms, s_b_h, s_b_conv = _lru_sample(xs, state_b_conv[0], state_b_h[0], *b_args)
    xp, xs = finish(xp, xs, mp, ms, b_w_out[0], 1)

    mp, kp, vp = _swa_prompt(xp.reshape(bp, lp, d), c_w_in[0], c_sinks[0], rel_bias)
    ms, _, _, s_c_k, s_c_v = _swa_sample(xs, cache_c_k[0], cache_c_v[0], c_w_in[0], c_sinks[0], rel_bias)
    p_c_k = kp[:, lp - WINDOW:].reshape(bp, WINDOW, HKV, HD)
    p_c_v = vp[:, lp - WINDOW:].reshape(bp, WINDOW, HKV, HD)
    xp, xs = finish(xp, xs, mp, ms, c_w_out[0], 2)

    mp, kp, vp, p_d_logf = _fox_prompt(xp.reshape(bp, lp, d), d_w_in[0], d_b_f[0])
    ms, ks, vs, lfs = _fox_sample(xs, cache_d_k[0], cache_d_v[0], cache_d_logf[0], page_table, d_w_in[0], d_b_f[0])
    xp, xs = finish(xp, xs, mp, ms, d_w_out[0], 3)

    return (xp.reshape(bp, lp, d), xs.reshape(nb, 1, d),
            p_a_ssm[None], p_a_conv[None], p_b_h[None], p_b_conv[None], p_c_k[None], p_c_v[None],
            kp.reshape(1, bp, lp, HKV, HD), vp.reshape(1, bp, lp, HKV, HD), p_d_logf[None],
            s_a_ssm[None], s_a_conv[None], s_b_h[None], s_b_conv[None], s_c_k[None], s_c_v[None],
            ks.reshape(1, nb, 1, HKV, HD), vs.reshape(1, nb, 1, HKV, HD), lfs.reshape(1, nb, 1, HQ))
```

```python
import functools
import math

import jax
import jax.numpy as jnp
import numpy as np
from jax import lax
from jax.experimental import pallas as pl
from jax.experimental.pallas import tpu as pltpu

f32 = jnp.float32
bf16 = jnp.bfloat16
HI = lax.Precision.HIGHEST

D_MODEL = 1024
DEPTH = 4
DN_ALPHA = (2 * DEPTH) ** 0.25
LN_EPS = 1e-5
CONV = 4
HA = 8
DKA = 128
DVA = 128
QKV_A = HA * (2 * DKA + DVA)
CHUNK_A = 64
LRU_W = 1024
LRU_BLOCKS = 8
LRU_BW = 128
LRU_C = 8.0
HD = 64
HQ = 16
HKV = 4
GQ = HQ // HKV
WINDOW = 128
N_BUCKETS = 32
MAX_DIST = 128
PAGE_SIZE = 128
N_EXPERTS = 16
N_GROUPS = 4
EPG = 4
D_EXPERT = 512

VMEM_LIMIT = 56 * 1024 * 1024


def _cp(*sem):
    return pltpu.CompilerParams(dimension_semantics=sem, vmem_limit_bytes=VMEM_LIMIT)


def _dot(a, b):
    return jnp.dot(a.astype(bf16), b.astype(bf16), preferred_element_type=f32)


def _dot_nt(a, b):
    return lax.dot_general(a.astype(bf16), b.astype(bf16), (((1,), (1,)), ((), ())), preferred_element_type=f32)


def _dot_tn(a, b):
    return lax.dot_general(a.astype(bf16), b.astype(bf16), (((0,), (0,)), ((), ())), preferred_element_type=f32)


def _dot_hi(a, b):
    return jnp.dot(a, b, precision=HI, preferred_element_type=f32)


def _split(a):
    hi = a.astype(bf16)
    return hi, (a - hi.astype(f32)).astype(bf16)


def _dot_split(a, b):
    a_hi, a_lo = _split(a)
    b_hi, b_lo = _split(b)
    d = lambda u, v: jnp.dot(u, v, preferred_element_type=f32)
    return d(a_hi, b_hi) + (d(a_hi, b_lo) + d(a_lo, b_hi))


def _bdot_split(a, b):
    a_hi, a_lo = _split(a)
    b_hi, b_lo = _split(b)
    d = lambda u, v: jnp.einsum('hij,hjk->hik', u, v, preferred_element_type=f32)
    return d(a_hi, b_hi) + (d(a_hi, b_lo) + d(a_lo, b_hi))


def _bdot_nt(a, b):
    return jnp.einsum('hid,hjd->hij', a.astype(bf16), b.astype(bf16), preferred_element_type=f32)


def _dot_nt_hi(a, b):
    return lax.dot_general(a, b, (((1,), (1,)), ((), ())), precision=HI, preferred_element_type=f32)


def _sigmoid(x):
    return 1.0 / (1.0 + jnp.exp(-x))


def _silu(x):
    return x * _sigmoid(x)


def _softplus(x):
    return jnp.maximum(x, 0.0) + jnp.log1p(jnp.exp(-jnp.abs(x)))


def _log_sigmoid(x):
    return -_softplus(-x)


def _eye(n):
    r = lax.broadcasted_iota(jnp.int32, (n, n), 0)
    c = lax.broadcasted_iota(jnp.int32, (n, n), 1)
    return (r == c).astype(f32)


def _proj_body(n_out, x_ref, *refs):
    x = x_ref[...].astype(bf16)
    for w_ref, o_ref in zip(refs[:n_out], refs[n_out:]):
        o_ref[...] = jnp.dot(x, w_ref[...], preferred_element_type=f32)


def _proj(x, ws, tm):
    m, k = x.shape
    n_out = len(ws)
    return pl.pallas_call(
        functools.partial(_proj_body, n_out),
        grid=(m // tm,),
        in_specs=[pl.BlockSpec((tm, k), lambda i: (i, 0))]
        + [pl.BlockSpec(w.shape, lambda i: (0, 0)) for w in ws],
        out_specs=[pl.BlockSpec((tm, w.shape[1]), lambda i: (i, 0)) for w in ws],
        out_shape=[jax.ShapeDtypeStruct((m, w.shape[1]), f32) for w in ws],
        compiler_params=_cp("parallel"),
        name="proj",
    )(x, *ws)


def _attn_proj_body(x_ref, wqt_ref, wk_ref, wv_ref, wvt_ref, qt_ref, k_ref, v_ref, vt_ref):
    x = x_ref[...].astype(bf16)
    nt = lambda w_ref: lax.dot_general(w_ref[...], x, (((1,), (1,)), ((), ())), preferred_element_type=f32)
    qt_ref[0] = (nt(wqt_ref) * (HD ** -0.5)).astype(bf16)
    vt_ref[0] = nt(wvt_ref).astype(bf16)
    k_ref[...] = jnp.dot(x, wk_ref[...], preferred_element_type=f32)
    v_ref[...] = jnp.dot(x, wv_ref[...], preferred_element_type=f32)


def _attn_proj(x, w_in, b, l, tm):
    m, d = x.shape
    nq = HQ * HD
    nk = HKV * HD
    wqt = w_in[:, :nq].T.astype(bf16)
    wk = w_in[:, nq:nq + nk].astype(bf16)
    wv = w_in[:, nq + nk:nq + 2 * nk].astype(bf16)
    nt = l // tm
    full = lambda a: pl.BlockSpec(a.shape, lambda i: (0, 0))
    rows = lambda n: pl.BlockSpec((tm, n), lambda i: (i, 0))
    cols = lambda n: pl.BlockSpec((1, n, tm), lambda i: (i // nt, 0, i % nt))
    return pl.pallas_call(
        _attn_proj_body,
        grid=(m // tm,),
        in_specs=[rows(d), full(wqt), full(wk), full(wv), full(wv.T)],
        out_specs=[cols(nq), rows(nk), rows(nk), cols(nk)],
        out_shape=[jax.ShapeDtypeStruct((b, nq, l), bf16), jax.ShapeDtypeStruct((m, nk), f32),
                   jax.ShapeDtypeStruct((m, nk), f32), jax.ShapeDtypeStruct((b, nk, l), bf16)],
        compiler_params=_cp("parallel"),
        name="attn_proj",
    )(x, wqt, wk, wv, wv.T)


def _layer_norm(z, g, b):
    mu = jnp.mean(z, -1, keepdims=True)
    zc = z - mu
    var = jnp.mean(zc * zc, -1, keepdims=True)
    return zc * lax.rsqrt(var + LN_EPS) * g + b


def _proj_ln_body(h_ref, w_ref, x_ref, g_ref, b_ref, o_ref):
    m = jnp.dot(h_ref[...].astype(bf16), w_ref[...], preferred_element_type=f32)
    o_ref[...] = _layer_norm(DN_ALPHA * x_ref[...] + m, g_ref[...], b_ref[...])


def _proj_ln_t_body(ht_ref, w_ref, x_ref, g_ref, b_ref, o_ref):
    m = lax.dot_general(ht_ref[0].astype(bf16), w_ref[...], (((0,), (0,)), ((), ())), preferred_element_type=f32)
    o_ref[...] = _layer_norm(DN_ALPHA * x_ref[...] + m, g_ref[...], b_ref[...])


def _proj_ln(h, w, x, g, b, tm, transposed=False):
    d = x.shape[1]
    if transposed:
        _, k, l = h.shape
        m = x.shape[0]
        nt = l // tm
        h_spec = pl.BlockSpec((1, k, tm), lambda i: (i // nt, 0, i % nt))
    else:
        m, k = h.shape
        h_spec = pl.BlockSpec((tm, k), lambda i: (i, 0))
    return pl.pallas_call(
        _proj_ln_t_body if transposed else _proj_ln_body,
        grid=(m // tm,),
        in_specs=[h_spec,
                  pl.BlockSpec((k, d), lambda i: (0, 0)),
                  pl.BlockSpec((tm, d), lambda i: (i, 0)),
                  pl.BlockSpec((1, d), lambda i: (0, 0)),
                  pl.BlockSpec((1, d), lambda i: (0, 0))],
        out_specs=pl.BlockSpec((tm, d), lambda i: (i, 0)),
        out_shape=jax.ShapeDtypeStruct((m, d), f32),
        compiler_params=_cp("parallel"),
        name="proj_ln",
    )(h, w, x, g.reshape(1, d), b.reshape(1, d))


GDN_TILE = 256


def _gdn_gates(ab, alog, dtb):
    g = -jnp.exp(alog) * _softplus(ab[:, 0:HA] + dtb)
    beta = _sigmoid(ab[:, HA:2 * HA])
    return g, beta


def _l2norm(t):
    return t * lax.rsqrt(jnp.sum(t * t, -1, keepdims=True) + 1e-6)


def _gdn_prep_body(qkv_ref, halo_ref, ab_ref, cw_ref, alog_ref, dtb_ref,
                   u_ref, w_ref, qd_ref, kd_ref, qk_ref, eg_ref, xp_scr, y_scr):
    i = pl.program_id(1)
    tc = qkv_ref.shape[1]
    c = CHUNK_A
    xp_scr[0:8, :] = jnp.where(i > 0, halo_ref[0], 0.0)
    xp_scr[8:8 + tc, :] = qkv_ref[0]
    cw = cw_ref[...]
    y = xp_scr[5:5 + tc, :] * cw[0:1, :]
    for j in range(1, CONV):
        y = y + xp_scr[5 + j:5 + j + tc, :] * cw[j:j + 1, :]
    y_scr[...] = _silu(y)

    row = lax.broadcasted_iota(jnp.int32, (c, c), 0)
    col = lax.broadcasted_iota(jnp.int32, (c, c), 1)
    incl = row >= col
    strict = row > col
    tri = incl.astype(f32)
    eye_c = (row == col).astype(f32)
    eye_h = _eye(HA)
    alog = alog_ref[...]
    dtb = dtb_ref[...]

    def chunk(n, carry):
        r0 = pl.multiple_of(n * c, c)
        rows = pl.ds(r0, c)
        g, beta = _gdn_gates(ab_ref[0, rows, :], alog, dtb)
        gc = _dot_hi(tri, g)
        gct = _dot_nt_hi(eye_h, gc)
        egc = jnp.exp(gc)
        eg_ref[0, rows, :] = egc
        heads = lambda t, off: jnp.stack([t(y_scr[rows, off + h * DKA:off + (h + 1) * DKA]) for h in range(HA)])
        q = heads(_l2norm, 0) * (DKA ** -0.5)
        k = heads(_l2norm, HA * DKA)
        v = heads(lambda t: t, 2 * HA * DKA)
        col = lambda t: jnp.stack([t[:, h:h + 1] for h in range(HA)])
        gcol, bcol, egcol = col(gc), col(beta), col(egc)
        diff = gcol - gct[:, None, :]
        dec = jnp.where(incl, jnp.exp(jnp.where(incl, diff, 0.0)), 0.0)
        a = jnp.where(strict, _bdot_nt(k, k) * dec, 0.0) * bcol
        x = eye_c - a
        p = _bdot_split(a, a)
        for _ in range(4):
            xp = _bdot_split(jnp.concatenate([x, p], axis=1), p)
            x = x + xp[:, :c]
            p = xp[:, c:]
        x = x + _bdot_split(x, p)
        sol = _bdot_split(x, jnp.concatenate([v * bcol, k * (bcol * egcol)], axis=-1))
        qd = q * egcol
        kd = k * jnp.exp(jnp.stack([gc[c - 1:c, h:h + 1] for h in range(HA)]) - gcol)
        qk = _bdot_nt(q, k) * dec
        for h in range(HA):
            hs = slice(h * DVA, (h + 1) * DVA)
            u_ref[0, rows, hs] = sol[h, :, :DVA]
            w_ref[0, rows, hs] = sol[h, :, DVA:]
            qd_ref[0, rows, hs] = qd[h]
            kd_ref[0, rows, hs] = kd[h]
            qk_ref[0, rows, h * c:(h + 1) * c] = qk[h]
        return carry

    lax.fori_loop(0, tc // c, chunk, 0)


def _gdn_prep(qkv, ab, conv_w, a_log, dt_bias):
    b, l, _ = qkv.shape
    tc = min(GDN_TILE, l)
    hb = tc // 8
    big = lambda n: pl.BlockSpec((1, tc, n), lambda bi, i: (bi, i, 0))
    full = lambda a: pl.BlockSpec(a.shape, lambda bi, i: (0, 0))
    wide = jax.ShapeDtypeStruct((b, l, HA * DVA), f32)
    return pl.pallas_call(
        _gdn_prep_body,
        grid=(b, l // tc),
        in_specs=[big(QKV_A),
                  pl.BlockSpec((1, 8, QKV_A), lambda bi, i: (bi, jnp.maximum(i * hb - 1, 0), 0)),
                  big(2 * HA), full(conv_w), full(a_log), full(dt_bias)],
        out_specs=[big(HA * DVA)] * 4 + [big(HA * CHUNK_A), big(HA)],
        out_shape=[wide] * 4 + [jax.ShapeDtypeStruct((b, l, HA * CHUNK_A), f32),
                                jax.ShapeDtypeStruct((b, l, HA), f32)],
        scratch_shapes=[pltpu.VMEM((tc + 8, QKV_A), f32), pltpu.VMEM((tc, QKV_A), f32)],
        compiler_params=_cp("parallel", "parallel"),
        name="gdn_prep",
    )(qkv, qkv, ab, conv_w, a_log, dt_bias)


def _gdn_scan_body(u_ref, w_ref, qd_ref, kd_ref, qk_ref, eg_ref, gate_ref, nw_ref, y_ref, s_ref):
    i = pl.program_id(1)
    tc = u_ref.shape[1]
    c = CHUNK_A

    @pl.when(i == 0)
    def _():
        s_ref[...] = jnp.zeros_like(s_ref)

    nw = nw_ref[...]

    def chunk(n, carry):
        r0 = pl.multiple_of(n * c, c)
        rows = pl.ds(r0, c)
        eg_last = eg_ref[0, pl.ds(r0 + c - 1, 1), :]
        for h in range(HA):
            hs = slice(h * DVA, (h + 1) * DVA)
            s = s_ref[0, h]
            v_new = u_ref[0, rows, hs] - _dot(w_ref[0, rows, hs], s)
            o = _dot(qd_ref[0, rows, hs], s) + _dot(qk_ref[0, rows, h * c:(h + 1) * c], v_new)
            s_ref[0, h] = s * eg_last[:, h:h + 1] + _dot_tn(kd_ref[0, rows, hs], v_new)
            of = o * lax.rsqrt(jnp.mean(o * o, -1, keepdims=True) + 1e-6)
            y_ref[0, rows, hs] = of * nw * _silu(gate_ref[0, rows, hs])
        return carry

    lax.fori_loop(0, tc // c, chunk, 0)


def _gdn_scan(u, w, qd, kd, qk, eg, gate, norm_w):
    b, l, _ = u.shape
    tc = min(GDN_TILE, l)
    big = lambda n: pl.BlockSpec((1, tc, n), lambda bi, i: (bi, i, 0))
    return pl.pallas_call(
        _gdn_scan_body,
        grid=(b, l // tc),
        in_specs=[big(HA * DVA)] * 4 + [big(HA * CHUNK_A), big(HA), big(HA * DVA),
                                       pl.BlockSpec((1, DVA), lambda bi, i: (0, 0))],
        out_specs=[big(HA * DVA), pl.BlockSpec((1, HA, DKA, DVA), lambda bi, i: (bi, 0, 0, 0))],
        out_shape=[jax.ShapeDtypeStruct((b, l, HA * DVA), f32),
                   jax.ShapeDtypeStruct((b, HA, DKA, DVA), f32)],
        compiler_params=_cp("parallel", "arbitrary"),
        name="gdn_scan",
    )(u, w, qd, kd, qk, eg, gate, norm_w.reshape(1, DVA))


def _gdn_prompt(x, w_in, conv_w, a_log, dt_bias, norm_w):
    b, l, d = x.shape
    wq = w_in[:, :QKV_A].astype(bf16)
    wg = w_in[:, QKV_A:QKV_A + HA * DVA].astype(bf16)
    wab = w_in[:, QKV_A + HA * DVA:].astype(bf16)
    qkv, gate, ab = _proj(x.reshape(b * l, d), [wq, wg, wab], 256)
    qkv = qkv.reshape(b, l, QKV_A)
    u, w, qd, kd, qk, eg = _gdn_prep(qkv, ab.reshape(b, l, 2 * HA), conv_w,
                                     a_log.reshape(1, HA), dt_bias.reshape(1, HA))
    y, s = _gdn_scan(u, w, qd, kd, qk, eg, gate.reshape(b, l, HA * DVA), norm_w)
    return y.reshape(b * l, HA * DVA), s, qkv[:, l - (CONV - 1):]


GDN_BT = 8


def _gdn_step_body(qkv_ref, gate_ref, ab_ref, buf_ref, s_ref, cw_ref, alog_ref, dtb_ref, nw_ref,
                   y_ref, s_out_ref, buf_out_ref, q_scr, k_scr, v_scr, eg_scr, beta_scr):
    n = QKV_A
    new = qkv_ref[...]
    cw = cw_ref[...]
    y = new * cw[CONV - 1:CONV, :]
    for j in range(CONV - 1):
        y = y + buf_ref[:, j * n:(j + 1) * n] * cw[j:j + 1, :]
    y = _silu(y)
    buf_out_ref[:, 0:2 * n] = buf_ref[:, n:3 * n]
    buf_out_ref[:, 2 * n:3 * n] = new
    for h in range(HA):
        q_scr[:, h * DKA:(h + 1) * DKA] = _l2norm(y[:, h * DKA:(h + 1) * DKA]) * (DKA ** -0.5)
        k_scr[:, h * DKA:(h + 1) * DKA] = _l2norm(y[:, HA * DKA + h * DKA:HA * DKA + (h + 1) * DKA])
    v_scr[...] = y[:, 2 * HA * DKA:]
    g, beta = _gdn_gates(ab_ref[...], alog_ref[...], dtb_ref[...])
    eg_scr[...] = jnp.exp(g)
    beta_scr[...] = beta
    eye = _eye(DKA)
    row8 = lax.broadcasted_iota(jnp.int32, (8, DKA), 0)
    nw = nw_ref[...]

    for bi in range(qkv_ref.shape[0]):
        r = slice(bi, bi + 1)
        for h in range(HA):
            hs = slice(h * DKA, (h + 1) * DKA)
            k_row = k_scr[r, hs]
            q_row = q_scr[r, hs]
            kq = jnp.where(row8 == 0, k_row, jnp.where(row8 == 1, q_row, 0.0))
            cols = _dot_nt_hi(eye, kq)
            k_col = cols[:, 0:1]
            q_col = cols[:, 1:2]
            sd = s_ref[bi, h] * eg_scr[r, h:h + 1]
            pred = jnp.sum(k_col * sd, axis=0, keepdims=True)
            delta = beta_scr[r, h:h + 1] * (v_scr[r, hs] - pred)
            s_new = sd + k_col * delta
            s_out_ref[bi, h] = s_new
            o = jnp.sum(q_col * s_new, axis=0, keepdims=True)
            of = o * lax.rsqrt(jnp.mean(o * o, -1, keepdims=True) + 1e-6)
            y_ref[r, hs] = of * nw * _silu(gate_ref[r, hs])


def _gdn_step(qkv, gate, ab, conv_buf, s0, conv_w, a_log, dt_bias, norm_w):
    nb = qkv.shape[0]
    bt = GDN_BT
    rowblk = lambda n: pl.BlockSpec((bt, n), lambda i: (i, 0))
    full = lambda a: pl.BlockSpec(a.shape, lambda i: (0, 0))
    sblk = pl.BlockSpec((bt, HA, DKA, DVA), lambda i: (i, 0, 0, 0))
    nw = norm_w.reshape(1, DVA)
    al = a_log.reshape(1, HA)
    db = dt_bias.reshape(1, HA)
    return pl.pallas_call(
        _gdn_step_body,
        grid=(nb // bt,),
        in_specs=[rowblk(QKV_A), rowblk(HA * DVA), rowblk(2 * HA), rowblk(3 * QKV_A), sblk,
                  full(conv_w), full(al), full(db), full(nw)],
        out_specs=[rowblk(HA * DVA), sblk, rowblk(3 * QKV_A)],
        out_shape=[jax.ShapeDtypeStruct((nb, HA * DVA), f32),
                   jax.ShapeDtypeStruct(s0.shape, f32),
                   jax.ShapeDtypeStruct((nb, 3 * QKV_A), f32)],
        scratch_shapes=[pltpu.VMEM((bt, HA * DKA), f32)] * 3 + [pltpu.VMEM((bt, HA), f32)] * 2,
        compiler_params=_cp("parallel"),
        name="gdn_step",
    )(qkv, gate, ab, conv_buf.reshape(nb, 3 * QKV_A), s0, conv_w, al, db, nw)


def _gdn_sample(x, conv_buf, s0, w_in, conv_w, a_log, dt_bias, norm_w):
    nb = x.shape[0]
    wq = w_in[:, :QKV_A].astype(bf16)
    wg = w_in[:, QKV_A:QKV_A + HA * DVA].astype(bf16)
    wab = w_in[:, QKV_A + HA * DVA:].astype(bf16)
    qkv, gate, ab = _proj(x, [wq, wg, wab], nb)
    y, s, buf = _gdn_step(qkv, gate, ab, conv_buf, s0, conv_w, a_log, dt_bias, norm_w)
    return y, s, buf.reshape(nb, CONV - 1, QKV_A)


LRU_TILE = 256


def _gelu(x):
    return 0.5 * x * (1.0 + jnp.tanh(math.sqrt(2.0 / math.pi) * (x + 0.044715 * x * x * x)))


def _lru_gates(u, wa_ref, ba, wx_ref, bx, lam):
    ra, xa = [], []
    for n in range(LRU_BLOCKS):
        ub = u[:, n * LRU_BW:(n + 1) * LRU_BW].astype(bf16)
        ra.append(jnp.dot(ub, wa_ref[n], preferred_element_type=f32))
        xa.append(jnp.dot(ub, wx_ref[n], preferred_element_type=f32))
    r = _sigmoid(jnp.concatenate(ra, axis=-1) + ba)
    i_g = _sigmoid(jnp.concatenate(xa, axis=-1) + bx)
    log_a = -LRU_C * r * _softplus(-lam)
    a = jnp.exp(log_a)
    t = jnp.tanh(log_a)
    b = jnp.sqrt(-2.0 * t / (1.0 - t)) * (i_g * u)
    return a, b


def _lru_prompt_body(gate_ref, rec_ref, halo_ref, cw_ref, cb_ref, wa_ref, ba_ref, wx_ref, bx_ref, lam_ref,
                     y_ref, h_ref, xp_scr):
    i = pl.program_id(1)
    tt = rec_ref.shape[1]

    @pl.when(i == 0)
    def _():
        h_ref[...] = jnp.zeros_like(h_ref)

    xp_scr[0:8, :] = jnp.where(i > 0, halo_ref[0], 0.0)
    xp_scr[8:8 + tt, :] = rec_ref[0]
    cw = cw_ref[...]
    u = xp_scr[5:5 + tt, :] * cw[0:1, :]
    for j in range(1, CONV):
        u = u + xp_scr[5 + j:5 + j + tt, :] * cw[j:j + 1, :]
    u = u + cb_ref[...]
    a, b = _lru_gates(u, wa_ref, ba_ref[...], wx_ref, bx_ref[...], lam_ref[...])
    row = lax.broadcasted_iota(jnp.int32, a.shape, 0)
    s = 1
    while s < tt:
        keep = row >= s
        a_sh = jnp.where(keep, pltpu.roll(a, s, 0), 1.0)
        b_sh = jnp.where(keep, pltpu.roll(b, s, 0), 0.0)
        b = a * b_sh + b
        a = a * a_sh
        s *= 2
    h = b + a * h_ref[0]
    h_ref[0] = h[tt - 1:tt, :]
    y_ref[0] = _gelu(gate_ref[0]) * h


def _lru_prompt(x, w_in, conv_w, conv_b, w_a, b_a, w_x, b_x, lam):
    bsz, l, d = x.shape
    w = LRU_W
    gate_in, rec_in = _proj(x.reshape(bsz * l, d), [w_in[:, :w].astype(bf16), w_in[:, w:].astype(bf16)], 256)
    rec3 = rec_in.reshape(bsz, l, w)
    tt = min(LRU_TILE, l)
    hb = tt // 8
    big = pl.BlockSpec((1, tt, w), lambda bi, i: (bi, i, 0))
    vec = pl.BlockSpec((1, w), lambda bi, i: (0, 0))
    blkw = pl.BlockSpec((LRU_BLOCKS, LRU_BW, LRU_BW), lambda bi, i: (0, 0, 0))
    y, h = pl.pallas_call(
        _lru_prompt_body,
        grid=(bsz, l // tt),
        in_specs=[big, big, pl.BlockSpec((1, 8, w), lambda bi, i: (bi, jnp.maximum(i * hb - 1, 0), 0)),
                  pl.BlockSpec((CONV, w), lambda bi, i: (0, 0)), vec, blkw, vec, blkw, vec, vec],
        out_specs=[big, pl.BlockSpec((1, 1, w), lambda bi, i: (bi, 0, 0))],
        out_shape=[jax.ShapeDtypeStruct((bsz, l, w), f32), jax.ShapeDtypeStruct((bsz, 1, w), f32)],
        scratch_shapes=[pltpu.VMEM((tt + 8, w), f32)],
        compiler_params=_cp("parallel", "arbitrary"),
        name="lru_prompt",
    )(gate_in.reshape(bsz, l, w), rec3, rec3, conv_w, conv_b.reshape(1, w), w_a.astype(bf16), b_a.reshape(1, w),
      w_x.astype(bf16), b_x.reshape(1, w), lam.reshape(1, w))
    return y.reshape(bsz * l, w), h.reshape(bsz, w), rec3[:, l - (CONV - 1):]


def _lru_step_body(gate_ref, rec_ref, buf_ref, h0_ref, cw_ref, cb_ref, wa_ref, ba_ref, wx_ref, bx_ref, lam_ref,
                   y_ref, h_ref, buf_out_ref):
    w = LRU_W
    new = rec_ref[...]
    cw = cw_ref[...]
    u = new * cw[CONV - 1:CONV, :]
    for j in range(CONV - 1):
        u = u + buf_ref[:, j * w:(j + 1) * w] * cw[j:j + 1, :]
    u = u + cb_ref[...]
    buf_out_ref[:, 0:2 * w] = buf_ref[:, w:3 * w]
    buf_out_ref[:, 2 * w:3 * w] = new
    a, b = _lru_gates(u, wa_ref, ba_ref[...], wx_ref, bx_ref[...], lam_ref[...])
    h = b + a * h0_ref[...]
    h_ref[...] = h
    y_ref[...] = _gelu(gate_ref[...]) * h


def _lru_sample(x, conv_buf, h0, w_in, conv_w, conv_b, w_a, b_a, w_x, b_x, lam):
    nb = x.shape[0]
    w = LRU_W
    gate_in, rec_in = _proj(x, [w_in[:, :w].astype(bf16), w_in[:, w:].astype(bf16)], nb)
    y, h, buf = pl.pallas_call(
        _lru_step_body,
        out_shape=[jax.ShapeDtypeStruct((nb, w), f32), jax.ShapeDtypeStruct((nb, w), f32),
                   jax.ShapeDtypeStruct((nb, 3 * w), f32)],
        compiler_params=pltpu.CompilerParams(vmem_limit_bytes=VMEM_LIMIT),
        name="lru_step",
    )(gate_in, rec_in, conv_buf.reshape(nb, 3 * w), h0, conv_w, conv_b.reshape(1, w), w_a.astype(bf16),
      b_a.reshape(1, w), w_x.astype(bf16), b_x.reshape(1, w), lam.reshape(1, w))
    return y, h, buf.reshape(nb, CONV - 1, w)


NEG = -1e30


def _t5_bucket(rel):
    n = jnp.maximum(rel, 0)
    max_exact = N_BUCKETS // 2
    large = max_exact + (jnp.log(jnp.maximum(n, 1).astype(f32) / max_exact)
                         / math.log(MAX_DIST / max_exact) * (N_BUCKETS - max_exact)).astype(jnp.int32)
    return jnp.where(n < max_exact, n, jnp.minimum(large, N_BUCKETS - 1))


def _swa_prompt_body(qt_ref, kp_ref, kc_ref, vtp_ref, vtc_ref, bucket_ref, sinks_ref, rb_ref, o_ref, bias_scr):
    i = pl.program_id(1)
    w = WINDOW

    @pl.when(i == 0)
    def _():
        bucket = bucket_ref[...]
        rel = (lax.broadcasted_iota(jnp.int32, (2 * w, w), 1) + w
               - lax.broadcasted_iota(jnp.int32, (2 * w, w), 0))
        in_window = (rel >= 0) & (rel <= w)
        for h in range(HQ):
            b = jnp.zeros((2 * w, w), f32)
            for n in range(N_BUCKETS):
                b = jnp.where(bucket == n, rb_ref[n, h], b)
            bias_scr[h // GQ, :, (h % GQ) * w:(h % GQ + 1) * w] = jnp.where(in_window, b, NEG)

    hidden_rows = jnp.where(i == 0, w, 0)
    hide = lax.broadcasted_iota(jnp.int32, (2 * w, GQ * w), 0) < hidden_rows
    for kvh in range(HKV):
        heads = range(kvh * GQ, (kvh + 1) * GQ)
        qt = jnp.concatenate([qt_ref[0, h] for h in heads], axis=-1)
        kk = jnp.concatenate([kp_ref[0, kvh], kc_ref[0, kvh]], axis=0)
        s = jnp.where(hide, NEG, jnp.dot(kk, qt, preferred_element_type=f32) + bias_scr[kvh])
        sink = jnp.concatenate([jnp.full((1, w), sinks_ref[h], f32) for h in heads], axis=-1)
        m = jnp.maximum(jnp.max(s, 0, keepdims=True), sink)
        e = jnp.exp(s - m)
        inv = 1.0 / (jnp.sum(e, 0, keepdims=True) + jnp.exp(sink - m))
        vt = jnp.concatenate([vtp_ref[0, kvh], vtc_ref[0, kvh]], axis=-1)
        ot = jnp.dot(vt, (e * inv).astype(bf16), preferred_element_type=f32)
        for g, h in enumerate(heads):
            o_ref[0, h] = ot[:, g * w:(g + 1) * w]


def _swa_prompt(x, w_in, sinks, rel_bias):
    b, l, d = x.shape
    nq = HQ * HD
    nk = HKV * HD
    qt, k, v, vt = _attn_proj(x.reshape(b * l, d), w_in, b, l, 256)
    k = k.reshape(b, l, nk)
    v = v.reshape(b, l, nk)
    w = WINDOW
    rel = jnp.arange(w)[None, :] + w - jnp.arange(2 * w)[:, None]
    bucket = _t5_bucket(rel).astype(jnp.int32)
    qt = qt.reshape(b, HQ, HD, l)
    kh = jnp.swapaxes(k.reshape(b, l, HKV, HD), 1, 2).astype(bf16)
    vt = vt.reshape(b, HKV, HD, l)
    qblk = pl.BlockSpec((1, HQ, HD, w), lambda bi, i: (bi, 0, 0, i))
    smem = pl.BlockSpec(memory_space=pltpu.SMEM)
    ot = pl.pallas_call(
        _swa_prompt_body,
        grid=(b, l // w),
        in_specs=[qblk,
                  pl.BlockSpec((1, HKV, w, HD), lambda bi, i: (bi, 0, jnp.maximum(i - 1, 0), 0)),
                  pl.BlockSpec((1, HKV, w, HD), lambda bi, i: (bi, 0, i, 0)),
                  pl.BlockSpec((1, HKV, HD, w), lambda bi, i: (bi, 0, 0, jnp.maximum(i - 1, 0))),
                  pl.BlockSpec((1, HKV, HD, w), lambda bi, i: (bi, 0, 0, i)),
                  pl.BlockSpec((2 * w, w), lambda bi, i: (0, 0)), smem, smem],
        out_specs=qblk,
        out_shape=jax.ShapeDtypeStruct((b, HQ, HD, l), f32),
        scratch_shapes=[pltpu.VMEM((HKV, 2 * w, GQ * w), f32)],
        compiler_params=_cp("parallel", "arbitrary"),
        name="swa_prompt",
    )(qt, kh, kh, vt, vt, bucket, sinks, rel_bias)
    return ot.reshape(b, nq, l), k, v


SWA_BT = 8


def _head_mask():
    r = lax.broadcasted_iota(jnp.int32, (HQ, HKV * HD), 0)
    c = lax.broadcasted_iota(jnp.int32, (HQ, HKV * HD), 1)
    return (r // GQ) == (c // HD)


def _fold_heads(o, mask):
    o = jnp.where(mask, o, 0.0)
    acc = o[:, 0:HD]
    for c in range(1, HKV):
        acc = acc + o[:, c * HD:(c + 1) * HD]
    return acc


def _swa_step_body(q_ref, kn_ref, vn_ref, kb_ref, vb_ref, onehot_ref, rbt_ref, sinks_ref, o_ref, kb_out, vb_out):
    w = WINDOW
    mask = _head_mask()
    bias_all = _dot_hi(rbt_ref[...], onehot_ref[...])
    bias = bias_all[:, :w]
    bias_new = bias_all[:, w:w + 1]
    sink = sinks_ref[...]
    for bi in range(q_ref.shape[0]):
        qm = jnp.where(mask, jnp.concatenate([q_ref[bi]] * HKV, axis=-1), 0.0)
        kn = kn_ref[bi]
        vn = vn_ref[bi]
        s = _dot_nt(qm, kb_ref[bi]) * (HD ** -0.5) + bias
        s_new = jnp.sum(qm * kn, -1, keepdims=True) * (HD ** -0.5) + bias_new
        m = jnp.maximum(jnp.maximum(jnp.max(s, -1, keepdims=True), s_new), sink)
        e = jnp.exp(s - m)
        e_new = jnp.exp(s_new - m)
        denom = jnp.sum(e, -1, keepdims=True) + e_new + jnp.exp(sink - m)
        o = (_dot(e / denom, vb_ref[bi]) + (e_new / denom) * vn)
        o_ref[bi] = _fold_heads(o, mask)
        kb_out[bi, 0:w - 1, :] = kb_ref[bi, 1:w, :]
        kb_out[bi, w - 1:w, :] = kn
        vb_out[bi, 0:w - 1, :] = vb_ref[bi, 1:w, :]
        vb_out[bi, w - 1:w, :] = vn


def _swa_sample(x, k_buf, v_buf, w_in, sinks, rel_bias):
    nb = x.shape[0]
    nq = HQ * HD
    nk = HKV * HD
    w = WINDOW
    q, k, v = _proj(x, [w_in[:, :nq].astype(bf16), w_in[:, nq:nq + nk].astype(bf16), w_in[:, nq + nk:].astype(bf16)], nb)
    slots = jnp.arange(w + 128)
    bucket = _t5_bucket(jnp.where(slots <= w, w - slots, 0))
    onehot = (bucket[None, :] == jnp.arange(N_BUCKETS)[:, None]).astype(f32)
    bt = SWA_BT
    full = lambda a: pl.BlockSpec(a.shape, lambda i: (0,) * a.ndim)
    bufblk = pl.BlockSpec((bt, w, nk), lambda i: (i, 0, 0))
    rbt = rel_bias.T
    sk = sinks.reshape(HQ, 1)
    o, kb, vb = pl.pallas_call(
        _swa_step_body,
        grid=(nb // bt,),
        in_specs=[pl.BlockSpec((bt, HQ, HD), lambda i: (i, 0, 0)),
                  pl.BlockSpec((bt, 1, nk), lambda i: (i, 0, 0)), pl.BlockSpec((bt, 1, nk), lambda i: (i, 0, 0)),
                  bufblk, bufblk, full(onehot), full(rbt), full(sk)],
        out_specs=[pl.BlockSpec((bt, HQ, HD), lambda i: (i, 0, 0)), bufblk, bufblk],
        out_shape=[jax.ShapeDtypeStruct((nb, HQ, HD), f32), jax.ShapeDtypeStruct((nb, w, nk), f32),
                   jax.ShapeDtypeStruct((nb, w, nk), f32)],
        compiler_params=_cp("parallel"),
        name="swa_step",
    )(q.reshape(nb, HQ, HD), k.reshape(nb, 1, nk), v.reshape(nb, 1, nk), k_buf.reshape(nb, w, nk),
      v_buf.reshape(nb, w, nk), onehot, rbt, sk)
    return o.reshape(nb, nq), k, v, kb.reshape(nb, w, HKV, HD), vb.reshape(nb, w, HKV, HD)


FOX_PREP_TILE = 512
FOX_TQ = 512
FOX_TK = 1024


def _fox_prep_body(x_ref, wft_ref, bf_ref, lf_ref, cum_ref, carry):
    i = pl.program_id(1)
    tt = x_ref.shape[1]

    @pl.when(i == 0)
    def _():
        carry[...] = jnp.zeros_like(carry)

    f = lax.dot_general(wft_ref[...], x_ref[0].astype(bf16), (((1,), (1,)), ((), ())), preferred_element_type=f32)
    lf = _log_sigmoid(f + bf_ref[...])
    r = lax.broadcasted_iota(jnp.int32, (tt, tt), 0)
    c = lax.broadcasted_iota(jnp.int32, (tt, tt), 1)
    cum = _dot_hi(lf, (r <= c).astype(f32)) + carry[...]
    lf_ref[0] = lf
    cum_ref[0] = cum
    carry[...] = cum[:, tt - 1:tt]


def _fox_flash_body(qt_ref, k_ref, vt_ref, fk_ref, o_ref, m_scr, l_scr, acc_scr):
    i = pl.program_id(2)
    tq = qt_ref.shape[3]
    tk = FOX_TK
    m_scr[...] = jnp.full_like(m_scr, NEG)
    l_scr[...] = jnp.zeros_like(l_scr)
    acc_scr[...] = jnp.zeros_like(acc_scr)

    def block(j, masked):
        c0 = pl.multiple_of(j * tk, tk)
        k = k_ref[0, 0, pl.ds(c0, tk), :]
        vt = vt_ref[0, 0, :, pl.ds(c0, tk)]
        fk = fk_ref[0, 0, pl.ds(c0, tk), :]
        if masked:
            s_pos = c0 + lax.broadcasted_iota(jnp.int32, (tk, tq), 0)
            t_pos = i * tq + lax.broadcasted_iota(jnp.int32, (tk, tq), 1)
            visible = s_pos <= t_pos
        for g in range(GQ):
            s = jnp.dot(k, qt_ref[0, g], preferred_element_type=f32) - fk[:, g:g + 1]
            if masked:
                s = jnp.where(visible, s, NEG)
            m_old = m_scr[g]
            m_new = jnp.maximum(m_old, jnp.max(s, 0, keepdims=True))
            alpha = jnp.exp(m_old - m_new)
            p = jnp.exp(s - m_new)
            l_scr[g] = alpha * l_scr[g] + jnp.sum(p, 0, keepdims=True)
            acc_scr[g] = alpha * acc_scr[g] + jnp.dot(vt, p.astype(bf16), preferred_element_type=f32)
            m_scr[g] = m_new

    n_full = (i * tq) // tk

    def full_block(j, carry):
        block(j, False)
        return carry

    lax.fori_loop(0, n_full, full_block, 0)
    for jj in range(pl.cdiv(tq, tk)):
        block(n_full + jj, True)
    for g in range(GQ):
        o_ref[0, g] = acc_scr[g] / l_scr[g]


def _fox_prompt(x, w_in, b_f):
    b, l, d = x.shape
    nq = HQ * HD
    nk = HKV * HD
    qt, k, v, vt = _attn_proj(x.reshape(b * l, d), w_in, b, l, 256)
    tt = min(FOX_PREP_TILE, l)
    wft = w_in[:, nq + 2 * nk:].T.astype(bf16)
    row = pl.BlockSpec((1, HQ, tt), lambda bi, i: (bi, 0, i))
    lft, cumt = pl.pallas_call(
        _fox_prep_body,
        grid=(b, l // tt),
        in_specs=[pl.BlockSpec((1, tt, d), lambda bi, i: (bi, i, 0)),
                  pl.BlockSpec((HQ, d), lambda bi, i: (0, 0)), pl.BlockSpec((HQ, 1), lambda bi, i: (0, 0))],
        out_specs=[row, row],
        out_shape=[jax.ShapeDtypeStruct((b, HQ, l), f32)] * 2,
        scratch_shapes=[pltpu.VMEM((HQ, 1), f32)],
        compiler_params=_cp("parallel", "arbitrary"),
        name="fox_prep",
    )(x, wft, b_f.reshape(HQ, 1))
    fk = jnp.swapaxes(cumt.reshape(b, HKV, GQ, l), 2, 3)
    qt = qt.reshape(b, HQ, HD, l)
    kh = jnp.swapaxes(k.reshape(b, l, HKV, HD), 1, 2).astype(bf16)
    vt = vt.reshape(b, HKV, HD, l)
    tq = min(FOX_TQ, l)
    qblk = pl.BlockSpec((1, GQ, HD, tq), lambda bi, h, i: (bi, h, 0, i))
    ot = pl.pallas_call(
        _fox_flash_body,
        grid=(b, HKV, l // tq),
        in_specs=[qblk,
                  pl.BlockSpec((1, 1, l, HD), lambda bi, h, i: (bi, h, 0, 0)),
                  pl.BlockSpec((1, 1, HD, l), lambda bi, h, i: (bi, h, 0, 0)),
                  pl.BlockSpec((1, 1, l, GQ), lambda bi, h, i: (bi, h, 0, 0))],
        out_specs=qblk,
        out_shape=jax.ShapeDtypeStruct((b, HQ, HD, l), f32),
        scratch_shapes=[pltpu.VMEM((GQ, 1, tq), f32), pltpu.VMEM((GQ, 1, tq), f32), pltpu.VMEM((GQ, HD, tq), f32)],
        compiler_params=_cp("parallel", "parallel", "arbitrary"),
        name="fox_flash",
    )(qt, kh, vt, fk)
    return ot.reshape(b, nq, l), k.reshape(b, l, nk), v.reshape(b, l, nk), jnp.swapaxes(lft, 1, 2)


FOX_PAGES = 16
FOX_ROWS = 1


def _fox_step_body(pt_ref, q_ref, kn_ref, vn_ref, lfn_ref, *refs):
    npg = FOX_PAGES
    n_in = 3 * npg * FOX_ROWS
    o_ref, m_scr, l_scr, acc_scr, f_scr = refs[n_in:]
    j = pl.program_id(1)
    ps = PAGE_SIZE
    mask = _head_mask()

    @pl.when(j == 0)
    def _():
        m_scr[...] = jnp.full_like(m_scr, NEG)
        l_scr[...] = jnp.zeros_like(l_scr)
        acc_scr[...] = jnp.zeros_like(acc_scr)
        f_scr[...] = jnp.zeros_like(f_scr)

    r = lax.broadcasted_iota(jnp.int32, (ps, ps), 0)
    c = lax.broadcasted_iota(jnp.int32, (ps, ps), 1)
    upper = (r <= c).astype(f32)

    for row in range(FOX_ROWS):
        base = 3 * npg * row
        k_refs = refs[base:base + npg]
        v_refs = refs[base + npg:base + 2 * npg]
        lf_refs = refs[base + 2 * npg:base + 3 * npg]
        qm = jnp.where(mask, jnp.concatenate([q_ref[row]] * HKV, axis=-1), 0.0) * (HD ** -0.5)

        def update(s, weighted_values, f_end, row=row):
            m_old = m_scr[row]
            m_new = jnp.maximum(m_old, jnp.max(s, -1, keepdims=True))
            alpha = jnp.exp(m_old - m_new)
            p = jnp.exp(s - m_new)
            l_scr[row] = alpha * l_scr[row] + jnp.sum(p, -1, keepdims=True)
            acc_scr[row] = alpha * acc_scr[row] + weighted_values(p)
            m_scr[row] = m_new
            f_scr[row] = f_end

        cum_all = _dot_hi(jnp.concatenate([lf_refs[pg][0] for pg in range(npg)], axis=0), upper)
        f_run = f_scr[row]
        cums = []
        for pg in range(npg):
            cums.append(cum_all[pg * HQ:(pg + 1) * HQ] + f_run)
            f_run = cums[-1][:, ps - 1:ps]
        kt = jnp.concatenate([k_refs[pg][0].reshape(HKV * HD, ps).astype(bf16) for pg in range(npg)], axis=-1)
        vt = jnp.concatenate([v_refs[pg][0].reshape(HKV * HD, ps).astype(bf16) for pg in range(npg)], axis=-1)
        s = jnp.dot(qm.astype(bf16), kt, preferred_element_type=f32) - jnp.concatenate(cums, axis=-1)
        update(s, lambda p, vt=vt: _dot_nt(p, vt), f_run)

        @pl.when(j == pl.num_programs(1) - 1)
        def _(row=row, qm=qm, update=update):
            kn = kn_ref[row]
            vn = vn_ref[row]
            cum = f_scr[row] + lfn_ref[row]
            s = jnp.sum(qm * kn, -1, keepdims=True) - cum
            update(s, lambda p: p * vn, cum)
            o_ref[row] = _fold_heads(acc_scr[row] / l_scr[row], mask)


def _logf_body(f_ref, b_ref, o_ref):
    o_ref[...] = _log_sigmoid(f_ref[...] + b_ref[...])


def _fox_sample(x, cache_k, cache_v, cache_logf, page_table, w_in, b_f):
    nb = x.shape[0]
    nq = HQ * HD
    nk = HKV * HD
    n_pool = cache_k.shape[0]
    n_pages = page_table.shape[1]
    npg = FOX_PAGES
    q, k, v, f = _proj(x, [w_in[:, :nq].astype(bf16), w_in[:, nq:nq + nk].astype(bf16),
                           w_in[:, nq + nk:nq + 2 * nk].astype(bf16), w_in[:, nq + 2 * nk:].astype(bf16)], nb)
    lf_new = pl.pallas_call(_logf_body, out_shape=jax.ShapeDtypeStruct((nb, HQ), f32))(f, b_f.reshape(1, HQ))

    nr = FOX_ROWS

    def page(row, pg, *shape):
        return pl.BlockSpec((1,) + shape,
                            lambda bi, j, pt: (pt[(bi * nr + row) * n_pages + j * npg + pg],) + (0,) * len(shape))

    tok = lambda *shape: pl.BlockSpec((nr,) + shape, lambda bi, j, pt: (bi,) + (0,) * len(shape))
    ck = jnp.transpose(cache_k, (0, 2, 3, 1))
    cv = jnp.transpose(cache_v, (0, 2, 3, 1))
    clf = jnp.transpose(cache_logf, (0, 2, 1))
    page_specs, page_args = [], []
    for row in range(nr):
        for arr, shape in ((ck, (HKV, HD, PAGE_SIZE)), (cv, (HKV, HD, PAGE_SIZE)), (clf, (HQ, PAGE_SIZE))):
            page_specs += [page(row, pg, *shape) for pg in range(npg)]
            page_args += [arr] * npg
    o = pl.pallas_call(
        _fox_step_body,
        grid_spec=pltpu.PrefetchScalarGridSpec(
            num_scalar_prefetch=1,
            grid=(nb // nr, n_pages // npg),
            in_specs=[tok(HQ, HD), tok(1, nk), tok(1, nk), tok(HQ, 1)] + page_specs,
            out_specs=tok(HQ, HD),
            scratch_shapes=[pltpu.VMEM((nr, HQ, 1), f32), pltpu.VMEM((nr, HQ, 1), f32),
                            pltpu.VMEM((nr, HQ, nk), f32), pltpu.VMEM((nr, HQ, 1), f32)]),
        out_shape=jax.ShapeDtypeStruct((nb, HQ, HD), f32),
        compiler_params=_cp("parallel", "arbitrary"),
        name="fox_decode",
    )(page_table.reshape(-1), q.reshape(nb, HQ, HD), k.reshape(nb, 1, nk), v.reshape(nb, 1, nk),
      lf_new.reshape(nb, HQ, 1), *page_args)
    return o.reshape(nb, nq), k, v, lf_new


ROUTER_TILE = 512
MOE_TM = 1024


def _router_body(x_ref, rwt_ref, rb_ref, comb_ref):
    scores = _sigmoid(_dot_nt(rwt_ref[...], x_ref[...]))
    sel = scores + rb_ref[...]
    rows = [sel[e:e + 1, :] for e in range(N_EXPERTS)]
    srow = [scores[e:e + 1, :] for e in range(N_EXPERTS)]
    gs = []
    for g in range(N_GROUPS):
        r = rows[g * EPG:(g + 1) * EPG]
        best = None
        for a in range(EPG):
            for b in range(a + 1, EPG):
                pair = r[a] + r[b]
                best = pair if best is None else jnp.maximum(best, pair)
        gs.append(best)
    g_best = gs[0]
    g_idx = jnp.zeros_like(gs[0], dtype=jnp.int32)
    for g in range(1, N_GROUPS):
        better = gs[g] > g_best
        g_best = jnp.where(better, gs[g], g_best)
        g_idx = jnp.where(better, g, g_idx)

    def in_group(vals, j):
        out = vals[j]
        for g in range(1, N_GROUPS):
            out = jnp.where(g_idx == g, vals[g * EPG + j], out)
        return out

    ig = [in_group(rows, j) for j in range(EPG)]
    sg = [in_group(srow, j) for j in range(EPG)]
    v1, i1, s1 = ig[0], jnp.zeros_like(g_idx), sg[0]
    for j in range(1, EPG):
        better = ig[j] > v1
        v1 = jnp.where(better, ig[j], v1)
        i1 = jnp.where(better, j, i1)
        s1 = jnp.where(better, sg[j], s1)
    v2 = jnp.full_like(v1, -jnp.inf)
    i2 = jnp.zeros_like(g_idx)
    s2 = jnp.zeros_like(s1)
    for j in range(EPG):
        better = (i1 != j) & (ig[j] > v2)
        v2 = jnp.where(better, ig[j], v2)
        i2 = jnp.where(better, j, i2)
        s2 = jnp.where(better, sg[j], s2)
    tot = s1 + s2
    e_row = lax.broadcasted_iota(jnp.int32, scores.shape, 0)
    comb_ref[...] = (jnp.where(e_row == g_idx * EPG + i1, s1 / tot, 0.0)
                     + jnp.where(e_row == g_idx * EPG + i2, s2 / tot, 0.0))


def _router(x, router_w, router_b):
    t, d = x.shape
    tt = min(ROUTER_TILE, t)
    comb_t = pl.pallas_call(
        _router_body,
        grid=(t // tt,),
        in_specs=[pl.BlockSpec((tt, d), lambda i: (i, 0)), pl.BlockSpec((N_EXPERTS, d), lambda i: (0, 0)),
                  pl.BlockSpec((N_EXPERTS, 1), lambda i: (0, 0))],
        out_specs=pl.BlockSpec((N_EXPERTS, tt), lambda i: (0, i)),
        out_shape=jax.ShapeDtypeStruct((N_EXPERTS, t), f32),
        compiler_params=_cp("parallel"),
        name="router",
    )(x, router_w.T, router_b.reshape(N_EXPERTS, 1))
    return comb_t.T


def _moe_dense_body(x_ref, comb_ref, wg_ref, wu_ref, wd_ref, g_ref, b_ref, o_ref, acc_scr):
    e = pl.program_id(1)

    @pl.when(e == 0)
    def _():
        acc_scr[...] = jnp.zeros_like(acc_scr)

    x = x_ref[...].astype(bf16)
    h = _silu(jnp.dot(x, wg_ref[0].astype(bf16), preferred_element_type=f32)) \
        * jnp.dot(x, wu_ref[0].astype(bf16), preferred_element_type=f32)
    y = jnp.dot(h.astype(bf16), wd_ref[0].astype(bf16), preferred_element_type=f32)
    lane = lax.broadcasted_iota(jnp.int32, comb_ref.shape, 1)
    c = jnp.sum(jnp.where(lane == e, comb_ref[...], 0.0), -1, keepdims=True)
    acc_scr[...] += c * y

    @pl.when(e == pl.num_programs(1) - 1)
    def _():
        o_ref[...] = _layer_norm(DN_ALPHA * x_ref[...] + acc_scr[...], g_ref[...], b_ref[...])


def _moe_ln(x, router_w, router_b, w_gate, w_up, w_down, layer, g, b):
    t, d = x.shape
    comb = _router(x, router_w, router_b)
    tm = min(MOE_TM, t)
    return pl.pallas_call(
        _moe_dense_body,
        grid=(t // tm, N_EXPERTS),
        in_specs=[pl.BlockSpec((tm, d), lambda i, e: (i, 0)), pl.BlockSpec((tm, N_EXPERTS), lambda i, e: (i, 0)),
                  pl.BlockSpec((None, 1, d, D_EXPERT), lambda i, e: (layer, e, 0, 0)),
                  pl.BlockSpec((None, 1, d, D_EXPERT), lambda i, e: (layer, e, 0, 0)),
                  pl.BlockSpec((None, 1, D_EXPERT, d), lambda i, e: (layer, e, 0, 0)),
                  pl.BlockSpec((1, d), lambda i, e: (0, 0)), pl.BlockSpec((1, d), lambda i, e: (0, 0))],
        out_specs=pl.BlockSpec((tm, d), lambda i, e: (i, 0)),
        out_shape=jax.ShapeDtypeStruct((t, d), f32),
        scratch_shapes=[pltpu.VMEM((tm, d), f32)],
        compiler_params=_cp("parallel", "arbitrary"),
        name="moe_dense",
    )(x, comb, w_gate, w_up, w_down, g.reshape(1, d), b.reshape(1, d))


def kernel(x_prompt, x_sample, state_a_ssm, state_a_conv, state_b_h, state_b_conv, cache_c_k, cache_c_v, cache_d_k, cache_d_v, cache_d_logf, page_table, ln_g, ln_b, a_w_in, a_conv_w, a_a_log, a_dt_bias, a_norm_w, a_w_out, b_w_in, b_conv_w, b_conv_b, b_w_a, b_b_a, b_w_x, b_b_x, b_lambda, b_w_out, c_w_in, c_sinks, c_w_out, rel_bias, d_w_in, d_b_f, d_w_out, router_w, router_b, moe_w_gate, moe_w_up, moe_w_down):
    bp, lp, d = x_prompt.shape
    nb = x_sample.shape[0]
    xp = x_prompt.reshape(bp * lp, d)
    xs = x_sample.reshape(nb, d)
    tm_p = 256

    def finish(xp, xs, mp, ms, w_out, i, transposed=False):
        w = w_out.astype(bf16)
        xp = _proj_ln(mp, w, xp, ln_g[i, 0], ln_b[i, 0], tm_p, transposed)
        xs = _proj_ln(ms, w, xs, ln_g[i, 0], ln_b[i, 0], nb)
        moe_args = (router_w, router_b, moe_w_gate, moe_w_up, moe_w_down, i, ln_g[i, 1], ln_b[i, 1])
        return _moe_ln(xp, *moe_args), _moe_ln(xs, *moe_args)

    a_args = (a_w_in[0], a_conv_w[0], a_a_log[0], a_dt_bias[0], a_norm_w[0])
    mp, p_a_ssm, p_a_conv = _gdn_prompt(xp.reshape(bp, lp, d), *a_args)
    ms, s_a_ssm, s_a_conv = _gdn_sample(xs, state_a_conv[0], state_a_ssm[0], *a_args)
    xp, xs = finish(xp, xs, mp, ms, a_w_out[0], 0)

    b_args = (b_w_in[0], b_conv_w[0], b_conv_b[0], b_w_a[0], b_b_a[0], b_w_x[0], b_b_x[0], b_lambda[0])
    mp, p_b_h, p_b_conv = _lru_prompt(xp.reshape(bp, lp, d), *b_args)
    ms, s_b_h, s_b_conv = _lru_sample(xs, state_b_conv[0], state_b_h[0], *b_args)
    xp, xs = finish(xp, xs, mp, ms, b_w_out[0], 1)

    mp, kp, vp = _swa_prompt(xp.reshape(bp, lp, d), c_w_in[0], c_sinks[0], rel_bias)
    ms, _, _, s_c_k, s_c_v = _swa_sample(xs, cache_c_k[0], cache_c_v[0], c_w_in[0], c_sinks[0], rel_bias)
    p_c_k = kp[:, lp - WINDOW:].reshape(bp, WINDOW, HKV, HD)
    p_c_v = vp[:, lp - WINDOW:].reshape(bp, WINDOW, HKV, HD)
    xp, xs = finish(xp, xs, mp, ms, c_w_out[0], 2, transposed=True)

    mp, kp, vp, p_d_logf = _fox_prompt(xp.reshape(bp, lp, d), d_w_in[0], d_b_f[0])
    ms, ks, vs, lfs = _fox_sample(xs, cache_d_k[0], cache_d_v[0], cache_d_logf[0], page_table, d_w_in[0], d_b_f[0])
    xp, xs = finish(xp, xs, mp, ms, d_w_out[0], 3, transposed=True)

    return (xp.reshape(bp, lp, d), xs.reshape(nb, 1, d),
            p_a_ssm[None], p_a_conv[None], p_b_h[None], p_b_conv[None], p_c_k[None], p_c_v[None],
            kp.reshape(1, bp, lp, HKV, HD), vp.reshape(1, bp, lp, HKV, HD), p_d_logf[None],
            s_a_ssm[None], s_a_conv[None], s_b_h[None], s_b_conv[None], s_c_k[None], s_c_v[None],
            ks.reshape(1, nb, 1, HKV, HD), vs.reshape(1, nb, 1, HKV, HD), lfs.reshape(1, nb, 1, HQ))
```

```python
import functools
import math

import jax
import jax.numpy as jnp
import numpy as np
from jax import lax
from jax.experimental import pallas as pl
from jax.experimental.pallas import tpu as pltpu

f32 = jnp.float32
bf16 = jnp.bfloat16
HI = lax.Precision.HIGHEST

D_MODEL = 1024
DEPTH = 4
DN_ALPHA = (2 * DEPTH) ** 0.25
LN_EPS = 1e-5
CONV = 4
HA = 8
DKA = 128
DVA = 128
QKV_A = HA * (2 * DKA + DVA)
CHUNK_A = 64
LRU_W = 1024
LRU_BLOCKS = 8
LRU_BW = 128
LRU_C = 8.0
HD = 64
HQ = 16
HKV = 4
GQ = HQ // HKV
WINDOW = 128
N_BUCKETS = 32
MAX_DIST = 128
PAGE_SIZE = 128
N_EXPERTS = 16
N_GROUPS = 4
EPG = 4
D_EXPERT = 512

VMEM_LIMIT = 56 * 1024 * 1024
PROJ_TM = 512


def _cp(*sem):
    return pltpu.CompilerParams(dimension_semantics=sem, vmem_limit_bytes=VMEM_LIMIT)


def _dot(a, b):
    return jnp.dot(a.astype(bf16), b.astype(bf16), preferred_element_type=f32)


def _dot_nt(a, b):
    return lax.dot_general(a.astype(bf16), b.astype(bf16), (((1,), (1,)), ((), ())), preferred_element_type=f32)


def _dot_tn(a, b):
    return lax.dot_general(a.astype(bf16), b.astype(bf16), (((0,), (0,)), ((), ())), preferred_element_type=f32)


def _dot_hi(a, b):
    return jnp.dot(a, b, precision=HI, preferred_element_type=f32)


def _split(a):
    hi = a.astype(bf16)
    return hi, (a - hi.astype(f32)).astype(bf16)


def _dot_split(a, b):
    a_hi, a_lo = _split(a)
    b_hi, b_lo = _split(b)
    d = lambda u, v: jnp.dot(u, v, preferred_element_type=f32)
    return d(a_hi, b_hi) + (d(a_hi, b_lo) + d(a_lo, b_hi))


def _bdot_split(a, b):
    a_hi, a_lo = _split(a)
    b_hi, b_lo = _split(b)
    d = lambda u, v: jnp.einsum('hij,hjk->hik', u, v, preferred_element_type=f32)
    return d(a_hi, b_hi) + (d(a_hi, b_lo) + d(a_lo, b_hi))


def _bdot_nt(a, b):
    return jnp.einsum('hid,hjd->hij', a.astype(bf16), b.astype(bf16), preferred_element_type=f32)


def _dot_nt_hi(a, b):
    return lax.dot_general(a, b, (((1,), (1,)), ((), ())), precision=HI, preferred_element_type=f32)


def _sigmoid(x):
    return 1.0 / (1.0 + jnp.exp(-x))


def _silu(x):
    return x * _sigmoid(x)


def _softplus(x):
    return jnp.maximum(x, 0.0) + jnp.log1p(jnp.exp(-jnp.abs(x)))


def _log_sigmoid(x):
    return -_softplus(-x)


def _eye(n):
    r = lax.broadcasted_iota(jnp.int32, (n, n), 0)
    c = lax.broadcasted_iota(jnp.int32, (n, n), 1)
    return (r == c).astype(f32)


def _proj_body(n_out, x_ref, *refs):
    x = x_ref[...].astype(bf16)
    for w_ref, o_ref in zip(refs[:n_out], refs[n_out:]):
        o_ref[...] = jnp.dot(x, w_ref[...], preferred_element_type=f32)


def _proj(x, ws, tm):
    m, k = x.shape
    n_out = len(ws)
    return pl.pallas_call(
        functools.partial(_proj_body, n_out),
        grid=(m // tm,),
        in_specs=[pl.BlockSpec((tm, k), lambda i: (i, 0))]
        + [pl.BlockSpec(w.shape, lambda i: (0, 0)) for w in ws],
        out_specs=[pl.BlockSpec((tm, w.shape[1]), lambda i: (i, 0)) for w in ws],
        out_shape=[jax.ShapeDtypeStruct((m, w.shape[1]), f32) for w in ws],
        compiler_params=_cp("parallel"),
        name="proj",
    )(x, *ws)


def _attn_proj_body(x_ref, wqt_ref, wk_ref, wv_ref, wvt_ref, qt_ref, k_ref, v_ref, vt_ref):
    x = x_ref[...].astype(bf16)
    nt = lambda w_ref: lax.dot_general(w_ref[...], x, (((1,), (1,)), ((), ())), preferred_element_type=f32)
    qt_ref[0] = (nt(wqt_ref) * (HD ** -0.5)).astype(bf16)
    vt_ref[0] = nt(wvt_ref).astype(bf16)
    k_ref[...] = jnp.dot(x, wk_ref[...], preferred_element_type=f32)
    v_ref[...] = jnp.dot(x, wv_ref[...], preferred_element_type=f32)


def _attn_proj(x, w_in, b, l, tm):
    m, d = x.shape
    nq = HQ * HD
    nk = HKV * HD
    wqt = w_in[:, :nq].T.astype(bf16)
    wk = w_in[:, nq:nq + nk].astype(bf16)
    wv = w_in[:, nq + nk:nq + 2 * nk].astype(bf16)
    nt = l // tm
    full = lambda a: pl.BlockSpec(a.shape, lambda i: (0, 0))
    rows = lambda n: pl.BlockSpec((tm, n), lambda i: (i, 0))
    cols = lambda n: pl.BlockSpec((1, n, tm), lambda i: (i // nt, 0, i % nt))
    return pl.pallas_call(
        _attn_proj_body,
        grid=(m // tm,),
        in_specs=[rows(d), full(wqt), full(wk), full(wv), full(wv.T)],
        out_specs=[cols(nq), rows(nk), rows(nk), cols(nk)],
        out_shape=[jax.ShapeDtypeStruct((b, nq, l), bf16), jax.ShapeDtypeStruct((m, nk), f32),
                   jax.ShapeDtypeStruct((m, nk), f32), jax.ShapeDtypeStruct((b, nk, l), bf16)],
        compiler_params=_cp("parallel"),
        name="attn_proj",
    )(x, wqt, wk, wv, wv.T)


def _layer_norm(z, g, b):
    mu = jnp.mean(z, -1, keepdims=True)
    zc = z - mu
    var = jnp.mean(zc * zc, -1, keepdims=True)
    return zc * lax.rsqrt(var + LN_EPS) * g + b


def _proj_ln_body(h_ref, w_ref, x_ref, g_ref, b_ref, o_ref):
    m = jnp.dot(h_ref[...].astype(bf16), w_ref[...], preferred_element_type=f32)
    o_ref[...] = _layer_norm(DN_ALPHA * x_ref[...] + m, g_ref[...], b_ref[...])


def _proj_ln_t_body(ht_ref, w_ref, x_ref, g_ref, b_ref, o_ref):
    m = lax.dot_general(ht_ref[0].astype(bf16), w_ref[...], (((0,), (0,)), ((), ())), preferred_element_type=f32)
    o_ref[...] = _layer_norm(DN_ALPHA * x_ref[...] + m, g_ref[...], b_ref[...])


def _proj_ln(h, w, x, g, b, tm, transposed=False):
    d = x.shape[1]
    if transposed:
        _, k, l = h.shape
        m = x.shape[0]
        nt = l // tm
        h_spec = pl.BlockSpec((1, k, tm), lambda i: (i // nt, 0, i % nt))
    else:
        m, k = h.shape
        h_spec = pl.BlockSpec((tm, k), lambda i: (i, 0))
    return pl.pallas_call(
        _proj_ln_t_body if transposed else _proj_ln_body,
        grid=(m // tm,),
        in_specs=[h_spec,
                  pl.BlockSpec((k, d), lambda i: (0, 0)),
                  pl.BlockSpec((tm, d), lambda i: (i, 0)),
                  pl.BlockSpec((1, d), lambda i: (0, 0)),
                  pl.BlockSpec((1, d), lambda i: (0, 0))],
        out_specs=pl.BlockSpec((tm, d), lambda i: (i, 0)),
        out_shape=jax.ShapeDtypeStruct((m, d), f32),
        compiler_params=_cp("parallel"),
        name="proj_ln",
    )(h, w, x, g.reshape(1, d), b.reshape(1, d))


GDN_TILE = 256


def _gdn_gates(ab, alog, dtb):
    g = -jnp.exp(alog) * _softplus(ab[:, 0:HA] + dtb)
    beta = _sigmoid(ab[:, HA:2 * HA])
    return g, beta


def _l2norm(t):
    return t * lax.rsqrt(jnp.sum(t * t, -1, keepdims=True) + 1e-6)


def _gdn_prep_body(qkv_ref, halo_ref, ab_ref, cw_ref, alog_ref, dtb_ref,
                   u_ref, w_ref, qd_ref, kd_ref, qk_ref, eg_ref, xp_scr, y_scr):
    i = pl.program_id(1)
    tc = qkv_ref.shape[1]
    c = CHUNK_A
    xp_scr[0:8, :] = jnp.where(i > 0, halo_ref[0], 0.0)
    xp_scr[8:8 + tc, :] = qkv_ref[0]
    cw = cw_ref[...]
    y = xp_scr[5:5 + tc, :] * cw[0:1, :]
    for j in range(1, CONV):
        y = y + xp_scr[5 + j:5 + j + tc, :] * cw[j:j + 1, :]
    y_scr[...] = _silu(y)

    row = lax.broadcasted_iota(jnp.int32, (c, c), 0)
    col = lax.broadcasted_iota(jnp.int32, (c, c), 1)
    incl = row >= col
    strict = row > col
    tri = incl.astype(f32)
    eye_c = (row == col).astype(f32)
    eye_h = _eye(HA)
    alog = alog_ref[...]
    dtb = dtb_ref[...]

    def chunk(n, carry):
        r0 = pl.multiple_of(n * c, c)
        rows = pl.ds(r0, c)
        g, beta = _gdn_gates(ab_ref[0, rows, :], alog, dtb)
        gc = _dot_hi(tri, g)
        gct = _dot_nt_hi(eye_h, gc)
        egc = jnp.exp(gc)
        eg_ref[0, rows, :] = egc
        heads = lambda t, off: jnp.stack([t(y_scr[rows, off + h * DKA:off + (h + 1) * DKA]) for h in range(HA)])
        q = heads(_l2norm, 0) * (DKA ** -0.5)
        k = heads(_l2norm, HA * DKA)
        v = heads(lambda t: t, 2 * HA * DKA)
        col = lambda t: jnp.stack([t[:, h:h + 1] for h in range(HA)])
        gcol, bcol, egcol = col(gc), col(beta), col(egc)
        diff = gcol - gct[:, None, :]
        dec = jnp.where(incl, jnp.exp(jnp.where(incl, diff, 0.0)), 0.0)
        a = jnp.where(strict, _bdot_nt(k, k) * dec, 0.0) * bcol
        x = eye_c - a
        p = _bdot_split(a, a)
        for _ in range(4):
            xp = _bdot_split(jnp.concatenate([x, p], axis=1), p)
            x = x + xp[:, :c]
            p = xp[:, c:]
        x = x + _bdot_split(x, p)
        sol = _bdot_split(x, jnp.concatenate([v * bcol, k * (bcol * egcol)], axis=-1))
        qd = q * egcol
        kd = k * jnp.exp(jnp.stack([gc[c - 1:c, h:h + 1] for h in range(HA)]) - gcol)
        qk = _bdot_nt(q, k) * dec
        for h in range(HA):
            hs = slice(h * DVA, (h + 1) * DVA)
            u_ref[0, rows, hs] = sol[h, :, :DVA]
            w_ref[0, rows, hs] = sol[h, :, DVA:]
            qd_ref[0, rows, hs] = qd[h]
            kd_ref[0, rows, hs] = kd[h]
            qk_ref[0, rows, h * c:(h + 1) * c] = qk[h]
        return carry

    lax.fori_loop(0, tc // c, chunk, 0)


def _gdn_prep(qkv, ab, conv_w, a_log, dt_bias):
    b, l, _ = qkv.shape
    tc = min(GDN_TILE, l)
    hb = tc // 8
    big = lambda n: pl.BlockSpec((1, tc, n), lambda bi, i: (bi, i, 0))
    full = lambda a: pl.BlockSpec(a.shape, lambda bi, i: (0, 0))
    wide = jax.ShapeDtypeStruct((b, l, HA * DVA), f32)
    return pl.pallas_call(
        _gdn_prep_body,
        grid=(b, l // tc),
        in_specs=[big(QKV_A),
                  pl.BlockSpec((1, 8, QKV_A), lambda bi, i: (bi, jnp.maximum(i * hb - 1, 0), 0)),
                  big(2 * HA), full(conv_w), full(a_log), full(dt_bias)],
        out_specs=[big(HA * DVA)] * 4 + [big(HA * CHUNK_A), big(HA)],
        out_shape=[wide] * 4 + [jax.ShapeDtypeStruct((b, l, HA * CHUNK_A), f32),
                                jax.ShapeDtypeStruct((b, l, HA), f32)],
        scratch_shapes=[pltpu.VMEM((tc + 8, QKV_A), f32), pltpu.VMEM((tc, QKV_A), f32)],
        compiler_params=_cp("parallel", "parallel"),
        name="gdn_prep",
    )(qkv, qkv, ab, conv_w, a_log, dt_bias)


def _gdn_scan_body(u_ref, w_ref, qd_ref, kd_ref, qk_ref, eg_ref, gate_ref, nw_ref, y_ref, s_ref):
    i = pl.program_id(1)
    tc = u_ref.shape[1]
    c = CHUNK_A

    @pl.when(i == 0)
    def _():
        s_ref[...] = jnp.zeros_like(s_ref)

    nw = nw_ref[...]

    def chunk(n, carry):
        r0 = pl.multiple_of(n * c, c)
        rows = pl.ds(r0, c)
        eg_last = eg_ref[0, pl.ds(r0 + c - 1, 1), :]
        heads = lambda ref, width: jnp.stack([ref[0, rows, h * width:(h + 1) * width] for h in range(HA)])
        bd = lambda a, b: jnp.einsum('hij,hjk->hik', a.astype(bf16), b.astype(bf16), preferred_element_type=f32)
        s = s_ref[0]
        ws_qs = bd(jnp.concatenate([heads(w_ref, DVA), heads(qd_ref, DVA)], axis=1), s)
        v_new = heads(u_ref, DVA) - ws_qs[:, :c]
        o = ws_qs[:, c:] + bd(heads(qk_ref, c), v_new)
        decay = jnp.stack([eg_last[:, h:h + 1] for h in range(HA)])
        s_ref[0] = s * decay + jnp.einsum('hcd,hce->hde', heads(kd_ref, DVA).astype(bf16), v_new.astype(bf16),
                                          preferred_element_type=f32)
        of = o * lax.rsqrt(jnp.mean(o * o, -1, keepdims=True) + 1e-6)
        for h in range(HA):
            hs = slice(h * DVA, (h + 1) * DVA)
            y_ref[0, rows, hs] = of[h] * nw * _silu(gate_ref[0, rows, hs])
        return carry

    lax.fori_loop(0, tc // c, chunk, 0)


def _gdn_scan(u, w, qd, kd, qk, eg, gate, norm_w):
    b, l, _ = u.shape
    tc = min(GDN_TILE, l)
    big = lambda n: pl.BlockSpec((1, tc, n), lambda bi, i: (bi, i, 0))
    return pl.pallas_call(
        _gdn_scan_body,
        grid=(b, l // tc),
        in_specs=[big(HA * DVA)] * 4 + [big(HA * CHUNK_A), big(HA), big(HA * DVA),
                                       pl.BlockSpec((1, DVA), lambda bi, i: (0, 0))],
        out_specs=[big(HA * DVA), pl.BlockSpec((1, HA, DKA, DVA), lambda bi, i: (bi, 0, 0, 0))],
        out_shape=[jax.ShapeDtypeStruct((b, l, HA * DVA), f32),
                   jax.ShapeDtypeStruct((b, HA, DKA, DVA), f32)],
        compiler_params=_cp("parallel", "arbitrary"),
        name="gdn_scan",
    )(u, w, qd, kd, qk, eg, gate, norm_w.reshape(1, DVA))


def _gdn_prompt(x, w_in, conv_w, a_log, dt_bias, norm_w):
    b, l, d = x.shape
    wq = w_in[:, :QKV_A].astype(bf16)
    wg = w_in[:, QKV_A:QKV_A + HA * DVA].astype(bf16)
    wab = w_in[:, QKV_A + HA * DVA:].astype(bf16)
    qkv, gate, ab = _proj(x.reshape(b * l, d), [wq, wg, wab], min(PROJ_TM, l))
    qkv = qkv.reshape(b, l, QKV_A)
    u, w, qd, kd, qk, eg = _gdn_prep(qkv, ab.reshape(b, l, 2 * HA), conv_w,
                                     a_log.reshape(1, HA), dt_bias.reshape(1, HA))
    y, s = _gdn_scan(u, w, qd, kd, qk, eg, gate.reshape(b, l, HA * DVA), norm_w)
    return y.reshape(b * l, HA * DVA), s, qkv[:, l - (CONV - 1):]


GDN_BT = 8


def _gdn_step_body(qkv_ref, gate_ref, ab_ref, buf_ref, s_ref, cw_ref, alog_ref, dtb_ref, nw_ref,
                   y_ref, s_out_ref, buf_out_ref, q_scr, k_scr, v_scr, eg_scr, beta_scr):
    n = QKV_A
    new = qkv_ref[...]
    cw = cw_ref[...]
    y = new * cw[CONV - 1:CONV, :]
    for j in range(CONV - 1):
        y = y + buf_ref[:, j * n:(j + 1) * n] * cw[j:j + 1, :]
    y = _silu(y)
    buf_out_ref[:, 0:2 * n] = buf_ref[:, n:3 * n]
    buf_out_ref[:, 2 * n:3 * n] = new
    for h in range(HA):
        q_scr[:, h * DKA:(h + 1) * DKA] = _l2norm(y[:, h * DKA:(h + 1) * DKA]) * (DKA ** -0.5)
        k_scr[:, h * DKA:(h + 1) * DKA] = _l2norm(y[:, HA * DKA + h * DKA:HA * DKA + (h + 1) * DKA])
    v_scr[...] = y[:, 2 * HA * DKA:]
    g, beta = _gdn_gates(ab_ref[...], alog_ref[...], dtb_ref[...])
    eg_scr[...] = jnp.exp(g)
    beta_scr[...] = beta
    eye = _eye(DKA)
    row8 = lax.broadcasted_iota(jnp.int32, (8, DKA), 0)
    nw = nw_ref[...]

    for bi in range(qkv_ref.shape[0]):
        r = slice(bi, bi + 1)
        for h in range(HA):
            hs = slice(h * DKA, (h + 1) * DKA)
            k_row = k_scr[r, hs]
            q_row = q_scr[r, hs]
            kq = jnp.where(row8 == 0, k_row, jnp.where(row8 == 1, q_row, 0.0))
            cols = _dot_nt_hi(eye, kq)
            k_col = cols[:, 0:1]
            q_col = cols[:, 1:2]
            sd = s_ref[bi, h] * eg_scr[r, h:h + 1]
            pred = jnp.sum(k_col * sd, axis=0, keepdims=True)
            delta = beta_scr[r, h:h + 1] * (v_scr[r, hs] - pred)
            s_new = sd + k_col * delta
            s_out_ref[bi, h] = s_new
            o = jnp.sum(q_col * s_new, axis=0, keepdims=True)
            of = o * lax.rsqrt(jnp.mean(o * o, -1, keepdims=True) + 1e-6)
            y_ref[r, hs] = of * nw * _silu(gate_ref[r, hs])


def _gdn_step(qkv, gate, ab, conv_buf, s0, conv_w, a_log, dt_bias, norm_w):
    nb = qkv.shape[0]
    bt = GDN_BT
    rowblk = lambda n: pl.BlockSpec((bt, n), lambda i: (i, 0))
    full = lambda a: pl.BlockSpec(a.shape, lambda i: (0, 0))
    sblk = pl.BlockSpec((bt, HA, DKA, DVA), lambda i: (i, 0, 0, 0))
    nw = norm_w.reshape(1, DVA)
    al = a_log.reshape(1, HA)
    db = dt_bias.reshape(1, HA)
    return pl.pallas_call(
        _gdn_step_body,
        grid=(nb // bt,),
        in_specs=[rowblk(QKV_A), rowblk(HA * DVA), rowblk(2 * HA), rowblk(3 * QKV_A), sblk,
                  full(conv_w), full(al), full(db), full(nw)],
        out_specs=[rowblk(HA * DVA), sblk, rowblk(3 * QKV_A)],
        out_shape=[jax.ShapeDtypeStruct((nb, HA * DVA), f32),
                   jax.ShapeDtypeStruct(s0.shape, f32),
                   jax.ShapeDtypeStruct((nb, 3 * QKV_A), f32)],
        scratch_shapes=[pltpu.VMEM((bt, HA * DKA), f32)] * 3 + [pltpu.VMEM((bt, HA), f32)] * 2,
        compiler_params=_cp("parallel"),
        name="gdn_step",
    )(qkv, gate, ab, conv_buf.reshape(nb, 3 * QKV_A), s0, conv_w, al, db, nw)


def _gdn_sample(x, conv_buf, s0, w_in, conv_w, a_log, dt_bias, norm_w):
    nb = x.shape[0]
    wq = w_in[:, :QKV_A].astype(bf16)
    wg = w_in[:, QKV_A:QKV_A + HA * DVA].astype(bf16)
    wab = w_in[:, QKV_A + HA * DVA:].astype(bf16)
    qkv, gate, ab = _proj(x, [wq, wg, wab], nb)
    y, s, buf = _gdn_step(qkv, gate, ab, conv_buf, s0, conv_w, a_log, dt_bias, norm_w)
    return y, s, buf.reshape(nb, CONV - 1, QKV_A)


LRU_TILE = 256


def _gelu(x):
    return 0.5 * x * (1.0 + jnp.tanh(math.sqrt(2.0 / math.pi) * (x + 0.044715 * x * x * x)))


def _lru_gates(u, wa_ref, ba, wx_ref, bx, lam):
    ra, xa = [], []
    for n in range(LRU_BLOCKS):
        ub = u[:, n * LRU_BW:(n + 1) * LRU_BW].astype(bf16)
        ra.append(jnp.dot(ub, wa_ref[n], preferred_element_type=f32))
        xa.append(jnp.dot(ub, wx_ref[n], preferred_element_type=f32))
    r = _sigmoid(jnp.concatenate(ra, axis=-1) + ba)
    i_g = _sigmoid(jnp.concatenate(xa, axis=-1) + bx)
    log_a = -LRU_C * r * _softplus(-lam)
    a = jnp.exp(log_a)
    t = jnp.tanh(log_a)
    b = jnp.sqrt(-2.0 * t / (1.0 - t)) * (i_g * u)
    return a, b


def _lru_prompt_body(gate_ref, rec_ref, halo_ref, cw_ref, cb_ref, wa_ref, ba_ref, wx_ref, bx_ref, lam_ref,
                     y_ref, h_ref, xp_scr):
    i = pl.program_id(1)
    tt = rec_ref.shape[1]

    @pl.when(i == 0)
    def _():
        h_ref[...] = jnp.zeros_like(h_ref)

    xp_scr[0:8, :] = jnp.where(i > 0, halo_ref[0], 0.0)
    xp_scr[8:8 + tt, :] = rec_ref[0]
    cw = cw_ref[...]
    u = xp_scr[5:5 + tt, :] * cw[0:1, :]
    for j in range(1, CONV):
        u = u + xp_scr[5 + j:5 + j + tt, :] * cw[j:j + 1, :]
    u = u + cb_ref[...]
    a, b = _lru_gates(u, wa_ref, ba_ref[...], wx_ref, bx_ref[...], lam_ref[...])
    row = lax.broadcasted_iota(jnp.int32, a.shape, 0)
    s = 1
    while s < tt:
        keep = row >= s
        a_sh = jnp.where(keep, pltpu.roll(a, s, 0), 1.0)
        b_sh = jnp.where(keep, pltpu.roll(b, s, 0), 0.0)
        b = a * b_sh + b
        a = a * a_sh
        s *= 2
    h = b + a * h_ref[0]
    h_ref[0] = h[tt - 1:tt, :]
    y_ref[0] = _gelu(gate_ref[0]) * h


def _lru_prompt(x, w_in, conv_w, conv_b, w_a, b_a, w_x, b_x, lam):
    bsz, l, d = x.shape
    w = LRU_W
    gate_in, rec_in = _proj(x.reshape(bsz * l, d), [w_in[:, :w].astype(bf16), w_in[:, w:].astype(bf16)],
                            min(PROJ_TM, l))
    rec3 = rec_in.reshape(bsz, l, w)
    tt = min(LRU_TILE, l)
    hb = tt // 8
    big = pl.BlockSpec((1, tt, w), lambda bi, i: (bi, i, 0))
    vec = pl.BlockSpec((1, w), lambda bi, i: (0, 0))
    blkw = pl.BlockSpec((LRU_BLOCKS, LRU_BW, LRU_BW), lambda bi, i: (0, 0, 0))
    y, h = pl.pallas_call(
        _lru_prompt_body,
        grid=(bsz, l // tt),
        in_specs=[big, big, pl.BlockSpec((1, 8, w), lambda bi, i: (bi, jnp.maximum(i * hb - 1, 0), 0)),
                  pl.BlockSpec((CONV, w), lambda bi, i: (0, 0)), vec, blkw, vec, blkw, vec, vec],
        out_specs=[big, pl.BlockSpec((1, 1, w), lambda bi, i: (bi, 0, 0))],
        out_shape=[jax.ShapeDtypeStruct((bsz, l, w), f32), jax.ShapeDtypeStruct((bsz, 1, w), f32)],
        scratch_shapes=[pltpu.VMEM((tt + 8, w), f32)],
        compiler_params=_cp("parallel", "arbitrary"),
        name="lru_prompt",
    )(gate_in.reshape(bsz, l, w), rec3, rec3, conv_w, conv_b.reshape(1, w), w_a.astype(bf16), b_a.reshape(1, w),
      w_x.astype(bf16), b_x.reshape(1, w), lam.reshape(1, w))
    return y.reshape(bsz * l, w), h.reshape(bsz, w), rec3[:, l - (CONV - 1):]


def _lru_step_body(gate_ref, rec_ref, buf_ref, h0_ref, cw_ref, cb_ref, wa_ref, ba_ref, wx_ref, bx_ref, lam_ref,
                   y_ref, h_ref, buf_out_ref):
    w = LRU_W
    new = rec_ref[...]
    cw = cw_ref[...]
    u = new * cw[CONV - 1:CONV, :]
    for j in range(CONV - 1):
        u = u + buf_ref[:, j * w:(j + 1) * w] * cw[j:j + 1, :]
    u = u + cb_ref[...]
    buf_out_ref[:, 0:2 * w] = buf_ref[:, w:3 * w]
    buf_out_ref[:, 2 * w:3 * w] = new
    a, b = _lru_gates(u, wa_ref, ba_ref[...], wx_ref, bx_ref[...], lam_ref[...])
    h = b + a * h0_ref[...]
    h_ref[...] = h
    y_ref[...] = _gelu(gate_ref[...]) * h


def _lru_sample(x, conv_buf, h0, w_in, conv_w, conv_b, w_a, b_a, w_x, b_x, lam):
    nb = x.shape[0]
    w = LRU_W
    gate_in, rec_in = _proj(x, [w_in[:, :w].astype(bf16), w_in[:, w:].astype(bf16)], nb)
    y, h, buf = pl.pallas_call(
        _lru_step_body,
        out_shape=[jax.ShapeDtypeStruct((nb, w), f32), jax.ShapeDtypeStruct((nb, w), f32),
                   jax.ShapeDtypeStruct((nb, 3 * w), f32)],
        compiler_params=pltpu.CompilerParams(vmem_limit_bytes=VMEM_LIMIT),
        name="lru_step",
    )(gate_in, rec_in, conv_buf.reshape(nb, 3 * w), h0, conv_w, conv_b.reshape(1, w), w_a.astype(bf16),
      b_a.reshape(1, w), w_x.astype(bf16), b_x.reshape(1, w), lam.reshape(1, w))
    return y, h, buf.reshape(nb, CONV - 1, w)


NEG = -1e30


def _t5_bucket(rel):
    n = jnp.maximum(rel, 0)
    max_exact = N_BUCKETS // 2
    large = max_exact + (jnp.log(jnp.maximum(n, 1).astype(f32) / max_exact)
                         / math.log(MAX_DIST / max_exact) * (N_BUCKETS - max_exact)).astype(jnp.int32)
    return jnp.where(n < max_exact, n, jnp.minimum(large, N_BUCKETS - 1))


def _swa_prompt_body(qt_ref, kp_ref, kc_ref, vtp_ref, vtc_ref, bucket_ref, sinks_ref, rb_ref, o_ref, bias_scr):
    i = pl.program_id(1)
    w = WINDOW

    @pl.when(i == 0)
    def _():
        bucket = bucket_ref[...]
        rel = (lax.broadcasted_iota(jnp.int32, (2 * w, w), 1) + w
               - lax.broadcasted_iota(jnp.int32, (2 * w, w), 0))
        in_window = (rel >= 0) & (rel <= w)
        for h in range(HQ):
            b = jnp.zeros((2 * w, w), f32)
            for n in range(N_BUCKETS):
                b = jnp.where(bucket == n, rb_ref[n, h], b)
            bias_scr[h // GQ, :, (h % GQ) * w:(h % GQ + 1) * w] = jnp.where(in_window, b, NEG)

    hidden_rows = jnp.where(i == 0, w, 0)
    hide = lax.broadcasted_iota(jnp.int32, (2 * w, GQ * w), 0) < hidden_rows
    for kvh in range(HKV):
        heads = range(kvh * GQ, (kvh + 1) * GQ)
        qt = jnp.concatenate([qt_ref[0, h] for h in heads], axis=-1)
        kk = jnp.concatenate([kp_ref[0, kvh], kc_ref[0, kvh]], axis=0)
        s = jnp.where(hide, NEG, jnp.dot(kk, qt, preferred_element_type=f32) + bias_scr[kvh])
        sink = jnp.concatenate([jnp.full((1, w), sinks_ref[h], f32) for h in heads], axis=-1)
        m = jnp.maximum(jnp.max(s, 0, keepdims=True), sink)
        e = jnp.exp(s - m)
        inv = 1.0 / (jnp.sum(e, 0, keepdims=True) + jnp.exp(sink - m))
        vt = jnp.concatenate([vtp_ref[0, kvh], vtc_ref[0, kvh]], axis=-1)
        ot = jnp.dot(vt, (e * inv).astype(bf16), preferred_element_type=f32)
        for g, h in enumerate(heads):
            o_ref[0, h] = ot[:, g * w:(g + 1) * w]


def _swa_prompt(x, w_in, sinks, rel_bias):
    b, l, d = x.shape
    nq = HQ * HD
    nk = HKV * HD
    qt, k, v, vt = _attn_proj(x.reshape(b * l, d), w_in, b, l, min(PROJ_TM, l))
    k = k.reshape(b, l, nk)
    v = v.reshape(b, l, nk)
    w = WINDOW
    rel = jnp.arange(w)[None, :] + w - jnp.arange(2 * w)[:, None]
    bucket = _t5_bucket(rel).astype(jnp.int32)
    qt = qt.reshape(b, HQ, HD, l)
    kh = jnp.swapaxes(k.reshape(b, l, HKV, HD), 1, 2).astype(bf16)
    vt = vt.reshape(b, HKV, HD, l)
    qblk = pl.BlockSpec((1, HQ, HD, w), lambda bi, i: (bi, 0, 0, i))
    smem = pl.BlockSpec(memory_space=pltpu.SMEM)
    ot = pl.pallas_call(
        _swa_prompt_body,
        grid=(b, l // w),
        in_specs=[qblk,
                  pl.BlockSpec((1, HKV, w, HD), lambda bi, i: (bi, 0, jnp.maximum(i - 1, 0), 0)),
                  pl.BlockSpec((1, HKV, w, HD), lambda bi, i: (bi, 0, i, 0)),
                  pl.BlockSpec((1, HKV, HD, w), lambda bi, i: (bi, 0, 0, jnp.maximum(i - 1, 0))),
                  pl.BlockSpec((1, HKV, HD, w), lambda bi, i: (bi, 0, 0, i)),
                  pl.BlockSpec((2 * w, w), lambda bi, i: (0, 0)), smem, smem],
        out_specs=qblk,
        out_shape=jax.ShapeDtypeStruct((b, HQ, HD, l), f32),
        scratch_shapes=[pltpu.VMEM((HKV, 2 * w, GQ * w), f32)],
        compiler_params=_cp("parallel", "arbitrary"),
        name="swa_prompt",
    )(qt, kh, kh, vt, vt, bucket, sinks, rel_bias)
    return ot.reshape(b, nq, l), k, v


SWA_BT = 8


def _head_mask():
    r = lax.broadcasted_iota(jnp.int32, (HQ, HKV * HD), 0)
    c = lax.broadcasted_iota(jnp.int32, (HQ, HKV * HD), 1)
    return (r // GQ) == (c // HD)


def _fold_heads(o, mask):
    o = jnp.where(mask, o, 0.0)
    acc = o[:, 0:HD]
    for c in range(1, HKV):
        acc = acc + o[:, c * HD:(c + 1) * HD]
    return acc


def _swa_step_body(q_ref, kn_ref, vn_ref, kb_ref, vb_ref, onehot_ref, rbt_ref, sinks_ref, o_ref, kb_out, vb_out):
    w = WINDOW
    mask = _head_mask()
    bias_all = _dot_hi(rbt_ref[...], onehot_ref[...])
    bias = bias_all[:, :w]
    bias_new = bias_all[:, w:w + 1]
    sink = sinks_ref[...]
    for bi in range(q_ref.shape[0]):
        qm = jnp.where(mask, jnp.concatenate([q_ref[bi]] * HKV, axis=-1), 0.0)
        kn = kn_ref[bi]
        vn = vn_ref[bi]
        s = _dot_nt(qm, kb_ref[bi]) * (HD ** -0.5) + bias
        s_new = jnp.sum(qm * kn, -1, keepdims=True) * (HD ** -0.5) + bias_new
        m = jnp.maximum(jnp.maximum(jnp.max(s, -1, keepdims=True), s_new), sink)
        e = jnp.exp(s - m)
        e_new = jnp.exp(s_new - m)
        denom = jnp.sum(e, -1, keepdims=True) + e_new + jnp.exp(sink - m)
        o = (_dot(e / denom, vb_ref[bi]) + (e_new / denom) * vn)
        o_ref[bi] = _fold_heads(o, mask)
        kb_out[bi, 0:w - 1, :] = kb_ref[bi, 1:w, :]
        kb_out[bi, w - 1:w, :] = kn
        vb_out[bi, 0:w - 1, :] = vb_ref[bi, 1:w, :]
        vb_out[bi, w - 1:w, :] = vn


def _swa_sample(x, k_buf, v_buf, w_in, sinks, rel_bias):
    nb = x.shape[0]
    nq = HQ * HD
    nk = HKV * HD
    w = WINDOW
    q, k, v = _proj(x, [w_in[:, :nq].astype(bf16), w_in[:, nq:nq + nk].astype(bf16), w_in[:, nq + nk:].astype(bf16)], nb)
    slots = jnp.arange(w + 128)
    bucket = _t5_bucket(jnp.where(slots <= w, w - slots, 0))
    onehot = (bucket[None, :] == jnp.arange(N_BUCKETS)[:, None]).astype(f32)
    bt = SWA_BT
    full = lambda a: pl.BlockSpec(a.shape, lambda i: (0,) * a.ndim)
    bufblk = pl.BlockSpec((bt, w, nk), lambda i: (i, 0, 0))
    rbt = rel_bias.T
    sk = sinks.reshape(HQ, 1)
    o, kb, vb = pl.pallas_call(
        _swa_step_body,
        grid=(nb // bt,),
        in_specs=[pl.BlockSpec((bt, HQ, HD), lambda i: (i, 0, 0)),
                  pl.BlockSpec((bt, 1, nk), lambda i: (i, 0, 0)), pl.BlockSpec((bt, 1, nk), lambda i: (i, 0, 0)),
                  bufblk, bufblk, full(onehot), full(rbt), full(sk)],
        out_specs=[pl.BlockSpec((bt, HQ, HD), lambda i: (i, 0, 0)), bufblk, bufblk],
        out_shape=[jax.ShapeDtypeStruct((nb, HQ, HD), f32), jax.ShapeDtypeStruct((nb, w, nk), f32),
                   jax.ShapeDtypeStruct((nb, w, nk), f32)],
        compiler_params=_cp("parallel"),
        name="swa_step",
    )(q.reshape(nb, HQ, HD), k.reshape(nb, 1, nk), v.reshape(nb, 1, nk), k_buf.reshape(nb, w, nk),
      v_buf.reshape(nb, w, nk), onehot, rbt, sk)
    return o.reshape(nb, nq), k, v, kb.reshape(nb, w, HKV, HD), vb.reshape(nb, w, HKV, HD)


FOX_PREP_TILE = 512
FOX_TQ = 512
FOX_TK = 1024


def _fox_prep_body(x_ref, wft_ref, bf_ref, lf_ref, cum_ref, carry):
    i = pl.program_id(1)
    tt = x_ref.shape[1]

    @pl.when(i == 0)
    def _():
        carry[...] = jnp.zeros_like(carry)

    f = lax.dot_general(wft_ref[...], x_ref[0].astype(bf16), (((1,), (1,)), ((), ())), preferred_element_type=f32)
    lf = _log_sigmoid(f + bf_ref[...])
    r = lax.broadcasted_iota(jnp.int32, (tt, tt), 0)
    c = lax.broadcasted_iota(jnp.int32, (tt, tt), 1)
    cum = _dot_hi(lf, (r <= c).astype(f32)) + carry[...]
    lf_ref[0] = lf
    cum_ref[0] = cum
    carry[...] = cum[:, tt - 1:tt]


FOX_SUM_ROWS = 16


def _fox_flash_body(qt_ref, k_ref, vt_ref, fk_ref, o_ref, m_scr, acc_scr):
    i = pl.program_id(2)
    tq = qt_ref.shape[3]
    tk = FOX_TK
    m_scr[...] = jnp.full_like(m_scr, NEG)
    acc_scr[...] = jnp.zeros_like(acc_scr)
    ones = jnp.ones((FOX_SUM_ROWS, tk), bf16)

    def block(j, masked):
        c0 = pl.multiple_of(j * tk, tk)
        k = k_ref[0, 0, pl.ds(c0, tk), :]
        vt = jnp.concatenate([vt_ref[0, 0, :, pl.ds(c0, tk)], ones], axis=0)
        fk = fk_ref[0, 0, pl.ds(c0, tk), :]
        if masked:
            s_pos = c0 + lax.broadcasted_iota(jnp.int32, (tk, tq), 0)
            t_pos = i * tq + lax.broadcasted_iota(jnp.int32, (tk, tq), 1)
            visible = s_pos <= t_pos
        for g in range(GQ):
            s = jnp.dot(k, qt_ref[0, g], preferred_element_type=f32) - fk[:, g:g + 1]
            if masked:
                s = jnp.where(visible, s, NEG)
            m_old = m_scr[g]
            m_new = jnp.maximum(m_old, jnp.max(s, 0, keepdims=True))
            alpha = jnp.exp(m_old - m_new)
            p = jnp.exp(s - m_new)
            acc_scr[g] = alpha * acc_scr[g] + jnp.dot(vt, p.astype(bf16), preferred_element_type=f32)
            m_scr[g] = m_new

    n_full = (i * tq) // tk

    def full_block(j, carry):
        block(j, False)
        return carry

    lax.fori_loop(0, n_full, full_block, 0)
    for jj in range(pl.cdiv(tq, tk)):
        block(n_full + jj, True)
    for g in range(GQ):
        o_ref[0, g] = acc_scr[g, :HD] / acc_scr[g, HD:HD + 1]


def _fox_prompt(x, w_in, b_f):
    b, l, d = x.shape
    nq = HQ * HD
    nk = HKV * HD
    qt, k, v, vt = _attn_proj(x.reshape(b * l, d), w_in, b, l, min(PROJ_TM, l))
    tt = min(FOX_PREP_TILE, l)
    wft = w_in[:, nq + 2 * nk:].T.astype(bf16)
    row = pl.BlockSpec((1, HQ, tt), lambda bi, i: (bi, 0, i))
    lft, cumt = pl.pallas_call(
        _fox_prep_body,
        grid=(b, l // tt),
        in_specs=[pl.BlockSpec((1, tt, d), lambda bi, i: (bi, i, 0)),
                  pl.BlockSpec((HQ, d), lambda bi, i: (0, 0)), pl.BlockSpec((HQ, 1), lambda bi, i: (0, 0))],
        out_specs=[row, row],
        out_shape=[jax.ShapeDtypeStruct((b, HQ, l), f32)] * 2,
        scratch_shapes=[pltpu.VMEM((HQ, 1), f32)],
        compiler_params=_cp("parallel", "arbitrary"),
        name="fox_prep",
    )(x, wft, b_f.reshape(HQ, 1))
    fk = jnp.swapaxes(cumt.reshape(b, HKV, GQ, l), 2, 3)
    qt = qt.reshape(b, HQ, HD, l)
    kh = jnp.swapaxes(k.reshape(b, l, HKV, HD), 1, 2).astype(bf16)
    vt = vt.reshape(b, HKV, HD, l)
    tq = min(FOX_TQ, l)
    qblk = pl.BlockSpec((1, GQ, HD, tq), lambda bi, h, i: (bi, h, 0, i))
    ot = pl.pallas_call(
        _fox_flash_body,
        grid=(b, HKV, l // tq),
        in_specs=[qblk,
                  pl.BlockSpec((1, 1, l, HD), lambda bi, h, i: (bi, h, 0, 0)),
                  pl.BlockSpec((1, 1, HD, l), lambda bi, h, i: (bi, h, 0, 0)),
                  pl.BlockSpec((1, 1, l, GQ), lambda bi, h, i: (bi, h, 0, 0))],
        out_specs=qblk,
        out_shape=jax.ShapeDtypeStruct((b, HQ, HD, l), f32),
        scratch_shapes=[pltpu.VMEM((GQ, 1, tq), f32), pltpu.VMEM((GQ, HD + FOX_SUM_ROWS, tq), f32)],
        compiler_params=_cp("parallel", "parallel", "arbitrary"),
        name="fox_flash",
    )(qt, kh, vt, fk)
    return ot.reshape(b, nq, l), k.reshape(b, l, nk), v.reshape(b, l, nk), jnp.swapaxes(lft, 1, 2)


FOX_PAGES = 16
FOX_ROWS = 1


def _fox_step_body(pt_ref, q_ref, kn_ref, vn_ref, lfn_ref, *refs):
    npg = FOX_PAGES
    n_in = 3 * npg * FOX_ROWS
    o_ref, m_scr, l_scr, acc_scr, f_scr = refs[n_in:]
    j = pl.program_id(1)
    ps = PAGE_SIZE
    mask = _head_mask()

    @pl.when(j == 0)
    def _():
        m_scr[...] = jnp.full_like(m_scr, NEG)
        l_scr[...] = jnp.zeros_like(l_scr)
        acc_scr[...] = jnp.zeros_like(acc_scr)
        f_scr[...] = jnp.zeros_like(f_scr)

    r = lax.broadcasted_iota(jnp.int32, (ps, ps), 0)
    c = lax.broadcasted_iota(jnp.int32, (ps, ps), 1)
    upper = (r <= c).astype(f32)

    for row in range(FOX_ROWS):
        base = 3 * npg * row
        k_refs = refs[base:base + npg]
        v_refs = refs[base + npg:base + 2 * npg]
        lf_refs = refs[base + 2 * npg:base + 3 * npg]
        qm = jnp.where(mask, jnp.concatenate([q_ref[row]] * HKV, axis=-1), 0.0) * (HD ** -0.5)

        def update(s, weighted_values, f_end, row=row):
            m_old = m_scr[row]
            m_new = jnp.maximum(m_old, jnp.max(s, -1, keepdims=True))
            alpha = jnp.exp(m_old - m_new)
            p = jnp.exp(s - m_new)
            l_scr[row] = alpha * l_scr[row] + jnp.sum(p, -1, keepdims=True)
            acc_scr[row] = alpha * acc_scr[row] + weighted_values(p)
            m_scr[row] = m_new
            f_scr[row] = f_end

        cum_all = _dot_hi(jnp.concatenate([lf_refs[pg][0] for pg in range(npg)], axis=0), upper)
        f_run = f_scr[row]
        cums = []
        for pg in range(npg):
            cums.append(cum_all[pg * HQ:(pg + 1) * HQ] + f_run)
            f_run = cums[-1][:, ps - 1:ps]
        kt = jnp.concatenate([k_refs[pg][0].reshape(HKV * HD, ps).astype(bf16) for pg in range(npg)], axis=-1)
        vt = jnp.concatenate([v_refs[pg][0].reshape(HKV * HD, ps).astype(bf16) for pg in range(npg)], axis=-1)
        s = jnp.dot(qm.astype(bf16), kt, preferred_element_type=f32) - jnp.concatenate(cums, axis=-1)
        update(s, lambda p, vt=vt: _dot_nt(p, vt), f_run)

        @pl.when(j == pl.num_programs(1) - 1)
        def _(row=row, qm=qm, update=update):
            kn = kn_ref[row]
            vn = vn_ref[row]
            cum = f_scr[row] + lfn_ref[row]
            s = jnp.sum(qm * kn, -1, keepdims=True) - cum
            update(s, lambda p: p * vn, cum)
            o_ref[row] = _fold_heads(acc_scr[row] / l_scr[row], mask)


def _logf_body(f_ref, b_ref, o_ref):
    o_ref[...] = _log_sigmoid(f_ref[...] + b_ref[...])


def _fox_sample(x, cache_k, cache_v, cache_logf, page_table, w_in, b_f):
    nb = x.shape[0]
    nq = HQ * HD
    nk = HKV * HD
    n_pool = cache_k.shape[0]
    n_pages = page_table.shape[1]
    npg = FOX_PAGES
    q, k, v, f = _proj(x, [w_in[:, :nq].astype(bf16), w_in[:, nq:nq + nk].astype(bf16),
                           w_in[:, nq + nk:nq + 2 * nk].astype(bf16), w_in[:, nq + 2 * nk:].astype(bf16)], nb)
    lf_new = pl.pallas_call(_logf_body, out_shape=jax.ShapeDtypeStruct((nb, HQ), f32))(f, b_f.reshape(1, HQ))

    nr = FOX_ROWS

    def page(row, pg, *shape):
        return pl.BlockSpec((1,) + shape,
                            lambda bi, j, pt: (pt[(bi * nr + row) * n_pages + j * npg + pg],) + (0,) * len(shape))

    tok = lambda *shape: pl.BlockSpec((nr,) + shape, lambda bi, j, pt: (bi,) + (0,) * len(shape))
    ck = jnp.transpose(cache_k, (0, 2, 3, 1))
    cv = jnp.transpose(cache_v, (0, 2, 3, 1))
    clf = jnp.transpose(cache_logf, (0, 2, 1))
    page_specs, page_args = [], []
    for row in range(nr):
        for arr, shape in ((ck, (HKV, HD, PAGE_SIZE)), (cv, (HKV, HD, PAGE_SIZE)), (clf, (HQ, PAGE_SIZE))):
            page_specs += [page(row, pg, *shape) for pg in range(npg)]
            page_args += [arr] * npg
    o = pl.pallas_call(
        _fox_step_body,
        grid_spec=pltpu.PrefetchScalarGridSpec(
            num_scalar_prefetch=1,
            grid=(nb // nr, n_pages // npg),
            in_specs=[tok(HQ, HD), tok(1, nk), tok(1, nk), tok(HQ, 1)] + page_specs,
            out_specs=tok(HQ, HD),
            scratch_shapes=[pltpu.VMEM((nr, HQ, 1), f32), pltpu.VMEM((nr, HQ, 1), f32),
                            pltpu.VMEM((nr, HQ, nk), f32), pltpu.VMEM((nr, HQ, 1), f32)]),
        out_shape=jax.ShapeDtypeStruct((nb, HQ, HD), f32),
        compiler_params=_cp("parallel", "arbitrary"),
        name="fox_decode",
    )(page_table.reshape(-1), q.reshape(nb, HQ, HD), k.reshape(nb, 1, nk), v.reshape(nb, 1, nk),
      lf_new.reshape(nb, HQ, 1), *page_args)
    return o.reshape(nb, nq), k, v, lf_new


ROUTER_TILE = 512
MOE_TM = 1024


def _router_body(x_ref, rwt_ref, rb_ref, comb_ref):
    scores = _sigmoid(_dot_nt(rwt_ref[...], x_ref[...]))
    sel = scores + rb_ref[...]
    rows = [sel[e:e + 1, :] for e in range(N_EXPERTS)]
    srow = [scores[e:e + 1, :] for e in range(N_EXPERTS)]
    gs = []
    for g in range(N_GROUPS):
        r = rows[g * EPG:(g + 1) * EPG]
        best = None
        for a in range(EPG):
            for b in range(a + 1, EPG):
                pair = r[a] + r[b]
                best = pair if best is None else jnp.maximum(best, pair)
        gs.append(best)
    g_best = gs[0]
    g_idx = jnp.zeros_like(gs[0], dtype=jnp.int32)
    for g in range(1, N_GROUPS):
        better = gs[g] > g_best
        g_best = jnp.where(better, gs[g], g_best)
        g_idx = jnp.where(better, g, g_idx)

    def in_group(vals, j):
        out = vals[j]
        for g in range(1, N_GROUPS):
            out = jnp.where(g_idx == g, vals[g * EPG + j], out)
        return out

    ig = [in_group(rows, j) for j in range(EPG)]
    sg = [in_group(srow, j) for j in range(EPG)]
    v1, i1, s1 = ig[0], jnp.zeros_like(g_idx), sg[0]
    for j in range(1, EPG):
        better = ig[j] > v1
        v1 = jnp.where(better, ig[j], v1)
        i1 = jnp.where(better, j, i1)
        s1 = jnp.where(better, sg[j], s1)
    v2 = jnp.full_like(v1, -jnp.inf)
    i2 = jnp.zeros_like(g_idx)
    s2 = jnp.zeros_like(s1)
    for j in range(EPG):
        better = (i1 != j) & (ig[j] > v2)
        v2 = jnp.where(better, ig[j], v2)
        i2 = jnp.where(better, j, i2)
        s2 = jnp.where(better, sg[j], s2)
    tot = s1 + s2
    e_row = lax.broadcasted_iota(jnp.int32, scores.shape, 0)
    comb_ref[...] = (jnp.where(e_row == g_idx * EPG + i1, s1 / tot, 0.0)
                     + jnp.where(e_row == g_idx * EPG + i2, s2 / tot, 0.0))


def _router(x, router_w, router_b):
    t, d = x.shape
    tt = min(ROUTER_TILE, t)
    comb_t = pl.pallas_call(
        _router_body,
        grid=(t // tt,),
        in_specs=[pl.BlockSpec((tt, d), lambda i: (i, 0)), pl.BlockSpec((N_EXPERTS, d), lambda i: (0, 0)),
                  pl.BlockSpec((N_EXPERTS, 1), lambda i: (0, 0))],
        out_specs=pl.BlockSpec((N_EXPERTS, tt), lambda i: (0, i)),
        out_shape=jax.ShapeDtypeStruct((N_EXPERTS, t), f32),
        compiler_params=_cp("parallel"),
        name="router",
    )(x, router_w.T, router_b.reshape(N_EXPERTS, 1))
    return comb_t.T


def _moe_dense_body(x_ref, comb_ref, wg_ref, wu_ref, wd_ref, g_ref, b_ref, o_ref, acc_scr):
    e = pl.program_id(1)

    @pl.when(e == 0)
    def _():
        acc_scr[...] = jnp.zeros_like(acc_scr)

    x = x_ref[...].astype(bf16)
    h = _silu(jnp.dot(x, wg_ref[0].astype(bf16), preferred_element_type=f32)) \
        * jnp.dot(x, wu_ref[0].astype(bf16), preferred_element_type=f32)
    y = jnp.dot(h.astype(bf16), wd_ref[0].astype(bf16), preferred_element_type=f32)
    lane = lax.broadcasted_iota(jnp.int32, comb_ref.shape, 1)
    c = jnp.sum(jnp.where(lane == e, comb_ref[...], 0.0), -1, keepdims=True)
    acc_scr[...] += c * y

    @pl.when(e == pl.num_programs(1) - 1)
    def _():
        o_ref[...] = _layer_norm(DN_ALPHA * x_ref[...] + acc_scr[...], g_ref[...], b_ref[...])


def _moe_ln(x, router_w, router_b, w_gate, w_up, w_down, layer, g, b):
    t, d = x.shape
    comb = _router(x, router_w, router_b)
    tm = min(MOE_TM, t)
    return pl.pallas_call(
        _moe_dense_body,
        grid=(t // tm, N_EXPERTS),
        in_specs=[pl.BlockSpec((tm, d), lambda i, e: (i, 0)), pl.BlockSpec((tm, N_EXPERTS), lambda i, e: (i, 0)),
                  pl.BlockSpec((None, 1, d, D_EXPERT), lambda i, e: (layer, e, 0, 0)),
                  pl.BlockSpec((None, 1, d, D_EXPERT), lambda i, e: (layer, e, 0, 0)),
                  pl.BlockSpec((None, 1, D_EXPERT, d), lambda i, e: (layer, e, 0, 0)),
                  pl.BlockSpec((1, d), lambda i, e: (0, 0)), pl.BlockSpec((1, d), lambda i, e: (0, 0))],
        out_specs=pl.BlockSpec((tm, d), lambda i, e: (i, 0)),
        out_shape=jax.ShapeDtypeStruct((t, d), f32),
        scratch_shapes=[pltpu.VMEM((tm, d), f32)],
        compiler_params=_cp("parallel", "arbitrary"),
        name="moe_dense",
    )(x, comb, w_gate, w_up, w_down, g.reshape(1, d), b.reshape(1, d))


def kernel(x_prompt, x_sample, state_a_ssm, state_a_conv, state_b_h, state_b_conv, cache_c_k, cache_c_v, cache_d_k, cache_d_v, cache_d_logf, page_table, ln_g, ln_b, a_w_in, a_conv_w, a_a_log, a_dt_bias, a_norm_w, a_w_out, b_w_in, b_conv_w, b_conv_b, b_w_a, b_b_a, b_w_x, b_b_x, b_lambda, b_w_out, c_w_in, c_sinks, c_w_out, rel_bias, d_w_in, d_b_f, d_w_out, router_w, router_b, moe_w_gate, moe_w_up, moe_w_down):
    bp, lp, d = x_prompt.shape
    nb = x_sample.shape[0]
    xp = x_prompt.reshape(bp * lp, d)
    xs = x_sample.reshape(nb, d)
    tm_p = PROJ_TM

    def finish(xp, xs, mp, ms, w_out, i, transposed=False):
        w = w_out.astype(bf16)
        xp = _proj_ln(mp, w, xp, ln_g[i, 0], ln_b[i, 0], tm_p, transposed)
        xs = _proj_ln(ms, w, xs, ln_g[i, 0], ln_b[i, 0], nb)
        moe_args = (router_w, router_b, moe_w_gate, moe_w_up, moe_w_down, i, ln_g[i, 1], ln_b[i, 1])
        return _moe_ln(xp, *moe_args), _moe_ln(xs, *moe_args)

    a_args = (a_w_in[0], a_conv_w[0], a_a_log[0], a_dt_bias[0], a_norm_w[0])
    mp, p_a_ssm, p_a_conv = _gdn_prompt(xp.reshape(bp, lp, d), *a_args)
    ms, s_a_ssm, s_a_conv = _gdn_sample(xs, state_a_conv[0], state_a_ssm[0], *a_args)
    xp, xs = finish(xp, xs, mp, ms, a_w_out[0], 0)

    b_args = (b_w_in[0], b_conv_w[0], b_conv_b[0], b_w_a[0], b_b_a[0], b_w_x[0], b_b_x[0], b_lambda[0])
    mp, p_b_h, p_b_conv = _lru_prompt(xp.reshape(bp, lp, d), *b_args)
    ms, s_b_h, s_b_conv = _lru_sample(xs, state_b_conv[0], state_b_h[0], *b_args)
    xp, xs = finish(xp, xs, mp, ms, b_w_out[0], 1)

    mp, kp, vp = _swa_prompt(xp.reshape(bp, lp, d), c_w_in[0], c_sinks[0], rel_bias)
    ms, _, _, s_c_k, s_c_v = _swa_sample(xs, cache_c_k[0], cache_c_v[0], c_w_in[0], c_sinks[0], rel_bias)
    p_c_k = kp[:, lp - WINDOW:].reshape(bp, WINDOW, HKV, HD)
    p_c_v = vp[:, lp - WINDOW:].reshape(bp, WINDOW, HKV, HD)
    xp, xs = finish(xp, xs, mp, ms, c_w_out[0], 2, transposed=True)

    mp, kp, vp, p_d_logf = _fox_prompt(xp.reshape(bp, lp, d), d_w_in[0], d_b_f[0])
    ms, ks, vs, lfs = _fox_sample(xs, cache_d_k[0], cache_d_v[0], cache_d_logf[0], page_table, d_w_in[0], d_b_f[0])
    xp, xs = finish(xp, xs, mp, ms, d_w_out[0], 3, transposed=True)

    return (xp.reshape(bp, lp, d), xs.reshape(nb, 1, d),
            p_a_ssm[None], p_a_conv[None], p_b_h[None], p_b_conv[None], p_c_k[None], p_c_v[None],
            kp.reshape(1, bp, lp, HKV, HD), vp.reshape(1, bp, lp, HKV, HD), p_d_logf[None],
            s_a_ssm[None], s_a_conv[None], s_b_h[None], s_b_conv[None], s_c_k[None], s_c_v[None],
            ks.reshape(1, nb, 1, HKV, HD), vs.reshape(1, nb, 1, HKV, HD), lfs.reshape(1, nb, 1, HQ))
```

```python
import functools
import math

import jax
import jax.numpy as jnp
import numpy as np
from jax import lax
from jax.experimental import pallas as pl
from jax.experimental.pallas import tpu as pltpu

f32 = jnp.float32
bf16 = jnp.bfloat16
HI = lax.Precision.HIGHEST

D_MODEL = 1024
DEPTH = 4
DN_ALPHA = (2 * DEPTH) ** 0.25
LN_EPS = 1e-5
CONV = 4
HA = 8
DKA = 128
DVA = 128
QKV_A = HA * (2 * DKA + DVA)
CHUNK_A = 64
LRU_W = 1024
LRU_BLOCKS = 8
LRU_BW = 128
LRU_C = 8.0
HD = 64
HQ = 16
HKV = 4
GQ = HQ // HKV
WINDOW = 128
N_BUCKETS = 32
MAX_DIST = 128
PAGE_SIZE = 128
N_EXPERTS = 16
N_GROUPS = 4
EPG = 4
D_EXPERT = 512

VMEM_LIMIT = 56 * 1024 * 1024
PROJ_TM = 512


def _cp(*sem):
    return pltpu.CompilerParams(dimension_semantics=sem, vmem_limit_bytes=VMEM_LIMIT)


def _dot(a, b):
    return jnp.dot(a.astype(bf16), b.astype(bf16), preferred_element_type=f32)


def _dot_nt(a, b):
    return lax.dot_general(a.astype(bf16), b.astype(bf16), (((1,), (1,)), ((), ())), preferred_element_type=f32)


def _dot_tn(a, b):
    return lax.dot_general(a.astype(bf16), b.astype(bf16), (((0,), (0,)), ((), ())), preferred_element_type=f32)


def _dot_hi(a, b):
    return jnp.dot(a, b, precision=HI, preferred_element_type=f32)


def _split(a):
    hi = a.astype(bf16)
    return hi, (a - hi.astype(f32)).astype(bf16)


def _dot_split(a, b):
    a_hi, a_lo = _split(a)
    b_hi, b_lo = _split(b)
    d = lambda u, v: jnp.dot(u, v, preferred_element_type=f32)
    return d(a_hi, b_hi) + (d(a_hi, b_lo) + d(a_lo, b_hi))


def _bdot_split(a, b):
    a_hi, a_lo = _split(a)
    b_hi, b_lo = _split(b)
    d = lambda u, v: jnp.einsum('hij,hjk->hik', u, v, preferred_element_type=f32)
    return d(a_hi, b_hi) + (d(a_hi, b_lo) + d(a_lo, b_hi))


def _bdot_nt(a, b):
    return jnp.einsum('hid,hjd->hij', a.astype(bf16), b.astype(bf16), preferred_element_type=f32)


def _dot_nt_hi(a, b):
    return lax.dot_general(a, b, (((1,), (1,)), ((), ())), precision=HI, preferred_element_type=f32)


def _sigmoid(x):
    return 1.0 / (1.0 + jnp.exp(-x))


def _silu(x):
    return x * _sigmoid(x)


def _softplus(x):
    return jnp.maximum(x, 0.0) + jnp.log1p(jnp.exp(-jnp.abs(x)))


def _log_sigmoid(x):
    return -_softplus(-x)


def _eye(n):
    r = lax.broadcasted_iota(jnp.int32, (n, n), 0)
    c = lax.broadcasted_iota(jnp.int32, (n, n), 1)
    return (r == c).astype(f32)


def _proj_body(n_out, x_ref, *refs):
    x = x_ref[...].astype(bf16)
    for w_ref, o_ref in zip(refs[:n_out], refs[n_out:]):
        o_ref[...] = jnp.dot(x, w_ref[...], preferred_element_type=f32)


def _proj(x, ws, tm):
    m, k = x.shape
    n_out = len(ws)
    return pl.pallas_call(
        functools.partial(_proj_body, n_out),
        grid=(m // tm,),
        in_specs=[pl.BlockSpec((tm, k), lambda i: (i, 0))]
        + [pl.BlockSpec(w.shape, lambda i: (0, 0)) for w in ws],
        out_specs=[pl.BlockSpec((tm, w.shape[1]), lambda i: (i, 0)) for w in ws],
        out_shape=[jax.ShapeDtypeStruct((m, w.shape[1]), f32) for w in ws],
        compiler_params=_cp("parallel"),
        name="proj",
    )(x, *ws)


def _attn_proj_body(x_ref, wqt_ref, wk_ref, wv_ref, wvt_ref, qt_ref, k_ref, v_ref, vt_ref):
    x = x_ref[...].astype(bf16)
    nt = lambda w_ref: lax.dot_general(w_ref[...], x, (((1,), (1,)), ((), ())), preferred_element_type=f32)
    qt_ref[0] = (nt(wqt_ref) * (HD ** -0.5)).astype(bf16)
    vt_ref[0] = nt(wvt_ref).astype(bf16)
    k_ref[...] = jnp.dot(x, wk_ref[...], preferred_element_type=f32)
    v_ref[...] = jnp.dot(x, wv_ref[...], preferred_element_type=f32)


def _attn_proj(x, w_in, b, l, tm):
    m, d = x.shape
    nq = HQ * HD
    nk = HKV * HD
    wqt = w_in[:, :nq].T.astype(bf16)
    wk = w_in[:, nq:nq + nk].astype(bf16)
    wv = w_in[:, nq + nk:nq + 2 * nk].astype(bf16)
    nt = l // tm
    full = lambda a: pl.BlockSpec(a.shape, lambda i: (0, 0))
    rows = lambda n: pl.BlockSpec((tm, n), lambda i: (i, 0))
    cols = lambda n: pl.BlockSpec((1, n, tm), lambda i: (i // nt, 0, i % nt))
    return pl.pallas_call(
        _attn_proj_body,
        grid=(m // tm,),
        in_specs=[rows(d), full(wqt), full(wk), full(wv), full(wv.T)],
        out_specs=[cols(nq), rows(nk), rows(nk), cols(nk)],
        out_shape=[jax.ShapeDtypeStruct((b, nq, l), bf16), jax.ShapeDtypeStruct((m, nk), f32),
                   jax.ShapeDtypeStruct((m, nk), f32), jax.ShapeDtypeStruct((b, nk, l), bf16)],
        compiler_params=_cp("parallel"),
        name="attn_proj",
    )(x, wqt, wk, wv, wv.T)


def _layer_norm(z, g, b):
    mu = jnp.mean(z, -1, keepdims=True)
    zc = z - mu
    var = jnp.mean(zc * zc, -1, keepdims=True)
    return zc * lax.rsqrt(var + LN_EPS) * g + b


def _proj_ln_body(h_ref, w_ref, x_ref, g_ref, b_ref, o_ref):
    m = jnp.dot(h_ref[...].astype(bf16), w_ref[...], preferred_element_type=f32)
    o_ref[...] = _layer_norm(DN_ALPHA * x_ref[...] + m, g_ref[...], b_ref[...])


def _proj_ln_t_body(ht_ref, w_ref, x_ref, g_ref, b_ref, o_ref):
    m = lax.dot_general(ht_ref[0].astype(bf16), w_ref[...], (((0,), (0,)), ((), ())), preferred_element_type=f32)
    o_ref[...] = _layer_norm(DN_ALPHA * x_ref[...] + m, g_ref[...], b_ref[...])


def _proj_ln(h, w, x, g, b, tm, transposed=False):
    d = x.shape[1]
    if transposed:
        _, k, l = h.shape
        m = x.shape[0]
        nt = l // tm
        h_spec = pl.BlockSpec((1, k, tm), lambda i: (i // nt, 0, i % nt))
    else:
        m, k = h.shape
        h_spec = pl.BlockSpec((tm, k), lambda i: (i, 0))
    return pl.pallas_call(
        _proj_ln_t_body if transposed else _proj_ln_body,
        grid=(m // tm,),
        in_specs=[h_spec,
                  pl.BlockSpec((k, d), lambda i: (0, 0)),
                  pl.BlockSpec((tm, d), lambda i: (i, 0)),
                  pl.BlockSpec((1, d), lambda i: (0, 0)),
                  pl.BlockSpec((1, d), lambda i: (0, 0))],
        out_specs=pl.BlockSpec((tm, d), lambda i: (i, 0)),
        out_shape=jax.ShapeDtypeStruct((m, d), f32),
        compiler_params=_cp("parallel"),
        name="proj_ln",
    )(h, w, x, g.reshape(1, d), b.reshape(1, d))


GDN_TILE = 256


def _gdn_gates(ab, alog, dtb):
    g = -jnp.exp(alog) * _softplus(ab[:, 0:HA] + dtb)
    beta = _sigmoid(ab[:, HA:2 * HA])
    return g, beta


def _l2norm(t):
    return t * lax.rsqrt(jnp.sum(t * t, -1, keepdims=True) + 1e-6)


def _gdn_prep_body(qkv_ref, halo_ref, ab_ref, cw_ref, alog_ref, dtb_ref,
                   u_ref, w_ref, qd_ref, kd_ref, qk_ref, eg_ref, xp_scr, y_scr):
    i = pl.program_id(1)
    tc = qkv_ref.shape[1]
    c = CHUNK_A
    xp_scr[0:8, :] = jnp.where(i > 0, halo_ref[0], 0.0)
    xp_scr[8:8 + tc, :] = qkv_ref[0]
    cw = cw_ref[...]
    lanes = 128
    for cb in range(QKV_A // lanes):
        cols = slice(cb * lanes, (cb + 1) * lanes)
        y = xp_scr[5:5 + tc, cols] * cw[0:1, cols]
        for j in range(1, CONV):
            y = y + xp_scr[5 + j:5 + j + tc, cols] * cw[j:j + 1, cols]
        y_scr[:, cols] = _silu(y)

    row = lax.broadcasted_iota(jnp.int32, (c, c), 0)
    col = lax.broadcasted_iota(jnp.int32, (c, c), 1)
    incl = row >= col
    strict = row > col
    tri = incl.astype(f32)
    eye_c = (row == col).astype(f32)
    eye_h = _eye(HA)
    alog = alog_ref[...]
    dtb = dtb_ref[...]

    def chunk(n, carry):
        r0 = pl.multiple_of(n * c, c)
        rows = pl.ds(r0, c)
        g, beta = _gdn_gates(ab_ref[0, rows, :], alog, dtb)
        gc = _dot_hi(tri, g)
        gct = _dot_nt_hi(eye_h, gc)
        egc = jnp.exp(gc)
        eg_ref[0, rows, :] = egc
        heads = lambda t, off: jnp.stack([t(y_scr[rows, off + h * DKA:off + (h + 1) * DKA]) for h in range(HA)])
        q = heads(_l2norm, 0) * (DKA ** -0.5)
        k = heads(_l2norm, HA * DKA)
        v = heads(lambda t: t, 2 * HA * DKA)
        col = lambda t: jnp.stack([t[:, h:h + 1] for h in range(HA)])
        gcol, bcol, egcol = col(gc), col(beta), col(egc)
        diff = gcol - gct[:, None, :]
        dec = jnp.where(incl, jnp.exp(jnp.where(incl, diff, 0.0)), 0.0)
        a = jnp.where(strict, _bdot_nt(k, k) * dec, 0.0) * bcol
        x = eye_c - a
        p = _bdot_split(a, a)
        for _ in range(4):
            xp = _bdot_split(jnp.concatenate([x, p], axis=1), p)
            x = x + xp[:, :c]
            p = xp[:, c:]
        x = x + _bdot_split(x, p)
        sol = _bdot_split(x, jnp.concatenate([v * bcol, k * (bcol * egcol)], axis=-1))
        qd = q * egcol
        kd = k * jnp.exp(jnp.stack([gc[c - 1:c, h:h + 1] for h in range(HA)]) - gcol)
        qk = _bdot_nt(q, k) * dec
        for h in range(HA):
            hs = slice(h * DVA, (h + 1) * DVA)
            u_ref[0, rows, hs] = sol[h, :, :DVA]
            w_ref[0, rows, hs] = sol[h, :, DVA:]
            qd_ref[0, rows, hs] = qd[h]
            kd_ref[0, rows, hs] = kd[h]
            qk_ref[0, rows, h * c:(h + 1) * c] = qk[h]
        return carry

    lax.fori_loop(0, tc // c, chunk, 0)


def _gdn_prep(qkv, ab, conv_w, a_log, dt_bias):
    b, l, _ = qkv.shape
    tc = min(GDN_TILE, l)
    hb = tc // 8
    big = lambda n: pl.BlockSpec((1, tc, n), lambda bi, i: (bi, i, 0))
    full = lambda a: pl.BlockSpec(a.shape, lambda bi, i: (0, 0))
    wide = jax.ShapeDtypeStruct((b, l, HA * DVA), f32)
    return pl.pallas_call(
        _gdn_prep_body,
        grid=(b, l // tc),
        in_specs=[big(QKV_A),
                  pl.BlockSpec((1, 8, QKV_A), lambda bi, i: (bi, jnp.maximum(i * hb - 1, 0), 0)),
                  big(2 * HA), full(conv_w), full(a_log), full(dt_bias)],
        out_specs=[big(HA * DVA)] * 4 + [big(HA * CHUNK_A), big(HA)],
        out_shape=[wide] * 4 + [jax.ShapeDtypeStruct((b, l, HA * CHUNK_A), f32),
                                jax.ShapeDtypeStruct((b, l, HA), f32)],
        scratch_shapes=[pltpu.VMEM((tc + 8, QKV_A), f32), pltpu.VMEM((tc, QKV_A), f32)],
        compiler_params=_cp("parallel", "parallel"),
        name="gdn_prep",
    )(qkv, qkv, ab, conv_w, a_log, dt_bias)


def _gdn_scan_body(u_ref, w_ref, qd_ref, kd_ref, qk_ref, eg_ref, gate_ref, nw_ref, y_ref, s_ref):
    i = pl.program_id(1)
    tc = u_ref.shape[1]
    c = CHUNK_A

    @pl.when(i == 0)
    def _():
        s_ref[...] = jnp.zeros_like(s_ref)

    nw = nw_ref[...]

    def chunk(n, carry):
        r0 = pl.multiple_of(n * c, c)
        rows = pl.ds(r0, c)
        eg_last = eg_ref[0, pl.ds(r0 + c - 1, 1), :]
        heads = lambda ref, width: jnp.stack([ref[0, rows, h * width:(h + 1) * width] for h in range(HA)])
        bd = lambda a, b: jnp.einsum('hij,hjk->hik', a.astype(bf16), b.astype(bf16), preferred_element_type=f32)
        s = s_ref[0]
        ws_qs = bd(jnp.concatenate([heads(w_ref, DVA), heads(qd_ref, DVA)], axis=1), s)
        v_new = heads(u_ref, DVA) - ws_qs[:, :c]
        o = ws_qs[:, c:] + bd(heads(qk_ref, c), v_new)
        decay = jnp.stack([eg_last[:, h:h + 1] for h in range(HA)])
        s_ref[0] = s * decay + jnp.einsum('hcd,hce->hde', heads(kd_ref, DVA).astype(bf16), v_new.astype(bf16),
                                          preferred_element_type=f32)
        of = o * lax.rsqrt(jnp.mean(o * o, -1, keepdims=True) + 1e-6)
        for h in range(HA):
            hs = slice(h * DVA, (h + 1) * DVA)
            y_ref[0, rows, hs] = of[h] * nw * _silu(gate_ref[0, rows, hs])
        return carry

    lax.fori_loop(0, tc // c, chunk, 0)


def _gdn_scan(u, w, qd, kd, qk, eg, gate, norm_w):
    b, l, _ = u.shape
    tc = min(GDN_TILE, l)
    big = lambda n: pl.BlockSpec((1, tc, n), lambda bi, i: (bi, i, 0))
    return pl.pallas_call(
        _gdn_scan_body,
        grid=(b, l // tc),
        in_specs=[big(HA * DVA)] * 4 + [big(HA * CHUNK_A), big(HA), big(HA * DVA),
                                       pl.BlockSpec((1, DVA), lambda bi, i: (0, 0))],
        out_specs=[big(HA * DVA), pl.BlockSpec((1, HA, DKA, DVA), lambda bi, i: (bi, 0, 0, 0))],
        out_shape=[jax.ShapeDtypeStruct((b, l, HA * DVA), f32),
                   jax.ShapeDtypeStruct((b, HA, DKA, DVA), f32)],
        compiler_params=_cp("parallel", "arbitrary"),
        name="gdn_scan",
    )(u, w, qd, kd, qk, eg, gate, norm_w.reshape(1, DVA))


def _gdn_prompt(x, w_in, conv_w, a_log, dt_bias, norm_w):
    b, l, d = x.shape
    wq = w_in[:, :QKV_A].astype(bf16)
    wg = w_in[:, QKV_A:QKV_A + HA * DVA].astype(bf16)
    wab = w_in[:, QKV_A + HA * DVA:].astype(bf16)
    qkv, gate, ab = _proj(x.reshape(b * l, d), [wq, wg, wab], min(PROJ_TM, l))
    qkv = qkv.reshape(b, l, QKV_A)
    u, w, qd, kd, qk, eg = _gdn_prep(qkv, ab.reshape(b, l, 2 * HA), conv_w,
                                     a_log.reshape(1, HA), dt_bias.reshape(1, HA))
    y, s = _gdn_scan(u, w, qd, kd, qk, eg, gate.reshape(b, l, HA * DVA), norm_w)
    return y.reshape(b * l, HA * DVA), s, qkv[:, l - (CONV - 1):]


GDN_BT = 8


def _gdn_step_body(qkv_ref, gate_ref, ab_ref, buf_ref, s_ref, cw_ref, alog_ref, dtb_ref, nw_ref,
                   y_ref, s_out_ref, buf_out_ref, q_scr, k_scr, v_scr, eg_scr, beta_scr):
    n = QKV_A
    new = qkv_ref[...]
    cw = cw_ref[...]
    y = new * cw[CONV - 1:CONV, :]
    for j in range(CONV - 1):
        y = y + buf_ref[:, j * n:(j + 1) * n] * cw[j:j + 1, :]
    y = _silu(y)
    buf_out_ref[:, 0:2 * n] = buf_ref[:, n:3 * n]
    buf_out_ref[:, 2 * n:3 * n] = new
    for h in range(HA):
        q_scr[:, h * DKA:(h + 1) * DKA] = _l2norm(y[:, h * DKA:(h + 1) * DKA]) * (DKA ** -0.5)
        k_scr[:, h * DKA:(h + 1) * DKA] = _l2norm(y[:, HA * DKA + h * DKA:HA * DKA + (h + 1) * DKA])
    v_scr[...] = y[:, 2 * HA * DKA:]
    g, beta = _gdn_gates(ab_ref[...], alog_ref[...], dtb_ref[...])
    eg_scr[...] = jnp.exp(g)
    beta_scr[...] = beta
    eye = _eye(DKA)
    row8 = lax.broadcasted_iota(jnp.int32, (8, DKA), 0)
    nw = nw_ref[...]

    for bi in range(qkv_ref.shape[0]):
        r = slice(bi, bi + 1)
        for h in range(HA):
            hs = slice(h * DKA, (h + 1) * DKA)
            k_row = k_scr[r, hs]
            q_row = q_scr[r, hs]
            kq = jnp.where(row8 == 0, k_row, jnp.where(row8 == 1, q_row, 0.0))
            cols = _dot_nt_hi(eye, kq)
            k_col = cols[:, 0:1]
            q_col = cols[:, 1:2]
            sd = s_ref[bi, h] * eg_scr[r, h:h + 1]
            pred = jnp.sum(k_col * sd, axis=0, keepdims=True)
            delta = beta_scr[r, h:h + 1] * (v_scr[r, hs] - pred)
            s_new = sd + k_col * delta
            s_out_ref[bi, h] = s_new
            o = jnp.sum(q_col * s_new, axis=0, keepdims=True)
            of = o * lax.rsqrt(jnp.mean(o * o, -1, keepdims=True) + 1e-6)
            y_ref[r, hs] = of * nw * _silu(gate_ref[r, hs])


def _gdn_step(qkv, gate, ab, conv_buf, s0, conv_w, a_log, dt_bias, norm_w):
    nb = qkv.shape[0]
    bt = GDN_BT
    rowblk = lambda n: pl.BlockSpec((bt, n), lambda i: (i, 0))
    full = lambda a: pl.BlockSpec(a.shape, lambda i: (0, 0))
    sblk = pl.BlockSpec((bt, HA, DKA, DVA), lambda i: (i, 0, 0, 0))
    nw = norm_w.reshape(1, DVA)
    al = a_log.reshape(1, HA)
    db = dt_bias.reshape(1, HA)
    return pl.pallas_call(
        _gdn_step_body,
        grid=(nb // bt,),
        in_specs=[rowblk(QKV_A), rowblk(HA * DVA), rowblk(2 * HA), rowblk(3 * QKV_A), sblk,
                  full(conv_w), full(al), full(db), full(nw)],
        out_specs=[rowblk(HA * DVA), sblk, rowblk(3 * QKV_A)],
        out_shape=[jax.ShapeDtypeStruct((nb, HA * DVA), f32),
                   jax.ShapeDtypeStruct(s0.shape, f32),
                   jax.ShapeDtypeStruct((nb, 3 * QKV_A), f32)],
        scratch_shapes=[pltpu.VMEM((bt, HA * DKA), f32)] * 3 + [pltpu.VMEM((bt, HA), f32)] * 2,
        compiler_params=_cp("parallel"),
        name="gdn_step",
    )(qkv, gate, ab, conv_buf.reshape(nb, 3 * QKV_A), s0, conv_w, al, db, nw)


def _gdn_sample(x, conv_buf, s0, w_in, conv_w, a_log, dt_bias, norm_w):
    nb = x.shape[0]
    wq = w_in[:, :QKV_A].astype(bf16)
    wg = w_in[:, QKV_A:QKV_A + HA * DVA].astype(bf16)
    wab = w_in[:, QKV_A + HA * DVA:].astype(bf16)
    qkv, gate, ab = _proj(x, [wq, wg, wab], nb)
    y, s, buf = _gdn_step(qkv, gate, ab, conv_buf, s0, conv_w, a_log, dt_bias, norm_w)
    return y, s, buf.reshape(nb, CONV - 1, QKV_A)


LRU_TILE = 256


def _gelu(x):
    return 0.5 * x * (1.0 + jnp.tanh(math.sqrt(2.0 / math.pi) * (x + 0.044715 * x * x * x)))


def _lru_gates(u, wa_ref, ba, wx_ref, bx, lam):
    ra, xa = [], []
    for n in range(LRU_BLOCKS):
        ub = u[:, n * LRU_BW:(n + 1) * LRU_BW].astype(bf16)
        ra.append(jnp.dot(ub, wa_ref[n], preferred_element_type=f32))
        xa.append(jnp.dot(ub, wx_ref[n], preferred_element_type=f32))
    r = _sigmoid(jnp.concatenate(ra, axis=-1) + ba)
    i_g = _sigmoid(jnp.concatenate(xa, axis=-1) + bx)
    log_a = -LRU_C * r * _softplus(-lam)
    a = jnp.exp(log_a)
    t = jnp.tanh(log_a)
    b = jnp.sqrt(-2.0 * t / (1.0 - t)) * (i_g * u)
    return a, b


def _lru_prompt_body(gate_ref, rec_ref, halo_ref, cw_ref, cb_ref, wa_ref, ba_ref, wx_ref, bx_ref, lam_ref,
                     y_ref, h_ref, xp_scr):
    i = pl.program_id(1)
    tt = rec_ref.shape[1]

    @pl.when(i == 0)
    def _():
        h_ref[...] = jnp.zeros_like(h_ref)

    xp_scr[0:8, :] = jnp.where(i > 0, halo_ref[0], 0.0)
    xp_scr[8:8 + tt, :] = rec_ref[0]
    cw = cw_ref[...]
    u = xp_scr[5:5 + tt, :] * cw[0:1, :]
    for j in range(1, CONV):
        u = u + xp_scr[5 + j:5 + j + tt, :] * cw[j:j + 1, :]
    u = u + cb_ref[...]
    a, b = _lru_gates(u, wa_ref, ba_ref[...], wx_ref, bx_ref[...], lam_ref[...])
    row = lax.broadcasted_iota(jnp.int32, a.shape, 0)
    s = 1
    while s < tt:
        keep = row >= s
        a_sh = jnp.where(keep, pltpu.roll(a, s, 0), 1.0)
        b_sh = jnp.where(keep, pltpu.roll(b, s, 0), 0.0)
        b = a * b_sh + b
        a = a * a_sh
        s *= 2
    h = b + a * h_ref[0]
    h_ref[0] = h[tt - 1:tt, :]
    y_ref[0] = _gelu(gate_ref[0]) * h


def _lru_prompt(x, w_in, conv_w, conv_b, w_a, b_a, w_x, b_x, lam):
    bsz, l, d = x.shape
    w = LRU_W
    gate_in, rec_in = _proj(x.reshape(bsz * l, d), [w_in[:, :w].astype(bf16), w_in[:, w:].astype(bf16)],
                            min(PROJ_TM, l))
    rec3 = rec_in.reshape(bsz, l, w)
    tt = min(LRU_TILE, l)
    hb = tt // 8
    big = pl.BlockSpec((1, tt, w), lambda bi, i: (bi, i, 0))
    vec = pl.BlockSpec((1, w), lambda bi, i: (0, 0))
    blkw = pl.BlockSpec((LRU_BLOCKS, LRU_BW, LRU_BW), lambda bi, i: (0, 0, 0))
    y, h = pl.pallas_call(
        _lru_prompt_body,
        grid=(bsz, l // tt),
        in_specs=[big, big, pl.BlockSpec((1, 8, w), lambda bi, i: (bi, jnp.maximum(i * hb - 1, 0), 0)),
                  pl.BlockSpec((CONV, w), lambda bi, i: (0, 0)), vec, blkw, vec, blkw, vec, vec],
        out_specs=[big, pl.BlockSpec((1, 1, w), lambda bi, i: (bi, 0, 0))],
        out_shape=[jax.ShapeDtypeStruct((bsz, l, w), f32), jax.ShapeDtypeStruct((bsz, 1, w), f32)],
        scratch_shapes=[pltpu.VMEM((tt + 8, w), f32)],
        compiler_params=_cp("parallel", "arbitrary"),
        name="lru_prompt",
    )(gate_in.reshape(bsz, l, w), rec3, rec3, conv_w, conv_b.reshape(1, w), w_a.astype(bf16), b_a.reshape(1, w),
      w_x.astype(bf16), b_x.reshape(1, w), lam.reshape(1, w))
    return y.reshape(bsz * l, w), h.reshape(bsz, w), rec3[:, l - (CONV - 1):]


def _lru_step_body(gate_ref, rec_ref, buf_ref, h0_ref, cw_ref, cb_ref, wa_ref, ba_ref, wx_ref, bx_ref, lam_ref,
                   y_ref, h_ref, buf_out_ref):
    w = LRU_W
    new = rec_ref[...]
    cw = cw_ref[...]
    u = new * cw[CONV - 1:CONV, :]
    for j in range(CONV - 1):
        u = u + buf_ref[:, j * w:(j + 1) * w] * cw[j:j + 1, :]
    u = u + cb_ref[...]
    buf_out_ref[:, 0:2 * w] = buf_ref[:, w:3 * w]
    buf_out_ref[:, 2 * w:3 * w] = new
    a, b = _lru_gates(u, wa_ref, ba_ref[...], wx_ref, bx_ref[...], lam_ref[...])
    h = b + a * h0_ref[...]
    h_ref[...] = h
    y_ref[...] = _gelu(gate_ref[...]) * h


def _lru_sample(x, conv_buf, h0, w_in, conv_w, conv_b, w_a, b_a, w_x, b_x, lam):
    nb = x.shape[0]
    w = LRU_W
    gate_in, rec_in = _proj(x, [w_in[:, :w].astype(bf16), w_in[:, w:].astype(bf16)], nb)
    y, h, buf = pl.pallas_call(
        _lru_step_body,
        out_shape=[jax.ShapeDtypeStruct((nb, w), f32), jax.ShapeDtypeStruct((nb, w), f32),
                   jax.ShapeDtypeStruct((nb, 3 * w), f32)],
        compiler_params=pltpu.CompilerParams(vmem_limit_bytes=VMEM_LIMIT),
        name="lru_step",
    )(gate_in, rec_in, conv_buf.reshape(nb, 3 * w), h0, conv_w, conv_b.reshape(1, w), w_a.astype(bf16),
      b_a.reshape(1, w), w_x.astype(bf16), b_x.reshape(1, w), lam.reshape(1, w))
    return y, h, buf.reshape(nb, CONV - 1, w)


NEG = -1e30


def _t5_bucket(rel):
    n = jnp.maximum(rel, 0)
    max_exact = N_BUCKETS // 2
    large = max_exact + (jnp.log(jnp.maximum(n, 1).astype(f32) / max_exact)
                         / math.log(MAX_DIST / max_exact) * (N_BUCKETS - max_exact)).astype(jnp.int32)
    return jnp.where(n < max_exact, n, jnp.minimum(large, N_BUCKETS - 1))


def _swa_prompt_body(qt_ref, kp_ref, kc_ref, vtp_ref, vtc_ref, bucket_ref, sinks_ref, rb_ref, o_ref, bias_scr):
    i = pl.program_id(1)
    w = WINDOW

    @pl.when(i == 0)
    def _():
        bucket = bucket_ref[...]
        rel = (lax.broadcasted_iota(jnp.int32, (2 * w, w), 1) + w
               - lax.broadcasted_iota(jnp.int32, (2 * w, w), 0))
        in_window = (rel >= 0) & (rel <= w)
        for h in range(HQ):
            b = jnp.zeros((2 * w, w), f32)
            for n in range(N_BUCKETS):
                b = jnp.where(bucket == n, rb_ref[n, h], b)
            bias_scr[h // GQ, :, (h % GQ) * w:(h % GQ + 1) * w] = jnp.where(in_window, b, NEG)

    hidden_rows = jnp.where(i == 0, w, 0)
    hide = lax.broadcasted_iota(jnp.int32, (2 * w, GQ * w), 0) < hidden_rows
    for kvh in range(HKV):
        heads = range(kvh * GQ, (kvh + 1) * GQ)
        qt = jnp.concatenate([qt_ref[0, h] for h in heads], axis=-1)
        kk = jnp.concatenate([kp_ref[0, kvh], kc_ref[0, kvh]], axis=0)
        s = jnp.where(hide, NEG, jnp.dot(kk, qt, preferred_element_type=f32) + bias_scr[kvh])
        sink = jnp.concatenate([jnp.full((1, w), sinks_ref[h], f32) for h in heads], axis=-1)
        m = jnp.maximum(jnp.max(s, 0, keepdims=True), sink)
        e = jnp.exp(s - m)
        inv = 1.0 / (jnp.sum(e, 0, keepdims=True) + jnp.exp(sink - m))
        vt = jnp.concatenate([vtp_ref[0, kvh], vtc_ref[0, kvh]], axis=-1)
        ot = jnp.dot(vt, (e * inv).astype(bf16), preferred_element_type=f32)
        for g, h in enumerate(heads):
            o_ref[0, h] = ot[:, g * w:(g + 1) * w]


def _swa_prompt(x, w_in, sinks, rel_bias):
    b, l, d = x.shape
    nq = HQ * HD
    nk = HKV * HD
    qt, k, v, vt = _attn_proj(x.reshape(b * l, d), w_in, b, l, min(PROJ_TM, l))
    k = k.reshape(b, l, nk)
    v = v.reshape(b, l, nk)
    w = WINDOW
    rel = jnp.arange(w)[None, :] + w - jnp.arange(2 * w)[:, None]
    bucket = _t5_bucket(rel).astype(jnp.int32)
    qt = qt.reshape(b, HQ, HD, l)
    kh = jnp.swapaxes(k.reshape(b, l, HKV, HD), 1, 2).astype(bf16)
    vt = vt.reshape(b, HKV, HD, l)
    qblk = pl.BlockSpec((1, HQ, HD, w), lambda bi, i: (bi, 0, 0, i))
    smem = pl.BlockSpec(memory_space=pltpu.SMEM)
    ot = pl.pallas_call(
        _swa_prompt_body,
        grid=(b, l // w),
        in_specs=[qblk,
                  pl.BlockSpec((1, HKV, w, HD), lambda bi, i: (bi, 0, jnp.maximum(i - 1, 0), 0)),
                  pl.BlockSpec((1, HKV, w, HD), lambda bi, i: (bi, 0, i, 0)),
                  pl.BlockSpec((1, HKV, HD, w), lambda bi, i: (bi, 0, 0, jnp.maximum(i - 1, 0))),
                  pl.BlockSpec((1, HKV, HD, w), lambda bi, i: (bi, 0, 0, i)),
                  pl.BlockSpec((2 * w, w), lambda bi, i: (0, 0)), smem, smem],
        out_specs=qblk,
        out_shape=jax.ShapeDtypeStruct((b, HQ, HD, l), f32),
        scratch_shapes=[pltpu.VMEM((HKV, 2 * w, GQ * w), f32)],
        compiler_params=_cp("parallel", "arbitrary"),
        name="swa_prompt",
    )(qt, kh, kh, vt, vt, bucket, sinks, rel_bias)
    return ot.reshape(b, nq, l), k, v


SWA_BT = 8


def _head_mask():
    r = lax.broadcasted_iota(jnp.int32, (HQ, HKV * HD), 0)
    c = lax.broadcasted_iota(jnp.int32, (HQ, HKV * HD), 1)
    return (r // GQ) == (c // HD)


def _fold_heads(o, mask):
    o = jnp.where(mask, o, 0.0)
    acc = o[:, 0:HD]
    for c in range(1, HKV):
        acc = acc + o[:, c * HD:(c + 1) * HD]
    return acc


def _swa_step_body(q_ref, kn_ref, vn_ref, kb_ref, vb_ref, onehot_ref, rbt_ref, sinks_ref, o_ref, kb_out, vb_out):
    w = WINDOW
    mask = _head_mask()
    bias_all = _dot_hi(rbt_ref[...], onehot_ref[...])
    bias = bias_all[:, :w]
    bias_new = bias_all[:, w:w + 1]
    sink = sinks_ref[...]
    for bi in range(q_ref.shape[0]):
        qm = jnp.where(mask, jnp.concatenate([q_ref[bi]] * HKV, axis=-1), 0.0)
        kn = kn_ref[bi]
        vn = vn_ref[bi]
        s = _dot_nt(qm, kb_ref[bi]) * (HD ** -0.5) + bias
        s_new = jnp.sum(qm * kn, -1, keepdims=True) * (HD ** -0.5) + bias_new
        m = jnp.maximum(jnp.maximum(jnp.max(s, -1, keepdims=True), s_new), sink)
        e = jnp.exp(s - m)
        e_new = jnp.exp(s_new - m)
        denom = jnp.sum(e, -1, keepdims=True) + e_new + jnp.exp(sink - m)
        o = (_dot(e / denom, vb_ref[bi]) + (e_new / denom) * vn)
        o_ref[bi] = _fold_heads(o, mask)
        kb_out[bi, 0:w - 1, :] = kb_ref[bi, 1:w, :]
        kb_out[bi, w - 1:w, :] = kn
        vb_out[bi, 0:w - 1, :] = vb_ref[bi, 1:w, :]
        vb_out[bi, w - 1:w, :] = vn


def _swa_sample(x, k_buf, v_buf, w_in, sinks, rel_bias):
    nb = x.shape[0]
    nq = HQ * HD
    nk = HKV * HD
    w = WINDOW
    q, k, v = _proj(x, [w_in[:, :nq].astype(bf16), w_in[:, nq:nq + nk].astype(bf16), w_in[:, nq + nk:].astype(bf16)], nb)
    slots = jnp.arange(w + 128)
    bucket = _t5_bucket(jnp.where(slots <= w, w - slots, 0))
    onehot = (bucket[None, :] == jnp.arange(N_BUCKETS)[:, None]).astype(f32)
    bt = SWA_BT
    full = lambda a: pl.BlockSpec(a.shape, lambda i: (0,) * a.ndim)
    bufblk = pl.BlockSpec((bt, w, nk), lambda i: (i, 0, 0))
    rbt = rel_bias.T
    sk = sinks.reshape(HQ, 1)
    o, kb, vb = pl.pallas_call(
        _swa_step_body,
        grid=(nb // bt,),
        in_specs=[pl.BlockSpec((bt, HQ, HD), lambda i: (i, 0, 0)),
                  pl.BlockSpec((bt, 1, nk), lambda i: (i, 0, 0)), pl.BlockSpec((bt, 1, nk), lambda i: (i, 0, 0)),
                  bufblk, bufblk, full(onehot), full(rbt), full(sk)],
        out_specs=[pl.BlockSpec((bt, HQ, HD), lambda i: (i, 0, 0)), bufblk, bufblk],
        out_shape=[jax.ShapeDtypeStruct((nb, HQ, HD), f32), jax.ShapeDtypeStruct((nb, w, nk), f32),
                   jax.ShapeDtypeStruct((nb, w, nk), f32)],
        compiler_params=_cp("parallel"),
        name="swa_step",
    )(q.reshape(nb, HQ, HD), k.reshape(nb, 1, nk), v.reshape(nb, 1, nk), k_buf.reshape(nb, w, nk),
      v_buf.reshape(nb, w, nk), onehot, rbt, sk)
    return o.reshape(nb, nq), k, v, kb.reshape(nb, w, HKV, HD), vb.reshape(nb, w, HKV, HD)


FOX_PREP_TILE = 512
FOX_TQ = 512
FOX_TK = 1024


def _fox_prep_body(x_ref, wft_ref, bf_ref, lf_ref, cum_ref, carry):
    i = pl.program_id(1)
    tt = x_ref.shape[1]

    @pl.when(i == 0)
    def _():
        carry[...] = jnp.zeros_like(carry)

    f = lax.dot_general(wft_ref[...], x_ref[0].astype(bf16), (((1,), (1,)), ((), ())), preferred_element_type=f32)
    lf = _log_sigmoid(f + bf_ref[...])
    r = lax.broadcasted_iota(jnp.int32, (tt, tt), 0)
    c = lax.broadcasted_iota(jnp.int32, (tt, tt), 1)
    cum = _dot_hi(lf, (r <= c).astype(f32)) + carry[...]
    lf_ref[0] = lf
    cum_ref[0] = cum
    carry[...] = cum[:, tt - 1:tt]


FOX_SUM_ROWS = 16


def _fox_flash_body(qt_ref, k_ref, vt_ref, fk_ref, o_ref, m_scr, acc_scr):
    i = pl.program_id(2)
    tq = qt_ref.shape[3]
    tk = FOX_TK
    m_scr[...] = jnp.full_like(m_scr, NEG)
    acc_scr[...] = jnp.zeros_like(acc_scr)
    ones = jnp.ones((FOX_SUM_ROWS, tk), bf16)
    qt_all = jnp.concatenate([qt_ref[0, g] for g in range(GQ)], axis=-1)

    def block(j, masked):
        c0 = pl.multiple_of(j * tk, tk)
        k = k_ref[0, 0, pl.ds(c0, tk), :]
        vt = jnp.concatenate([vt_ref[0, 0, :, pl.ds(c0, tk)], ones], axis=0)
        fk = fk_ref[0, 0, pl.ds(c0, tk), :]
        if masked:
            s_pos = c0 + lax.broadcasted_iota(jnp.int32, (tk, tq), 0)
            t_pos = i * tq + lax.broadcasted_iota(jnp.int32, (tk, tq), 1)
            visible = s_pos <= t_pos
        s_all = jnp.dot(k, qt_all, preferred_element_type=f32)
        for g in range(GQ):
            s = s_all[:, g * tq:(g + 1) * tq] - fk[:, g:g + 1]
            if masked:
                s = jnp.where(visible, s, NEG)
            m_old = m_scr[g]
            m_new = jnp.maximum(m_old, jnp.max(s, 0, keepdims=True))
            alpha = jnp.exp(m_old - m_new)
            p = jnp.exp(s - m_new)
            acc_scr[g] = alpha * acc_scr[g] + jnp.dot(vt, p.astype(bf16), preferred_element_type=f32)
            m_scr[g] = m_new

    n_full = (i * tq) // tk

    def full_block(j, carry):
        block(j, False)
        return carry

    lax.fori_loop(0, n_full, full_block, 0)
    for jj in range(pl.cdiv(tq, tk)):
        block(n_full + jj, True)
    for g in range(GQ):
        o_ref[0, g] = acc_scr[g, :HD] / acc_scr[g, HD:HD + 1]


def _fox_prompt(x, w_in, b_f):
    b, l, d = x.shape
    nq = HQ * HD
    nk = HKV * HD
    qt, k, v, vt = _attn_proj(x.reshape(b * l, d), w_in, b, l, min(PROJ_TM, l))
    tt = min(FOX_PREP_TILE, l)
    wft = w_in[:, nq + 2 * nk:].T.astype(bf16)
    row = pl.BlockSpec((1, HQ, tt), lambda bi, i: (bi, 0, i))
    lft, cumt = pl.pallas_call(
        _fox_prep_body,
        grid=(b, l // tt),
        in_specs=[pl.BlockSpec((1, tt, d), lambda bi, i: (bi, i, 0)),
                  pl.BlockSpec((HQ, d), lambda bi, i: (0, 0)), pl.BlockSpec((HQ, 1), lambda bi, i: (0, 0))],
        out_specs=[row, row],
        out_shape=[jax.ShapeDtypeStruct((b, HQ, l), f32)] * 2,
        scratch_shapes=[pltpu.VMEM((HQ, 1), f32)],
        compiler_params=_cp("parallel", "arbitrary"),
        name="fox_prep",
    )(x, wft, b_f.reshape(HQ, 1))
    fk = jnp.swapaxes(cumt.reshape(b, HKV, GQ, l), 2, 3)
    qt = qt.reshape(b, HQ, HD, l)
    kh = jnp.swapaxes(k.reshape(b, l, HKV, HD), 1, 2).astype(bf16)
    vt = vt.reshape(b, HKV, HD, l)
    tq = min(FOX_TQ, l)
    qblk = pl.BlockSpec((1, GQ, HD, tq), lambda bi, h, i: (bi, h, 0, i))
    ot = pl.pallas_call(
        _fox_flash_body,
        grid=(b, HKV, l // tq),
        in_specs=[qblk,
                  pl.BlockSpec((1, 1, l, HD), lambda bi, h, i: (bi, h, 0, 0)),
                  pl.BlockSpec((1, 1, HD, l), lambda bi, h, i: (bi, h, 0, 0)),
                  pl.BlockSpec((1, 1, l, GQ), lambda bi, h, i: (bi, h, 0, 0))],
        out_specs=qblk,
        out_shape=jax.ShapeDtypeStruct((b, HQ, HD, l), f32),
        scratch_shapes=[pltpu.VMEM((GQ, 1, tq), f32), pltpu.VMEM((GQ, HD + FOX_SUM_ROWS, tq), f32)],
        compiler_params=_cp("parallel", "parallel", "arbitrary"),
        name="fox_flash",
    )(qt, kh, vt, fk)
    return ot.reshape(b, nq, l), k.reshape(b, l, nk), v.reshape(b, l, nk), jnp.swapaxes(lft, 1, 2)


FOX_PAGES = 16
FOX_ROWS = 1


def _fox_step_body(pt_ref, q_ref, kn_ref, vn_ref, lfn_ref, *refs):
    npg = FOX_PAGES
    n_in = 3 * npg * FOX_ROWS
    o_ref, m_scr, l_scr, acc_scr, f_scr = refs[n_in:]
    j = pl.program_id(1)
    ps = PAGE_SIZE
    mask = _head_mask()

    @pl.when(j == 0)
    def _():
        m_scr[...] = jnp.full_like(m_scr, NEG)
        l_scr[...] = jnp.zeros_like(l_scr)
        acc_scr[...] = jnp.zeros_like(acc_scr)
        f_scr[...] = jnp.zeros_like(f_scr)

    r = lax.broadcasted_iota(jnp.int32, (ps, ps), 0)
    c = lax.broadcasted_iota(jnp.int32, (ps, ps), 1)
    upper = (r <= c).astype(f32)

    for row in range(FOX_ROWS):
        base = 3 * npg * row
        k_refs = refs[base:base + npg]
        v_refs = refs[base + npg:base + 2 * npg]
        lf_refs = refs[base + 2 * npg:base + 3 * npg]
        qm = jnp.where(mask, jnp.concatenate([q_ref[row]] * HKV, axis=-1), 0.0) * (HD ** -0.5)

        def update(s, weighted_values, f_end, row=row):
            m_old = m_scr[row]
            m_new = jnp.maximum(m_old, jnp.max(s, -1, keepdims=True))
            alpha = jnp.exp(m_old - m_new)
            p = jnp.exp(s - m_new)
            l_scr[row] = alpha * l_scr[row] + jnp.sum(p, -1, keepdims=True)
            acc_scr[row] = alpha * acc_scr[row] + weighted_values(p)
            m_scr[row] = m_new
            f_scr[row] = f_end

        cum_all = _dot_hi(jnp.concatenate([lf_refs[pg][0] for pg in range(npg)], axis=0), upper)
        f_run = f_scr[row]
        cums = []
        for pg in range(npg):
            cums.append(cum_all[pg * HQ:(pg + 1) * HQ] + f_run)
            f_run = cums[-1][:, ps - 1:ps]
        kt = jnp.concatenate([k_refs[pg][0].reshape(HKV * HD, ps).astype(bf16) for pg in range(npg)], axis=-1)
        vt = jnp.concatenate([v_refs[pg][0].reshape(HKV * HD, ps).astype(bf16) for pg in range(npg)], axis=-1)
        s = jnp.dot(qm.astype(bf16), kt, preferred_element_type=f32) - jnp.concatenate(cums, axis=-1)
        update(s, lambda p, vt=vt: _dot_nt(p, vt), f_run)

        @pl.when(j == pl.num_programs(1) - 1)
        def _(row=row, qm=qm, update=update):
            kn = kn_ref[row]
            vn = vn_ref[row]
            cum = f_scr[row] + lfn_ref[row]
            s = jnp.sum(qm * kn, -1, keepdims=True) - cum
            update(s, lambda p: p * vn, cum)
            o_ref[row] = _fold_heads(acc_scr[row] / l_scr[row], mask)


def _logf_body(f_ref, b_ref, o_ref):
    o_ref[...] = _log_sigmoid(f_ref[...] + b_ref[...])


def _fox_sample(x, cache_k, cache_v, cache_logf, page_table, w_in, b_f):
    nb = x.shape[0]
    nq = HQ * HD
    nk = HKV * HD
    n_pool = cache_k.shape[0]
    n_pages = page_table.shape[1]
    npg = FOX_PAGES
    q, k, v, f = _proj(x, [w_in[:, :nq].astype(bf16), w_in[:, nq:nq + nk].astype(bf16),
                           w_in[:, nq + nk:nq + 2 * nk].astype(bf16), w_in[:, nq + 2 * nk:].astype(bf16)], nb)
    lf_new = pl.pallas_call(_logf_body, out_shape=jax.ShapeDtypeStruct((nb, HQ), f32))(f, b_f.reshape(1, HQ))

    nr = FOX_ROWS

    def page(row, pg, *shape):
        return pl.BlockSpec((1,) + shape,
                            lambda bi, j, pt: (pt[(bi * nr + row) * n_pages + j * npg + pg],) + (0,) * len(shape))

    tok = lambda *shape: pl.BlockSpec((nr,) + shape, lambda bi, j, pt: (bi,) + (0,) * len(shape))
    ck = jnp.transpose(cache_k, (0, 2, 3, 1))
    cv = jnp.transpose(cache_v, (0, 2, 3, 1))
    clf = jnp.transpose(cache_logf, (0, 2, 1))
    page_specs, page_args = [], []
    for row in range(nr):
        for arr, shape in ((ck, (HKV, HD, PAGE_SIZE)), (cv, (HKV, HD, PAGE_SIZE)), (clf, (HQ, PAGE_SIZE))):
            page_specs += [page(row, pg, *shape) for pg in range(npg)]
            page_args += [arr] * npg
    o = pl.pallas_call(
        _fox_step_body,
        grid_spec=pltpu.PrefetchScalarGridSpec(
            num_scalar_prefetch=1,
            grid=(nb // nr, n_pages // npg),
            in_specs=[tok(HQ, HD), tok(1, nk), tok(1, nk), tok(HQ, 1)] + page_specs,
            out_specs=tok(HQ, HD),
            scratch_shapes=[pltpu.VMEM((nr, HQ, 1), f32), pltpu.VMEM((nr, HQ, 1), f32),
                            pltpu.VMEM((nr, HQ, nk), f32), pltpu.VMEM((nr, HQ, 1), f32)]),
        out_shape=jax.ShapeDtypeStruct((nb, HQ, HD), f32),
        compiler_params=_cp("parallel", "arbitrary"),
        name="fox_decode",
    )(page_table.reshape(-1), q.reshape(nb, HQ, HD), k.reshape(nb, 1, nk), v.reshape(nb, 1, nk),
      lf_new.reshape(nb, HQ, 1), *page_args)
    return o.reshape(nb, nq), k, v, lf_new


ROUTER_TILE = 512
MOE_TM = 1024


def _router_body(x_ref, rwt_ref, rb_ref, comb_ref):
    scores = _sigmoid(_dot_nt(rwt_ref[...], x_ref[...]))
    sel = scores + rb_ref[...]
    rows = [sel[e:e + 1, :] for e in range(N_EXPERTS)]
    srow = [scores[e:e + 1, :] for e in range(N_EXPERTS)]
    gs = []
    for g in range(N_GROUPS):
        r = rows[g * EPG:(g + 1) * EPG]
        best = None
        for a in range(EPG):
            for b in range(a + 1, EPG):
                pair = r[a] + r[b]
                best = pair if best is None else jnp.maximum(best, pair)
        gs.append(best)
    g_best = gs[0]
    g_idx = jnp.zeros_like(gs[0], dtype=jnp.int32)
    for g in range(1, N_GROUPS):
        better = gs[g] > g_best
        g_best = jnp.where(better, gs[g], g_best)
        g_idx = jnp.where(better, g, g_idx)

    def in_group(vals, j):
        out = vals[j]
        for g in range(1, N_GROUPS):
            out = jnp.where(g_idx == g, vals[g * EPG + j], out)
        return out

    ig = [in_group(rows, j) for j in range(EPG)]
    sg = [in_group(srow, j) for j in range(EPG)]
    v1, i1, s1 = ig[0], jnp.zeros_like(g_idx), sg[0]
    for j in range(1, EPG):
        better = ig[j] > v1
        v1 = jnp.where(better, ig[j], v1)
        i1 = jnp.where(better, j, i1)
        s1 = jnp.where(better, sg[j], s1)
    v2 = jnp.full_like(v1, -jnp.inf)
    i2 = jnp.zeros_like(g_idx)
    s2 = jnp.zeros_like(s1)
    for j in range(EPG):
        better = (i1 != j) & (ig[j] > v2)
        v2 = jnp.where(better, ig[j], v2)
        i2 = jnp.where(better, j, i2)
        s2 = jnp.where(better, sg[j], s2)
    tot = s1 + s2
    e_row = lax.broadcasted_iota(jnp.int32, scores.shape, 0)
    comb_ref[...] = (jnp.where(e_row == g_idx * EPG + i1, s1 / tot, 0.0)
                     + jnp.where(e_row == g_idx * EPG + i2, s2 / tot, 0.0))


def _router(x, router_w, router_b):
    t, d = x.shape
    tt = min(ROUTER_TILE, t)
    comb_t = pl.pallas_call(
        _router_body,
        grid=(t // tt,),
        in_specs=[pl.BlockSpec((tt, d), lambda i: (i, 0)), pl.BlockSpec((N_EXPERTS, d), lambda i: (0, 0)),
                  pl.BlockSpec((N_EXPERTS, 1), lambda i: (0, 0))],
        out_specs=pl.BlockSpec((N_EXPERTS, tt), lambda i: (0, i)),
        out_shape=jax.ShapeDtypeStruct((N_EXPERTS, t), f32),
        compiler_params=_cp("parallel"),
        name="router",
    )(x, router_w.T, router_b.reshape(N_EXPERTS, 1))
    return comb_t.T


def _moe_dense_body(x_ref, comb_ref, wg_ref, wu_ref, wd_ref, g_ref, b_ref, o_ref, acc_scr):
    e = pl.program_id(1)

    @pl.when(e == 0)
    def _():
        acc_scr[...] = jnp.zeros_like(acc_scr)

    x = x_ref[...].astype(bf16)
    h = _silu(jnp.dot(x, wg_ref[0].astype(bf16), preferred_element_type=f32)) \
        * jnp.dot(x, wu_ref[0].astype(bf16), preferred_element_type=f32)
    y = jnp.dot(h.astype(bf16), wd_ref[0].astype(bf16), preferred_element_type=f32)
    lane = lax.broadcasted_iota(jnp.int32, comb_ref.shape, 1)
    c = jnp.sum(jnp.where(lane == e, comb_ref[...], 0.0), -1, keepdims=True)
    acc_scr[...] += c * y

    @pl.when(e == pl.num_programs(1) - 1)
    def _():
        o_ref[...] = _layer_norm(DN_ALPHA * x_ref[...] + acc_scr[...], g_ref[...], b_ref[...])


def _moe_ln(x, router_w, router_b, w_gate, w_up, w_down, layer, g, b):
    t, d = x.shape
    comb = _router(x, router_w, router_b)
    tm = min(MOE_TM, t)
    return pl.pallas_call(
        _moe_dense_body,
        grid=(t // tm, N_EXPERTS),
        in_specs=[pl.BlockSpec((tm, d), lambda i, e: (i, 0)), pl.BlockSpec((tm, N_EXPERTS), lambda i, e: (i, 0)),
                  pl.BlockSpec((None, 1, d, D_EXPERT), lambda i, e: (layer, e, 0, 0)),
                  pl.BlockSpec((None, 1, d, D_EXPERT), lambda i, e: (layer, e, 0, 0)),
                  pl.BlockSpec((None, 1, D_EXPERT, d), lambda i, e: (layer, e, 0, 0)),
                  pl.BlockSpec((1, d), lambda i, e: (0, 0)), pl.BlockSpec((1, d), lambda i, e: (0, 0))],
        out_specs=pl.BlockSpec((tm, d), lambda i, e: (i, 0)),
        out_shape=jax.ShapeDtypeStruct((t, d), f32),
        scratch_shapes=[pltpu.VMEM((tm, d), f32)],
        compiler_params=_cp("parallel", "arbitrary"),
        name="moe_dense",
    )(x, comb, w_gate, w_up, w_down, g.reshape(1, d), b.reshape(1, d))


def kernel(x_prompt, x_sample, state_a_ssm, state_a_conv, state_b_h, state_b_conv, cache_c_k, cache_c_v, cache_d_k, cache_d_v, cache_d_logf, page_table, ln_g, ln_b, a_w_in, a_conv_w, a_a_log, a_dt_bias, a_norm_w, a_w_out, b_w_in, b_conv_w, b_conv_b, b_w_a, b_b_a, b_w_x, b_b_x, b_lambda, b_w_out, c_w_in, c_sinks, c_w_out, rel_bias, d_w_in, d_b_f, d_w_out, router_w, router_b, moe_w_gate, moe_w_up, moe_w_down):
    bp, lp, d = x_prompt.shape
    nb = x_sample.shape[0]
    xp = x_prompt.reshape(bp * lp, d)
    xs = x_sample.reshape(nb, d)
    tm_p = PROJ_TM

    def finish(xp, xs, mp, ms, w_out, i, transposed=False):
        w = w_out.astype(bf16)
        xp = _proj_ln(mp, w, xp, ln_g[i, 0], ln_b[i, 0], tm_p, transposed)
        xs = _proj_ln(ms, w, xs, ln_g[i, 0], ln_b[i, 0], nb)
        moe_args = (router_w, router_b, moe_w_gate, moe_w_up, moe_w_down, i, ln_g[i, 1], ln_b[i, 1])
        return _moe_ln(xp, *moe_args), _moe_ln(xs, *moe_args)

    a_args = (a_w_in[0], a_conv_w[0], a_a_log[0], a_dt_bias[0], a_norm_w[0])
    mp, p_a_ssm, p_a_conv = _gdn_prompt(xp.reshape(bp, lp, d), *a_args)
    ms, s_a_ssm, s_a_conv = _gdn_sample(xs, state_a_conv[0], state_a_ssm[0], *a_args)
    xp, xs = finish(xp, xs, mp, ms, a_w_out[0], 0)

    b_args = (b_w_in[0], b_conv_w[0], b_conv_b[0], b_w_a[0], b_b_a[0], b_w_x[0], b_b_x[0], b_lambda[0])
    mp, p_b_h, p_b_conv = _lru_prompt(xp.reshape(bp, lp, d), *b_args)
    ms, s_b_h, s_b_conv = _lru_sample(xs, state_b_conv[0], state_b_h[0], *b_args)
    xp, xs = finish(xp, xs, mp, ms, b_w_out[0], 1)

    mp, kp, vp = _swa_prompt(xp.reshape(bp, lp, d), c_w_in[0], c_sinks[0], rel_bias)
    ms, _, _, s_c_k, s_c_v = _swa_sample(xs, cache_c_k[0], cache_c_v[0], c_w_in[0], c_sinks[0], rel_bias)
    p_c_k = kp[:, lp - WINDOW:].reshape(bp, WINDOW, HKV, HD)
    p_c_v = vp[:, lp - WINDOW:].reshape(bp, WINDOW, HKV, HD)
    xp, xs = finish(xp, xs, mp, ms, c_w_out[0], 2, transposed=True)

    mp, kp, vp, p_d_logf = _fox_prompt(xp.reshape(bp, lp, d), d_w_in[0], d_b_f[0])
    ms, ks, vs, lfs = _fox_sample(xs, cache_d_k[0], cache_d_v[0], cache_d_logf[0], page_table, d_w_in[0], d_b_f[0])
    xp, xs = finish(xp, xs, mp, ms, d_w_out[0], 3, transposed=True)

    return (xp.reshape(bp, lp, d), xs.reshape(nb, 1, d),
            p_a_ssm[None], p_a_conv[None], p_b_h[None], p_b_conv[None], p_c_k[None], p_c_v[None],
            kp.reshape(1, bp, lp, HKV, HD), vp.reshape(1, bp, lp, HKV, HD), p_d_logf[None],
            s_a_ssm[None], s_a_conv[None], s_b_h[None], s_b_conv[None], s_c_k[None], s_c_v[None],
            ks.reshape(1, nb, 1, HKV, HD), vs.reshape(1, nb, 1, HKV, HD), lfs.reshape(1, nb, 1, HQ))
```

```python
import functools
import math

import jax
import jax.numpy as jnp
import numpy as np
from jax import lax
from jax.experimental import pallas as pl
from jax.experimental.pallas import tpu as pltpu

f32 = jnp.float32
bf16 = jnp.bfloat16
HI = lax.Precision.HIGHEST

D_MODEL = 1024
DEPTH = 4
DN_ALPHA = (2 * DEPTH) ** 0.25
LN_EPS = 1e-5
CONV = 4
HA = 8
DKA = 128
DVA = 128
QKV_A = HA * (2 * DKA + DVA)
CHUNK_A = 64
LRU_W = 1024
LRU_BLOCKS = 8
LRU_BW = 128
LRU_C = 8.0
HD = 64
HQ = 16
HKV = 4
GQ = HQ // HKV
WINDOW = 128
N_BUCKETS = 32
MAX_DIST = 128
PAGE_SIZE = 128
N_EXPERTS = 16
N_GROUPS = 4
EPG = 4
D_EXPERT = 512

VMEM_LIMIT = 56 * 1024 * 1024
PROJ_TM = 512


def _cp(*sem):
    return pltpu.CompilerParams(dimension_semantics=sem, vmem_limit_bytes=VMEM_LIMIT)


def _dot(a, b):
    return jnp.dot(a.astype(bf16), b.astype(bf16), preferred_element_type=f32)


def _dot_nt(a, b):
    return lax.dot_general(a.astype(bf16), b.astype(bf16), (((1,), (1,)), ((), ())), preferred_element_type=f32)


def _dot_tn(a, b):
    return lax.dot_general(a.astype(bf16), b.astype(bf16), (((0,), (0,)), ((), ())), preferred_element_type=f32)


def _dot_hi(a, b):
    return jnp.dot(a, b, precision=HI, preferred_element_type=f32)


def _split(a):
    hi = a.astype(bf16)
    return hi, (a - hi.astype(f32)).astype(bf16)


def _dot_split(a, b):
    a_hi, a_lo = _split(a)
    b_hi, b_lo = _split(b)
    d = lambda u, v: jnp.dot(u, v, preferred_element_type=f32)
    return d(a_hi, b_hi) + (d(a_hi, b_lo) + d(a_lo, b_hi))


def _bdot_split(a, b):
    a_hi, a_lo = _split(a)
    b_hi, b_lo = _split(b)
    d = lambda u, v: jnp.einsum('hij,hjk->hik', u, v, preferred_element_type=f32)
    return d(a_hi, b_hi) + (d(a_hi, b_lo) + d(a_lo, b_hi))


def _bdot(a, b):
    return jnp.einsum('hij,hjk->hik', a.astype(bf16), b.astype(bf16), preferred_element_type=f32)


def _bdot_nt(a, b):
    return jnp.einsum('hid,hjd->hij', a.astype(bf16), b.astype(bf16), preferred_element_type=f32)


def _dot_nt_hi(a, b):
    return lax.dot_general(a, b, (((1,), (1,)), ((), ())), precision=HI, preferred_element_type=f32)


def _sigmoid(x):
    return 1.0 / (1.0 + jnp.exp(-x))


def _silu(x):
    return x * _sigmoid(x)


def _softplus(x):
    return jnp.maximum(x, 0.0) + jnp.log1p(jnp.exp(-jnp.abs(x)))


def _log_sigmoid(x):
    return -_softplus(-x)


def _eye(n):
    r = lax.broadcasted_iota(jnp.int32, (n, n), 0)
    c = lax.broadcasted_iota(jnp.int32, (n, n), 1)
    return (r == c).astype(f32)


def _proj_body(n_out, x_ref, *refs):
    x = x_ref[...].astype(bf16)
    for w_ref, o_ref in zip(refs[:n_out], refs[n_out:]):
        o_ref[...] = jnp.dot(x, w_ref[...], preferred_element_type=f32)


def _proj(x, ws, tm):
    m, k = x.shape
    n_out = len(ws)
    return pl.pallas_call(
        functools.partial(_proj_body, n_out),
        grid=(m // tm,),
        in_specs=[pl.BlockSpec((tm, k), lambda i: (i, 0))]
        + [pl.BlockSpec(w.shape, lambda i: (0, 0)) for w in ws],
        out_specs=[pl.BlockSpec((tm, w.shape[1]), lambda i: (i, 0)) for w in ws],
        out_shape=[jax.ShapeDtypeStruct((m, w.shape[1]), f32) for w in ws],
        compiler_params=_cp("parallel"),
        name="proj",
    )(x, *ws)


def _attn_proj_body(x_ref, wqt_ref, wk_ref, wv_ref, wvt_ref, qt_ref, k_ref, v_ref, vt_ref):
    x = x_ref[...].astype(bf16)
    nt = lambda w_ref: lax.dot_general(w_ref[...], x, (((1,), (1,)), ((), ())), preferred_element_type=f32)
    qt_ref[0] = (nt(wqt_ref) * (HD ** -0.5)).astype(bf16)
    vt_ref[0] = nt(wvt_ref).astype(bf16)
    k_ref[...] = jnp.dot(x, wk_ref[...], preferred_element_type=f32)
    v_ref[...] = jnp.dot(x, wv_ref[...], preferred_element_type=f32)


def _attn_proj(x, w_in, b, l, tm):
    m, d = x.shape
    nq = HQ * HD
    nk = HKV * HD
    wqt = w_in[:, :nq].T.astype(bf16)
    wk = w_in[:, nq:nq + nk].astype(bf16)
    wv = w_in[:, nq + nk:nq + 2 * nk].astype(bf16)
    nt = l // tm
    full = lambda a: pl.BlockSpec(a.shape, lambda i: (0, 0))
    rows = lambda n: pl.BlockSpec((tm, n), lambda i: (i, 0))
    cols = lambda n: pl.BlockSpec((1, n, tm), lambda i: (i // nt, 0, i % nt))
    return pl.pallas_call(
        _attn_proj_body,
        grid=(m // tm,),
        in_specs=[rows(d), full(wqt), full(wk), full(wv), full(wv.T)],
        out_specs=[cols(nq), rows(nk), rows(nk), cols(nk)],
        out_shape=[jax.ShapeDtypeStruct((b, nq, l), bf16), jax.ShapeDtypeStruct((m, nk), f32),
                   jax.ShapeDtypeStruct((m, nk), f32), jax.ShapeDtypeStruct((b, nk, l), bf16)],
        compiler_params=_cp("parallel"),
        name="attn_proj",
    )(x, wqt, wk, wv, wv.T)


def _layer_norm(z, g, b):
    mu = jnp.mean(z, -1, keepdims=True)
    zc = z - mu
    var = jnp.mean(zc * zc, -1, keepdims=True)
    return zc * lax.rsqrt(var + LN_EPS) * g + b


def _proj_ln_body(h_ref, w_ref, x_ref, g_ref, b_ref, o_ref):
    m = jnp.dot(h_ref[...].astype(bf16), w_ref[...], preferred_element_type=f32)
    o_ref[...] = _layer_norm(DN_ALPHA * x_ref[...] + m, g_ref[...], b_ref[...])


def _proj_ln_t_body(ht_ref, w_ref, x_ref, g_ref, b_ref, o_ref):
    m = lax.dot_general(ht_ref[0].astype(bf16), w_ref[...], (((0,), (0,)), ((), ())), preferred_element_type=f32)
    o_ref[...] = _layer_norm(DN_ALPHA * x_ref[...] + m, g_ref[...], b_ref[...])


def _proj_ln(h, w, x, g, b, tm, transposed=False):
    d = x.shape[1]
    if transposed:
        _, k, l = h.shape
        m = x.shape[0]
        nt = l // tm
        h_spec = pl.BlockSpec((1, k, tm), lambda i: (i // nt, 0, i % nt))
    else:
        m, k = h.shape
        h_spec = pl.BlockSpec((tm, k), lambda i: (i, 0))
    return pl.pallas_call(
        _proj_ln_t_body if transposed else _proj_ln_body,
        grid=(m // tm,),
        in_specs=[h_spec,
                  pl.BlockSpec((k, d), lambda i: (0, 0)),
                  pl.BlockSpec((tm, d), lambda i: (i, 0)),
                  pl.BlockSpec((1, d), lambda i: (0, 0)),
                  pl.BlockSpec((1, d), lambda i: (0, 0))],
        out_specs=pl.BlockSpec((tm, d), lambda i: (i, 0)),
        out_shape=jax.ShapeDtypeStruct((m, d), f32),
        compiler_params=_cp("parallel"),
        name="proj_ln",
    )(h, w, x, g.reshape(1, d), b.reshape(1, d))


GDN_TILE = 256


def _gdn_gates(ab, alog, dtb):
    g = -jnp.exp(alog) * _softplus(ab[:, 0:HA] + dtb)
    beta = _sigmoid(ab[:, HA:2 * HA])
    return g, beta


def _l2norm(t):
    return t * lax.rsqrt(jnp.sum(t * t, -1, keepdims=True) + 1e-6)


def _gdn_prep_body(qkv_ref, halo_ref, ab_ref, cw_ref, alog_ref, dtb_ref,
                   u_ref, w_ref, qd_ref, kd_ref, qk_ref, eg_ref, xp_scr, y_scr):
    i = pl.program_id(1)
    tc = qkv_ref.shape[1]
    c = CHUNK_A
    xp_scr[0:8, :] = jnp.where(i > 0, halo_ref[0], 0.0)
    xp_scr[8:8 + tc, :] = qkv_ref[0]
    cw = cw_ref[...]
    lanes = 128
    for cb in range(QKV_A // lanes):
        cols = slice(cb * lanes, (cb + 1) * lanes)
        y = xp_scr[5:5 + tc, cols] * cw[0:1, cols]
        for j in range(1, CONV):
            y = y + xp_scr[5 + j:5 + j + tc, cols] * cw[j:j + 1, cols]
        y_scr[:, cols] = _silu(y)

    row = lax.broadcasted_iota(jnp.int32, (c, c), 0)
    col = lax.broadcasted_iota(jnp.int32, (c, c), 1)
    incl = row >= col
    strict = row > col
    tri = incl.astype(f32)
    eye_c = (row == col).astype(f32)
    eye_h = _eye(HA)
    alog = alog_ref[...]
    dtb = dtb_ref[...]

    def chunk(n, carry):
        r0 = pl.multiple_of(n * c, c)
        rows = pl.ds(r0, c)
        g, beta = _gdn_gates(ab_ref[0, rows, :], alog, dtb)
        gc = _dot_hi(tri, g)
        gct = _dot_nt_hi(eye_h, gc)
        egc = jnp.exp(gc)
        eg_ref[0, rows, :] = egc
        heads = lambda t, off: jnp.stack([t(y_scr[rows, off + h * DKA:off + (h + 1) * DKA]) for h in range(HA)])
        q = heads(_l2norm, 0) * (DKA ** -0.5)
        k = heads(_l2norm, HA * DKA)
        v = heads(lambda t: t, 2 * HA * DKA)
        col = lambda t: jnp.stack([t[:, h:h + 1] for h in range(HA)])
        gcol, bcol, egcol = col(gc), col(beta), col(egc)
        diff = gcol - gct[:, None, :]
        dec = jnp.where(incl, jnp.exp(jnp.where(incl, diff, 0.0)), 0.0)
        a = jnp.where(strict, _bdot_nt(k, k) * dec, 0.0) * bcol
        x = eye_c - a
        p = _bdot_split(a, a)
        for level in range(4):
            prod = _bdot_split if level == 0 else _bdot
            xp = prod(jnp.concatenate([x, p], axis=1), p)
            x = x + xp[:, :c]
            p = xp[:, c:]
        x = x + _bdot(x, p)
        sol = _bdot_split(x, jnp.concatenate([v * bcol, k * (bcol * egcol)], axis=-1))
        qd = q * egcol
        kd = k * jnp.exp(jnp.stack([gc[c - 1:c, h:h + 1] for h in range(HA)]) - gcol)
        qk = _bdot_nt(q, k) * dec
        for h in range(HA):
            hs = slice(h * DVA, (h + 1) * DVA)
            u_ref[0, rows, hs] = sol[h, :, :DVA]
            w_ref[0, rows, hs] = sol[h, :, DVA:]
            qd_ref[0, rows, hs] = qd[h]
            kd_ref[0, rows, hs] = kd[h]
            qk_ref[0, rows, h * c:(h + 1) * c] = qk[h]
        return carry

    lax.fori_loop(0, tc // c, chunk, 0)


def _gdn_prep(qkv, ab, conv_w, a_log, dt_bias):
    b, l, _ = qkv.shape
    tc = min(GDN_TILE, l)
    hb = tc // 8
    big = lambda n: pl.BlockSpec((1, tc, n), lambda bi, i: (bi, i, 0))
    full = lambda a: pl.BlockSpec(a.shape, lambda bi, i: (0, 0))
    wide = jax.ShapeDtypeStruct((b, l, HA * DVA), f32)
    return pl.pallas_call(
        _gdn_prep_body,
        grid=(b, l // tc),
        in_specs=[big(QKV_A),
                  pl.BlockSpec((1, 8, QKV_A), lambda bi, i: (bi, jnp.maximum(i * hb - 1, 0), 0)),
                  big(2 * HA), full(conv_w), full(a_log), full(dt_bias)],
        out_specs=[big(HA * DVA)] * 4 + [big(HA * CHUNK_A), big(HA)],
        out_shape=[wide] * 4 + [jax.ShapeDtypeStruct((b, l, HA * CHUNK_A), f32),
                                jax.ShapeDtypeStruct((b, l, HA), f32)],
        scratch_shapes=[pltpu.VMEM((tc + 8, QKV_A), f32), pltpu.VMEM((tc, QKV_A), f32)],
        compiler_params=_cp("parallel", "parallel"),
        name="gdn_prep",
    )(qkv, qkv, ab, conv_w, a_log, dt_bias)


def _gdn_scan_body(u_ref, w_ref, qd_ref, kd_ref, qk_ref, eg_ref, gate_ref, nw_ref, y_ref, s_ref):
    i = pl.program_id(1)
    tc = u_ref.shape[1]
    c = CHUNK_A

    @pl.when(i == 0)
    def _():
        s_ref[...] = jnp.zeros_like(s_ref)

    nw = nw_ref[...]

    def chunk(n, carry):
        r0 = pl.multiple_of(n * c, c)
        rows = pl.ds(r0, c)
        eg_last = eg_ref[0, pl.ds(r0 + c - 1, 1), :]
        heads = lambda ref, width: jnp.stack([ref[0, rows, h * width:(h + 1) * width] for h in range(HA)])
        bd = lambda a, b: jnp.einsum('hij,hjk->hik', a.astype(bf16), b.astype(bf16), preferred_element_type=f32)
        s = s_ref[0]
        ws_qs = bd(jnp.concatenate([heads(w_ref, DVA), heads(qd_ref, DVA)], axis=1), s)
        v_new = heads(u_ref, DVA) - ws_qs[:, :c]
        o = ws_qs[:, c:] + bd(heads(qk_ref, c), v_new)
        decay = jnp.stack([eg_last[:, h:h + 1] for h in range(HA)])
        s_ref[0] = s * decay + jnp.einsum('hcd,hce->hde', heads(kd_ref, DVA).astype(bf16), v_new.astype(bf16),
                                          preferred_element_type=f32)
        of = o * lax.rsqrt(jnp.mean(o * o, -1, keepdims=True) + 1e-6)
        for h in range(HA):
            hs = slice(h * DVA, (h + 1) * DVA)
            y_ref[0, rows, hs] = of[h] * nw * _silu(gate_ref[0, rows, hs])
        return carry

    lax.fori_loop(0, tc // c, chunk, 0)


def _gdn_scan(u, w, qd, kd, qk, eg, gate, norm_w):
    b, l, _ = u.shape
    tc = min(GDN_TILE, l)
    big = lambda n: pl.BlockSpec((1, tc, n), lambda bi, i: (bi, i, 0))
    return pl.pallas_call(
        _gdn_scan_body,
        grid=(b, l // tc),
        in_specs=[big(HA * DVA)] * 4 + [big(HA * CHUNK_A), big(HA), big(HA * DVA),
                                       pl.BlockSpec((1, DVA), lambda bi, i: (0, 0))],
        out_specs=[big(HA * DVA), pl.BlockSpec((1, HA, DKA, DVA), lambda bi, i: (bi, 0, 0, 0))],
        out_shape=[jax.ShapeDtypeStruct((b, l, HA * DVA), f32),
                   jax.ShapeDtypeStruct((b, HA, DKA, DVA), f32)],
        compiler_params=_cp("parallel", "arbitrary"),
        name="gdn_scan",
    )(u, w, qd, kd, qk, eg, gate, norm_w.reshape(1, DVA))


def _gdn_prompt(x, w_in, conv_w, a_log, dt_bias, norm_w):
    b, l, d = x.shape
    wq = w_in[:, :QKV_A].astype(bf16)
    wg = w_in[:, QKV_A:QKV_A + HA * DVA].astype(bf16)
    wab = w_in[:, QKV_A + HA * DVA:].astype(bf16)
    qkv, gate, ab = _proj(x.reshape(b * l, d), [wq, wg, wab], min(PROJ_TM, l))
    qkv = qkv.reshape(b, l, QKV_A)
    u, w, qd, kd, qk, eg = _gdn_prep(qkv, ab.reshape(b, l, 2 * HA), conv_w,
                                     a_log.reshape(1, HA), dt_bias.reshape(1, HA))
    y, s = _gdn_scan(u, w, qd, kd, qk, eg, gate.reshape(b, l, HA * DVA), norm_w)
    return y.reshape(b * l, HA * DVA), s, qkv[:, l - (CONV - 1):]


GDN_BT = 8


def _gdn_step_body(qkv_ref, gate_ref, ab_ref, buf_ref, s_ref, cw_ref, alog_ref, dtb_ref, nw_ref,
                   y_ref, s_out_ref, buf_out_ref, q_scr, k_scr, v_scr, eg_scr, beta_scr):
    n = QKV_A
    new = qkv_ref[...]
    cw = cw_ref[...]
    y = new * cw[CONV - 1:CONV, :]
    for j in range(CONV - 1):
        y = y + buf_ref[:, j * n:(j + 1) * n] * cw[j:j + 1, :]
    y = _silu(y)
    buf_out_ref[:, 0:2 * n] = buf_ref[:, n:3 * n]
    buf_out_ref[:, 2 * n:3 * n] = new
    for h in range(HA):
        q_scr[:, h * DKA:(h + 1) * DKA] = _l2norm(y[:, h * DKA:(h + 1) * DKA]) * (DKA ** -0.5)
        k_scr[:, h * DKA:(h + 1) * DKA] = _l2norm(y[:, HA * DKA + h * DKA:HA * DKA + (h + 1) * DKA])
    v_scr[...] = y[:, 2 * HA * DKA:]
    g, beta = _gdn_gates(ab_ref[...], alog_ref[...], dtb_ref[...])
    eg_scr[...] = jnp.exp(g)
    beta_scr[...] = beta
    eye = _eye(DKA)
    row8 = lax.broadcasted_iota(jnp.int32, (8, DKA), 0)
    nw = nw_ref[...]

    for bi in range(qkv_ref.shape[0]):
        r = slice(bi, bi + 1)
        for h in range(HA):
            hs = slice(h * DKA, (h + 1) * DKA)
            k_row = k_scr[r, hs]
            q_row = q_scr[r, hs]
            kq = jnp.where(row8 == 0, k_row, jnp.where(row8 == 1, q_row, 0.0))
            cols = _dot_nt_hi(eye, kq)
            k_col = cols[:, 0:1]
            q_col = cols[:, 1:2]
            sd = s_ref[bi, h] * eg_scr[r, h:h + 1]
            pred = jnp.sum(k_col * sd, axis=0, keepdims=True)
            delta = beta_scr[r, h:h + 1] * (v_scr[r, hs] - pred)
            s_new = sd + k_col * delta
            s_out_ref[bi, h] = s_new
            o = jnp.sum(q_col * s_new, axis=0, keepdims=True)
            of = o * lax.rsqrt(jnp.mean(o * o, -1, keepdims=True) + 1e-6)
            y_ref[r, hs] = of * nw * _silu(gate_ref[r, hs])


def _gdn_step(qkv, gate, ab, conv_buf, s0, conv_w, a_log, dt_bias, norm_w):
    nb = qkv.shape[0]
    bt = GDN_BT
    rowblk = lambda n: pl.BlockSpec((bt, n), lambda i: (i, 0))
    full = lambda a: pl.BlockSpec(a.shape, lambda i: (0, 0))
    sblk = pl.BlockSpec((bt, HA, DKA, DVA), lambda i: (i, 0, 0, 0))
    nw = norm_w.reshape(1, DVA)
    al = a_log.reshape(1, HA)
    db = dt_bias.reshape(1, HA)
    return pl.pallas_call(
        _gdn_step_body,
        grid=(nb // bt,),
        in_specs=[rowblk(QKV_A), rowblk(HA * DVA), rowblk(2 * HA), rowblk(3 * QKV_A), sblk,
                  full(conv_w), full(al), full(db), full(nw)],
        out_specs=[rowblk(HA * DVA), sblk, rowblk(3 * QKV_A)],
        out_shape=[jax.ShapeDtypeStruct((nb, HA * DVA), f32),
                   jax.ShapeDtypeStruct(s0.shape, f32),
                   jax.ShapeDtypeStruct((nb, 3 * QKV_A), f32)],
        scratch_shapes=[pltpu.VMEM((bt, HA * DKA), f32)] * 3 + [pltpu.VMEM((bt, HA), f32)] * 2,
        compiler_params=_cp("parallel"),
        name="gdn_step",
    )(qkv, gate, ab, conv_buf.reshape(nb, 3 * QKV_A), s0, conv_w, al, db, nw)


def _gdn_sample(x, conv_buf, s0, w_in, conv_w, a_log, dt_bias, norm_w):
    nb = x.shape[0]
    wq = w_in[:, :QKV_A].astype(bf16)
    wg = w_in[:, QKV_A:QKV_A + HA * DVA].astype(bf16)
    wab = w_in[:, QKV_A + HA * DVA:].astype(bf16)
    qkv, gate, ab = _proj(x, [wq, wg, wab], nb)
    y, s, buf = _gdn_step(qkv, gate, ab, conv_buf, s0, conv_w, a_log, dt_bias, norm_w)
    return y, s, buf.reshape(nb, CONV - 1, QKV_A)


LRU_TILE = 256


def _gelu(x):
    return 0.5 * x * (1.0 + jnp.tanh(math.sqrt(2.0 / math.pi) * (x + 0.044715 * x * x * x)))


def _lru_gates(u, wa_ref, ba, wx_ref, bx, lam):
    ra, xa = [], []
    for n in range(LRU_BLOCKS):
        ub = u[:, n * LRU_BW:(n + 1) * LRU_BW].astype(bf16)
        ra.append(jnp.dot(ub, wa_ref[n], preferred_element_type=f32))
        xa.append(jnp.dot(ub, wx_ref[n], preferred_element_type=f32))
    r = _sigmoid(jnp.concatenate(ra, axis=-1) + ba)
    i_g = _sigmoid(jnp.concatenate(xa, axis=-1) + bx)
    log_a = -LRU_C * r * _softplus(-lam)
    a = jnp.exp(log_a)
    t = jnp.tanh(log_a)
    b = jnp.sqrt(-2.0 * t / (1.0 - t)) * (i_g * u)
    return a, b


def _lru_prompt_body(gate_ref, rec_ref, halo_ref, cw_ref, cb_ref, wa_ref, ba_ref, wx_ref, bx_ref, lam_ref,
                     y_ref, h_ref, xp_scr):
    i = pl.program_id(1)
    tt = rec_ref.shape[1]

    @pl.when(i == 0)
    def _():
        h_ref[...] = jnp.zeros_like(h_ref)

    xp_scr[0:8, :] = jnp.where(i > 0, halo_ref[0], 0.0)
    xp_scr[8:8 + tt, :] = rec_ref[0]
    cw = cw_ref[...]
    u = xp_scr[5:5 + tt, :] * cw[0:1, :]
    for j in range(1, CONV):
        u = u + xp_scr[5 + j:5 + j + tt, :] * cw[j:j + 1, :]
    u = u + cb_ref[...]
    a, b = _lru_gates(u, wa_ref, ba_ref[...], wx_ref, bx_ref[...], lam_ref[...])
    row = lax.broadcasted_iota(jnp.int32, a.shape, 0)
    s = 1
    while s < tt:
        keep = row >= s
        a_sh = jnp.where(keep, pltpu.roll(a, s, 0), 1.0)
        b_sh = jnp.where(keep, pltpu.roll(b, s, 0), 0.0)
        b = a * b_sh + b
        a = a * a_sh
        s *= 2
    h = b + a * h_ref[0]
    h_ref[0] = h[tt - 1:tt, :]
    y_ref[0] = _gelu(gate_ref[0]) * h


def _lru_prompt(x, w_in, conv_w, conv_b, w_a, b_a, w_x, b_x, lam):
    bsz, l, d = x.shape
    w = LRU_W
    gate_in, rec_in = _proj(x.reshape(bsz * l, d), [w_in[:, :w].astype(bf16), w_in[:, w:].astype(bf16)],
                            min(PROJ_TM, l))
    rec3 = rec_in.reshape(bsz, l, w)
    tt = min(LRU_TILE, l)
    hb = tt // 8
    big = pl.BlockSpec((1, tt, w), lambda bi, i: (bi, i, 0))
    vec = pl.BlockSpec((1, w), lambda bi, i: (0, 0))
    blkw = pl.BlockSpec((LRU_BLOCKS, LRU_BW, LRU_BW), lambda bi, i: (0, 0, 0))
    y, h = pl.pallas_call(
        _lru_prompt_body,
        grid=(bsz, l // tt),
        in_specs=[big, big, pl.BlockSpec((1, 8, w), lambda bi, i: (bi, jnp.maximum(i * hb - 1, 0), 0)),
                  pl.BlockSpec((CONV, w), lambda bi, i: (0, 0)), vec, blkw, vec, blkw, vec, vec],
        out_specs=[big, pl.BlockSpec((1, 1, w), lambda bi, i: (bi, 0, 0))],
        out_shape=[jax.ShapeDtypeStruct((bsz, l, w), f32), jax.ShapeDtypeStruct((bsz, 1, w), f32)],
        scratch_shapes=[pltpu.VMEM((tt + 8, w), f32)],
        compiler_params=_cp("parallel", "arbitrary"),
        name="lru_prompt",
    )(gate_in.reshape(bsz, l, w), rec3, rec3, conv_w, conv_b.reshape(1, w), w_a.astype(bf16), b_a.reshape(1, w),
      w_x.astype(bf16), b_x.reshape(1, w), lam.reshape(1, w))
    return y.reshape(bsz * l, w), h.reshape(bsz, w), rec3[:, l - (CONV - 1):]


def _lru_step_body(gate_ref, rec_ref, buf_ref, h0_ref, cw_ref, cb_ref, wa_ref, ba_ref, wx_ref, bx_ref, lam_ref,
                   y_ref, h_ref, buf_out_ref):
    w = LRU_W
    new = rec_ref[...]
    cw = cw_ref[...]
    u = new * cw[CONV - 1:CONV, :]
    for j in range(CONV - 1):
        u = u + buf_ref[:, j * w:(j + 1) * w] * cw[j:j + 1, :]
    u = u + cb_ref[...]
    buf_out_ref[:, 0:2 * w] = buf_ref[:, w:3 * w]
    buf_out_ref[:, 2 * w:3 * w] = new
    a, b = _lru_gates(u, wa_ref, ba_ref[...], wx_ref, bx_ref[...], lam_ref[...])
    h = b + a * h0_ref[...]
    h_ref[...] = h
    y_ref[...] = _gelu(gate_ref[...]) * h


def _lru_sample(x, conv_buf, h0, w_in, conv_w, conv_b, w_a, b_a, w_x, b_x, lam):
    nb = x.shape[0]
    w = LRU_W
    gate_in, rec_in = _proj(x, [w_in[:, :w].astype(bf16), w_in[:, w:].astype(bf16)], nb)
    y, h, buf = pl.pallas_call(
        _lru_step_body,
        out_shape=[jax.ShapeDtypeStruct((nb, w), f32), jax.ShapeDtypeStruct((nb, w), f32),
                   jax.ShapeDtypeStruct((nb, 3 * w), f32)],
        compiler_params=pltpu.CompilerParams(vmem_limit_bytes=VMEM_LIMIT),
        name="lru_step",
    )(gate_in, rec_in, conv_buf.reshape(nb, 3 * w), h0, conv_w, conv_b.reshape(1, w), w_a.astype(bf16),
      b_a.reshape(1, w), w_x.astype(bf16), b_x.reshape(1, w), lam.reshape(1, w))
    return y, h, buf.reshape(nb, CONV - 1, w)


NEG = -1e30


def _t5_bucket(rel):
    n = jnp.maximum(rel, 0)
    max_exact = N_BUCKETS // 2
    large = max_exact + (jnp.log(jnp.maximum(n, 1).astype(f32) / max_exact)
                         / math.log(MAX_DIST / max_exact) * (N_BUCKETS - max_exact)).astype(jnp.int32)
    return jnp.where(n < max_exact, n, jnp.minimum(large, N_BUCKETS - 1))


def _swa_prompt_body(qt_ref, kp_ref, kc_ref, vtp_ref, vtc_ref, bucket_ref, sinks_ref, rb_ref, o_ref, bias_scr):
    i = pl.program_id(1)
    w = WINDOW

    @pl.when(i == 0)
    def _():
        bucket = bucket_ref[...]
        rel = (lax.broadcasted_iota(jnp.int32, (2 * w, w), 1) + w
               - lax.broadcasted_iota(jnp.int32, (2 * w, w), 0))
        in_window = (rel >= 0) & (rel <= w)
        for h in range(HQ):
            b = jnp.zeros((2 * w, w), f32)
            for n in range(N_BUCKETS):
                b = jnp.where(bucket == n, rb_ref[n, h], b)
            bias_scr[h // GQ, :, (h % GQ) * w:(h % GQ + 1) * w] = jnp.where(in_window, b, NEG)

    hidden_rows = jnp.where(i == 0, w, 0)
    hide = lax.broadcasted_iota(jnp.int32, (2 * w, GQ * w), 0) < hidden_rows
    for kvh in range(HKV):
        heads = range(kvh * GQ, (kvh + 1) * GQ)
        qt = jnp.concatenate([qt_ref[0, h] for h in heads], axis=-1)
        kk = jnp.concatenate([kp_ref[0, kvh], kc_ref[0, kvh]], axis=0)
        s = jnp.where(hide, NEG, jnp.dot(kk, qt, preferred_element_type=f32) + bias_scr[kvh])
        sink = jnp.concatenate([jnp.full((1, w), sinks_ref[h], f32) for h in heads], axis=-1)
        m = jnp.maximum(jnp.max(s, 0, keepdims=True), sink)
        e = jnp.exp(s - m)
        inv = 1.0 / (jnp.sum(e, 0, keepdims=True) + jnp.exp(sink - m))
        vt = jnp.concatenate([vtp_ref[0, kvh], vtc_ref[0, kvh]], axis=-1)
        ot = jnp.dot(vt, (e * inv).astype(bf16), preferred_element_type=f32)
        for g, h in enumerate(heads):
            o_ref[0, h] = ot[:, g * w:(g + 1) * w]


def _swa_prompt(x, w_in, sinks, rel_bias):
    b, l, d = x.shape
    nq = HQ * HD
    nk = HKV * HD
    qt, k, v, vt = _attn_proj(x.reshape(b * l, d), w_in, b, l, min(PROJ_TM, l))
    k = k.reshape(b, l, nk)
    v = v.reshape(b, l, nk)
    w = WINDOW
    rel = jnp.arange(w)[None, :] + w - jnp.arange(2 * w)[:, None]
    bucket = _t5_bucket(rel).astype(jnp.int32)
    qt = qt.reshape(b, HQ, HD, l)
    kh = jnp.swapaxes(k.reshape(b, l, HKV, HD), 1, 2).astype(bf16)
    vt = vt.reshape(b, HKV, HD, l)
    qblk = pl.BlockSpec((1, HQ, HD, w), lambda bi, i: (bi, 0, 0, i))
    smem = pl.BlockSpec(memory_space=pltpu.SMEM)
    ot = pl.pallas_call(
        _swa_prompt_body,
        grid=(b, l // w),
        in_specs=[qblk,
                  pl.BlockSpec((1, HKV, w, HD), lambda bi, i: (bi, 0, jnp.maximum(i - 1, 0), 0)),
                  pl.BlockSpec((1, HKV, w, HD), lambda bi, i: (bi, 0, i, 0)),
                  pl.BlockSpec((1, HKV, HD, w), lambda bi, i: (bi, 0, 0, jnp.maximum(i - 1, 0))),
                  pl.BlockSpec((1, HKV, HD, w), lambda bi, i: (bi, 0, 0, i)),
                  pl.BlockSpec((2 * w, w), lambda bi, i: (0, 0)), smem, smem],
        out_specs=qblk,
        out_shape=jax.ShapeDtypeStruct((b, HQ, HD, l), f32),
        scratch_shapes=[pltpu.VMEM((HKV, 2 * w, GQ * w), f32)],
        compiler_params=_cp("parallel", "arbitrary"),
        name="swa_prompt",
    )(qt, kh, kh, vt, vt, bucket, sinks, rel_bias)
    return ot.reshape(b, nq, l), k, v


SWA_BT = 8


def _head_mask():
    r = lax.broadcasted_iota(jnp.int32, (HQ, HKV * HD), 0)
    c = lax.broadcasted_iota(jnp.int32, (HQ, HKV * HD), 1)
    return (r // GQ) == (c // HD)


def _fold_heads(o, mask):
    o = jnp.where(mask, o, 0.0)
    acc = o[:, 0:HD]
    for c in range(1, HKV):
        acc = acc + o[:, c * HD:(c + 1) * HD]
    return acc


def _swa_step_body(q_ref, kn_ref, vn_ref, kb_ref, vb_ref, onehot_ref, rbt_ref, sinks_ref, o_ref, kb_out, vb_out):
    w = WINDOW
    mask = _head_mask()
    bias_all = _dot_hi(rbt_ref[...], onehot_ref[...])
    bias = bias_all[:, :w]
    bias_new = bias_all[:, w:w + 1]
    sink = sinks_ref[...]
    for bi in range(q_ref.shape[0]):
        qm = jnp.where(mask, jnp.concatenate([q_ref[bi]] * HKV, axis=-1), 0.0)
        kn = kn_ref[bi]
        vn = vn_ref[bi]
        s = _dot_nt(qm, kb_ref[bi]) * (HD ** -0.5) + bias
        s_new = jnp.sum(qm * kn, -1, keepdims=True) * (HD ** -0.5) + bias_new
        m = jnp.maximum(jnp.maximum(jnp.max(s, -1, keepdims=True), s_new), sink)
        e = jnp.exp(s - m)
        e_new = jnp.exp(s_new - m)
        denom = jnp.sum(e, -1, keepdims=True) + e_new + jnp.exp(sink - m)
        o = (_dot(e / denom, vb_ref[bi]) + (e_new / denom) * vn)
        o_ref[bi] = _fold_heads(o, mask)
        kb_out[bi, 0:w - 1, :] = kb_ref[bi, 1:w, :]
        kb_out[bi, w - 1:w, :] = kn
        vb_out[bi, 0:w - 1, :] = vb_ref[bi, 1:w, :]
        vb_out[bi, w - 1:w, :] = vn


def _swa_sample(x, k_buf, v_buf, w_in, sinks, rel_bias):
    nb = x.shape[0]
    nq = HQ * HD
    nk = HKV * HD
    w = WINDOW
    q, k, v = _proj(x, [w_in[:, :nq].astype(bf16), w_in[:, nq:nq + nk].astype(bf16), w_in[:, nq + nk:].astype(bf16)], nb)
    slots = jnp.arange(w + 128)
    bucket = _t5_bucket(jnp.where(slots <= w, w - slots, 0))
    onehot = (bucket[None, :] == jnp.arange(N_BUCKETS)[:, None]).astype(f32)
    bt = SWA_BT
    full = lambda a: pl.BlockSpec(a.shape, lambda i: (0,) * a.ndim)
    bufblk = pl.BlockSpec((bt, w, nk), lambda i: (i, 0, 0))
    rbt = rel_bias.T
    sk = sinks.reshape(HQ, 1)
    o, kb, vb = pl.pallas_call(
        _swa_step_body,
        grid=(nb // bt,),
        in_specs=[pl.BlockSpec((bt, HQ, HD), lambda i: (i, 0, 0)),
                  pl.BlockSpec((bt, 1, nk), lambda i: (i, 0, 0)), pl.BlockSpec((bt, 1, nk), lambda i: (i, 0, 0)),
                  bufblk, bufblk, full(onehot), full(rbt), full(sk)],
        out_specs=[pl.BlockSpec((bt, HQ, HD), lambda i: (i, 0, 0)), bufblk, bufblk],
        out_shape=[jax.ShapeDtypeStruct((nb, HQ, HD), f32), jax.ShapeDtypeStruct((nb, w, nk), f32),
                   jax.ShapeDtypeStruct((nb, w, nk), f32)],
        compiler_params=_cp("parallel"),
        name="swa_step",
    )(q.reshape(nb, HQ, HD), k.reshape(nb, 1, nk), v.reshape(nb, 1, nk), k_buf.reshape(nb, w, nk),
      v_buf.reshape(nb, w, nk), onehot, rbt, sk)
    return o.reshape(nb, nq), k, v, kb.reshape(nb, w, HKV, HD), vb.reshape(nb, w, HKV, HD)


FOX_PREP_TILE = 512
FOX_TQ = 512
FOX_TK = 1024


def _fox_prep_body(x_ref, wft_ref, bf_ref, lf_ref, cum_ref, carry):
    i = pl.program_id(1)
    tt = x_ref.shape[1]

    @pl.when(i == 0)
    def _():
        carry[...] = jnp.zeros_like(carry)

    f = lax.dot_general(wft_ref[...], x_ref[0].astype(bf16), (((1,), (1,)), ((), ())), preferred_element_type=f32)
    lf = _log_sigmoid(f + bf_ref[...])
    r = lax.broadcasted_iota(jnp.int32, (tt, tt), 0)
    c = lax.broadcasted_iota(jnp.int32, (tt, tt), 1)
    cum = _dot_hi(lf, (r <= c).astype(f32)) + carry[...]
    lf_ref[0] = lf
    cum_ref[0] = cum
    carry[...] = cum[:, tt - 1:tt]


FOX_SUM_ROWS = 16


def _fox_flash_body(qt_ref, k_ref, vt_ref, fk_ref, o_ref, m_scr, acc_scr):
    i = pl.program_id(2)
    tq = qt_ref.shape[3]
    tk = FOX_TK
    m_scr[...] = jnp.full_like(m_scr, NEG)
    acc_scr[...] = jnp.zeros_like(acc_scr)
    ones = jnp.ones((FOX_SUM_ROWS, tk), bf16)
    qt_all = jnp.concatenate([qt_ref[0, g] for g in range(GQ)], axis=-1)

    def block(j, masked):
        c0 = pl.multiple_of(j * tk, tk)
        k = k_ref[0, 0, pl.ds(c0, tk), :]
        vt = jnp.concatenate([vt_ref[0, 0, :, pl.ds(c0, tk)], ones], axis=0)
        fk = fk_ref[0, 0, pl.ds(c0, tk), :]
        if masked:
            s_pos = c0 + lax.broadcasted_iota(jnp.int32, (tk, tq), 0)
            t_pos = i * tq + lax.broadcasted_iota(jnp.int32, (tk, tq), 1)
            visible = s_pos <= t_pos
        s_all = jnp.dot(k, qt_all, preferred_element_type=f32)
        for g in range(GQ):
            s = s_all[:, g * tq:(g + 1) * tq] - fk[:, g:g + 1]
            if masked:
                s = jnp.where(visible, s, NEG)
            m_old = m_scr[g]
            m_new = jnp.maximum(m_old, jnp.max(s, 0, keepdims=True))
            alpha = jnp.exp(m_old - m_new)
            p = jnp.exp(s - m_new)
            acc_scr[g] = alpha * acc_scr[g] + jnp.dot(vt, p.astype(bf16), preferred_element_type=f32)
            m_scr[g] = m_new

    n_full = (i * tq) // tk

    def full_block(j, carry):
        block(j, False)
        return carry

    lax.fori_loop(0, n_full, full_block, 0)
    for jj in range(pl.cdiv(tq, tk)):
        block(n_full + jj, True)
    for g in range(GQ):
        o_ref[0, g] = acc_scr[g, :HD] / acc_scr[g, HD:HD + 1]


def _fox_prompt(x, w_in, b_f):
    b, l, d = x.shape
    nq = HQ * HD
    nk = HKV * HD
    qt, k, v, vt = _attn_proj(x.reshape(b * l, d), w_in, b, l, min(PROJ_TM, l))
    tt = min(FOX_PREP_TILE, l)
    wft = w_in[:, nq + 2 * nk:].T.astype(bf16)
    row = pl.BlockSpec((1, HQ, tt), lambda bi, i: (bi, 0, i))
    lft, cumt = pl.pallas_call(
        _fox_prep_body,
        grid=(b, l // tt),
        in_specs=[pl.BlockSpec((1, tt, d), lambda bi, i: (bi, i, 0)),
                  pl.BlockSpec((HQ, d), lambda bi, i: (0, 0)), pl.BlockSpec((HQ, 1), lambda bi, i: (0, 0))],
        out_specs=[row, row],
        out_shape=[jax.ShapeDtypeStruct((b, HQ, l), f32)] * 2,
        scratch_shapes=[pltpu.VMEM((HQ, 1), f32)],
        compiler_params=_cp("parallel", "arbitrary"),
        name="fox_prep",
    )(x, wft, b_f.reshape(HQ, 1))
    fk = jnp.swapaxes(cumt.reshape(b, HKV, GQ, l), 2, 3)
    qt = qt.reshape(b, HQ, HD, l)
    kh = jnp.swapaxes(k.reshape(b, l, HKV, HD), 1, 2).astype(bf16)
    vt = vt.reshape(b, HKV, HD, l)
    tq = min(FOX_TQ, l)
    qblk = pl.BlockSpec((1, GQ, HD, tq), lambda bi, h, i: (bi, h, 0, i))
    ot = pl.pallas_call(
        _fox_flash_body,
        grid=(b, HKV, l // tq),
        in_specs=[qblk,
                  pl.BlockSpec((1, 1, l, HD), lambda bi, h, i: (bi, h, 0, 0)),
                  pl.BlockSpec((1, 1, HD, l), lambda bi, h, i: (bi, h, 0, 0)),
                  pl.BlockSpec((1, 1, l, GQ), lambda bi, h, i: (bi, h, 0, 0))],
        out_specs=qblk,
        out_shape=jax.ShapeDtypeStruct((b, HQ, HD, l), f32),
        scratch_shapes=[pltpu.VMEM((GQ, 1, tq), f32), pltpu.VMEM((GQ, HD + FOX_SUM_ROWS, tq), f32)],
        compiler_params=_cp("parallel", "parallel", "arbitrary"),
        name="fox_flash",
    )(qt, kh, vt, fk)
    return ot.reshape(b, nq, l), k.reshape(b, l, nk), v.reshape(b, l, nk), jnp.swapaxes(lft, 1, 2)


FOX_PAGES = 16
FOX_ROWS = 1


def _fox_step_body(pt_ref, q_ref, kn_ref, vn_ref, lfn_ref, *refs):
    npg = FOX_PAGES
    n_in = 3 * npg * FOX_ROWS
    o_ref, m_scr, l_scr, acc_scr, f_scr = refs[n_in:]
    j = pl.program_id(1)
    ps = PAGE_SIZE
    mask = _head_mask()

    @pl.when(j == 0)
    def _():
        m_scr[...] = jnp.full_like(m_scr, NEG)
        l_scr[...] = jnp.zeros_like(l_scr)
        acc_scr[...] = jnp.zeros_like(acc_scr)
        f_scr[...] = jnp.zeros_like(f_scr)

    r = lax.broadcasted_iota(jnp.int32, (ps, ps), 0)
    c = lax.broadcasted_iota(jnp.int32, (ps, ps), 1)
    upper = (r <= c).astype(f32)

    for row in range(FOX_ROWS):
        base = 3 * npg * row
        k_refs = refs[base:base + npg]
        v_refs = refs[base + npg:base + 2 * npg]
        lf_refs = refs[base + 2 * npg:base + 3 * npg]
        qm = jnp.where(mask, jnp.concatenate([q_ref[row]] * HKV, axis=-1), 0.0) * (HD ** -0.5)

        def update(s, weighted_values, f_end, row=row):
            m_old = m_scr[row]
            m_new = jnp.maximum(m_old, jnp.max(s, -1, keepdims=True))
            alpha = jnp.exp(m_old - m_new)
            p = jnp.exp(s - m_new)
            l_scr[row] = alpha * l_scr[row] + jnp.sum(p, -1, keepdims=True)
            acc_scr[row] = alpha * acc_scr[row] + weighted_values(p)
            m_scr[row] = m_new
            f_scr[row] = f_end

        cum_all = _dot_hi(jnp.concatenate([lf_refs[pg][0] for pg in range(npg)], axis=0), upper)
        f_run = f_scr[row]
        cums = []
        for pg in range(npg):
            cums.append(cum_all[pg * HQ:(pg + 1) * HQ] + f_run)
            f_run = cums[-1][:, ps - 1:ps]
        kt = jnp.concatenate([k_refs[pg][0].reshape(HKV * HD, ps).astype(bf16) for pg in range(npg)], axis=-1)
        vt = jnp.concatenate([v_refs[pg][0].reshape(HKV * HD, ps).astype(bf16) for pg in range(npg)], axis=-1)
        s = jnp.dot(qm.astype(bf16), kt, preferred_element_type=f32) - jnp.concatenate(cums, axis=-1)
        update(s, lambda p, vt=vt: _dot_nt(p, vt), f_run)

        @pl.when(j == pl.num_programs(1) - 1)
        def _(row=row, qm=qm, update=update):
            kn = kn_ref[row]
            vn = vn_ref[row]
            cum = f_scr[row] + lfn_ref[row]
            s = jnp.sum(qm * kn, -1, keepdims=True) - cum
            update(s, lambda p: p * vn, cum)
            o_ref[row] = _fold_heads(acc_scr[row] / l_scr[row], mask)


def _logf_body(f_ref, b_ref, o_ref):
    o_ref[...] = _log_sigmoid(f_ref[...] + b_ref[...])


def _fox_sample(x, cache_k, cache_v, cache_logf, page_table, w_in, b_f):
    nb = x.shape[0]
    nq = HQ * HD
    nk = HKV * HD
    n_pool = cache_k.shape[0]
    n_pages = page_table.shape[1]
    npg = FOX_PAGES
    q, k, v, f = _proj(x, [w_in[:, :nq].astype(bf16), w_in[:, nq:nq + nk].astype(bf16),
                           w_in[:, nq + nk:nq + 2 * nk].astype(bf16), w_in[:, nq + 2 * nk:].astype(bf16)], nb)
    lf_new = pl.pallas_call(_logf_body, out_shape=jax.ShapeDtypeStruct((nb, HQ), f32))(f, b_f.reshape(1, HQ))

    nr = FOX_ROWS

    def page(row, pg, *shape):
        return pl.BlockSpec((1,) + shape,
                            lambda bi, j, pt: (pt[(bi * nr + row) * n_pages + j * npg + pg],) + (0,) * len(shape))

    tok = lambda *shape: pl.BlockSpec((nr,) + shape, lambda bi, j, pt: (bi,) + (0,) * len(shape))
    ck = jnp.transpose(cache_k, (0, 2, 3, 1))
    cv = jnp.transpose(cache_v, (0, 2, 3, 1))
    clf = jnp.transpose(cache_logf, (0, 2, 1))
    page_specs, page_args = [], []
    for row in range(nr):
        for arr, shape in ((ck, (HKV, HD, PAGE_SIZE)), (cv, (HKV, HD, PAGE_SIZE)), (clf, (HQ, PAGE_SIZE))):
            page_specs += [page(row, pg, *shape) for pg in range(npg)]
            page_args += [arr] * npg
    o = pl.pallas_call(
        _fox_step_body,
        grid_spec=pltpu.PrefetchScalarGridSpec(
            num_scalar_prefetch=1,
            grid=(nb // nr, n_pages // npg),
            in_specs=[tok(HQ, HD), tok(1, nk), tok(1, nk), tok(HQ, 1)] + page_specs,
            out_specs=tok(HQ, HD),
            scratch_shapes=[pltpu.VMEM((nr, HQ, 1), f32), pltpu.VMEM((nr, HQ, 1), f32),
                            pltpu.VMEM((nr, HQ, nk), f32), pltpu.VMEM((nr, HQ, 1), f32)]),
        out_shape=jax.ShapeDtypeStruct((nb, HQ, HD), f32),
        compiler_params=_cp("parallel", "arbitrary"),
        name="fox_decode",
    )(page_table.reshape(-1), q.reshape(nb, HQ, HD), k.reshape(nb, 1, nk), v.reshape(nb, 1, nk),
      lf_new.reshape(nb, HQ, 1), *page_args)
    return o.reshape(nb, nq), k, v, lf_new


ROUTER_TILE = 512
MOE_TM = 1024


def _router_body(x_ref, rwt_ref, rb_ref, comb_ref):
    scores = _sigmoid(_dot_nt(rwt_ref[...], x_ref[...]))
    sel = scores + rb_ref[...]
    rows = [sel[e:e + 1, :] for e in range(N_EXPERTS)]
    srow = [scores[e:e + 1, :] for e in range(N_EXPERTS)]
    gs = []
    for g in range(N_GROUPS):
        r = rows[g * EPG:(g + 1) * EPG]
        best = None
        for a in range(EPG):
            for b in range(a + 1, EPG):
                pair = r[a] + r[b]
                best = pair if best is None else jnp.maximum(best, pair)
        gs.append(best)
    g_best = gs[0]
    g_idx = jnp.zeros_like(gs[0], dtype=jnp.int32)
    for g in range(1, N_GROUPS):
        better = gs[g] > g_best
        g_best = jnp.where(better, gs[g], g_best)
        g_idx = jnp.where(better, g, g_idx)

    def in_group(vals, j):
        out = vals[j]
        for g in range(1, N_GROUPS):
            out = jnp.where(g_idx == g, vals[g * EPG + j], out)
        return out

    ig = [in_group(rows, j) for j in range(EPG)]
    sg = [in_group(srow, j) for j in range(EPG)]
    v1, i1, s1 = ig[0], jnp.zeros_like(g_idx), sg[0]
    for j in range(1, EPG):
        better = ig[j] > v1
        v1 = jnp.where(better, ig[j], v1)
        i1 = jnp.where(better, j, i1)
        s1 = jnp.where(better, sg[j], s1)
    v2 = jnp.full_like(v1, -jnp.inf)
    i2 = jnp.zeros_like(g_idx)
    s2 = jnp.zeros_like(s1)
    for j in range(EPG):
        better = (i1 != j) & (ig[j] > v2)
        v2 = jnp.where(better, ig[j], v2)
        i2 = jnp.where(better, j, i2)
        s2 = jnp.where(better, sg[j], s2)
    tot = s1 + s2
    e_row = lax.broadcasted_iota(jnp.int32, scores.shape, 0)
    comb_ref[...] = (jnp.where(e_row == g_idx * EPG + i1, s1 / tot, 0.0)
                     + jnp.where(e_row == g_idx * EPG + i2, s2 / tot, 0.0))


def _router(x, router_w, router_b):
    t, d = x.shape
    tt = min(ROUTER_TILE, t)
    comb_t = pl.pallas_call(
        _router_body,
        grid=(t // tt,),
        in_specs=[pl.BlockSpec((tt, d), lambda i: (i, 0)), pl.BlockSpec((N_EXPERTS, d), lambda i: (0, 0)),
                  pl.BlockSpec((N_EXPERTS, 1), lambda i: (0, 0))],
        out_specs=pl.BlockSpec((N_EXPERTS, tt), lambda i: (0, i)),
        out_shape=jax.ShapeDtypeStruct((N_EXPERTS, t), f32),
        compiler_params=_cp("parallel"),
        name="router",
    )(x, router_w.T, router_b.reshape(N_EXPERTS, 1))
    return comb_t.T


def _moe_dense_body(x_ref, comb_ref, wg_ref, wu_ref, wd_ref, g_ref, b_ref, o_ref, acc_scr):
    e = pl.program_id(1)

    @pl.when(e == 0)
    def _():
        acc_scr[...] = jnp.zeros_like(acc_scr)

    x = x_ref[...].astype(bf16)
    h = _silu(jnp.dot(x, wg_ref[0].astype(bf16), preferred_element_type=f32)) \
        * jnp.dot(x, wu_ref[0].astype(bf16), preferred_element_type=f32)
    y = jnp.dot(h.astype(bf16), wd_ref[0].astype(bf16), preferred_element_type=f32)
    lane = lax.broadcasted_iota(jnp.int32, comb_ref.shape, 1)
    c = jnp.sum(jnp.where(lane == e, comb_ref[...], 0.0), -1, keepdims=True)
    acc_scr[...] += c * y

    @pl.when(e == pl.num_programs(1) - 1)
    def _():
        o_ref[...] = _layer_norm(DN_ALPHA * x_ref[...] + acc_scr[...], g_ref[...], b_ref[...])


def _moe_ln(x, router_w, router_b, w_gate, w_up, w_down, layer, g, b):
    t, d = x.shape
    comb = _router(x, router_w, router_b)
    tm = min(MOE_TM, t)
    return pl.pallas_call(
        _moe_dense_body,
        grid=(t // tm, N_EXPERTS),
        in_specs=[pl.BlockSpec((tm, d), lambda i, e: (i, 0)), pl.BlockSpec((tm, N_EXPERTS), lambda i, e: (i, 0)),
                  pl.BlockSpec((None, 1, d, D_EXPERT), lambda i, e: (layer, e, 0, 0)),
                  pl.BlockSpec((None, 1, d, D_EXPERT), lambda i, e: (layer, e, 0, 0)),
                  pl.BlockSpec((None, 1, D_EXPERT, d), lambda i, e: (layer, e, 0, 0)),
                  pl.BlockSpec((1, d), lambda i, e: (0, 0)), pl.BlockSpec((1, d), lambda i, e: (0, 0))],
        out_specs=pl.BlockSpec((tm, d), lambda i, e: (i, 0)),
        out_shape=jax.ShapeDtypeStruct((t, d), f32),
        scratch_shapes=[pltpu.VMEM((tm, d), f32)],
        compiler_params=_cp("parallel", "arbitrary"),
        name="moe_dense",
    )(x, comb, w_gate, w_up, w_down, g.reshape(1, d), b.reshape(1, d))


def kernel(x_prompt, x_sample, state_a_ssm, state_a_conv, state_b_h, state_b_conv, cache_c_k, cache_c_v, cache_d_k, cache_d_v, cache_d_logf, page_table, ln_g, ln_b, a_w_in, a_conv_w, a_a_log, a_dt_bias, a_norm_w, a_w_out, b_w_in, b_conv_w, b_conv_b, b_w_a, b_b_a, b_w_x, b_b_x, b_lambda, b_w_out, c_w_in, c_sinks, c_w_out, rel_bias, d_w_in, d_b_f, d_w_out, router_w, router_b, moe_w_gate, moe_w_up, moe_w_down):
    bp, lp, d = x_prompt.shape
    nb = x_sample.shape[0]
    xp = x_prompt.reshape(bp * lp, d)
    xs = x_sample.reshape(nb, d)
    tm_p = PROJ_TM

    def finish(xp, xs, mp, ms, w_out, i, transposed=False):
        w = w_out.astype(bf16)
        xp = _proj_ln(mp, w, xp, ln_g[i, 0], ln_b[i, 0], tm_p, transposed)
        xs = _proj_ln(ms, w, xs, ln_g[i, 0], ln_b[i, 0], nb)
        moe_args = (router_w, router_b, moe_w_gate, moe_w_up, moe_w_down, i, ln_g[i, 1], ln_b[i, 1])
        return _moe_ln(xp, *moe_args), _moe_ln(xs, *moe_args)

    a_args = (a_w_in[0], a_conv_w[0], a_a_log[0], a_dt_bias[0], a_norm_w[0])
    mp, p_a_ssm, p_a_conv = _gdn_prompt(xp.reshape(bp, lp, d), *a_args)
    ms, s_a_ssm, s_a_conv = _gdn_sample(xs, state_a_conv[0], state_a_ssm[0], *a_args)
    xp, xs = finish(xp, xs, mp, ms, a_w_out[0], 0)

    b_args = (b_w_in[0], b_conv_w[0], b_conv_b[0], b_w_a[0], b_b_a[0], b_w_x[0], b_b_x[0], b_lambda[0])
    mp, p_b_h, p_b_conv = _lru_prompt(xp.reshape(bp, lp, d), *b_args)
    ms, s_b_h, s_b_conv = _lru_sample(xs, state_b_conv[0], state_b_h[0], *b_args)
    xp, xs = finish(xp, xs, mp, ms, b_w_out[0], 1)

    mp, kp, vp = _swa_prompt(xp.reshape(bp, lp, d), c_w_in[0], c_sinks[0], rel_bias)
    ms, _, _, s_c_k, s_c_v = _swa_sample(xs, cache_c_k[0], cache_c_v[0], c_w_in[0], c_sinks[0], rel_bias)
    p_c_k = kp[:, lp - WINDOW:].reshape(bp, WINDOW, HKV, HD)
    p_c_v = vp[:, lp - WINDOW:].reshape(bp, WINDOW, HKV, HD)
    xp, xs = finish(xp, xs, mp, ms, c_w_out[0], 2, transposed=True)

    mp, kp, vp, p_d_logf = _fox_prompt(xp.reshape(bp, lp, d), d_w_in[0], d_b_f[0])
    ms, ks, vs, lfs = _fox_sample(xs, cache_d_k[0], cache_d_v[0], cache_d_logf[0], page_table, d_w_in[0], d_b_f[0])
    xp, xs = finish(xp, xs, mp, ms, d_w_out[0], 3, transposed=True)

    return (xp.reshape(bp, lp, d), xs.reshape(nb, 1, d),
            p_a_ssm[None], p_a_conv[None], p_b_h[None], p_b_conv[None], p_c_k[None], p_c_v[None],
            kp.reshape(1, bp, lp, HKV, HD), vp.reshape(1, bp, lp, HKV, HD), p_d_logf[None],
            s_a_ssm[None], s_a_conv[None], s_b_h[None], s_b_conv[None], s_c_k[None], s_c_v[None],
            ks.reshape(1, nb, 1, HKV, HD), vs.reshape(1, nb, 1, HKV, HD), lfs.reshape(1, nb, 1, HQ))
```

```python
import functools
import math

import jax
import jax.numpy as jnp
import numpy as np
from jax import lax
from jax.experimental import pallas as pl
from jax.experimental.pallas import tpu as pltpu

f32 = jnp.float32
bf16 = jnp.bfloat16
HI = lax.Precision.HIGHEST

D_MODEL = 1024
DEPTH = 4
DN_ALPHA = (2 * DEPTH) ** 0.25
LN_EPS = 1e-5
CONV = 4
HA = 8
DKA = 128
DVA = 128
QKV_A = HA * (2 * DKA + DVA)
CHUNK_A = 64
LRU_W = 1024
LRU_BLOCKS = 8
LRU_BW = 128
LRU_C = 8.0
HD = 64
HQ = 16
HKV = 4
GQ = HQ // HKV
WINDOW = 128
N_BUCKETS = 32
MAX_DIST = 128
PAGE_SIZE = 128
N_EXPERTS = 16
N_GROUPS = 4
EPG = 4
D_EXPERT = 512

VMEM_LIMIT = 56 * 1024 * 1024
PROJ_TM = 512


def _cp(*sem):
    return pltpu.CompilerParams(dimension_semantics=sem, vmem_limit_bytes=VMEM_LIMIT)


def _dot(a, b):
    return jnp.dot(a.astype(bf16), b.astype(bf16), preferred_element_type=f32)


def _dot_nt(a, b):
    return lax.dot_general(a.astype(bf16), b.astype(bf16), (((1,), (1,)), ((), ())), preferred_element_type=f32)


def _dot_tn(a, b):
    return lax.dot_general(a.astype(bf16), b.astype(bf16), (((0,), (0,)), ((), ())), preferred_element_type=f32)


def _dot_hi(a, b):
    return jnp.dot(a, b, precision=HI, preferred_element_type=f32)


def _split(a):
    hi = a.astype(bf16)
    return hi, (a - hi.astype(f32)).astype(bf16)


def _dot_split(a, b):
    a_hi, a_lo = _split(a)
    b_hi, b_lo = _split(b)
    d = lambda u, v: jnp.dot(u, v, preferred_element_type=f32)
    return d(a_hi, b_hi) + (d(a_hi, b_lo) + d(a_lo, b_hi))


def _dot_exact01(a, b01):
    b = b01.astype(bf16)
    out = None
    for _ in range(3):
        piece = a.astype(bf16)
        a = a - piece.astype(f32)
        term = jnp.dot(piece, b, preferred_element_type=f32)
        out = term if out is None else out + term
    return out


def _bdot_split(a, b):
    a_hi, a_lo = _split(a)
    b_hi, b_lo = _split(b)
    d = lambda u, v: jnp.einsum('hij,hjk->hik', u, v, preferred_element_type=f32)
    return d(a_hi, b_hi) + (d(a_hi, b_lo) + d(a_lo, b_hi))


def _bdot(a, b):
    return jnp.einsum('hij,hjk->hik', a.astype(bf16), b.astype(bf16), preferred_element_type=f32)


def _bdot_nt(a, b):
    return jnp.einsum('hid,hjd->hij', a.astype(bf16), b.astype(bf16), preferred_element_type=f32)


def _dot_nt_hi(a, b):
    return lax.dot_general(a, b, (((1,), (1,)), ((), ())), precision=HI, preferred_element_type=f32)


def _sigmoid(x):
    return 1.0 / (1.0 + jnp.exp(-x))


def _silu(x):
    return x * _sigmoid(x)


def _softplus(x):
    return jnp.maximum(x, 0.0) + jnp.log1p(jnp.exp(-jnp.abs(x)))


def _log_sigmoid(x):
    return -_softplus(-x)


def _eye(n):
    r = lax.broadcasted_iota(jnp.int32, (n, n), 0)
    c = lax.broadcasted_iota(jnp.int32, (n, n), 1)
    return (r == c).astype(f32)


def _proj_body(n_out, x_ref, *refs):
    x = x_ref[...].astype(bf16)
    for w_ref, o_ref in zip(refs[:n_out], refs[n_out:]):
        o_ref[...] = jnp.dot(x, w_ref[...], preferred_element_type=f32)


def _proj(x, ws, tm):
    m, k = x.shape
    n_out = len(ws)
    return pl.pallas_call(
        functools.partial(_proj_body, n_out),
        grid=(m // tm,),
        in_specs=[pl.BlockSpec((tm, k), lambda i: (i, 0))]
        + [pl.BlockSpec(w.shape, lambda i: (0, 0)) for w in ws],
        out_specs=[pl.BlockSpec((tm, w.shape[1]), lambda i: (i, 0)) for w in ws],
        out_shape=[jax.ShapeDtypeStruct((m, w.shape[1]), f32) for w in ws],
        compiler_params=_cp("parallel"),
        name="proj",
    )(x, *ws)


def _attn_proj_body(x_ref, wqt_ref, wk_ref, wv_ref, wvt_ref, qt_ref, k_ref, v_ref, vt_ref):
    x = x_ref[...].astype(bf16)
    nt = lambda w_ref: lax.dot_general(w_ref[...], x, (((1,), (1,)), ((), ())), preferred_element_type=f32)
    qt_ref[0] = (nt(wqt_ref) * (HD ** -0.5)).astype(bf16)
    vt_ref[0] = nt(wvt_ref).astype(bf16)
    k_ref[...] = jnp.dot(x, wk_ref[...], preferred_element_type=f32)
    v_ref[...] = jnp.dot(x, wv_ref[...], preferred_element_type=f32)


def _attn_proj(x, w_in, b, l, tm):
    m, d = x.shape
    nq = HQ * HD
    nk = HKV * HD
    wqt = w_in[:, :nq].T.astype(bf16)
    wk = w_in[:, nq:nq + nk].astype(bf16)
    wv = w_in[:, nq + nk:nq + 2 * nk].astype(bf16)
    nt = l // tm
    full = lambda a: pl.BlockSpec(a.shape, lambda i: (0, 0))
    rows = lambda n: pl.BlockSpec((tm, n), lambda i: (i, 0))
    cols = lambda n: pl.BlockSpec((1, n, tm), lambda i: (i // nt, 0, i % nt))
    return pl.pallas_call(
        _attn_proj_body,
        grid=(m // tm,),
        in_specs=[rows(d), full(wqt), full(wk), full(wv), full(wv.T)],
        out_specs=[cols(nq), rows(nk), rows(nk), cols(nk)],
        out_shape=[jax.ShapeDtypeStruct((b, nq, l), bf16), jax.ShapeDtypeStruct((m, nk), f32),
                   jax.ShapeDtypeStruct((m, nk), f32), jax.ShapeDtypeStruct((b, nk, l), bf16)],
        compiler_params=_cp("parallel"),
        name="attn_proj",
    )(x, wqt, wk, wv, wv.T)


def _layer_norm(z, g, b):
    mu = jnp.mean(z, -1, keepdims=True)
    zc = z - mu
    var = jnp.mean(zc * zc, -1, keepdims=True)
    return zc * lax.rsqrt(var + LN_EPS) * g + b


def _proj_ln_body(h_ref, w_ref, x_ref, g_ref, b_ref, o_ref):
    m = jnp.dot(h_ref[...].astype(bf16), w_ref[...], preferred_element_type=f32)
    o_ref[...] = _layer_norm(DN_ALPHA * x_ref[...] + m, g_ref[...], b_ref[...])


def _proj_ln_t_body(ht_ref, w_ref, x_ref, g_ref, b_ref, o_ref):
    m = lax.dot_general(ht_ref[0].astype(bf16), w_ref[...], (((0,), (0,)), ((), ())), preferred_element_type=f32)
    o_ref[...] = _layer_norm(DN_ALPHA * x_ref[...] + m, g_ref[...], b_ref[...])


def _proj_ln(h, w, x, g, b, tm, transposed=False):
    d = x.shape[1]
    if transposed:
        _, k, l = h.shape
        m = x.shape[0]
        nt = l // tm
        h_spec = pl.BlockSpec((1, k, tm), lambda i: (i // nt, 0, i % nt))
    else:
        m, k = h.shape
        h_spec = pl.BlockSpec((tm, k), lambda i: (i, 0))
    return pl.pallas_call(
        _proj_ln_t_body if transposed else _proj_ln_body,
        grid=(m // tm,),
        in_specs=[h_spec,
                  pl.BlockSpec((k, d), lambda i: (0, 0)),
                  pl.BlockSpec((tm, d), lambda i: (i, 0)),
                  pl.BlockSpec((1, d), lambda i: (0, 0)),
                  pl.BlockSpec((1, d), lambda i: (0, 0))],
        out_specs=pl.BlockSpec((tm, d), lambda i: (i, 0)),
        out_shape=jax.ShapeDtypeStruct((m, d), f32),
        compiler_params=_cp("parallel"),
        name="proj_ln",
    )(h, w, x, g.reshape(1, d), b.reshape(1, d))


GDN_TILE = 256


def _gdn_gates(ab, alog, dtb):
    g = -jnp.exp(alog) * _softplus(ab[:, 0:HA] + dtb)
    beta = _sigmoid(ab[:, HA:2 * HA])
    return g, beta


def _l2norm(t):
    return t * lax.rsqrt(jnp.sum(t * t, -1, keepdims=True) + 1e-6)


def _gdn_prep_body(qkv_ref, halo_ref, ab_ref, cw_ref, alog_ref, dtb_ref,
                   u_ref, w_ref, qd_ref, kd_ref, qk_ref, eg_ref, xp_scr, y_scr):
    i = pl.program_id(1)
    tc = qkv_ref.shape[1]
    c = CHUNK_A
    xp_scr[0:8, :] = jnp.where(i > 0, halo_ref[0], 0.0)
    xp_scr[8:8 + tc, :] = qkv_ref[0]
    cw = cw_ref[...]
    lanes = 128
    for cb in range(QKV_A // lanes):
        cols = slice(cb * lanes, (cb + 1) * lanes)
        y = xp_scr[5:5 + tc, cols] * cw[0:1, cols]
        for j in range(1, CONV):
            y = y + xp_scr[5 + j:5 + j + tc, cols] * cw[j:j + 1, cols]
        y_scr[:, cols] = _silu(y)

    row = lax.broadcasted_iota(jnp.int32, (c, c), 0)
    col = lax.broadcasted_iota(jnp.int32, (c, c), 1)
    incl = row >= col
    strict = row > col
    tri = incl.astype(f32)
    eye_c = (row == col).astype(f32)
    eye_h = _eye(HA)
    alog = alog_ref[...]
    dtb = dtb_ref[...]

    def chunk(n, carry):
        r0 = pl.multiple_of(n * c, c)
        rows = pl.ds(r0, c)
        g, beta = _gdn_gates(ab_ref[0, rows, :], alog, dtb)
        gc = _dot_hi(tri, g)
        gct = _dot_nt_hi(eye_h, gc)
        egc = jnp.exp(gc)
        eg_ref[0, rows, :] = egc
        heads = lambda t, off: jnp.stack([t(y_scr[rows, off + h * DKA:off + (h + 1) * DKA]) for h in range(HA)])
        q = heads(_l2norm, 0) * (DKA ** -0.5)
        k = heads(_l2norm, HA * DKA)
        v = heads(lambda t: t, 2 * HA * DKA)
        col = lambda t: jnp.stack([t[:, h:h + 1] for h in range(HA)])
        gcol, bcol, egcol = col(gc), col(beta), col(egc)
        diff = gcol - gct[:, None, :]
        dec = jnp.where(incl, jnp.exp(jnp.where(incl, diff, 0.0)), 0.0)
        a = jnp.where(strict, _bdot_nt(k, k) * dec, 0.0) * bcol
        x = eye_c - a
        p = _bdot_split(a, a)
        for level in range(4):
            prod = _bdot_split if level == 0 else _bdot
            xp = prod(jnp.concatenate([x, p], axis=1), p)
            x = x + xp[:, :c]
            p = xp[:, c:]
        x = x + _bdot(x, p)
        sol = _bdot_split(x, jnp.concatenate([v * bcol, k * (bcol * egcol)], axis=-1))
        qd = q * egcol
        kd = k * jnp.exp(jnp.stack([gc[c - 1:c, h:h + 1] for h in range(HA)]) - gcol)
        qk = _bdot_nt(q, k) * dec
        for h in range(HA):
            hs = slice(h * DVA, (h + 1) * DVA)
            u_ref[0, rows, hs] = sol[h, :, :DVA]
            w_ref[0, rows, hs] = sol[h, :, DVA:]
            qd_ref[0, rows, hs] = qd[h]
            kd_ref[0, rows, hs] = kd[h]
            qk_ref[0, rows, h * c:(h + 1) * c] = qk[h]
        return carry

    lax.fori_loop(0, tc // c, chunk, 0)


def _gdn_prep(qkv, ab, conv_w, a_log, dt_bias):
    b, l, _ = qkv.shape
    tc = min(GDN_TILE, l)
    hb = tc // 8
    big = lambda n: pl.BlockSpec((1, tc, n), lambda bi, i: (bi, i, 0))
    full = lambda a: pl.BlockSpec(a.shape, lambda bi, i: (0, 0))
    wide = jax.ShapeDtypeStruct((b, l, HA * DVA), f32)
    return pl.pallas_call(
        _gdn_prep_body,
        grid=(b, l // tc),
        in_specs=[big(QKV_A),
                  pl.BlockSpec((1, 8, QKV_A), lambda bi, i: (bi, jnp.maximum(i * hb - 1, 0), 0)),
                  big(2 * HA), full(conv_w), full(a_log), full(dt_bias)],
        out_specs=[big(HA * DVA)] * 4 + [big(HA * CHUNK_A), big(HA)],
        out_shape=[wide] * 4 + [jax.ShapeDtypeStruct((b, l, HA * CHUNK_A), f32),
                                jax.ShapeDtypeStruct((b, l, HA), f32)],
        scratch_shapes=[pltpu.VMEM((tc + 8, QKV_A), f32), pltpu.VMEM((tc, QKV_A), f32)],
        compiler_params=_cp("parallel", "parallel"),
        name="gdn_prep",
    )(qkv, qkv, ab, conv_w, a_log, dt_bias)


def _gdn_scan_body(u_ref, w_ref, qd_ref, kd_ref, qk_ref, eg_ref, gate_ref, nw_ref, y_ref, s_ref):
    i = pl.program_id(1)
    tc = u_ref.shape[1]
    c = CHUNK_A

    @pl.when(i == 0)
    def _():
        s_ref[...] = jnp.zeros_like(s_ref)

    nw = nw_ref[...]

    def chunk(n, carry):
        r0 = pl.multiple_of(n * c, c)
        rows = pl.ds(r0, c)
        eg_last = eg_ref[0, pl.ds(r0 + c - 1, 1), :]
        heads = lambda ref, width: jnp.stack([ref[0, rows, h * width:(h + 1) * width] for h in range(HA)])
        bd = lambda a, b: jnp.einsum('hij,hjk->hik', a.astype(bf16), b.astype(bf16), preferred_element_type=f32)
        s = s_ref[0]
        ws_qs = bd(jnp.concatenate([heads(w_ref, DVA), heads(qd_ref, DVA)], axis=1), s)
        v_new = heads(u_ref, DVA) - ws_qs[:, :c]
        o = ws_qs[:, c:] + bd(heads(qk_ref, c), v_new)
        decay = jnp.stack([eg_last[:, h:h + 1] for h in range(HA)])
        s_ref[0] = s * decay + jnp.einsum('hcd,hce->hde', heads(kd_ref, DVA).astype(bf16), v_new.astype(bf16),
                                          preferred_element_type=f32)
        of = o * lax.rsqrt(jnp.mean(o * o, -1, keepdims=True) + 1e-6)
        for h in range(HA):
            hs = slice(h * DVA, (h + 1) * DVA)
            y_ref[0, rows, hs] = of[h] * nw * _silu(gate_ref[0, rows, hs])
        return carry

    lax.fori_loop(0, tc // c, chunk, 0)


def _gdn_scan(u, w, qd, kd, qk, eg, gate, norm_w):
    b, l, _ = u.shape
    tc = min(GDN_TILE, l)
    big = lambda n: pl.BlockSpec((1, tc, n), lambda bi, i: (bi, i, 0))
    return pl.pallas_call(
        _gdn_scan_body,
        grid=(b, l // tc),
        in_specs=[big(HA * DVA)] * 4 + [big(HA * CHUNK_A), big(HA), big(HA * DVA),
                                       pl.BlockSpec((1, DVA), lambda bi, i: (0, 0))],
        out_specs=[big(HA * DVA), pl.BlockSpec((1, HA, DKA, DVA), lambda bi, i: (bi, 0, 0, 0))],
        out_shape=[jax.ShapeDtypeStruct((b, l, HA * DVA), f32),
                   jax.ShapeDtypeStruct((b, HA, DKA, DVA), f32)],
        compiler_params=_cp("parallel", "arbitrary"),
        name="gdn_scan",
    )(u, w, qd, kd, qk, eg, gate, norm_w.reshape(1, DVA))


def _gdn_prompt(x, w_in, conv_w, a_log, dt_bias, norm_w):
    b, l, d = x.shape
    wq = w_in[:, :QKV_A].astype(bf16)
    wg = w_in[:, QKV_A:QKV_A + HA * DVA].astype(bf16)
    wab = w_in[:, QKV_A + HA * DVA:].astype(bf16)
    qkv, gate, ab = _proj(x.reshape(b * l, d), [wq, wg, wab], min(PROJ_TM, l))
    qkv = qkv.reshape(b, l, QKV_A)
    u, w, qd, kd, qk, eg = _gdn_prep(qkv, ab.reshape(b, l, 2 * HA), conv_w,
                                     a_log.reshape(1, HA), dt_bias.reshape(1, HA))
    y, s = _gdn_scan(u, w, qd, kd, qk, eg, gate.reshape(b, l, HA * DVA), norm_w)
    return y.reshape(b * l, HA * DVA), s, qkv[:, l - (CONV - 1):]


GDN_BT = 8


def _gdn_step_body(qkv_ref, gate_ref, ab_ref, buf_ref, s_ref, cw_ref, alog_ref, dtb_ref, nw_ref,
                   y_ref, s_out_ref, buf_out_ref, q_scr, k_scr, v_scr, eg_scr, beta_scr):
    n = QKV_A
    new = qkv_ref[...]
    cw = cw_ref[...]
    y = new * cw[CONV - 1:CONV, :]
    for j in range(CONV - 1):
        y = y + buf_ref[:, j * n:(j + 1) * n] * cw[j:j + 1, :]
    y = _silu(y)
    buf_out_ref[:, 0:2 * n] = buf_ref[:, n:3 * n]
    buf_out_ref[:, 2 * n:3 * n] = new
    for h in range(HA):
        q_scr[:, h * DKA:(h + 1) * DKA] = _l2norm(y[:, h * DKA:(h + 1) * DKA]) * (DKA ** -0.5)
        k_scr[:, h * DKA:(h + 1) * DKA] = _l2norm(y[:, HA * DKA + h * DKA:HA * DKA + (h + 1) * DKA])
    v_scr[...] = y[:, 2 * HA * DKA:]
    g, beta = _gdn_gates(ab_ref[...], alog_ref[...], dtb_ref[...])
    eg_scr[...] = jnp.exp(g)
    beta_scr[...] = beta
    eye = _eye(DKA)
    row8 = lax.broadcasted_iota(jnp.int32, (8, DKA), 0)
    nw = nw_ref[...]

    for bi in range(qkv_ref.shape[0]):
        r = slice(bi, bi + 1)
        for h in range(HA):
            hs = slice(h * DKA, (h + 1) * DKA)
            k_row = k_scr[r, hs]
            q_row = q_scr[r, hs]
            kq = jnp.where(row8 == 0, k_row, jnp.where(row8 == 1, q_row, 0.0))
            cols = _dot_nt_hi(eye, kq)
            k_col = cols[:, 0:1]
            q_col = cols[:, 1:2]
            sd = s_ref[bi, h] * eg_scr[r, h:h + 1]
            pred = jnp.sum(k_col * sd, axis=0, keepdims=True)
            delta = beta_scr[r, h:h + 1] * (v_scr[r, hs] - pred)
            s_new = sd + k_col * delta
            s_out_ref[bi, h] = s_new
            o = jnp.sum(q_col * s_new, axis=0, keepdims=True)
            of = o * lax.rsqrt(jnp.mean(o * o, -1, keepdims=True) + 1e-6)
            y_ref[r, hs] = of * nw * _silu(gate_ref[r, hs])


def _gdn_step(qkv, gate, ab, conv_buf, s0, conv_w, a_log, dt_bias, norm_w):
    nb = qkv.shape[0]
    bt = GDN_BT
    rowblk = lambda n: pl.BlockSpec((bt, n), lambda i: (i, 0))
    full = lambda a: pl.BlockSpec(a.shape, lambda i: (0, 0))
    sblk = pl.BlockSpec((bt, HA, DKA, DVA), lambda i: (i, 0, 0, 0))
    nw = norm_w.reshape(1, DVA)
    al = a_log.reshape(1, HA)
    db = dt_bias.reshape(1, HA)
    return pl.pallas_call(
        _gdn_step_body,
        grid=(nb // bt,),
        in_specs=[rowblk(QKV_A), rowblk(HA * DVA), rowblk(2 * HA), rowblk(3 * QKV_A), sblk,
                  full(conv_w), full(al), full(db), full(nw)],
        out_specs=[rowblk(HA * DVA), sblk, rowblk(3 * QKV_A)],
        out_shape=[jax.ShapeDtypeStruct((nb, HA * DVA), f32),
                   jax.ShapeDtypeStruct(s0.shape, f32),
                   jax.ShapeDtypeStruct((nb, 3 * QKV_A), f32)],
        scratch_shapes=[pltpu.VMEM((bt, HA * DKA), f32)] * 3 + [pltpu.VMEM((bt, HA), f32)] * 2,
        compiler_params=_cp("parallel"),
        name="gdn_step",
    )(qkv, gate, ab, conv_buf.reshape(nb, 3 * QKV_A), s0, conv_w, al, db, nw)


def _gdn_sample(x, conv_buf, s0, w_in, conv_w, a_log, dt_bias, norm_w):
    nb = x.shape[0]
    wq = w_in[:, :QKV_A].astype(bf16)
    wg = w_in[:, QKV_A:QKV_A + HA * DVA].astype(bf16)
    wab = w_in[:, QKV_A + HA * DVA:].astype(bf16)
    qkv, gate, ab = _proj(x, [wq, wg, wab], nb)
    y, s, buf = _gdn_step(qkv, gate, ab, conv_buf, s0, conv_w, a_log, dt_bias, norm_w)
    return y, s, buf.reshape(nb, CONV - 1, QKV_A)


LRU_TILE = 256


def _gelu(x):
    return 0.5 * x * (1.0 + jnp.tanh(math.sqrt(2.0 / math.pi) * (x + 0.044715 * x * x * x)))


def _lru_gates(u, wa_ref, ba, wx_ref, bx, lam):
    ra, xa = [], []
    for n in range(LRU_BLOCKS):
        ub = u[:, n * LRU_BW:(n + 1) * LRU_BW].astype(bf16)
        ra.append(jnp.dot(ub, wa_ref[n], preferred_element_type=f32))
        xa.append(jnp.dot(ub, wx_ref[n], preferred_element_type=f32))
    r = _sigmoid(jnp.concatenate(ra, axis=-1) + ba)
    i_g = _sigmoid(jnp.concatenate(xa, axis=-1) + bx)
    log_a = -LRU_C * r * _softplus(-lam)
    a = jnp.exp(log_a)
    t = jnp.tanh(log_a)
    z = -2.0 * t / (1.0 - t)
    b = jnp.where(z > 0.0, z * lax.rsqrt(z), 0.0) * (i_g * u)
    return a, b


def _lru_prompt_body(gate_ref, rec_ref, halo_ref, cw_ref, cb_ref, wa_ref, ba_ref, wx_ref, bx_ref, lam_ref,
                     y_ref, h_ref, xp_scr):
    i = pl.program_id(1)
    tt = rec_ref.shape[1]

    @pl.when(i == 0)
    def _():
        h_ref[...] = jnp.zeros_like(h_ref)

    xp_scr[0:8, :] = jnp.where(i > 0, halo_ref[0], 0.0)
    xp_scr[8:8 + tt, :] = rec_ref[0]
    cw = cw_ref[...]
    u = xp_scr[5:5 + tt, :] * cw[0:1, :]
    for j in range(1, CONV):
        u = u + xp_scr[5 + j:5 + j + tt, :] * cw[j:j + 1, :]
    u = u + cb_ref[...]
    a, b = _lru_gates(u, wa_ref, ba_ref[...], wx_ref, bx_ref[...], lam_ref[...])
    row = lax.broadcasted_iota(jnp.int32, a.shape, 0)
    s = 1
    while s < tt:
        keep = row >= s
        a_sh = jnp.where(keep, pltpu.roll(a, s, 0), 1.0)
        b_sh = jnp.where(keep, pltpu.roll(b, s, 0), 0.0)
        b = a * b_sh + b
        a = a * a_sh
        s *= 2
    h = b + a * h_ref[0]
    h_ref[0] = h[tt - 1:tt, :]
    y_ref[0] = _gelu(gate_ref[0]) * h


def _lru_prompt(x, w_in, conv_w, conv_b, w_a, b_a, w_x, b_x, lam):
    bsz, l, d = x.shape
    w = LRU_W
    gate_in, rec_in = _proj(x.reshape(bsz * l, d), [w_in[:, :w].astype(bf16), w_in[:, w:].astype(bf16)],
                            min(PROJ_TM, l))
    rec3 = rec_in.reshape(bsz, l, w)
    tt = min(LRU_TILE, l)
    hb = tt // 8
    big = pl.BlockSpec((1, tt, w), lambda bi, i: (bi, i, 0))
    vec = pl.BlockSpec((1, w), lambda bi, i: (0, 0))
    blkw = pl.BlockSpec((LRU_BLOCKS, LRU_BW, LRU_BW), lambda bi, i: (0, 0, 0))
    y, h = pl.pallas_call(
        _lru_prompt_body,
        grid=(bsz, l // tt),
        in_specs=[big, big, pl.BlockSpec((1, 8, w), lambda bi, i: (bi, jnp.maximum(i * hb - 1, 0), 0)),
                  pl.BlockSpec((CONV, w), lambda bi, i: (0, 0)), vec, blkw, vec, blkw, vec, vec],
        out_specs=[big, pl.BlockSpec((1, 1, w), lambda bi, i: (bi, 0, 0))],
        out_shape=[jax.ShapeDtypeStruct((bsz, l, w), f32), jax.ShapeDtypeStruct((bsz, 1, w), f32)],
        scratch_shapes=[pltpu.VMEM((tt + 8, w), f32)],
        compiler_params=_cp("parallel", "arbitrary"),
        name="lru_prompt",
    )(gate_in.reshape(bsz, l, w), rec3, rec3, conv_w, conv_b.reshape(1, w), w_a.astype(bf16), b_a.reshape(1, w),
      w_x.astype(bf16), b_x.reshape(1, w), lam.reshape(1, w))
    return y.reshape(bsz * l, w), h.reshape(bsz, w), rec3[:, l - (CONV - 1):]


def _lru_step_body(gate_ref, rec_ref, buf_ref, h0_ref, cw_ref, cb_ref, wa_ref, ba_ref, wx_ref, bx_ref, lam_ref,
                   y_ref, h_ref, buf_out_ref):
    w = LRU_W
    new = rec_ref[...]
    cw = cw_ref[...]
    u = new * cw[CONV - 1:CONV, :]
    for j in range(CONV - 1):
        u = u + buf_ref[:, j * w:(j + 1) * w] * cw[j:j + 1, :]
    u = u + cb_ref[...]
    buf_out_ref[:, 0:2 * w] = buf_ref[:, w:3 * w]
    buf_out_ref[:, 2 * w:3 * w] = new
    a, b = _lru_gates(u, wa_ref, ba_ref[...], wx_ref, bx_ref[...], lam_ref[...])
    h = b + a * h0_ref[...]
    h_ref[...] = h
    y_ref[...] = _gelu(gate_ref[...]) * h


def _lru_sample(x, conv_buf, h0, w_in, conv_w, conv_b, w_a, b_a, w_x, b_x, lam):
    nb = x.shape[0]
    w = LRU_W
    gate_in, rec_in = _proj(x, [w_in[:, :w].astype(bf16), w_in[:, w:].astype(bf16)], nb)
    y, h, buf = pl.pallas_call(
        _lru_step_body,
        out_shape=[jax.ShapeDtypeStruct((nb, w), f32), jax.ShapeDtypeStruct((nb, w), f32),
                   jax.ShapeDtypeStruct((nb, 3 * w), f32)],
        compiler_params=pltpu.CompilerParams(vmem_limit_bytes=VMEM_LIMIT),
        name="lru_step",
    )(gate_in, rec_in, conv_buf.reshape(nb, 3 * w), h0, conv_w, conv_b.reshape(1, w), w_a.astype(bf16),
      b_a.reshape(1, w), w_x.astype(bf16), b_x.reshape(1, w), lam.reshape(1, w))
    return y, h, buf.reshape(nb, CONV - 1, w)


NEG = -1e30


def _t5_bucket(rel):
    n = jnp.maximum(rel, 0)
    max_exact = N_BUCKETS // 2
    large = max_exact + (jnp.log(jnp.maximum(n, 1).astype(f32) / max_exact)
                         / math.log(MAX_DIST / max_exact) * (N_BUCKETS - max_exact)).astype(jnp.int32)
    return jnp.where(n < max_exact, n, jnp.minimum(large, N_BUCKETS - 1))


def _swa_prompt_body(qt_ref, kp_ref, kc_ref, vtp_ref, vtc_ref, bucket_ref, sinks_ref, rb_ref, o_ref, bias_scr):
    i = pl.program_id(1)
    w = WINDOW

    @pl.when(i == 0)
    def _():
        bucket = bucket_ref[...]
        rel = (lax.broadcasted_iota(jnp.int32, (2 * w, w), 1) + w
               - lax.broadcasted_iota(jnp.int32, (2 * w, w), 0))
        in_window = (rel >= 0) & (rel <= w)
        for h in range(HQ):
            b = jnp.zeros((2 * w, w), f32)
            for n in range(N_BUCKETS):
                b = jnp.where(bucket == n, rb_ref[n, h], b)
            bias_scr[h // GQ, :, (h % GQ) * w:(h % GQ + 1) * w] = jnp.where(in_window, b, NEG)

    hidden_rows = jnp.where(i == 0, w, 0)
    hide = lax.broadcasted_iota(jnp.int32, (2 * w, GQ * w), 0) < hidden_rows
    for kvh in range(HKV):
        heads = range(kvh * GQ, (kvh + 1) * GQ)
        qt = jnp.concatenate([qt_ref[0, h] for h in heads], axis=-1)
        kk = jnp.concatenate([kp_ref[0, kvh], kc_ref[0, kvh]], axis=0)
        s = jnp.where(hide, NEG, jnp.dot(kk, qt, preferred_element_type=f32) + bias_scr[kvh])
        sink = jnp.concatenate([jnp.full((1, w), sinks_ref[h], f32) for h in heads], axis=-1)
        m = jnp.maximum(jnp.max(s, 0, keepdims=True), sink)
        e = jnp.exp(s - m)
        inv = 1.0 / (jnp.sum(e, 0, keepdims=True) + jnp.exp(sink - m))
        vt = jnp.concatenate([vtp_ref[0, kvh], vtc_ref[0, kvh]], axis=-1)
        ot = jnp.dot(vt, (e * inv).astype(bf16), preferred_element_type=f32)
        for g, h in enumerate(heads):
            o_ref[0, h] = ot[:, g * w:(g + 1) * w]


def _swa_prompt(x, w_in, sinks, rel_bias):
    b, l, d = x.shape
    nq = HQ * HD
    nk = HKV * HD
    qt, k, v, vt = _attn_proj(x.reshape(b * l, d), w_in, b, l, min(PROJ_TM, l))
    k = k.reshape(b, l, nk)
    v = v.reshape(b, l, nk)
    w = WINDOW
    rel = jnp.arange(w)[None, :] + w - jnp.arange(2 * w)[:, None]
    bucket = _t5_bucket(rel).astype(jnp.int32)
    qt = qt.reshape(b, HQ, HD, l)
    kh = jnp.swapaxes(k.reshape(b, l, HKV, HD), 1, 2).astype(bf16)
    vt = vt.reshape(b, HKV, HD, l)
    qblk = pl.BlockSpec((1, HQ, HD, w), lambda bi, i: (bi, 0, 0, i))
    smem = pl.BlockSpec(memory_space=pltpu.SMEM)
    ot = pl.pallas_call(
        _swa_prompt_body,
        grid=(b, l // w),
        in_specs=[qblk,
                  pl.BlockSpec((1, HKV, w, HD), lambda bi, i: (bi, 0, jnp.maximum(i - 1, 0), 0)),
                  pl.BlockSpec((1, HKV, w, HD), lambda bi, i: (bi, 0, i, 0)),
                  pl.BlockSpec((1, HKV, HD, w), lambda bi, i: (bi, 0, 0, jnp.maximum(i - 1, 0))),
                  pl.BlockSpec((1, HKV, HD, w), lambda bi, i: (bi, 0, 0, i)),
                  pl.BlockSpec((2 * w, w), lambda bi, i: (0, 0)), smem, smem],
        out_specs=qblk,
        out_shape=jax.ShapeDtypeStruct((b, HQ, HD, l), f32),
        scratch_shapes=[pltpu.VMEM((HKV, 2 * w, GQ * w), f32)],
        compiler_params=_cp("parallel", "arbitrary"),
        name="swa_prompt",
    )(qt, kh, kh, vt, vt, bucket, sinks, rel_bias)
    return ot.reshape(b, nq, l), k, v


SWA_BT = 8


def _head_mask():
    r = lax.broadcasted_iota(jnp.int32, (HQ, HKV * HD), 0)
    c = lax.broadcasted_iota(jnp.int32, (HQ, HKV * HD), 1)
    return (r // GQ) == (c // HD)


def _fold_heads(o, mask):
    o = jnp.where(mask, o, 0.0)
    acc = o[:, 0:HD]
    for c in range(1, HKV):
        acc = acc + o[:, c * HD:(c + 1) * HD]
    return acc


def _swa_step_body(q_ref, kn_ref, vn_ref, kb_ref, vb_ref, onehot_ref, rbt_ref, sinks_ref, o_ref, kb_out, vb_out):
    w = WINDOW
    mask = _head_mask()
    bias_all = _dot_hi(rbt_ref[...], onehot_ref[...])
    bias = bias_all[:, :w]
    bias_new = bias_all[:, w:w + 1]
    sink = sinks_ref[...]
    for bi in range(q_ref.shape[0]):
        qm = jnp.where(mask, jnp.concatenate([q_ref[bi]] * HKV, axis=-1), 0.0)
        kn = kn_ref[bi]
        vn = vn_ref[bi]
        s = _dot_nt(qm, kb_ref[bi]) * (HD ** -0.5) + bias
        s_new = jnp.sum(qm * kn, -1, keepdims=True) * (HD ** -0.5) + bias_new
        m = jnp.maximum(jnp.maximum(jnp.max(s, -1, keepdims=True), s_new), sink)
        e = jnp.exp(s - m)
        e_new = jnp.exp(s_new - m)
        denom = jnp.sum(e, -1, keepdims=True) + e_new + jnp.exp(sink - m)
        o = (_dot(e / denom, vb_ref[bi]) + (e_new / denom) * vn)
        o_ref[bi] = _fold_heads(o, mask)
        kb_out[bi, 0:w - 1, :] = kb_ref[bi, 1:w, :]
        kb_out[bi, w - 1:w, :] = kn
        vb_out[bi, 0:w - 1, :] = vb_ref[bi, 1:w, :]
        vb_out[bi, w - 1:w, :] = vn


def _swa_sample(x, k_buf, v_buf, w_in, sinks, rel_bias):
    nb = x.shape[0]
    nq = HQ * HD
    nk = HKV * HD
    w = WINDOW
    q, k, v = _proj(x, [w_in[:, :nq].astype(bf16), w_in[:, nq:nq + nk].astype(bf16), w_in[:, nq + nk:].astype(bf16)], nb)
    slots = jnp.arange(w + 128)
    bucket = _t5_bucket(jnp.where(slots <= w, w - slots, 0))
    onehot = (bucket[None, :] == jnp.arange(N_BUCKETS)[:, None]).astype(f32)
    bt = SWA_BT
    full = lambda a: pl.BlockSpec(a.shape, lambda i: (0,) * a.ndim)
    bufblk = pl.BlockSpec((bt, w, nk), lambda i: (i, 0, 0))
    rbt = rel_bias.T
    sk = sinks.reshape(HQ, 1)
    o, kb, vb = pl.pallas_call(
        _swa_step_body,
        grid=(nb // bt,),
        in_specs=[pl.BlockSpec((bt, HQ, HD), lambda i: (i, 0, 0)),
                  pl.BlockSpec((bt, 1, nk), lambda i: (i, 0, 0)), pl.BlockSpec((bt, 1, nk), lambda i: (i, 0, 0)),
                  bufblk, bufblk, full(onehot), full(rbt), full(sk)],
        out_specs=[pl.BlockSpec((bt, HQ, HD), lambda i: (i, 0, 0)), bufblk, bufblk],
        out_shape=[jax.ShapeDtypeStruct((nb, HQ, HD), f32), jax.ShapeDtypeStruct((nb, w, nk), f32),
                   jax.ShapeDtypeStruct((nb, w, nk), f32)],
        compiler_params=_cp("parallel"),
        name="swa_step",
    )(q.reshape(nb, HQ, HD), k.reshape(nb, 1, nk), v.reshape(nb, 1, nk), k_buf.reshape(nb, w, nk),
      v_buf.reshape(nb, w, nk), onehot, rbt, sk)
    return o.reshape(nb, nq), k, v, kb.reshape(nb, w, HKV, HD), vb.reshape(nb, w, HKV, HD)


FOX_PREP_TILE = 512
FOX_TQ = 512
FOX_TK = 1024


def _fox_prep_body(x_ref, wft_ref, bf_ref, lf_ref, cum_ref, carry):
    i = pl.program_id(1)
    tt = x_ref.shape[1]

    @pl.when(i == 0)
    def _():
        carry[...] = jnp.zeros_like(carry)

    f = lax.dot_general(wft_ref[...], x_ref[0].astype(bf16), (((1,), (1,)), ((), ())), preferred_element_type=f32)
    lf = _log_sigmoid(f + bf_ref[...])
    r = lax.broadcasted_iota(jnp.int32, (tt, tt), 0)
    c = lax.broadcasted_iota(jnp.int32, (tt, tt), 1)
    cum = _dot_exact01(lf, (r <= c).astype(f32)) + carry[...]
    lf_ref[0] = lf
    cum_ref[0] = cum
    carry[...] = cum[:, tt - 1:tt]


FOX_SUM_ROWS = 16


def _fox_flash_body(qt_ref, k_ref, vt_ref, fk_ref, o_ref, m_scr, acc_scr):
    i = pl.program_id(2)
    tq = qt_ref.shape[3]
    tk = FOX_TK
    m_scr[...] = jnp.full_like(m_scr, NEG)
    acc_scr[...] = jnp.zeros_like(acc_scr)
    ones = jnp.ones((FOX_SUM_ROWS, tk), bf16)
    qt_all = jnp.concatenate([qt_ref[0, g] for g in range(GQ)], axis=-1)

    def block(j, masked):
        c0 = pl.multiple_of(j * tk, tk)
        k = k_ref[0, 0, pl.ds(c0, tk), :]
        vt = jnp.concatenate([vt_ref[0, 0, :, pl.ds(c0, tk)], ones], axis=0)
        fk = fk_ref[0, 0, pl.ds(c0, tk), :]
        if masked:
            s_pos = c0 + lax.broadcasted_iota(jnp.int32, (tk, tq), 0)
            t_pos = i * tq + lax.broadcasted_iota(jnp.int32, (tk, tq), 1)
            visible = s_pos <= t_pos
        s_all = jnp.dot(k, qt_all, preferred_element_type=f32)
        for g in range(GQ):
            s = s_all[:, g * tq:(g + 1) * tq] - fk[:, g:g + 1]
            if masked:
                s = jnp.where(visible, s, NEG)
            m_old = m_scr[g]
            m_new = jnp.maximum(m_old, jnp.max(s, 0, keepdims=True))
            alpha = jnp.exp(m_old - m_new)
            p = jnp.exp(s - m_new)
            acc_scr[g] = alpha * acc_scr[g] + jnp.dot(vt, p.astype(bf16), preferred_element_type=f32)
            m_scr[g] = m_new

    n_full = (i * tq) // tk

    def full_block(j, carry):
        block(j, False)
        return carry

    lax.fori_loop(0, n_full, full_block, 0)
    for jj in range(pl.cdiv(tq, tk)):
        block(n_full + jj, True)
    for g in range(GQ):
        o_ref[0, g] = acc_scr[g, :HD] / acc_scr[g, HD:HD + 1]


def _fox_prompt(x, w_in, b_f):
    b, l, d = x.shape
    nq = HQ * HD
    nk = HKV * HD
    qt, k, v, vt = _attn_proj(x.reshape(b * l, d), w_in, b, l, min(PROJ_TM, l))
    tt = min(FOX_PREP_TILE, l)
    wft = w_in[:, nq + 2 * nk:].T.astype(bf16)
    row = pl.BlockSpec((1, HQ, tt), lambda bi, i: (bi, 0, i))
    lft, cumt = pl.pallas_call(
        _fox_prep_body,
        grid=(b, l // tt),
        in_specs=[pl.BlockSpec((1, tt, d), lambda bi, i: (bi, i, 0)),
                  pl.BlockSpec((HQ, d), lambda bi, i: (0, 0)), pl.BlockSpec((HQ, 1), lambda bi, i: (0, 0))],
        out_specs=[row, row],
        out_shape=[jax.ShapeDtypeStruct((b, HQ, l), f32)] * 2,
        scratch_shapes=[pltpu.VMEM((HQ, 1), f32)],
        compiler_params=_cp("parallel", "arbitrary"),
        name="fox_prep",
    )(x, wft, b_f.reshape(HQ, 1))
    fk = jnp.swapaxes(cumt.reshape(b, HKV, GQ, l), 2, 3)
    qt = qt.reshape(b, HQ, HD, l)
    kh = jnp.swapaxes(k.reshape(b, l, HKV, HD), 1, 2).astype(bf16)
    vt = vt.reshape(b, HKV, HD, l)
    tq = min(FOX_TQ, l)
    qblk = pl.BlockSpec((1, GQ, HD, tq), lambda bi, h, i: (bi, h, 0, i))
    ot = pl.pallas_call(
        _fox_flash_body,
        grid=(b, HKV, l // tq),
        in_specs=[qblk,
                  pl.BlockSpec((1, 1, l, HD), lambda bi, h, i: (bi, h, 0, 0)),
                  pl.BlockSpec((1, 1, HD, l), lambda bi, h, i: (bi, h, 0, 0)),
                  pl.BlockSpec((1, 1, l, GQ), lambda bi, h, i: (bi, h, 0, 0))],
        out_specs=qblk,
        out_shape=jax.ShapeDtypeStruct((b, HQ, HD, l), f32),
        scratch_shapes=[pltpu.VMEM((GQ, 1, tq), f32), pltpu.VMEM((GQ, HD + FOX_SUM_ROWS, tq), f32)],
        compiler_params=_cp("parallel", "parallel", "arbitrary"),
        name="fox_flash",
    )(qt, kh, vt, fk)
    return ot.reshape(b, nq, l), k.reshape(b, l, nk), v.reshape(b, l, nk), jnp.swapaxes(lft, 1, 2)


FOX_PAGES = 16
FOX_ROWS = 1


def _fox_step_body(pt_ref, q_ref, kn_ref, vn_ref, lfn_ref, *refs):
    npg = FOX_PAGES
    n_in = 3 * npg * FOX_ROWS
    o_ref, m_scr, l_scr, acc_scr, f_scr = refs[n_in:]
    j = pl.program_id(1)
    ps = PAGE_SIZE
    mask = _head_mask()

    @pl.when(j == 0)
    def _():
        m_scr[...] = jnp.full_like(m_scr, NEG)
        l_scr[...] = jnp.zeros_like(l_scr)
        acc_scr[...] = jnp.zeros_like(acc_scr)
        f_scr[...] = jnp.zeros_like(f_scr)

    r = lax.broadcasted_iota(jnp.int32, (ps, ps), 0)
    c = lax.broadcasted_iota(jnp.int32, (ps, ps), 1)
    upper = (r <= c).astype(f32)

    for row in range(FOX_ROWS):
        base = 3 * npg * row
        k_refs = refs[base:base + npg]
        v_refs = refs[base + npg:base + 2 * npg]
        lf_refs = refs[base + 2 * npg:base + 3 * npg]
        qm = jnp.where(mask, jnp.concatenate([q_ref[row]] * HKV, axis=-1), 0.0) * (HD ** -0.5)

        def update(s, weighted_values, f_end, row=row):
            m_old = m_scr[row]
            m_new = jnp.maximum(m_old, jnp.max(s, -1, keepdims=True))
            alpha = jnp.exp(m_old - m_new)
            p = jnp.exp(s - m_new)
            l_scr[row] = alpha * l_scr[row] + jnp.sum(p, -1, keepdims=True)
            acc_scr[row] = alpha * acc_scr[row] + weighted_values(p)
            m_scr[row] = m_new
            f_scr[row] = f_end

        cum_all = _dot_exact01(jnp.concatenate([lf_refs[pg][0] for pg in range(npg)], axis=0), upper)
        f_run = f_scr[row]
        cums = []
        for pg in range(npg):
            cums.append(cum_all[pg * HQ:(pg + 1) * HQ] + f_run)
            f_run = cums[-1][:, ps - 1:ps]
        kt = jnp.concatenate([k_refs[pg][0].reshape(HKV * HD, ps).astype(bf16) for pg in range(npg)], axis=-1)
        vt = jnp.concatenate([v_refs[pg][0].reshape(HKV * HD, ps).astype(bf16) for pg in range(npg)], axis=-1)
        s = jnp.dot(qm.astype(bf16), kt, preferred_element_type=f32) - jnp.concatenate(cums, axis=-1)
        update(s, lambda p, vt=vt: _dot_nt(p, vt), f_run)

        @pl.when(j == pl.num_programs(1) - 1)
        def _(row=row, qm=qm, update=update):
            kn = kn_ref[row]
            vn = vn_ref[row]
            cum = f_scr[row] + lfn_ref[row]
            s = jnp.sum(qm * kn, -1, keepdims=True) - cum
            update(s, lambda p: p * vn, cum)
            o_ref[row] = _fold_heads(acc_scr[row] / l_scr[row], mask)


def _logf_body(f_ref, b_ref, o_ref):
    o_ref[...] = _log_sigmoid(f_ref[...] + b_ref[...])


def _fox_sample(x, cache_k, cache_v, cache_logf, page_table, w_in, b_f):
    nb = x.shape[0]
    nq = HQ * HD
    nk = HKV * HD
    n_pool = cache_k.shape[0]
    n_pages = page_table.shape[1]
    npg = FOX_PAGES
    q, k, v, f = _proj(x, [w_in[:, :nq].astype(bf16), w_in[:, nq:nq + nk].astype(bf16),
                           w_in[:, nq + nk:nq + 2 * nk].astype(bf16), w_in[:, nq + 2 * nk:].astype(bf16)], nb)
    lf_new = pl.pallas_call(_logf_body, out_shape=jax.ShapeDtypeStruct((nb, HQ), f32))(f, b_f.reshape(1, HQ))

    nr = FOX_ROWS

    def page(row, pg, *shape):
        return pl.BlockSpec((1,) + shape,
                            lambda bi, j, pt: (pt[(bi * nr + row) * n_pages + j * npg + pg],) + (0,) * len(shape))

    tok = lambda *shape: pl.BlockSpec((nr,) + shape, lambda bi, j, pt: (bi,) + (0,) * len(shape))
    ck = jnp.transpose(cache_k, (0, 2, 3, 1))
    cv = jnp.transpose(cache_v, (0, 2, 3, 1))
    clf = jnp.transpose(cache_logf, (0, 2, 1))
    page_specs, page_args = [], []
    for row in range(nr):
        for arr, shape in ((ck, (HKV, HD, PAGE_SIZE)), (cv, (HKV, HD, PAGE_SIZE)), (clf, (HQ, PAGE_SIZE))):
            page_specs += [page(row, pg, *shape) for pg in range(npg)]
            page_args += [arr] * npg
    o = pl.pallas_call(
        _fox_step_body,
        grid_spec=pltpu.PrefetchScalarGridSpec(
            num_scalar_prefetch=1,
            grid=(nb // nr, n_pages // npg),
            in_specs=[tok(HQ, HD), tok(1, nk), tok(1, nk), tok(HQ, 1)] + page_specs,
            out_specs=tok(HQ, HD),
            scratch_shapes=[pltpu.VMEM((nr, HQ, 1), f32), pltpu.VMEM((nr, HQ, 1), f32),
                            pltpu.VMEM((nr, HQ, nk), f32), pltpu.VMEM((nr, HQ, 1), f32)]),
        out_shape=jax.ShapeDtypeStruct((nb, HQ, HD), f32),
        compiler_params=_cp("parallel", "arbitrary"),
        name="fox_decode",
    )(page_table.reshape(-1), q.reshape(nb, HQ, HD), k.reshape(nb, 1, nk), v.reshape(nb, 1, nk),
      lf_new.reshape(nb, HQ, 1), *page_args)
    return o.reshape(nb, nq), k, v, lf_new


ROUTER_TILE = 512
MOE_TM = 1024


def _router_body(x_ref, rwt_ref, rb_ref, comb_ref):
    scores = _sigmoid(_dot_nt(rwt_ref[...], x_ref[...]))
    sel = scores + rb_ref[...]
    rows = [sel[e:e + 1, :] for e in range(N_EXPERTS)]
    srow = [scores[e:e + 1, :] for e in range(N_EXPERTS)]
    gs = []
    for g in range(N_GROUPS):
        r = rows[g * EPG:(g + 1) * EPG]
        best = None
        for a in range(EPG):
            for b in range(a + 1, EPG):
                pair = r[a] + r[b]
                best = pair if best is None else jnp.maximum(best, pair)
        gs.append(best)
    g_best = gs[0]
    g_idx = jnp.zeros_like(gs[0], dtype=jnp.int32)
    for g in range(1, N_GROUPS):
        better = gs[g] > g_best
        g_best = jnp.where(better, gs[g], g_best)
        g_idx = jnp.where(better, g, g_idx)

    def in_group(vals, j):
        out = vals[j]
        for g in range(1, N_GROUPS):
            out = jnp.where(g_idx == g, vals[g * EPG + j], out)
        return out

    ig = [in_group(rows, j) for j in range(EPG)]
    sg = [in_group(srow, j) for j in range(EPG)]
    v1, i1, s1 = ig[0], jnp.zeros_like(g_idx), sg[0]
    for j in range(1, EPG):
        better = ig[j] > v1
        v1 = jnp.where(better, ig[j], v1)
        i1 = jnp.where(better, j, i1)
        s1 = jnp.where(better, sg[j], s1)
    v2 = jnp.full_like(v1, -jnp.inf)
    i2 = jnp.zeros_like(g_idx)
    s2 = jnp.zeros_like(s1)
    for j in range(EPG):
        better = (i1 != j) & (ig[j] > v2)
        v2 = jnp.where(better, ig[j], v2)
        i2 = jnp.where(better, j, i2)
        s2 = jnp.where(better, sg[j], s2)
    tot = s1 + s2
    e_row = lax.broadcasted_iota(jnp.int32, scores.shape, 0)
    comb_ref[...] = (jnp.where(e_row == g_idx * EPG + i1, s1 / tot, 0.0)
                     + jnp.where(e_row == g_idx * EPG + i2, s2 / tot, 0.0))


def _router(x, router_w, router_b):
    t, d = x.shape
    tt = min(ROUTER_TILE, t)
    comb_t = pl.pallas_call(
        _router_body,
        grid=(t // tt,),
        in_specs=[pl.BlockSpec((tt, d), lambda i: (i, 0)), pl.BlockSpec((N_EXPERTS, d), lambda i: (0, 0)),
                  pl.BlockSpec((N_EXPERTS, 1), lambda i: (0, 0))],
        out_specs=pl.BlockSpec((N_EXPERTS, tt), lambda i: (0, i)),
        out_shape=jax.ShapeDtypeStruct((N_EXPERTS, t), f32),
        compiler_params=_cp("parallel"),
        name="router",
    )(x, router_w.T, router_b.reshape(N_EXPERTS, 1))
    return comb_t.T


def _moe_dense_body(x_ref, comb_ref, wg_ref, wu_ref, wd_ref, g_ref, b_ref, o_ref, acc_scr):
    e = pl.program_id(1)

    @pl.when(e == 0)
    def _():
        acc_scr[...] = jnp.zeros_like(acc_scr)

    x = x_ref[...].astype(bf16)
    h = _silu(jnp.dot(x, wg_ref[0].astype(bf16), preferred_element_type=f32)) \
        * jnp.dot(x, wu_ref[0].astype(bf16), preferred_element_type=f32)
    y = jnp.dot(h.astype(bf16), wd_ref[0].astype(bf16), preferred_element_type=f32)
    lane = lax.broadcasted_iota(jnp.int32, comb_ref.shape, 1)
    c = jnp.sum(jnp.where(lane == e, comb_ref[...], 0.0), -1, keepdims=True)
    acc_scr[...] += c * y

    @pl.when(e == pl.num_programs(1) - 1)
    def _():
        o_ref[...] = _layer_norm(DN_ALPHA * x_ref[...] + acc_scr[...], g_ref[...], b_ref[...])


def _moe_ln(x, router_w, router_b, w_gate, w_up, w_down, layer, g, b):
    t, d = x.shape
    comb = _router(x, router_w, router_b)
    tm = min(MOE_TM, t)
    return pl.pallas_call(
        _moe_dense_body,
        grid=(t // tm, N_EXPERTS),
        in_specs=[pl.BlockSpec((tm, d), lambda i, e: (i, 0)), pl.BlockSpec((tm, N_EXPERTS), lambda i, e: (i, 0)),
                  pl.BlockSpec((None, 1, d, D_EXPERT), lambda i, e: (layer, e, 0, 0)),
                  pl.BlockSpec((None, 1, d, D_EXPERT), lambda i, e: (layer, e, 0, 0)),
                  pl.BlockSpec((None, 1, D_EXPERT, d), lambda i, e: (layer, e, 0, 0)),
                  pl.BlockSpec((1, d), lambda i, e: (0, 0)), pl.BlockSpec((1, d), lambda i, e: (0, 0))],
        out_specs=pl.BlockSpec((tm, d), lambda i, e: (i, 0)),
        out_shape=jax.ShapeDtypeStruct((t, d), f32),
        scratch_shapes=[pltpu.VMEM((tm, d), f32)],
        compiler_params=_cp("parallel", "arbitrary"),
        name="moe_dense",
    )(x, comb, w_gate, w_up, w_down, g.reshape(1, d), b.reshape(1, d))


def kernel(x_prompt, x_sample, state_a_ssm, state_a_conv, state_b_h, state_b_conv, cache_c_k, cache_c_v, cache_d_k, cache_d_v, cache_d_logf, page_table, ln_g, ln_b, a_w_in, a_conv_w, a_a_log, a_dt_bias, a_norm_w, a_w_out, b_w_in, b_conv_w, b_conv_b, b_w_a, b_b_a, b_w_x, b_b_x, b_lambda, b_w_out, c_w_in, c_sinks, c_w_out, rel_bias, d_w_in, d_b_f, d_w_out, router_w, router_b, moe_w_gate, moe_w_up, moe_w_down):
    bp, lp, d = x_prompt.shape
    nb = x_sample.shape[0]
    xp = x_prompt.reshape(bp * lp, d)
    xs = x_sample.reshape(nb, d)
    tm_p = PROJ_TM

    def finish(xp, xs, mp, ms, w_out, i, transposed=False):
        w = w_out.astype(bf16)
        xp = _proj_ln(mp, w, xp, ln_g[i, 0], ln_b[i, 0], tm_p, transposed)
        xs = _proj_ln(ms, w, xs, ln_g[i, 0], ln_b[i, 0], nb)
        moe_args = (router_w, router_b, moe_w_gate, moe_w_up, moe_w_down, i, ln_g[i, 1], ln_b[i, 1])
        return _moe_ln(xp, *moe_args), _moe_ln(xs, *moe_args)

    a_args = (a_w_in[0], a_conv_w[0], a_a_log[0], a_dt_bias[0], a_norm_w[0])
    mp, p_a_ssm, p_a_conv = _gdn_prompt(xp.reshape(bp, lp, d), *a_args)
    ms, s_a_ssm, s_a_conv = _gdn_sample(xs, state_a_conv[0], state_a_ssm[0], *a_args)
    xp, xs = finish(xp, xs, mp, ms, a_w_out[0], 0)

    b_args = (b_w_in[0], b_conv_w[0], b_conv_b[0], b_w_a[0], b_b_a[0], b_w_x[0], b_b_x[0], b_lambda[0])
    mp, p_b_h, p_b_conv = _lru_prompt(xp.reshape(bp, lp, d), *b_args)
    ms, s_b_h, s_b_conv = _lru_sample(xs, state_b_conv[0], state_b_h[0], *b_args)
    xp, xs = finish(xp, xs, mp, ms, b_w_out[0], 1)

    mp, kp, vp = _swa_prompt(xp.reshape(bp, lp, d), c_w_in[0], c_sinks[0], rel_bias)
    ms, _, _, s_c_k, s_c_v = _swa_sample(xs, cache_c_k[0], cache_c_v[0], c_w_in[0], c_sinks[0], rel_bias)
    p_c_k = kp[:, lp - WINDOW:].reshape(bp, WINDOW, HKV, HD)
    p_c_v = vp[:, lp - WINDOW:].reshape(bp, WINDOW, HKV, HD)
    xp, xs = finish(xp, xs, mp, ms, c_w_out[0], 2, transposed=True)

    mp, kp, vp, p_d_logf = _fox_prompt(xp.reshape(bp, lp, d), d_w_in[0], d_b_f[0])
    ms, ks, vs, lfs = _fox_sample(xs, cache_d_k[0], cache_d_v[0], cache_d_logf[0], page_table, d_w_in[0], d_b_f[0])
    xp, xs = finish(xp, xs, mp, ms, d_w_out[0], 3, transposed=True)

    return (xp.reshape(bp, lp, d), xs.reshape(nb, 1, d),
            p_a_ssm[None], p_a_conv[None], p_b_h[None], p_b_conv[None], p_c_k[None], p_c_v[None],
            kp.reshape(1, bp, lp, HKV, HD), vp.reshape(1, bp, lp, HKV, HD), p_d_logf[None],
            s_a_ssm[None], s_a_conv[None], s_b_h[None], s_b_conv[None], s_c_k[None], s_c_v[None],
            ks.reshape(1, nb, 1, HKV, HD), vs.reshape(1, nb, 1, HKV, HD), lfs.reshape(1, nb, 1, HQ))
```

```python
import functools
import math

import jax
import jax.numpy as jnp
import numpy as np
from jax import lax
from jax.experimental import pallas as pl
from jax.experimental.pallas import tpu as pltpu

f32 = jnp.float32
bf16 = jnp.bfloat16
HI = lax.Precision.HIGHEST

D_MODEL = 1024
DEPTH = 4
DN_ALPHA = (2 * DEPTH) ** 0.25
LN_EPS = 1e-5
CONV = 4
HA = 8
DKA = 128
DVA = 128
QKV_A = HA * (2 * DKA + DVA)
CHUNK_A = 64
LRU_W = 1024
LRU_BLOCKS = 8
LRU_BW = 128
LRU_C = 8.0
HD = 64
HQ = 16
HKV = 4
GQ = HQ // HKV
WINDOW = 128
N_BUCKETS = 32
MAX_DIST = 128
PAGE_SIZE = 128
N_EXPERTS = 16
N_GROUPS = 4
EPG = 4
D_EXPERT = 512

VMEM_LIMIT = 56 * 1024 * 1024
PROJ_TM = 512


def _cp(*sem):
    return pltpu.CompilerParams(dimension_semantics=sem, vmem_limit_bytes=VMEM_LIMIT)


def _dot(a, b):
    return jnp.dot(a.astype(bf16), b.astype(bf16), preferred_element_type=f32)


def _dot_nt(a, b):
    return lax.dot_general(a.astype(bf16), b.astype(bf16), (((1,), (1,)), ((), ())), preferred_element_type=f32)


def _dot_tn(a, b):
    return lax.dot_general(a.astype(bf16), b.astype(bf16), (((0,), (0,)), ((), ())), preferred_element_type=f32)


def _dot_hi(a, b):
    return jnp.dot(a, b, precision=HI, preferred_element_type=f32)


def _split(a):
    hi = a.astype(bf16)
    return hi, (a - hi.astype(f32)).astype(bf16)


def _dot_split(a, b):
    a_hi, a_lo = _split(a)
    b_hi, b_lo = _split(b)
    d = lambda u, v: jnp.dot(u, v, preferred_element_type=f32)
    return d(a_hi, b_hi) + (d(a_hi, b_lo) + d(a_lo, b_hi))


def _dot_exact01(a, b01):
    b = b01.astype(bf16)
    out = None
    for _ in range(3):
        piece = a.astype(bf16)
        a = a - piece.astype(f32)
        term = jnp.dot(piece, b, preferred_element_type=f32)
        out = term if out is None else out + term
    return out


def _bdot_split(a, b):
    a_hi, a_lo = _split(a)
    b_hi, b_lo = _split(b)
    d = lambda u, v: jnp.einsum('hij,hjk->hik', u, v, preferred_element_type=f32)
    return d(a_hi, b_hi) + (d(a_hi, b_lo) + d(a_lo, b_hi))


def _bdot(a, b):
    return jnp.einsum('hij,hjk->hik', a.astype(bf16), b.astype(bf16), preferred_element_type=f32)


def _bdot_nt(a, b):
    return jnp.einsum('hid,hjd->hij', a.astype(bf16), b.astype(bf16), preferred_element_type=f32)


def _dot_nt_hi(a, b):
    return lax.dot_general(a, b, (((1,), (1,)), ((), ())), precision=HI, preferred_element_type=f32)


def _sigmoid(x):
    return 1.0 / (1.0 + jnp.exp(-x))


def _silu(x):
    return x * _sigmoid(x)


def _softplus(x):
    return jnp.maximum(x, 0.0) + jnp.log1p(jnp.exp(-jnp.abs(x)))


def _log_sigmoid(x):
    return -_softplus(-x)


def _eye(n):
    r = lax.broadcasted_iota(jnp.int32, (n, n), 0)
    c = lax.broadcasted_iota(jnp.int32, (n, n), 1)
    return (r == c).astype(f32)


def _proj_body(n_out, x_ref, *refs):
    x = x_ref[...].astype(bf16)
    for w_ref, o_ref in zip(refs[:n_out], refs[n_out:]):
        o_ref[...] = jnp.dot(x, w_ref[...], preferred_element_type=f32)


def _proj(x, ws, tm):
    m, k = x.shape
    n_out = len(ws)
    return pl.pallas_call(
        functools.partial(_proj_body, n_out),
        grid=(m // tm,),
        in_specs=[pl.BlockSpec((tm, k), lambda i: (i, 0))]
        + [pl.BlockSpec(w.shape, lambda i: (0, 0)) for w in ws],
        out_specs=[pl.BlockSpec((tm, w.shape[1]), lambda i: (i, 0)) for w in ws],
        out_shape=[jax.ShapeDtypeStruct((m, w.shape[1]), f32) for w in ws],
        compiler_params=_cp("parallel"),
        name="proj",
    )(x, *ws)


def _attn_proj_body(x_ref, wqt_ref, wk_ref, wv_ref, wvt_ref, qt_ref, k_ref, v_ref, vt_ref):
    x = x_ref[...].astype(bf16)
    nt = lambda w_ref: lax.dot_general(w_ref[...], x, (((1,), (1,)), ((), ())), preferred_element_type=f32)
    qt_ref[0] = (nt(wqt_ref) * (HD ** -0.5)).astype(bf16)
    vt_ref[0] = nt(wvt_ref).astype(bf16)
    k_ref[...] = jnp.dot(x, wk_ref[...], preferred_element_type=f32)
    v_ref[...] = jnp.dot(x, wv_ref[...], preferred_element_type=f32)


def _attn_proj(x, w_in, b, l, tm):
    m, d = x.shape
    nq = HQ * HD
    nk = HKV * HD
    wqt = w_in[:, :nq].T.astype(bf16)
    wk = w_in[:, nq:nq + nk].astype(bf16)
    wv = w_in[:, nq + nk:nq + 2 * nk].astype(bf16)
    nt = l // tm
    full = lambda a: pl.BlockSpec(a.shape, lambda i: (0, 0))
    rows = lambda n: pl.BlockSpec((tm, n), lambda i: (i, 0))
    cols = lambda n: pl.BlockSpec((1, n, tm), lambda i: (i // nt, 0, i % nt))
    return pl.pallas_call(
        _attn_proj_body,
        grid=(m // tm,),
        in_specs=[rows(d), full(wqt), full(wk), full(wv), full(wv.T)],
        out_specs=[cols(nq), rows(nk), rows(nk), cols(nk)],
        out_shape=[jax.ShapeDtypeStruct((b, nq, l), bf16), jax.ShapeDtypeStruct((m, nk), f32),
                   jax.ShapeDtypeStruct((m, nk), f32), jax.ShapeDtypeStruct((b, nk, l), bf16)],
        compiler_params=_cp("parallel"),
        name="attn_proj",
    )(x, wqt, wk, wv, wv.T)


def _layer_norm(z, g, b):
    mu = jnp.mean(z, -1, keepdims=True)
    zc = z - mu
    var = jnp.mean(zc * zc, -1, keepdims=True)
    return zc * lax.rsqrt(var + LN_EPS) * g + b


def _proj_ln_body(transposed, h_ref, w_ref, x_ref, g_ref, b_ref, rwt_ref, rb_ref, o_ref, comb_ref):
    if transposed:
        m = lax.dot_general(h_ref[0].astype(bf16), w_ref[...], (((0,), (0,)), ((), ())),
                            preferred_element_type=f32)
    else:
        m = jnp.dot(h_ref[...].astype(bf16), w_ref[...], preferred_element_type=f32)
    o = _layer_norm(DN_ALPHA * x_ref[...] + m, g_ref[...], b_ref[...])
    o_ref[...] = o
    comb_ref[...] = _route(o, rwt_ref[...], rb_ref[...])


def _proj_ln(h, w, x, g, b, tm, router_w, router_b, transposed=False):
    d = x.shape[1]
    if transposed:
        _, k, l = h.shape
        m = x.shape[0]
        nt = l // tm
        h_spec = pl.BlockSpec((1, k, tm), lambda i: (i // nt, 0, i % nt))
    else:
        m, k = h.shape
        h_spec = pl.BlockSpec((tm, k), lambda i: (i, 0))
    out, comb_t = pl.pallas_call(
        functools.partial(_proj_ln_body, transposed),
        grid=(m // tm,),
        in_specs=[h_spec,
                  pl.BlockSpec((k, d), lambda i: (0, 0)),
                  pl.BlockSpec((tm, d), lambda i: (i, 0)),
                  pl.BlockSpec((1, d), lambda i: (0, 0)),
                  pl.BlockSpec((1, d), lambda i: (0, 0)),
                  pl.BlockSpec((N_EXPERTS, d), lambda i: (0, 0)),
                  pl.BlockSpec((N_EXPERTS, 1), lambda i: (0, 0))],
        out_specs=[pl.BlockSpec((tm, d), lambda i: (i, 0)), pl.BlockSpec((N_EXPERTS, tm), lambda i: (0, i))],
        out_shape=[jax.ShapeDtypeStruct((m, d), f32), jax.ShapeDtypeStruct((N_EXPERTS, m), f32)],
        compiler_params=_cp("parallel"),
        name="proj_ln",
    )(h, w, x, g.reshape(1, d), b.reshape(1, d), router_w.T, router_b.reshape(N_EXPERTS, 1))
    return out, comb_t.T


GDN_TILE = 256


def _gdn_gates(ab, alog, dtb):
    g = -jnp.exp(alog) * _softplus(ab[:, 0:HA] + dtb)
    beta = _sigmoid(ab[:, HA:2 * HA])
    return g, beta


def _l2norm(t):
    return t * lax.rsqrt(jnp.sum(t * t, -1, keepdims=True) + 1e-6)


def _gdn_prep_body(qkv_ref, halo_ref, ab_ref, cw_ref, alog_ref, dtb_ref,
                   u_ref, w_ref, qd_ref, kd_ref, qk_ref, eg_ref, xp_scr, y_scr):
    i = pl.program_id(1)
    tc = qkv_ref.shape[1]
    c = CHUNK_A
    xp_scr[0:8, :] = jnp.where(i > 0, halo_ref[0], 0.0)
    xp_scr[8:8 + tc, :] = qkv_ref[0]
    cw = cw_ref[...]
    lanes = 128
    for cb in range(QKV_A // lanes):
        cols = slice(cb * lanes, (cb + 1) * lanes)
        y = xp_scr[5:5 + tc, cols] * cw[0:1, cols]
        for j in range(1, CONV):
            y = y + xp_scr[5 + j:5 + j + tc, cols] * cw[j:j + 1, cols]
        y_scr[:, cols] = _silu(y)

    row = lax.broadcasted_iota(jnp.int32, (c, c), 0)
    col = lax.broadcasted_iota(jnp.int32, (c, c), 1)
    incl = row >= col
    strict = row > col
    tri = incl.astype(f32)
    eye_c = (row == col).astype(f32)
    eye_h = _eye(HA)
    alog = alog_ref[...]
    dtb = dtb_ref[...]

    def chunk(n, carry):
        r0 = pl.multiple_of(n * c, c)
        rows = pl.ds(r0, c)
        g, beta = _gdn_gates(ab_ref[0, rows, :], alog, dtb)
        gc = _dot_hi(tri, g)
        gct = _dot_nt_hi(eye_h, gc)
        egc = jnp.exp(gc)
        eg_ref[0, rows, :] = egc
        heads = lambda t, off: jnp.stack([t(y_scr[rows, off + h * DKA:off + (h + 1) * DKA]) for h in range(HA)])
        q = heads(_l2norm, 0) * (DKA ** -0.5)
        k = heads(_l2norm, HA * DKA)
        v = heads(lambda t: t, 2 * HA * DKA)
        col = lambda t: jnp.stack([t[:, h:h + 1] for h in range(HA)])
        gcol, bcol, egcol = col(gc), col(beta), col(egc)
        diff = gcol - gct[:, None, :]
        dec = jnp.where(incl, jnp.exp(jnp.where(incl, diff, 0.0)), 0.0)
        a = jnp.where(strict, _bdot_nt(k, k) * dec, 0.0) * bcol
        x = eye_c - a
        p = _bdot_split(a, a)
        for level in range(4):
            prod = _bdot_split if level == 0 else _bdot
            xp = prod(jnp.concatenate([x, p], axis=1), p)
            x = x + xp[:, :c]
            p = xp[:, c:]
        x = x + _bdot(x, p)
        sol = _bdot_split(x, jnp.concatenate([v * bcol, k * (bcol * egcol)], axis=-1))
        qd = q * egcol
        kd = k * jnp.exp(jnp.stack([gc[c - 1:c, h:h + 1] for h in range(HA)]) - gcol)
        qk = _bdot_nt(q, k) * dec
        for h in range(HA):
            hs = slice(h * DVA, (h + 1) * DVA)
            u_ref[0, rows, hs] = sol[h, :, :DVA]
            w_ref[0, rows, hs] = sol[h, :, DVA:]
            qd_ref[0, rows, hs] = qd[h]
            kd_ref[0, rows, hs] = kd[h]
            qk_ref[0, rows, h * c:(h + 1) * c] = qk[h]
        return carry

    lax.fori_loop(0, tc // c, chunk, 0)


def _gdn_prep(qkv, ab, conv_w, a_log, dt_bias):
    b, l, _ = qkv.shape
    tc = min(GDN_TILE, l)
    hb = tc // 8
    big = lambda n: pl.BlockSpec((1, tc, n), lambda bi, i: (bi, i, 0))
    full = lambda a: pl.BlockSpec(a.shape, lambda bi, i: (0, 0))
    wide = jax.ShapeDtypeStruct((b, l, HA * DVA), f32)
    return pl.pallas_call(
        _gdn_prep_body,
        grid=(b, l // tc),
        in_specs=[big(QKV_A),
                  pl.BlockSpec((1, 8, QKV_A), lambda bi, i: (bi, jnp.maximum(i * hb - 1, 0), 0)),
                  big(2 * HA), full(conv_w), full(a_log), full(dt_bias)],
        out_specs=[big(HA * DVA)] * 4 + [big(HA * CHUNK_A), big(HA)],
        out_shape=[wide] * 4 + [jax.ShapeDtypeStruct((b, l, HA * CHUNK_A), f32),
                                jax.ShapeDtypeStruct((b, l, HA), f32)],
        scratch_shapes=[pltpu.VMEM((tc + 8, QKV_A), f32), pltpu.VMEM((tc, QKV_A), f32)],
        compiler_params=_cp("parallel", "parallel"),
        name="gdn_prep",
    )(qkv, qkv, ab, conv_w, a_log, dt_bias)


def _gdn_scan_body(u_ref, w_ref, qd_ref, kd_ref, qk_ref, eg_ref, gate_ref, nw_ref, y_ref, s_ref):
    i = pl.program_id(1)
    tc = u_ref.shape[1]
    c = CHUNK_A

    @pl.when(i == 0)
    def _():
        s_ref[...] = jnp.zeros_like(s_ref)

    nw = nw_ref[...]

    def chunk(n, carry):
        r0 = pl.multiple_of(n * c, c)
        rows = pl.ds(r0, c)
        eg_last = eg_ref[0, pl.ds(r0 + c - 1, 1), :]
        heads = lambda ref, width: jnp.stack([ref[0, rows, h * width:(h + 1) * width] for h in range(HA)])
        bd = lambda a, b: jnp.einsum('hij,hjk->hik', a.astype(bf16), b.astype(bf16), preferred_element_type=f32)
        s = s_ref[0]
        ws_qs = bd(jnp.concatenate([heads(w_ref, DVA), heads(qd_ref, DVA)], axis=1), s)
        v_new = heads(u_ref, DVA) - ws_qs[:, :c]
        o = ws_qs[:, c:] + bd(heads(qk_ref, c), v_new)
        decay = jnp.stack([eg_last[:, h:h + 1] for h in range(HA)])
        s_ref[0] = s * decay + jnp.einsum('hcd,hce->hde', heads(kd_ref, DVA).astype(bf16), v_new.astype(bf16),
                                          preferred_element_type=f32)
        of = o * lax.rsqrt(jnp.mean(o * o, -1, keepdims=True) + 1e-6)
        for h in range(HA):
            hs = slice(h * DVA, (h + 1) * DVA)
            y_ref[0, rows, hs] = of[h] * nw * _silu(gate_ref[0, rows, hs])
        return carry

    lax.fori_loop(0, tc // c, chunk, 0)


def _gdn_scan(u, w, qd, kd, qk, eg, gate, norm_w):
    b, l, _ = u.shape
    tc = min(GDN_TILE, l)
    big = lambda n: pl.BlockSpec((1, tc, n), lambda bi, i: (bi, i, 0))
    return pl.pallas_call(
        _gdn_scan_body,
        grid=(b, l // tc),
        in_specs=[big(HA * DVA)] * 4 + [big(HA * CHUNK_A), big(HA), big(HA * DVA),
                                       pl.BlockSpec((1, DVA), lambda bi, i: (0, 0))],
        out_specs=[big(HA * DVA), pl.BlockSpec((1, HA, DKA, DVA), lambda bi, i: (bi, 0, 0, 0))],
        out_shape=[jax.ShapeDtypeStruct((b, l, HA * DVA), f32),
                   jax.ShapeDtypeStruct((b, HA, DKA, DVA), f32)],
        compiler_params=_cp("parallel", "arbitrary"),
        name="gdn_scan",
    )(u, w, qd, kd, qk, eg, gate, norm_w.reshape(1, DVA))


def _gdn_prompt(x, w_in, conv_w, a_log, dt_bias, norm_w):
    b, l, d = x.shape
    wq = w_in[:, :QKV_A].astype(bf16)
    wg = w_in[:, QKV_A:QKV_A + HA * DVA].astype(bf16)
    wab = w_in[:, QKV_A + HA * DVA:].astype(bf16)
    qkv, gate, ab = _proj(x.reshape(b * l, d), [wq, wg, wab], min(PROJ_TM, l))
    qkv = qkv.reshape(b, l, QKV_A)
    u, w, qd, kd, qk, eg = _gdn_prep(qkv, ab.reshape(b, l, 2 * HA), conv_w,
                                     a_log.reshape(1, HA), dt_bias.reshape(1, HA))
    y, s = _gdn_scan(u, w, qd, kd, qk, eg, gate.reshape(b, l, HA * DVA), norm_w)
    return y.reshape(b * l, HA * DVA), s, qkv[:, l - (CONV - 1):]


GDN_BT = 8


def _gdn_step_body(qkv_ref, gate_ref, ab_ref, buf_ref, s_ref, cw_ref, alog_ref, dtb_ref, nw_ref,
                   y_ref, s_out_ref, buf_out_ref, q_scr, k_scr, v_scr, eg_scr, beta_scr):
    n = QKV_A
    new = qkv_ref[...]
    cw = cw_ref[...]
    y = new * cw[CONV - 1:CONV, :]
    for j in range(CONV - 1):
        y = y + buf_ref[:, j * n:(j + 1) * n] * cw[j:j + 1, :]
    y = _silu(y)
    buf_out_ref[:, 0:2 * n] = buf_ref[:, n:3 * n]
    buf_out_ref[:, 2 * n:3 * n] = new
    for h in range(HA):
        q_scr[:, h * DKA:(h + 1) * DKA] = _l2norm(y[:, h * DKA:(h + 1) * DKA]) * (DKA ** -0.5)
        k_scr[:, h * DKA:(h + 1) * DKA] = _l2norm(y[:, HA * DKA + h * DKA:HA * DKA + (h + 1) * DKA])
    v_scr[...] = y[:, 2 * HA * DKA:]
    g, beta = _gdn_gates(ab_ref[...], alog_ref[...], dtb_ref[...])
    eg_scr[...] = jnp.exp(g)
    beta_scr[...] = beta
    eye = _eye(DKA)
    row8 = lax.broadcasted_iota(jnp.int32, (8, DKA), 0)
    nw = nw_ref[...]

    for bi in range(qkv_ref.shape[0]):
        r = slice(bi, bi + 1)
        for h in range(HA):
            hs = slice(h * DKA, (h + 1) * DKA)
            k_row = k_scr[r, hs]
            q_row = q_scr[r, hs]
            kq = jnp.where(row8 == 0, k_row, jnp.where(row8 == 1, q_row, 0.0))
            cols = _dot_nt_hi(eye, kq)
            k_col = cols[:, 0:1]
            q_col = cols[:, 1:2]
            sd = s_ref[bi, h] * eg_scr[r, h:h + 1]
            pred = jnp.sum(k_col * sd, axis=0, keepdims=True)
            delta = beta_scr[r, h:h + 1] * (v_scr[r, hs] - pred)
            s_new = sd + k_col * delta
            s_out_ref[bi, h] = s_new
            o = jnp.sum(q_col * s_new, axis=0, keepdims=True)
            of = o * lax.rsqrt(jnp.mean(o * o, -1, keepdims=True) + 1e-6)
            y_ref[r, hs] = of * nw * _silu(gate_ref[r, hs])


def _gdn_step(qkv, gate, ab, conv_buf, s0, conv_w, a_log, dt_bias, norm_w):
    nb = qkv.shape[0]
    bt = GDN_BT
    rowblk = lambda n: pl.BlockSpec((bt, n), lambda i: (i, 0))
    full = lambda a: pl.BlockSpec(a.shape, lambda i: (0, 0))
    sblk = pl.BlockSpec((bt, HA, DKA, DVA), lambda i: (i, 0, 0, 0))
    nw = norm_w.reshape(1, DVA)
    al = a_log.reshape(1, HA)
    db = dt_bias.reshape(1, HA)
    return pl.pallas_call(
        _gdn_step_body,
        grid=(nb // bt,),
        in_specs=[rowblk(QKV_A), rowblk(HA * DVA), rowblk(2 * HA), rowblk(3 * QKV_A), sblk,
                  full(conv_w), full(al), full(db), full(nw)],
        out_specs=[rowblk(HA * DVA), sblk, rowblk(3 * QKV_A)],
        out_shape=[jax.ShapeDtypeStruct((nb, HA * DVA), f32),
                   jax.ShapeDtypeStruct(s0.shape, f32),
                   jax.ShapeDtypeStruct((nb, 3 * QKV_A), f32)],
        scratch_shapes=[pltpu.VMEM((bt, HA * DKA), f32)] * 3 + [pltpu.VMEM((bt, HA), f32)] * 2,
        compiler_params=_cp("parallel"),
        name="gdn_step",
    )(qkv, gate, ab, conv_buf.reshape(nb, 3 * QKV_A), s0, conv_w, al, db, nw)


def _gdn_sample(x, conv_buf, s0, w_in, conv_w, a_log, dt_bias, norm_w):
    nb = x.shape[0]
    wq = w_in[:, :QKV_A].astype(bf16)
    wg = w_in[:, QKV_A:QKV_A + HA * DVA].astype(bf16)
    wab = w_in[:, QKV_A + HA * DVA:].astype(bf16)
    qkv, gate, ab = _proj(x, [wq, wg, wab], nb)
    y, s, buf = _gdn_step(qkv, gate, ab, conv_buf, s0, conv_w, a_log, dt_bias, norm_w)
    return y, s, buf.reshape(nb, CONV - 1, QKV_A)


LRU_TILE = 256


def _gelu(x):
    return 0.5 * x * (1.0 + jnp.tanh(math.sqrt(2.0 / math.pi) * (x + 0.044715 * x * x * x)))


def _lru_gates(u, wa_ref, ba, wx_ref, bx, lam):
    ra, xa = [], []
    for n in range(LRU_BLOCKS):
        ub = u[:, n * LRU_BW:(n + 1) * LRU_BW].astype(bf16)
        ra.append(jnp.dot(ub, wa_ref[n], preferred_element_type=f32))
        xa.append(jnp.dot(ub, wx_ref[n], preferred_element_type=f32))
    r = _sigmoid(jnp.concatenate(ra, axis=-1) + ba)
    i_g = _sigmoid(jnp.concatenate(xa, axis=-1) + bx)
    log_a = -LRU_C * r * _softplus(-lam)
    a = jnp.exp(log_a)
    t = jnp.tanh(log_a)
    z = -2.0 * t / (1.0 - t)
    b = jnp.where(z > 0.0, z * lax.rsqrt(z), 0.0) * (i_g * u)
    return a, b


def _lru_prompt_body(gate_ref, rec_ref, halo_ref, cw_ref, cb_ref, wa_ref, ba_ref, wx_ref, bx_ref, lam_ref,
                     y_ref, h_ref, xp_scr):
    i = pl.program_id(1)
    tt = rec_ref.shape[1]

    @pl.when(i == 0)
    def _():
        h_ref[...] = jnp.zeros_like(h_ref)

    xp_scr[0:8, :] = jnp.where(i > 0, halo_ref[0], 0.0)
    xp_scr[8:8 + tt, :] = rec_ref[0]
    cw = cw_ref[...]
    u = xp_scr[5:5 + tt, :] * cw[0:1, :]
    for j in range(1, CONV):
        u = u + xp_scr[5 + j:5 + j + tt, :] * cw[j:j + 1, :]
    u = u + cb_ref[...]
    a, b = _lru_gates(u, wa_ref, ba_ref[...], wx_ref, bx_ref[...], lam_ref[...])
    row = lax.broadcasted_iota(jnp.int32, a.shape, 0)
    s = 1
    while s < tt:
        keep = row >= s
        a_sh = jnp.where(keep, pltpu.roll(a, s, 0), 1.0)
        b_sh = jnp.where(keep, pltpu.roll(b, s, 0), 0.0)
        b = a * b_sh + b
        a = a * a_sh
        s *= 2
    h = b + a * h_ref[0]
    h_ref[0] = h[tt - 1:tt, :]
    y_ref[0] = _gelu(gate_ref[0]) * h


def _lru_prompt(x, w_in, conv_w, conv_b, w_a, b_a, w_x, b_x, lam):
    bsz, l, d = x.shape
    w = LRU_W
    gate_in, rec_in = _proj(x.reshape(bsz * l, d), [w_in[:, :w].astype(bf16), w_in[:, w:].astype(bf16)],
                            min(PROJ_TM, l))
    rec3 = rec_in.reshape(bsz, l, w)
    tt = min(LRU_TILE, l)
    hb = tt // 8
    big = pl.BlockSpec((1, tt, w), lambda bi, i: (bi, i, 0))
    vec = pl.BlockSpec((1, w), lambda bi, i: (0, 0))
    blkw = pl.BlockSpec((LRU_BLOCKS, LRU_BW, LRU_BW), lambda bi, i: (0, 0, 0))
    y, h = pl.pallas_call(
        _lru_prompt_body,
        grid=(bsz, l // tt),
        in_specs=[big, big, pl.BlockSpec((1, 8, w), lambda bi, i: (bi, jnp.maximum(i * hb - 1, 0), 0)),
                  pl.BlockSpec((CONV, w), lambda bi, i: (0, 0)), vec, blkw, vec, blkw, vec, vec],
        out_specs=[big, pl.BlockSpec((1, 1, w), lambda bi, i: (bi, 0, 0))],
        out_shape=[jax.ShapeDtypeStruct((bsz, l, w), f32), jax.ShapeDtypeStruct((bsz, 1, w), f32)],
        scratch_shapes=[pltpu.VMEM((tt + 8, w), f32)],
        compiler_params=_cp("parallel", "arbitrary"),
        name="lru_prompt",
    )(gate_in.reshape(bsz, l, w), rec3, rec3, conv_w, conv_b.reshape(1, w), w_a.astype(bf16), b_a.reshape(1, w),
      w_x.astype(bf16), b_x.reshape(1, w), lam.reshape(1, w))
    return y.reshape(bsz * l, w), h.reshape(bsz, w), rec3[:, l - (CONV - 1):]


def _lru_step_body(gate_ref, rec_ref, buf_ref, h0_ref, cw_ref, cb_ref, wa_ref, ba_ref, wx_ref, bx_ref, lam_ref,
                   y_ref, h_ref, buf_out_ref):
    w = LRU_W
    new = rec_ref[...]
    cw = cw_ref[...]
    u = new * cw[CONV - 1:CONV, :]
    for j in range(CONV - 1):
        u = u + buf_ref[:, j * w:(j + 1) * w] * cw[j:j + 1, :]
    u = u + cb_ref[...]
    buf_out_ref[:, 0:2 * w] = buf_ref[:, w:3 * w]
    buf_out_ref[:, 2 * w:3 * w] = new
    a, b = _lru_gates(u, wa_ref, ba_ref[...], wx_ref, bx_ref[...], lam_ref[...])
    h = b + a * h0_ref[...]
    h_ref[...] = h
    y_ref[...] = _gelu(gate_ref[...]) * h


def _lru_sample(x, conv_buf, h0, w_in, conv_w, conv_b, w_a, b_a, w_x, b_x, lam):
    nb = x.shape[0]
    w = LRU_W
    gate_in, rec_in = _proj(x, [w_in[:, :w].astype(bf16), w_in[:, w:].astype(bf16)], nb)
    y, h, buf = pl.pallas_call(
        _lru_step_body,
        out_shape=[jax.ShapeDtypeStruct((nb, w), f32), jax.ShapeDtypeStruct((nb, w), f32),
                   jax.ShapeDtypeStruct((nb, 3 * w), f32)],
        compiler_params=pltpu.CompilerParams(vmem_limit_bytes=VMEM_LIMIT),
        name="lru_step",
    )(gate_in, rec_in, conv_buf.reshape(nb, 3 * w), h0, conv_w, conv_b.reshape(1, w), w_a.astype(bf16),
      b_a.reshape(1, w), w_x.astype(bf16), b_x.reshape(1, w), lam.reshape(1, w))
    return y, h, buf.reshape(nb, CONV - 1, w)


NEG = -1e30


def _t5_bucket(rel):
    n = jnp.maximum(rel, 0)
    max_exact = N_BUCKETS // 2
    large = max_exact + (jnp.log(jnp.maximum(n, 1).astype(f32) / max_exact)
                         / math.log(MAX_DIST / max_exact) * (N_BUCKETS - max_exact)).astype(jnp.int32)
    return jnp.where(n < max_exact, n, jnp.minimum(large, N_BUCKETS - 1))


def _swa_prompt_body(qt_ref, kp_ref, kc_ref, vtp_ref, vtc_ref, bucket_ref, sinks_ref, rb_ref, o_ref, bias_scr):
    i = pl.program_id(1)
    w = WINDOW

    @pl.when(i == 0)
    def _():
        bucket = bucket_ref[...]
        rel = (lax.broadcasted_iota(jnp.int32, (2 * w, w), 1) + w
               - lax.broadcasted_iota(jnp.int32, (2 * w, w), 0))
        in_window = (rel >= 0) & (rel <= w)
        for h in range(HQ):
            b = jnp.zeros((2 * w, w), f32)
            for n in range(N_BUCKETS):
                b = jnp.where(bucket == n, rb_ref[n, h], b)
            bias_scr[h // GQ, :, (h % GQ) * w:(h % GQ + 1) * w] = jnp.where(in_window, b, NEG)

    hidden_rows = jnp.where(i == 0, w, 0)
    hide = lax.broadcasted_iota(jnp.int32, (2 * w, GQ * w), 0) < hidden_rows
    for kvh in range(HKV):
        heads = range(kvh * GQ, (kvh + 1) * GQ)
        qt = jnp.concatenate([qt_ref[0, h] for h in heads], axis=-1)
        kk = jnp.concatenate([kp_ref[0, kvh], kc_ref[0, kvh]], axis=0)
        s = jnp.where(hide, NEG, jnp.dot(kk, qt, preferred_element_type=f32) + bias_scr[kvh])
        sink = jnp.concatenate([jnp.full((1, w), sinks_ref[h], f32) for h in heads], axis=-1)
        m = jnp.maximum(jnp.max(s, 0, keepdims=True), sink)
        e = jnp.exp(s - m)
        inv = 1.0 / (jnp.sum(e, 0, keepdims=True) + jnp.exp(sink - m))
        vt = jnp.concatenate([vtp_ref[0, kvh], vtc_ref[0, kvh]], axis=-1)
        ot = jnp.dot(vt, (e * inv).astype(bf16), preferred_element_type=f32)
        for g, h in enumerate(heads):
            o_ref[0, h] = ot[:, g * w:(g + 1) * w]


def _swa_prompt(x, w_in, sinks, rel_bias):
    b, l, d = x.shape
    nq = HQ * HD
    nk = HKV * HD
    qt, k, v, vt = _attn_proj(x.reshape(b * l, d), w_in, b, l, min(PROJ_TM, l))
    k = k.reshape(b, l, nk)
    v = v.reshape(b, l, nk)
    w = WINDOW
    rel = jnp.arange(w)[None, :] + w - jnp.arange(2 * w)[:, None]
    bucket = _t5_bucket(rel).astype(jnp.int32)
    qt = qt.reshape(b, HQ, HD, l)
    kh = jnp.swapaxes(k.reshape(b, l, HKV, HD), 1, 2).astype(bf16)
    vt = vt.reshape(b, HKV, HD, l)
    qblk = pl.BlockSpec((1, HQ, HD, w), lambda bi, i: (bi, 0, 0, i))
    smem = pl.BlockSpec(memory_space=pltpu.SMEM)
    ot = pl.pallas_call(
        _swa_prompt_body,
        grid=(b, l // w),
        in_specs=[qblk,
                  pl.BlockSpec((1, HKV, w, HD), lambda bi, i: (bi, 0, jnp.maximum(i - 1, 0), 0)),
                  pl.BlockSpec((1, HKV, w, HD), lambda bi, i: (bi, 0, i, 0)),
                  pl.BlockSpec((1, HKV, HD, w), lambda bi, i: (bi, 0, 0, jnp.maximum(i - 1, 0))),
                  pl.BlockSpec((1, HKV, HD, w), lambda bi, i: (bi, 0, 0, i)),
                  pl.BlockSpec((2 * w, w), lambda bi, i: (0, 0)), smem, smem],
        out_specs=qblk,
        out_shape=jax.ShapeDtypeStruct((b, HQ, HD, l), f32),
        scratch_shapes=[pltpu.VMEM((HKV, 2 * w, GQ * w), f32)],
        compiler_params=_cp("parallel", "arbitrary"),
        name="swa_prompt",
    )(qt, kh, kh, vt, vt, bucket, sinks, rel_bias)
    return ot.reshape(b, nq, l), k, v


SWA_BT = 8


def _head_mask():
    r = lax.broadcasted_iota(jnp.int32, (HQ, HKV * HD), 0)
    c = lax.broadcasted_iota(jnp.int32, (HQ, HKV * HD), 1)
    return (r // GQ) == (c // HD)


def _fold_heads(o, mask):
    o = jnp.where(mask, o, 0.0)
    acc = o[:, 0:HD]
    for c in range(1, HKV):
        acc = acc + o[:, c * HD:(c + 1) * HD]
    return acc


def _swa_step_body(q_ref, kn_ref, vn_ref, kb_ref, vb_ref, onehot_ref, rbt_ref, sinks_ref, o_ref, kb_out, vb_out):
    w = WINDOW
    mask = _head_mask()
    bias_all = _dot_hi(rbt_ref[...], onehot_ref[...])
    bias = bias_all[:, :w]
    bias_new = bias_all[:, w:w + 1]
    sink = sinks_ref[...]
    for bi in range(q_ref.shape[0]):
        qm = jnp.where(mask, jnp.concatenate([q_ref[bi]] * HKV, axis=-1), 0.0)
        kn = kn_ref[bi]
        vn = vn_ref[bi]
        s = _dot_nt(qm, kb_ref[bi]) * (HD ** -0.5) + bias
        s_new = jnp.sum(qm * kn, -1, keepdims=True) * (HD ** -0.5) + bias_new
        m = jnp.maximum(jnp.maximum(jnp.max(s, -1, keepdims=True), s_new), sink)
        e = jnp.exp(s - m)
        e_new = jnp.exp(s_new - m)
        denom = jnp.sum(e, -1, keepdims=True) + e_new + jnp.exp(sink - m)
        o = (_dot(e / denom, vb_ref[bi]) + (e_new / denom) * vn)
        o_ref[bi] = _fold_heads(o, mask)
        kb_out[bi, 0:w - 1, :] = kb_ref[bi, 1:w, :]
        kb_out[bi, w - 1:w, :] = kn
        vb_out[bi, 0:w - 1, :] = vb_ref[bi, 1:w, :]
        vb_out[bi, w - 1:w, :] = vn


def _swa_sample(x, k_buf, v_buf, w_in, sinks, rel_bias):
    nb = x.shape[0]
    nq = HQ * HD
    nk = HKV * HD
    w = WINDOW
    q, k, v = _proj(x, [w_in[:, :nq].astype(bf16), w_in[:, nq:nq + nk].astype(bf16), w_in[:, nq + nk:].astype(bf16)], nb)
    slots = jnp.arange(w + 128)
    bucket = _t5_bucket(jnp.where(slots <= w, w - slots, 0))
    onehot = (bucket[None, :] == jnp.arange(N_BUCKETS)[:, None]).astype(f32)
    bt = SWA_BT
    full = lambda a: pl.BlockSpec(a.shape, lambda i: (0,) * a.ndim)
    bufblk = pl.BlockSpec((bt, w, nk), lambda i: (i, 0, 0))
    rbt = rel_bias.T
    sk = sinks.reshape(HQ, 1)
    o, kb, vb = pl.pallas_call(
        _swa_step_body,
        grid=(nb // bt,),
        in_specs=[pl.BlockSpec((bt, HQ, HD), lambda i: (i, 0, 0)),
                  pl.BlockSpec((bt, 1, nk), lambda i: (i, 0, 0)), pl.BlockSpec((bt, 1, nk), lambda i: (i, 0, 0)),
                  bufblk, bufblk, full(onehot), full(rbt), full(sk)],
        out_specs=[pl.BlockSpec((bt, HQ, HD), lambda i: (i, 0, 0)), bufblk, bufblk],
        out_shape=[jax.ShapeDtypeStruct((nb, HQ, HD), f32), jax.ShapeDtypeStruct((nb, w, nk), f32),
                   jax.ShapeDtypeStruct((nb, w, nk), f32)],
        compiler_params=_cp("parallel"),
        name="swa_step",
    )(q.reshape(nb, HQ, HD), k.reshape(nb, 1, nk), v.reshape(nb, 1, nk), k_buf.reshape(nb, w, nk),
      v_buf.reshape(nb, w, nk), onehot, rbt, sk)
    return o.reshape(nb, nq), k, v, kb.reshape(nb, w, HKV, HD), vb.reshape(nb, w, HKV, HD)


FOX_PREP_TILE = 512
FOX_TQ = 512
FOX_TK = 1024


def _fox_prep_body(x_ref, wft_ref, bf_ref, lf_ref, cum_ref, carry):
    i = pl.program_id(1)
    tt = x_ref.shape[1]

    @pl.when(i == 0)
    def _():
        carry[...] = jnp.zeros_like(carry)

    f = lax.dot_general(wft_ref[...], x_ref[0].astype(bf16), (((1,), (1,)), ((), ())), preferred_element_type=f32)
    lf = _log_sigmoid(f + bf_ref[...])
    r = lax.broadcasted_iota(jnp.int32, (tt, tt), 0)
    c = lax.broadcasted_iota(jnp.int32, (tt, tt), 1)
    cum = _dot_exact01(lf, (r <= c).astype(f32)) + carry[...]
    lf_ref[0] = lf
    cum_ref[0] = cum
    carry[...] = cum[:, tt - 1:tt]


FOX_SUM_ROWS = 16


def _fox_flash_body(qt_ref, k_ref, vt_ref, fk_ref, o_ref, m_scr, acc_scr):
    i = pl.program_id(2)
    tq = qt_ref.shape[3]
    tk = FOX_TK
    m_scr[...] = jnp.full_like(m_scr, NEG)
    acc_scr[...] = jnp.zeros_like(acc_scr)
    ones = jnp.ones((FOX_SUM_ROWS, tk), bf16)
    qt_all = jnp.concatenate([qt_ref[0, g] for g in range(GQ)], axis=-1)

    def block(j, masked):
        c0 = pl.multiple_of(j * tk, tk)
        k = k_ref[0, 0, pl.ds(c0, tk), :]
        vt = jnp.concatenate([vt_ref[0, 0, :, pl.ds(c0, tk)], ones], axis=0)
        fk = fk_ref[0, 0, pl.ds(c0, tk), :]
        if masked:
            s_pos = c0 + lax.broadcasted_iota(jnp.int32, (tk, tq), 0)
            t_pos = i * tq + lax.broadcasted_iota(jnp.int32, (tk, tq), 1)
            visible = s_pos <= t_pos
        s_all = jnp.dot(k, qt_all, preferred_element_type=f32)
        for g in range(GQ):
            s = s_all[:, g * tq:(g + 1) * tq] - fk[:, g:g + 1]
            if masked:
                s = jnp.where(visible, s, NEG)
            m_old = m_scr[g]
            m_new = jnp.maximum(m_old, jnp.max(s, 0, keepdims=True))
            alpha = jnp.exp(m_old - m_new)
            p = jnp.exp(s - m_new)
            acc_scr[g] = alpha * acc_scr[g] + jnp.dot(vt, p.astype(bf16), preferred_element_type=f32)
            m_scr[g] = m_new

    n_full = (i * tq) // tk

    def full_block(j, carry):
        block(j, False)
        return carry

    lax.fori_loop(0, n_full, full_block, 0)
    for jj in range(pl.cdiv(tq, tk)):
        block(n_full + jj, True)
    for g in range(GQ):
        o_ref[0, g] = acc_scr[g, :HD] / acc_scr[g, HD:HD + 1]


def _fox_prompt(x, w_in, b_f):
    b, l, d = x.shape
    nq = HQ * HD
    nk = HKV * HD
    qt, k, v, vt = _attn_proj(x.reshape(b * l, d), w_in, b, l, min(PROJ_TM, l))
    tt = min(FOX_PREP_TILE, l)
    wft = w_in[:, nq + 2 * nk:].T.astype(bf16)
    row = pl.BlockSpec((1, HQ, tt), lambda bi, i: (bi, 0, i))
    lft, cumt = pl.pallas_call(
        _fox_prep_body,
        grid=(b, l // tt),
        in_specs=[pl.BlockSpec((1, tt, d), lambda bi, i: (bi, i, 0)),
                  pl.BlockSpec((HQ, d), lambda bi, i: (0, 0)), pl.BlockSpec((HQ, 1), lambda bi, i: (0, 0))],
        out_specs=[row, row],
        out_shape=[jax.ShapeDtypeStruct((b, HQ, l), f32)] * 2,
        scratch_shapes=[pltpu.VMEM((HQ, 1), f32)],
        compiler_params=_cp("parallel", "arbitrary"),
        name="fox_prep",
    )(x, wft, b_f.reshape(HQ, 1))
    fk = jnp.swapaxes(cumt.reshape(b, HKV, GQ, l), 2, 3)
    qt = qt.reshape(b, HQ, HD, l)
    kh = jnp.swapaxes(k.reshape(b, l, HKV, HD), 1, 2).astype(bf16)
    vt = vt.reshape(b, HKV, HD, l)
    tq = min(FOX_TQ, l)
    qblk = pl.BlockSpec((1, GQ, HD, tq), lambda bi, h, i: (bi, h, 0, i))
    ot = pl.pallas_call(
        _fox_flash_body,
        grid=(b, HKV, l // tq),
        in_specs=[qblk,
                  pl.BlockSpec((1, 1, l, HD), lambda bi, h, i: (bi, h, 0, 0)),
                  pl.BlockSpec((1, 1, HD, l), lambda bi, h, i: (bi, h, 0, 0)),
                  pl.BlockSpec((1, 1, l, GQ), lambda bi, h, i: (bi, h, 0, 0))],
        out_specs=qblk,
        out_shape=jax.ShapeDtypeStruct((b, HQ, HD, l), f32),
        scratch_shapes=[pltpu.VMEM((GQ, 1, tq), f32), pltpu.VMEM((GQ, HD + FOX_SUM_ROWS, tq), f32)],
        compiler_params=_cp("parallel", "parallel", "arbitrary"),
        name="fox_flash",
    )(qt, kh, vt, fk)
    return ot.reshape(b, nq, l), k.reshape(b, l, nk), v.reshape(b, l, nk), jnp.swapaxes(lft, 1, 2)


FOX_PAGES = 16
FOX_ROWS = 1


def _fox_step_body(pt_ref, q_ref, kn_ref, vn_ref, lfn_ref, *refs):
    npg = FOX_PAGES
    n_in = 3 * npg * FOX_ROWS
    o_ref, m_scr, l_scr, acc_scr, f_scr = refs[n_in:]
    j = pl.program_id(1)
    ps = PAGE_SIZE
    mask = _head_mask()

    @pl.when(j == 0)
    def _():
        m_scr[...] = jnp.full_like(m_scr, NEG)
        l_scr[...] = jnp.zeros_like(l_scr)
        acc_scr[...] = jnp.zeros_like(acc_scr)
        f_scr[...] = jnp.zeros_like(f_scr)

    r = lax.broadcasted_iota(jnp.int32, (ps, ps), 0)
    c = lax.broadcasted_iota(jnp.int32, (ps, ps), 1)
    upper = (r <= c).astype(f32)

    for row in range(FOX_ROWS):
        base = 3 * npg * row
        k_refs = refs[base:base + npg]
        v_refs = refs[base + npg:base + 2 * npg]
        lf_refs = refs[base + 2 * npg:base + 3 * npg]
        qm = jnp.where(mask, jnp.concatenate([q_ref[row]] * HKV, axis=-1), 0.0) * (HD ** -0.5)

        def update(s, weighted_values, f_end, row=row):
            m_old = m_scr[row]
            m_new = jnp.maximum(m_old, jnp.max(s, -1, keepdims=True))
            alpha = jnp.exp(m_old - m_new)
            p = jnp.exp(s - m_new)
            l_scr[row] = alpha * l_scr[row] + jnp.sum(p, -1, keepdims=True)
            acc_scr[row] = alpha * acc_scr[row] + weighted_values(p)
            m_scr[row] = m_new
            f_scr[row] = f_end

        cum_all = _dot_exact01(jnp.concatenate([lf_refs[pg][0] for pg in range(npg)], axis=0), upper)
        f_run = f_scr[row]
        cums = []
        for pg in range(npg):
            cums.append(cum_all[pg * HQ:(pg + 1) * HQ] + f_run)
            f_run = cums[-1][:, ps - 1:ps]
        kt = jnp.concatenate([k_refs[pg][0].reshape(HKV * HD, ps).astype(bf16) for pg in range(npg)], axis=-1)
        vt = jnp.concatenate([v_refs[pg][0].reshape(HKV * HD, ps).astype(bf16) for pg in range(npg)], axis=-1)
        s = jnp.dot(qm.astype(bf16), kt, preferred_element_type=f32) - jnp.concatenate(cums, axis=-1)
        update(s, lambda p, vt=vt: _dot_nt(p, vt), f_run)

        @pl.when(j == pl.num_programs(1) - 1)
        def _(row=row, qm=qm, update=update):
            kn = kn_ref[row]
            vn = vn_ref[row]
            cum = f_scr[row] + lfn_ref[row]
            s = jnp.sum(qm * kn, -1, keepdims=True) - cum
            update(s, lambda p: p * vn, cum)
            o_ref[row] = _fold_heads(acc_scr[row] / l_scr[row], mask)


def _logf_body(f_ref, b_ref, o_ref):
    o_ref[...] = _log_sigmoid(f_ref[...] + b_ref[...])


def _fox_sample(x, cache_k, cache_v, cache_logf, page_table, w_in, b_f):
    nb = x.shape[0]
    nq = HQ * HD
    nk = HKV * HD
    n_pool = cache_k.shape[0]
    n_pages = page_table.shape[1]
    npg = FOX_PAGES
    q, k, v, f = _proj(x, [w_in[:, :nq].astype(bf16), w_in[:, nq:nq + nk].astype(bf16),
                           w_in[:, nq + nk:nq + 2 * nk].astype(bf16), w_in[:, nq + 2 * nk:].astype(bf16)], nb)
    lf_new = pl.pallas_call(_logf_body, out_shape=jax.ShapeDtypeStruct((nb, HQ), f32))(f, b_f.reshape(1, HQ))

    nr = FOX_ROWS

    def page(row, pg, *shape):
        return pl.BlockSpec((1,) + shape,
                            lambda bi, j, pt: (pt[(bi * nr + row) * n_pages + j * npg + pg],) + (0,) * len(shape))

    tok = lambda *shape: pl.BlockSpec((nr,) + shape, lambda bi, j, pt: (bi,) + (0,) * len(shape))
    ck = jnp.transpose(cache_k, (0, 2, 3, 1))
    cv = jnp.transpose(cache_v, (0, 2, 3, 1))
    clf = jnp.transpose(cache_logf, (0, 2, 1))
    page_specs, page_args = [], []
    for row in range(nr):
        for arr, shape in ((ck, (HKV, HD, PAGE_SIZE)), (cv, (HKV, HD, PAGE_SIZE)), (clf, (HQ, PAGE_SIZE))):
            page_specs += [page(row, pg, *shape) for pg in range(npg)]
            page_args += [arr] * npg
    o = pl.pallas_call(
        _fox_step_body,
        grid_spec=pltpu.PrefetchScalarGridSpec(
            num_scalar_prefetch=1,
            grid=(nb // nr, n_pages // npg),
            in_specs=[tok(HQ, HD), tok(1, nk), tok(1, nk), tok(HQ, 1)] + page_specs,
            out_specs=tok(HQ, HD),
            scratch_shapes=[pltpu.VMEM((nr, HQ, 1), f32), pltpu.VMEM((nr, HQ, 1), f32),
                            pltpu.VMEM((nr, HQ, nk), f32), pltpu.VMEM((nr, HQ, 1), f32)]),
        out_shape=jax.ShapeDtypeStruct((nb, HQ, HD), f32),
        compiler_params=_cp("parallel", "arbitrary"),
        name="fox_decode",
    )(page_table.reshape(-1), q.reshape(nb, HQ, HD), k.reshape(nb, 1, nk), v.reshape(nb, 1, nk),
      lf_new.reshape(nb, HQ, 1), *page_args)
    return o.reshape(nb, nq), k, v, lf_new


ROUTER_TILE = 512
MOE_TM = 1024


def _router_body(x_ref, rwt_ref, rb_ref, comb_ref):
    comb_ref[...] = _route(x_ref[...], rwt_ref[...], rb_ref[...])


def _route(x, rwt, rb):
    scores = _sigmoid(_dot_nt(rwt, x))
    sel = scores + rb
    rows = [sel[e:e + 1, :] for e in range(N_EXPERTS)]
    srow = [scores[e:e + 1, :] for e in range(N_EXPERTS)]
    gs = []
    for g in range(N_GROUPS):
        r = rows[g * EPG:(g + 1) * EPG]
        best = None
        for a in range(EPG):
            for b in range(a + 1, EPG):
                pair = r[a] + r[b]
                best = pair if best is None else jnp.maximum(best, pair)
        gs.append(best)
    g_best = gs[0]
    g_idx = jnp.zeros_like(gs[0], dtype=jnp.int32)
    for g in range(1, N_GROUPS):
        better = gs[g] > g_best
        g_best = jnp.where(better, gs[g], g_best)
        g_idx = jnp.where(better, g, g_idx)

    def in_group(vals, j):
        out = vals[j]
        for g in range(1, N_GROUPS):
            out = jnp.where(g_idx == g, vals[g * EPG + j], out)
        return out

    ig = [in_group(rows, j) for j in range(EPG)]
    sg = [in_group(srow, j) for j in range(EPG)]
    v1, i1, s1 = ig[0], jnp.zeros_like(g_idx), sg[0]
    for j in range(1, EPG):
        better = ig[j] > v1
        v1 = jnp.where(better, ig[j], v1)
        i1 = jnp.where(better, j, i1)
        s1 = jnp.where(better, sg[j], s1)
    v2 = jnp.full_like(v1, -jnp.inf)
    i2 = jnp.zeros_like(g_idx)
    s2 = jnp.zeros_like(s1)
    for j in range(EPG):
        better = (i1 != j) & (ig[j] > v2)
        v2 = jnp.where(better, ig[j], v2)
        i2 = jnp.where(better, j, i2)
        s2 = jnp.where(better, sg[j], s2)
    tot = s1 + s2
    e_row = lax.broadcasted_iota(jnp.int32, scores.shape, 0)
    return (jnp.where(e_row == g_idx * EPG + i1, s1 / tot, 0.0)
            + jnp.where(e_row == g_idx * EPG + i2, s2 / tot, 0.0))


def _router(x, router_w, router_b):
    t, d = x.shape
    tt = min(ROUTER_TILE, t)
    comb_t = pl.pallas_call(
        _router_body,
        grid=(t // tt,),
        in_specs=[pl.BlockSpec((tt, d), lambda i: (i, 0)), pl.BlockSpec((N_EXPERTS, d), lambda i: (0, 0)),
                  pl.BlockSpec((N_EXPERTS, 1), lambda i: (0, 0))],
        out_specs=pl.BlockSpec((N_EXPERTS, tt), lambda i: (0, i)),
        out_shape=jax.ShapeDtypeStruct((N_EXPERTS, t), f32),
        compiler_params=_cp("parallel"),
        name="router",
    )(x, router_w.T, router_b.reshape(N_EXPERTS, 1))
    return comb_t.T


def _moe_dense_body(x_ref, comb_ref, wg_ref, wu_ref, wd_ref, g_ref, b_ref, o_ref, acc_scr):
    e = pl.program_id(1)

    @pl.when(e == 0)
    def _():
        acc_scr[...] = jnp.zeros_like(acc_scr)

    x = x_ref[...].astype(bf16)
    h = _silu(jnp.dot(x, wg_ref[0].astype(bf16), preferred_element_type=f32)) \
        * jnp.dot(x, wu_ref[0].astype(bf16), preferred_element_type=f32)
    y = jnp.dot(h.astype(bf16), wd_ref[0].astype(bf16), preferred_element_type=f32)
    lane = lax.broadcasted_iota(jnp.int32, comb_ref.shape, 1)
    c = jnp.sum(jnp.where(lane == e, comb_ref[...], 0.0), -1, keepdims=True)
    acc_scr[...] += c * y

    @pl.when(e == pl.num_programs(1) - 1)
    def _():
        o_ref[...] = _layer_norm(DN_ALPHA * x_ref[...] + acc_scr[...], g_ref[...], b_ref[...])


def _moe_ln(x, comb, w_gate, w_up, w_down, layer, g, b):
    t, d = x.shape
    tm = min(MOE_TM, t)
    return pl.pallas_call(
        _moe_dense_body,
        grid=(t // tm, N_EXPERTS),
        in_specs=[pl.BlockSpec((tm, d), lambda i, e: (i, 0)), pl.BlockSpec((tm, N_EXPERTS), lambda i, e: (i, 0)),
                  pl.BlockSpec((None, 1, d, D_EXPERT), lambda i, e: (layer, e, 0, 0)),
                  pl.BlockSpec((None, 1, d, D_EXPERT), lambda i, e: (layer, e, 0, 0)),
                  pl.BlockSpec((None, 1, D_EXPERT, d), lambda i, e: (layer, e, 0, 0)),
                  pl.BlockSpec((1, d), lambda i, e: (0, 0)), pl.BlockSpec((1, d), lambda i, e: (0, 0))],
        out_specs=pl.BlockSpec((tm, d), lambda i, e: (i, 0)),
        out_shape=jax.ShapeDtypeStruct((t, d), f32),
        scratch_shapes=[pltpu.VMEM((tm, d), f32)],
        compiler_params=_cp("parallel", "arbitrary"),
        name="moe_dense",
    )(x, comb, w_gate, w_up, w_down, g.reshape(1, d), b.reshape(1, d))


def kernel(x_prompt, x_sample, state_a_ssm, state_a_conv, state_b_h, state_b_conv, cache_c_k, cache_c_v, cache_d_k, cache_d_v, cache_d_logf, page_table, ln_g, ln_b, a_w_in, a_conv_w, a_a_log, a_dt_bias, a_norm_w, a_w_out, b_w_in, b_conv_w, b_conv_b, b_w_a, b_b_a, b_w_x, b_b_x, b_lambda, b_w_out, c_w_in, c_sinks, c_w_out, rel_bias, d_w_in, d_b_f, d_w_out, router_w, router_b, moe_w_gate, moe_w_up, moe_w_down):
    bp, lp, d = x_prompt.shape
    nb = x_sample.shape[0]
    xp = x_prompt.reshape(bp * lp, d)
    xs = x_sample.reshape(nb, d)
    tm_p = PROJ_TM

    def finish(xp, xs, mp, ms, w_out, i, transposed=False):
        w = w_out.astype(bf16)
        xp, comb_p = _proj_ln(mp, w, xp, ln_g[i, 0], ln_b[i, 0], tm_p, router_w, router_b, transposed)
        xs, comb_s = _proj_ln(ms, w, xs, ln_g[i, 0], ln_b[i, 0], nb, router_w, router_b)
        moe_args = (moe_w_gate, moe_w_up, moe_w_down, i, ln_g[i, 1], ln_b[i, 1])
        return _moe_ln(xp, comb_p, *moe_args), _moe_ln(xs, comb_s, *moe_args)

    a_args = (a_w_in[0], a_conv_w[0], a_a_log[0], a_dt_bias[0], a_norm_w[0])
    mp, p_a_ssm, p_a_conv = _gdn_prompt(xp.reshape(bp, lp, d), *a_args)
    ms, s_a_ssm, s_a_conv = _gdn_sample(xs, state_a_conv[0], state_a_ssm[0], *a_args)
    xp, xs = finish(xp, xs, mp, ms, a_w_out[0], 0)

    b_args = (b_w_in[0], b_conv_w[0], b_conv_b[0], b_w_a[0], b_b_a[0], b_w_x[0], b_b_x[0], b_lambda[0])
    mp, p_b_h, p_b_conv = _lru_prompt(xp.reshape(bp, lp, d), *b_args)
    ms, s_b_h, s_b_conv = _lru_sample(xs, state_b_conv[0], state_b_h[0], *b_args)
    xp, xs = finish(xp, xs, mp, ms, b_w_out[0], 1)

    mp, kp, vp = _swa_prompt(xp.reshape(bp, lp, d), c_w_in[0], c_sinks[0], rel_bias)
    ms, _, _, s_c_k, s_c_v = _swa_sample(xs, cache_c_k[0], cache_c_v[0], c_w_in[0], c_sinks[0], rel_bias)
    p_c_k = kp[:, lp - WINDOW:].reshape(bp, WINDOW, HKV, HD)
    p_c_v = vp[:, lp - WINDOW:].reshape(bp, WINDOW, HKV, HD)
    xp, xs = finish(xp, xs, mp, ms, c_w_out[0], 2, transposed=True)

    mp, kp, vp, p_d_logf = _fox_prompt(xp.reshape(bp, lp, d), d_w_in[0], d_b_f[0])
    ms, ks, vs, lfs = _fox_sample(xs, cache_d_k[0], cache_d_v[0], cache_d_logf[0], page_table, d_w_in[0], d_b_f[0])
    xp, xs = finish(xp, xs, mp, ms, d_w_out[0], 3, transposed=True)

    return (xp.reshape(bp, lp, d), xs.reshape(nb, 1, d),
            p_a_ssm[None], p_a_conv[None], p_b_h[None], p_b_conv[None], p_c_k[None], p_c_v[None],
            kp.reshape(1, bp, lp, HKV, HD), vp.reshape(1, bp, lp, HKV, HD), p_d_logf[None],
            s_a_ssm[None], s_a_conv[None], s_b_h[None], s_b_conv[None], s_c_k[None], s_c_v[None],
            ks.reshape(1, nb, 1, HKV, HD), vs.reshape(1, nb, 1, HKV, HD), lfs.reshape(1, nb, 1, HQ))
```
